```python
import jax
import jax.numpy as jnp
from jax import lax
import numpy as np

D_MODEL = 1024
BATCH = 8
SEQ = 2048
DEPTH = 1
DEC_BATCH = 128
DEC_SEQ = 4
PAST_LEN = 2048
PAGE_SIZE = 128

HEAD_DIM = 128
N_KV_HEADS = 4
DILATED_GROUPS = ((128, 1), (512, 4), (2048, 16))
N_ATTN_GROUPS = 3
N_Q_HEADS = N_ATTN_GROUPS * N_KV_HEADS
ATTN_WIDTH = N_KV_HEADS * HEAD_DIM
MAX_WINDOW = 2048
ROPE_THETA = 500000.0
ROT_DIM = HEAD_DIM // 4
CONV_CH = D_MODEL // 2
CONV_WIDTH = 31
N_MEM = 256
MEM_HEADS = 4
MEM_HEAD_DIM = 128
MEM_WIDTH = MEM_HEADS * MEM_HEAD_DIM
N_BRANCHES = 3
N_EXPERT_GROUPS = 4
EXPERTS_PER_GROUP = 8
N_EXPERTS = N_EXPERT_GROUPS * EXPERTS_PER_GROUP
EXPERT_FF = 512
TOP_K_IN_GROUP = 2
EPS = 1e-6
IN_SPLIT = (N_Q_HEADS * HEAD_DIM, ATTN_WIDTH, ATTN_WIDTH, 2 * CONV_CH, MEM_WIDTH, N_BRANCHES * D_MODEL)
IN_COLS = N_Q_HEADS * HEAD_DIM + 2 * ATTN_WIDTH + 2 * CONV_CH + MEM_WIDTH + N_BRANCHES * D_MODEL

kernel_name = 'hybrid_dilated_conformer_memory_hmoe_step'


def rms_norm(x, g):
    xf = x.astype(jnp.float32)
    y = xf * lax.rsqrt(jnp.mean(xf * xf, axis=-1, keepdims=True) + EPS)
    return (y * g.astype(jnp.float32)).astype(x.dtype)


def layer_norm(x, g, b):
    xf = x.astype(jnp.float32)
    mu = jnp.mean(xf, axis=-1, keepdims=True)
    xc = xf - mu
    y = xc * lax.rsqrt(jnp.mean(xc * xc, axis=-1, keepdims=True) + EPS)
    return (y * g.astype(jnp.float32) + b.astype(jnp.float32)).astype(x.dtype)


def rope_partial(x, pos):
    half = ROT_DIM // 2
    inv_freq = jnp.power(jnp.float32(ROPE_THETA), -jnp.arange(half, dtype=jnp.float32) * (2.0 / ROT_DIM))
    ang = pos.astype(jnp.float32)[:, None] * inv_freq[None, :]
    cos = jnp.cos(ang)[:, None, :]
    sin = jnp.sin(ang)[:, None, :]
    xr = x[..., :ROT_DIM].astype(jnp.float32)
    x1, x2 = xr[..., :half], xr[..., half:]
    rot = jnp.concatenate([x1 * cos - x2 * sin, x2 * cos + x1 * sin], axis=-1).astype(x.dtype)
    return jnp.concatenate([rot, x[..., ROT_DIM:]], axis=-1)


def split_combined(z):
    idx = [int(i) for i in np.cumsum(IN_SPLIT)[:-1]]
    return jnp.split(z, idx, axis=-1)


def attention_qkv(z_q, z_k, z_v, pos, lw):
    B, S = z_q.shape[:2]
    q = rope_partial(rms_norm(z_q.reshape(B, S, N_Q_HEADS, HEAD_DIM), lw['q_norm_g']), pos)
    k = rope_partial(rms_norm(z_k.reshape(B, S, N_KV_HEADS, HEAD_DIM), lw['k_norm_g']), pos)
    v = z_v.reshape(B, S, N_KV_HEADS, HEAD_DIM)
    return q, k, v


def dilated_band_attention(q, k, v, window, dilation):
    B, S, H, Dh = q.shape
    n = window // dilation
    L = S // dilation
    nb = -(-L // n)
    Lp = nb * n

    def to_blocks(t):
        t = t.reshape(B, L, dilation, H, Dh).transpose(0, 2, 1, 3, 4)
        t = jnp.pad(t, ((0, 0), (0, 0), (0, Lp - L), (0, 0), (0, 0)))
        return t.reshape(B, dilation, nb, n, H, Dh)

    def with_prev(t):
        prev = jnp.pad(t[:, :, :-1], ((0, 0), (0, 0), (1, 0), (0, 0), (0, 0), (0, 0)))
        return jnp.concatenate([prev, t], axis=3)

    qb = to_blocks(q)
    kc = with_prev(to_blocks(k))
    vc = with_prev(to_blocks(v))
    s = jnp.einsum('brnqhd,brnkhd->brnhqk', qb, kc).astype(jnp.float32) * (HEAD_DIM ** -0.5)
    iq = jnp.arange(n)[:, None]
    ck = jnp.arange(2 * n)[None, :]
    diff = n + iq - ck
    blk = jnp.arange(nb)[:, None, None]
    valid = (diff >= 0)[None] & (diff <= n)[None] & ((blk > 0) | (ck >= n)[None])
    s = jnp.where(valid[:, None, :, :], s, -jnp.inf)
    lse = jax.nn.logsumexp(s, axis=-1)
    p = jnp.exp(s - lse[..., None])
    o = jnp.einsum('brnhqk,brnkhd->brnqhd', p.astype(v.dtype), vc)
    o = o.reshape(B, dilation, Lp, H, Dh)[:, :, :L].transpose(0, 2, 1, 3, 4).reshape(B, S, H, Dh)
    lse = lse.transpose(0, 1, 2, 4, 3).reshape(B, dilation, Lp, H)[:, :, :L]
    lse = lse.transpose(0, 2, 1, 3).reshape(B, S, H)
    return o, lse


def dilated_gather_attention(q, k_all, v_all, window, dilation):
    T = q.shape[1]
    L = k_all.shape[1] - T
    n = window // dilation
    idx = L + jnp.arange(T)[:, None] - dilation * jnp.arange(n + 1)[None, :]
    valid = idx >= 0
    idx = jnp.maximum(idx, 0)
    kg = jnp.take(k_all, idx, axis=1)
    vg = jnp.take(v_all, idx, axis=1)
    s = jnp.einsum('bthd,btjhd->bhtj', q, kg).astype(jnp.float32) * (HEAD_DIM ** -0.5)
    s = jnp.where(valid[None, None], s, -jnp.inf)
    lse = jax.nn.logsumexp(s, axis=-1)
    p = jnp.exp(s - lse[..., None])
    o = jnp.einsum('bhtj,btjhd->bthd', p.astype(v_all.dtype), vg)
    return o, lse.transpose(0, 2, 1)


def combine_dilated(outs, lses):
    w = jax.nn.softmax(jnp.stack(lses, axis=0), axis=0)
    return jnp.einsum('gbsh,gbshd->bshd', w.astype(outs[0].dtype), jnp.stack(outs, axis=0))


def glu(z):
    a, b = jnp.split(z, 2, axis=-1)
    return a * jax.nn.sigmoid(b)


def conformer_tail(u_ext, lw):
    c = lax.conv_general_dilated(u_ext, lw['conv_w'][:, None, :], window_strides=(1,), padding='VALID',
                                 dimension_numbers=('NWC', 'WIO', 'NWC'),
                                 feature_group_count=CONV_CH) + lw['conv_b']
    c = jax.nn.silu(layer_norm(c, lw['conv_ln_g'], lw['conv_ln_b']))
    return c @ lw['w_conv_proj']


def memory_kv(mem, lw):
    B, M, _ = mem.shape
    kv = rms_norm(mem, lw['mem_norm_g']) @ lw['w_mem_kv']
    k, v = jnp.split(kv, 2, axis=-1)
    k = rms_norm(k.reshape(B, M, MEM_HEADS, MEM_HEAD_DIM), lw['mk_norm_g'])
    return k, v.reshape(B, M, MEM_HEADS, MEM_HEAD_DIM)


def memory_branch(z_mq, mem_k, mem_v, lw):
    B, S, _ = z_mq.shape
    mq = rms_norm(z_mq.reshape(B, S, MEM_HEADS, MEM_HEAD_DIM), lw['mq_norm_g'])
    s = jnp.einsum('bshd,bmhd->bhsm', mq, mem_k).astype(jnp.float32) * (MEM_HEAD_DIM ** -0.5)
    p = jax.nn.softmax(s, axis=-1)
    o = jnp.einsum('bhsm,bmhd->bshd', p.astype(mem_v.dtype), mem_v)
    return o.reshape(B, S, MEM_WIDTH) @ lw['w_mem_proj']


def merge_branches(z_gate, a, c, m, w_out):
    g_a, g_c, g_m = jnp.split(z_gate, N_BRANCHES, axis=-1)
    z = jax.nn.sigmoid(g_a) * a + jax.nn.sigmoid(g_c) * c + jax.nn.sigmoid(g_m) * m
    return z @ w_out


def mixer_prompt(h, mem, lw):
    B, S, _ = h.shape
    pos = jnp.arange(S, dtype=jnp.int32)
    z_q, z_k, z_v, z_conv, z_mq, z_gate = split_combined(h @ lw['w_in'])
    q, k, v = attention_qkv(z_q, z_k, z_v, pos, lw)
    outs, lses = [], []
    for g, (window, dilation) in enumerate(DILATED_GROUPS):
        o, l = dilated_band_attention(q[:, :, g * N_KV_HEADS:(g + 1) * N_KV_HEADS], k, v, window, dilation)
        outs.append(o)
        lses.append(l)
    a = combine_dilated(outs, lses).reshape(B, S, ATTN_WIDTH) @ lw['w_attn_proj']
    u = glu(z_conv)
    c = conformer_tail(jnp.pad(u, ((0, 0), (CONV_WIDTH - 1, 0), (0, 0))), lw)
    mem_k, mem_v = memory_kv(mem, lw)
    m = memory_branch(z_mq, mem_k, mem_v, lw)
    out = merge_branches(z_gate, a, c, m, lw['w_out'])
    lw_len = min(MAX_WINDOW, S)
    state = (k[:, S - lw_len:], v[:, S - lw_len:], u[:, S - (CONV_WIDTH - 1):], mem_k, mem_v)
    return out, state


def mixer_sample(h, cache_k, cache_v, state_conv, mem_k, mem_v, lw):
    B, T, _ = h.shape
    pos = PAST_LEN + jnp.arange(T, dtype=jnp.int32)
    z_q, z_k, z_v, z_conv, z_mq, z_gate = split_combined(h @ lw['w_in'])
    q, k, v = attention_qkv(z_q, z_k, z_v, pos, lw)
    k_all = jnp.concatenate([cache_k, k], axis=1)
    v_all = jnp.concatenate([cache_v, v], axis=1)
    outs, lses = [], []
    for g, (window, dilation) in enumerate(DILATED_GROUPS):
        o, l = dilated_gather_attention(q[:, :, g * N_KV_HEADS:(g + 1) * N_KV_HEADS], k_all, v_all, window, dilation)
        outs.append(o)
        lses.append(l)
    a = combine_dilated(outs, lses).reshape(B, T, ATTN_WIDTH) @ lw['w_attn_proj']
    u_all = jnp.concatenate([state_conv, glu(z_conv)], axis=1)
    c = conformer_tail(u_all, lw)
    m = memory_branch(z_mq, mem_k, mem_v, lw)
    out = merge_branches(z_gate, a, c, m, lw['w_out'])
    state = (k_all[:, T:], v_all[:, T:], u_all[:, T:])
    return out, state


def hier_moe(h, lw):
    N = h.shape[0]
    lg = (h @ lw['w_router_group']).astype(jnp.float32) + lw['b_router_group'].astype(jnp.float32)
    pg = jax.nn.softmax(lg, axis=-1)
    pg_top, g_idx = lax.top_k(pg, 1)
    le = jnp.einsum('nd,gde->nge', h, lw['w_router_expert']).astype(jnp.float32) + lw['b_router_expert'].astype(jnp.float32)
    sel = jnp.broadcast_to(g_idx[:, :, None], (N, 1, EXPERTS_PER_GROUP))
    le_sel = jnp.take_along_axis(le, sel, axis=1)[:, 0]
    pe = jax.nn.softmax(le_sel, axis=-1)
    pe_top, e_idx = lax.top_k(pe, TOP_K_IN_GROUP)
    w = pg_top * pe_top / jnp.sum(pe_top, axis=-1, keepdims=True)
    expert_ids = g_idx * EXPERTS_PER_GROUP + e_idx
    gates = jnp.einsum('nk,nke->ne', w, jax.nn.one_hot(expert_ids, N_EXPERTS, dtype=jnp.float32)).astype(h.dtype)
    y = jnp.zeros_like(h)
    for e in range(N_EXPERTS):
        hid = jax.nn.silu(h @ lw['w_expert_gate'][e]) * (h @ lw['w_expert_up'][e])
        y = y + gates[:, e:e + 1] * (hid @ lw['w_expert_down'][e])
    return y


def channel_mixer(x, lw):
    B, S, D = x.shape
    h = rms_norm(x, lw['norm_ffn_g']).reshape(B * S, D)
    return x + hier_moe(h, lw).reshape(B, S, D)


def setup_inputs(seed: int = 0) -> dict:
    key = jax.random.key(seed)
    ks = iter(jax.random.split(key, 40))

    def nrm(shape, scale=1.0):
        return scale * jax.random.normal(next(ks), shape, jnp.float32)

    def gain(shape):
        return 1.0 + 0.05 * nrm(shape)

    lw_past = min(MAX_WINDOW, PAST_LEN)
    return {
        'x_prompt': nrm((BATCH, SEQ, D_MODEL)),
        'x_sample': nrm((DEC_BATCH, DEC_SEQ, D_MODEL)),
        'mem_prompt': nrm((BATCH, N_MEM, D_MODEL)),
        'cache_k': nrm((DEPTH, DEC_BATCH, lw_past, N_KV_HEADS, HEAD_DIM)),
        'cache_v': nrm((DEPTH, DEC_BATCH, lw_past, N_KV_HEADS, HEAD_DIM)),
        'state_conv': nrm((DEPTH, DEC_BATCH, CONV_WIDTH - 1, CONV_CH)),
        'cache_mem_k': nrm((DEPTH, DEC_BATCH, N_MEM, MEM_HEADS, MEM_HEAD_DIM)),
        'cache_mem_v': nrm((DEPTH, DEC_BATCH, N_MEM, MEM_HEADS, MEM_HEAD_DIM)),
        'norm_mix_g': gain((DEPTH, D_MODEL)),
        'w_in': nrm((DEPTH, D_MODEL, IN_COLS), D_MODEL ** -0.5),
        'q_norm_g': gain((DEPTH, HEAD_DIM)),
        'k_norm_g': gain((DEPTH, HEAD_DIM)),
        'conv_w': nrm((DEPTH, CONV_WIDTH, CONV_CH), CONV_WIDTH ** -0.5),
        'conv_b': nrm((DEPTH, CONV_CH), 0.02),
        'conv_ln_g': gain((DEPTH, CONV_CH)),
        'conv_ln_b': nrm((DEPTH, CONV_CH), 0.02),
        'mem_norm_g': gain((DEPTH, D_MODEL)),
        'w_mem_kv': nrm((DEPTH, D_MODEL, 2 * MEM_WIDTH), D_MODEL ** -0.5),
        'mq_norm_g': gain((DEPTH, MEM_HEAD_DIM)),
        'mk_norm_g': gain((DEPTH, MEM_HEAD_DIM)),
        'w_attn_proj': nrm((DEPTH, ATTN_WIDTH, D_MODEL), ATTN_WIDTH ** -0.5),
        'w_conv_proj': nrm((DEPTH, CONV_CH, D_MODEL), CONV_CH ** -0.5),
        'w_mem_proj': nrm((DEPTH, MEM_WIDTH, D_MODEL), MEM_WIDTH ** -0.5),
        'w_out': nrm((DEPTH, D_MODEL, D_MODEL), D_MODEL ** -0.5),
        'norm_ffn_g': gain((DEPTH, D_MODEL)),
        'w_router_group': nrm((DEPTH, D_MODEL, N_EXPERT_GROUPS), D_MODEL ** -0.5),
        'b_router_group': nrm((DEPTH, N_EXPERT_GROUPS), 0.01),
        'w_router_expert': nrm((DEPTH, N_EXPERT_GROUPS, D_MODEL, EXPERTS_PER_GROUP), D_MODEL ** -0.5),
        'b_router_expert': nrm((DEPTH, N_EXPERT_GROUPS, EXPERTS_PER_GROUP), 0.01),
        'w_expert_gate': nrm((DEPTH, N_EXPERTS, D_MODEL, EXPERT_FF), D_MODEL ** -0.5),
        'w_expert_up': nrm((DEPTH, N_EXPERTS, D_MODEL, EXPERT_FF), D_MODEL ** -0.5),
        'w_expert_down': nrm((DEPTH, N_EXPERTS, EXPERT_FF, D_MODEL), EXPERT_FF ** -0.5),
    }


def reference(x_prompt, x_sample, mem_prompt, cache_k, cache_v, state_conv, cache_mem_k, cache_mem_v,
              norm_mix_g, w_in, q_norm_g, k_norm_g, conv_w, conv_b, conv_ln_g, conv_ln_b,
              mem_norm_g, w_mem_kv, mq_norm_g, mk_norm_g, w_attn_proj, w_conv_proj, w_mem_proj, w_out,
              norm_ffn_g, w_router_group, b_router_group, w_router_expert, b_router_expert,
              w_expert_gate, w_expert_up, w_expert_down):
    y_prompt, y_sample = x_prompt, x_sample
    kp, vp, cp, mkp, mvp, ksm, vsm, csm = [], [], [], [], [], [], [], []
    for layer in range(DEPTH):
        lw = {
            'norm_mix_g': norm_mix_g[layer], 'w_in': w_in[layer],
            'q_norm_g': q_norm_g[layer], 'k_norm_g': k_norm_g[layer],
            'conv_w': conv_w[layer], 'conv_b': conv_b[layer],
            'conv_ln_g': conv_ln_g[layer], 'conv_ln_b': conv_ln_b[layer],
            'mem_norm_g': mem_norm_g[layer], 'w_mem_kv': w_mem_kv[layer],
            'mq_norm_g': mq_norm_g[layer], 'mk_norm_g': mk_norm_g[layer],
            'w_attn_proj': w_attn_proj[layer], 'w_conv_proj': w_conv_proj[layer],
            'w_mem_proj': w_mem_proj[layer], 'w_out': w_out[layer],
            'norm_ffn_g': norm_ffn_g[layer],
            'w_router_group': w_router_group[layer], 'b_router_group': b_router_group[layer],
            'w_router_expert': w_router_expert[layer], 'b_router_expert': b_router_expert[layer],
            'w_expert_gate': w_expert_gate[layer], 'w_expert_up': w_expert_up[layer],
            'w_expert_down': w_expert_down[layer],
        }
        mix_p, st_p = mixer_prompt(rms_norm(y_prompt, lw['norm_mix_g']), mem_prompt, lw)
        y_prompt = channel_mixer(y_prompt + mix_p, lw)
        mix_s, st_s = mixer_sample(rms_norm(y_sample, lw['norm_mix_g']), cache_k[layer], cache_v[layer],
                                   state_conv[layer], cache_mem_k[layer], cache_mem_v[layer], lw)
        y_sample = channel_mixer(y_sample + mix_s, lw)
        kp.append(st_p[0]); vp.append(st_p[1]); cp.append(st_p[2]); mkp.append(st_p[3]); mvp.append(st_p[4])
        ksm.append(st_s[0]); vsm.append(st_s[1]); csm.append(st_s[2])
    k_win_prompt = jnp.stack(kp, axis=0)
    v_win_prompt = jnp.stack(vp, axis=0)
    conv_prompt = jnp.stack(cp, axis=0)
    mem_k_prompt = jnp.stack(mkp, axis=0)
    mem_v_prompt = jnp.stack(mvp, axis=0)
    k_win_sample = jnp.stack(ksm, axis=0)
    v_win_sample = jnp.stack(vsm, axis=0)
    conv_sample = jnp.stack(csm, axis=0)
    return (y_prompt, y_sample, k_win_prompt, v_win_prompt, conv_prompt, mem_k_prompt, mem_v_prompt, k_win_sample, v_win_sample, conv_sample)
```

```python
import functools

import jax
import jax.numpy as jnp
import numpy as np
from jax import lax
from jax.experimental import pallas as pl
from jax.experimental.pallas import tpu as pltpu

HEAD_DIM = 128
N_KV_HEADS = 4
DILATED_GROUPS = ((128, 1), (512, 4), (2048, 16))
N_GROUPS = len(DILATED_GROUPS)
N_Q_HEADS = N_GROUPS * N_KV_HEADS
ATTN_WIDTH = N_KV_HEADS * HEAD_DIM
BAND = 128
ROPE_THETA = 500000.0
ROT_DIM = HEAD_DIM // 4
CONV_WIDTH = 31
MEM_HEADS = 4
MEM_HEAD_DIM = 128
MEM_WIDTH = MEM_HEADS * MEM_HEAD_DIM
N_EXPERT_GROUPS = 4
EXPERTS_PER_GROUP = 8
N_EXPERTS = N_EXPERT_GROUPS * EXPERTS_PER_GROUP
TOP_K = 2
EPS = 1e-6
NEG = -1e30

LANES = 128
ROW_TILE = 256
MOE_TILE = 256
VMEM_LIMIT = 56 * 1024 * 1024

BF16 = jnp.bfloat16
F32 = jnp.float32


def _params(*sem):
    return pltpu.CompilerParams(dimension_semantics=sem, vmem_limit_bytes=VMEM_LIMIT)


def _dot(a, b):
    return jnp.dot(a, b, preferred_element_type=F32)


def _dot_nt(a, b):
    return lax.dot_general(a, b, (((1,), (1,)), ((), ())), preferred_element_type=F32)


def _dot_tn(a, b):
    return lax.dot_general(a, b, (((0,), (0,)), ((), ())), preferred_element_type=F32)


def _rms(x, g):
    return x * lax.rsqrt(jnp.mean(x * x, axis=-1, keepdims=True) + EPS) * g


def _const_spec(shape):
    nd = len(shape)
    return pl.BlockSpec(shape, lambda *_: (0,) * nd)


def _in_proj_kernel(x_ref, g_ref, w_ref, qg_ref, kg_ref, mqg_ref, rc_ref, ra_ref, rb_ref,
                    q_ref, k_ref, v_ref, u_ref, mq_ref):
    h = _rms(x_ref[...], g_ref[...]).astype(BF16)
    rc, ra, rb = rc_ref[...], ra_ref[...], rb_ref[...]

    def rope(y):
        return y * rc + pltpu.roll(y, LANES - ROT_DIM // 2, 1) * ra + pltpu.roll(y, ROT_DIM // 2, 1) * rb

    col = 0
    zq = _dot(h, w_ref[:, col:col + N_Q_HEADS * HEAD_DIM])
    for j in range(N_Q_HEADS):
        sl = slice(j * HEAD_DIM, (j + 1) * HEAD_DIM)
        q_ref[:, sl] = rope(_rms(zq[:, sl], qg_ref[...])).astype(BF16)
    col += N_Q_HEADS * HEAD_DIM
    zk = _dot(h, w_ref[:, col:col + ATTN_WIDTH])
    for j in range(N_KV_HEADS):
        sl = slice(j * HEAD_DIM, (j + 1) * HEAD_DIM)
        k_ref[:, sl] = rope(_rms(zk[:, sl], kg_ref[...]))
    col += ATTN_WIDTH
    v_ref[...] = _dot(h, w_ref[:, col:col + ATTN_WIDTH])
    col += ATTN_WIDTH
    conv_ch = u_ref.shape[-1]
    za = _dot(h, w_ref[:, col:col + conv_ch])
    zb = _dot(h, w_ref[:, col + conv_ch:col + 2 * conv_ch])
    u_ref[...] = za * jax.nn.sigmoid(zb)
    col += 2 * conv_ch
    zm = _dot(h, w_ref[:, col:col + MEM_WIDTH])
    for j in range(MEM_HEADS):
        sl = slice(j * MEM_HEAD_DIM, (j + 1) * MEM_HEAD_DIM)
        mq_ref[:, sl] = _rms(zm[:, sl], mqg_ref[...]).astype(BF16)


def _in_proj(x, g_mix, w_bf16, q_g, k_g, mq_g, rope_tabs, conv_ch):
    m, d = x.shape
    tm = min(ROW_TILE, m)
    n_tab_blocks = rope_tabs[0].shape[0] // tm
    row = lambda i: (i, 0)
    tab = lambda i: (i % n_tab_blocks, 0)
    ncols = w_bf16.shape[1]
    out_shape = (
        jax.ShapeDtypeStruct((m, N_Q_HEADS * HEAD_DIM), BF16),
        jax.ShapeDtypeStruct((m, ATTN_WIDTH), F32),
        jax.ShapeDtypeStruct((m, ATTN_WIDTH), F32),
        jax.ShapeDtypeStruct((m, conv_ch), F32),
        jax.ShapeDtypeStruct((m, MEM_WIDTH), BF16),
    )
    return pl.pallas_call(
        _in_proj_kernel,
        grid=(m // tm,),
        in_specs=[
            pl.BlockSpec((tm, d), row),
            _const_spec((1, d)),
            _const_spec((d, ncols)),
            _const_spec((1, HEAD_DIM)), _const_spec((1, HEAD_DIM)), _const_spec((1, MEM_HEAD_DIM)),
            pl.BlockSpec((tm, LANES), tab), pl.BlockSpec((tm, LANES), tab), pl.BlockSpec((tm, LANES), tab),
        ],
        out_specs=tuple(pl.BlockSpec((tm, s.shape[1]), row) for s in out_shape),
        out_shape=out_shape,
        compiler_params=_params("parallel"),
        name="in_proj",
    )(x, g_mix, w_bf16, q_g, k_g, mq_g, *rope_tabs)


def _rope_tables(pos):
    half = ROT_DIM // 2
    inv_freq = jnp.power(jnp.float32(ROPE_THETA), -jnp.arange(half, dtype=F32) * (2.0 / ROT_DIM))
    ang = pos.astype(F32)[:, None] * inv_freq[None, :]
    cos, sin = jnp.cos(ang), jnp.sin(ang)
    n = pos.shape[0]
    ones = jnp.ones((n, LANES - ROT_DIM), F32)
    zeros = jnp.zeros((n, LANES - half), F32)
    rc = jnp.concatenate([cos, cos, ones], axis=1)
    ra = jnp.concatenate([-sin, zeros], axis=1)
    rb = jnp.concatenate([jnp.zeros((n, half), F32), sin, jnp.zeros((n, LANES - ROT_DIM), F32)], axis=1)
    return rc, ra, rb


def _band_attn_kernel(q_ref, kc_ref, kp_ref, vc_ref, vp_ref, o_ref, lse_ref, *, has_prev):
    blk = pl.program_id(2)
    iq = lax.broadcasted_iota(jnp.int32, (BAND, BAND), 0)
    ik = lax.broadcasted_iota(jnp.int32, (BAND, BAND), 1)
    keep_c = iq >= ik
    keep_p = jnp.logical_and(ik >= iq, blk > 0)
    lane = lax.broadcasted_iota(jnp.int32, (BAND, LANES), 1)
    scale = HEAD_DIM ** -0.5
    lse_all = jnp.zeros((BAND, LANES), F32)
    for h in range(N_KV_HEADS):
        sl = slice(h * HEAD_DIM, (h + 1) * HEAD_DIM)
        q = q_ref[0, :, sl]
        s_c = jnp.where(keep_c, _dot_nt(q, kc_ref[0, :, sl].astype(BF16)) * scale, NEG)
        m = jnp.max(s_c, axis=-1, keepdims=True)
        if has_prev:
            s_p = jnp.where(keep_p, _dot_nt(q, kp_ref[0, :, sl].astype(BF16)) * scale, NEG)
            m = jnp.maximum(m, jnp.max(s_p, axis=-1, keepdims=True))
        p_c = jnp.exp(s_c - m)
        l = jnp.sum(p_c, axis=-1, keepdims=True)
        acc = _dot(p_c.astype(BF16), vc_ref[0, :, sl].astype(BF16))
        if has_prev:
            p_p = jnp.exp(s_p - m)
            l = l + jnp.sum(p_p, axis=-1, keepdims=True)
            acc = acc + _dot(p_p.astype(BF16), vp_ref[0, :, sl].astype(BF16))
        o_ref[0, :, sl] = acc * (1.0 / l)
        lse_all = jnp.where(lane == h, m + jnp.log(l), lse_all)
    lse_ref[0] = lse_all


def _band_attention(q, k, v, batch, seq, group):
    window, dil = DILATED_GROUPS[group]
    assert window // dil == BAND and seq % (dil * BAND) == 0
    length = seq // dil
    nb = length // BAND
    qv = q.reshape(batch, length, dil * N_Q_HEADS * HEAD_DIM)
    kv = k.reshape(batch, length, dil * ATTN_WIDTH)
    vv = v.reshape(batch, length, dil * ATTN_WIDTH)
    cur = lambda b, r, i: (b, i, r)
    prev = lambda b, r, i: (b, jnp.maximum(i - 1, 0), r)
    o, lse = pl.pallas_call(
        functools.partial(_band_attn_kernel, has_prev=nb > 1),
        grid=(batch, dil, nb),
        in_specs=[
            pl.BlockSpec((1, BAND, ATTN_WIDTH), lambda b, r, i: (b, i, r * N_GROUPS + group)),
            pl.BlockSpec((1, BAND, ATTN_WIDTH), cur), pl.BlockSpec((1, BAND, ATTN_WIDTH), prev),
            pl.BlockSpec((1, BAND, ATTN_WIDTH), cur), pl.BlockSpec((1, BAND, ATTN_WIDTH), prev),
        ],
        out_specs=(pl.BlockSpec((1, BAND, ATTN_WIDTH), cur), pl.BlockSpec((1, BAND, LANES), cur)),
        out_shape=(jax.ShapeDtypeStruct((batch, length, dil * ATTN_WIDTH), F32),
                   jax.ShapeDtypeStruct((batch, length, dil * LANES), F32)),
        compiler_params=_params("parallel", "parallel", "parallel"),
        name=f"band_attn_g{group}",
    )(qv, kv, kv, vv, vv)
    return o.reshape(batch * seq, ATTN_WIDTH), lse.reshape(batch * seq, LANES)


def _mem_kv_kernel(x_ref, g_ref, w_ref, kg_ref, k_ref, v_ref):
    h = _rms(x_ref[...], g_ref[...]).astype(BF16)
    zk = _dot(h, w_ref[:, :MEM_WIDTH])
    for j in range(MEM_HEADS):
        sl = slice(j * MEM_HEAD_DIM, (j + 1) * MEM_HEAD_DIM)
        k_ref[:, sl] = _rms(zk[:, sl], kg_ref[...])
    v_ref[...] = _dot(h, w_ref[:, MEM_WIDTH:])


def _mem_kv(mem, g, w_bf16, k_g):
    m, d = mem.shape
    tm = min(ROW_TILE, m)
    row = lambda i: (i, 0)
    shp = jax.ShapeDtypeStruct((m, MEM_WIDTH), F32)
    return pl.pallas_call(
        _mem_kv_kernel,
        grid=(m // tm,),
        in_specs=[pl.BlockSpec((tm, d), row), _const_spec((1, d)), _const_spec((d, 2 * MEM_WIDTH)),
                  _const_spec((1, MEM_HEAD_DIM))],
        out_specs=(pl.BlockSpec((tm, MEM_WIDTH), row), pl.BlockSpec((tm, MEM_WIDTH), row)),
        out_shape=(shp, shp),
        compiler_params=_params("parallel"),
        name="mem_kv",
    )(mem, g, w_bf16, k_g)


def _mem_attn_kernel(q_ref, k_ref, v_ref, o_ref):
    scale = MEM_HEAD_DIM ** -0.5
    for h in range(MEM_HEADS):
        sl = slice(h * MEM_HEAD_DIM, (h + 1) * MEM_HEAD_DIM)
        s = _dot_nt(q_ref[:, sl], k_ref[0, :, sl].astype(BF16)) * scale
        p = jnp.exp(s - jnp.max(s, axis=-1, keepdims=True))
        l = jnp.sum(p, axis=-1, keepdims=True)
        o_ref[:, sl] = (_dot(p.astype(BF16), v_ref[0, :, sl].astype(BF16)) * (1.0 / l)).astype(BF16)


def _mem_attention(mq, mem_k, mem_v, batch, seq):
    n_mem = mem_k.shape[0] // batch
    tq = min(512, seq)
    nq = seq // tq
    kspec = pl.BlockSpec((1, n_mem, MEM_WIDTH), lambda b, i: (b, 0, 0))
    return pl.pallas_call(
        _mem_attn_kernel,
        grid=(batch, nq),
        in_specs=[pl.BlockSpec((tq, MEM_WIDTH), lambda b, i: (b * nq + i, 0)), kspec, kspec],
        out_specs=pl.BlockSpec((tq, MEM_WIDTH), lambda b, i: (b * nq + i, 0)),
        out_shape=jax.ShapeDtypeStruct((batch * seq, MEM_WIDTH), BF16),
        compiler_params=_params("parallel", "parallel"),
        name="mem_attn",
    )(mq, mem_k.reshape(batch, n_mem, MEM_WIDTH), mem_v.reshape(batch, n_mem, MEM_WIDTH))


CONV_HALO = 32
CONV_CHUNK = 32


def _ln_swish(c, g, b):
    mu = jnp.mean(c, axis=-1, keepdims=True)
    xc = c - mu
    y = xc * lax.rsqrt(jnp.mean(xc * xc, axis=-1, keepdims=True) + EPS) * g + b
    return y * jax.nn.sigmoid(y)


def _conv_prompt_kernel(halo_ref, u_ref, w_ref, b_ref, g_ref, beta_ref, o_ref, ext_ref):
    tc = u_ref.shape[0]
    first = pl.program_id(1) == 0
    ext_ref[0:CONV_HALO, :] = jnp.where(first, 0.0, halo_ref[...])
    ext_ref[CONV_HALO:, :] = u_ref[...]
    lead = CONV_HALO - (CONV_WIDTH - 1)
    for c0 in range(0, tc, CONV_CHUNK):
        acc = jnp.zeros((CONV_CHUNK, u_ref.shape[1]), F32) + b_ref[...]
        for w in range(CONV_WIDTH):
            acc = acc + w_ref[w:w + 1, :] * ext_ref[c0 + lead + w:c0 + lead + w + CONV_CHUNK, :]
        o_ref[c0:c0 + CONV_CHUNK, :] = _ln_swish(acc, g_ref[...], beta_ref[...]).astype(BF16)


def _conv_prompt(u, conv_w, conv_b, ln_g, ln_b, batch, seq):
    ch = u.shape[1]
    tc = min(ROW_TILE, seq)
    nt = seq // tc
    ratio = tc // CONV_HALO
    return pl.pallas_call(
        _conv_prompt_kernel,
        grid=(batch, nt),
        in_specs=[
            pl.BlockSpec((CONV_HALO, ch), lambda b, i: (jnp.maximum((b * nt + i) * ratio - 1, 0), 0)),
            pl.BlockSpec((tc, ch), lambda b, i: (b * nt + i, 0)),
            _const_spec((CONV_WIDTH, ch)), _const_spec((1, ch)), _const_spec((1, ch)), _const_spec((1, ch)),
        ],
        out_specs=pl.BlockSpec((tc, ch), lambda b, i: (b * nt + i, 0)),
        out_shape=jax.ShapeDtypeStruct((batch * seq, ch), BF16),
        scratch_shapes=[pltpu.VMEM((CONV_HALO + tc, ch), F32)],
        compiler_params=_params("parallel", "parallel"),
        name="conv_prompt",
    )(u, u, conv_w, conv_b, ln_g, ln_b)


def _conv_sample_kernel(state_ref, new_ref, w_ref, b_ref, g_ref, beta_ref, o_ref, state_out_ref, ext_ref):
    nb, ctx, ch = state_ref.shape
    t = new_ref.shape[1]
    ext_ref[:, 0:ctx, :] = state_ref[...]
    ext_ref[:, ctx:ctx + t, :] = new_ref[...]
    acc = jnp.zeros((nb, t, ch), F32) + b_ref[...]
    for w in range(CONV_WIDTH):
        acc = acc + w_ref[w:w + 1, :] * ext_ref[:, w:w + t, :]
    o_ref[...] = _ln_swish(acc, g_ref[...], beta_ref[...])
    state_out_ref[...] = ext_ref[:, t:t + ctx, :]


def _conv_sample(state, u_new, conv_w, conv_b, ln_g, ln_b):
    batch, ctx, ch = state.shape
    t = u_new.shape[1]
    nb = 8
    blk = lambda n: pl.BlockSpec((nb, n, ch), lambda i: (i, 0, 0))
    return pl.pallas_call(
        _conv_sample_kernel,
        grid=(batch // nb,),
        in_specs=[blk(ctx), blk(t), _const_spec((CONV_WIDTH, ch)), _const_spec((1, ch)), _const_spec((1, ch)),
                  _const_spec((1, ch))],
        out_specs=(blk(t), blk(ctx)),
        out_shape=(jax.ShapeDtypeStruct((batch, t, ch), F32), jax.ShapeDtypeStruct((batch, ctx, ch), F32)),
        scratch_shapes=[pltpu.VMEM((nb, ctx + t + 6, ch), F32)],
        compiler_params=_params("parallel"),
        name="conv_sample",
    )(state, u_new, conv_w, conv_b, ln_g, ln_b)


GROUP_LANES = 16


def _spread_groups(vec, combine):
    t = combine(combine(vec, pltpu.roll(vec, LANES - GROUP_LANES, 1)), pltpu.roll(vec, LANES - 2 * GROUP_LANES, 1))
    lane = lax.broadcasted_iota(jnp.int32, vec.shape, 1)
    return jnp.where(lane < GROUP_LANES, t,
                     jnp.where(lane < 2 * GROUP_LANES, pltpu.roll(t, GROUP_LANES, 1), pltpu.roll(t, 2 * GROUP_LANES, 1)))


def _sample_attn_kernel(kc_ref, vc_ref, kn_ref, vn_ref, wq_ref, bias_ref, biasn_ref, mk_ref, mv_ref, wm_ref,
                        biasm_ref, kw_ref, vw_ref, a_ref, m_ref, kx_ref, vx_ref):
    cache_len = kc_ref.shape[1]
    t = kn_ref.shape[1]
    n_used = N_GROUPS * GROUP_LANES
    lane = lax.broadcasted_iota(jnp.int32, (1, LANES), 1)

    kw_ref[0, 0:cache_len - t, :] = kc_ref[0, t:cache_len, :]
    kw_ref[0, cache_len - t:cache_len, :] = kn_ref[0]
    vw_ref[0, 0:cache_len - t, :] = vc_ref[0, t:cache_len, :]
    vw_ref[0, cache_len - t:cache_len, :] = vn_ref[0]

    kx_ref[...] = jnp.zeros(kx_ref.shape, F32)
    vx_ref[...] = jnp.zeros(vx_ref.shape, F32)
    kx_ref[0:t, :] = kn_ref[0]
    vx_ref[0:t, :] = vn_ref[0]

    scale = HEAD_DIM ** -0.5
    wq = wq_ref[0]
    s = _dot(kc_ref[0].astype(BF16), wq) * scale + bias_ref[...]
    sn = _dot(kx_ref[...].astype(BF16), wq) * scale + biasn_ref[...]
    m_col = jnp.maximum(jnp.max(s, axis=0, keepdims=True), jnp.max(sn, axis=0, keepdims=True))
    m_joint = jnp.where(lane < n_used, _spread_groups(m_col, jnp.maximum), 0.0)
    p = jnp.exp(s - m_joint)
    pn = jnp.exp(sn - m_joint)
    l_col = jnp.sum(p, axis=0, keepdims=True) + jnp.sum(pn, axis=0, keepdims=True)
    l_joint = jnp.where(lane < n_used, _spread_groups(l_col, jnp.add), 1.0)
    inv = 1.0 / l_joint
    o = (_dot_tn((p * inv).astype(BF16), vc_ref[0].astype(BF16))
         + _dot_tn((pn * inv).astype(BF16), vx_ref[...].astype(BF16)))
    for h in range(N_KV_HEADS):
        sl = slice(h * HEAD_DIM, (h + 1) * HEAD_DIM)
        r = h * t
        a_ref[0, :, sl] = (o[r:r + t, sl] + o[GROUP_LANES + r:GROUP_LANES + r + t, sl]
                           + o[2 * GROUP_LANES + r:2 * GROUP_LANES + r + t, sl])

    sm = _dot(mk_ref[0].astype(BF16), wm_ref[0]) * (MEM_HEAD_DIM ** -0.5) + biasm_ref[...]
    pm = jnp.exp(sm - jnp.where(lane < MEM_HEADS * t, jnp.max(sm, axis=0, keepdims=True), 0.0))
    lm = jnp.where(lane < MEM_HEADS * t, jnp.sum(pm, axis=0, keepdims=True), 1.0)
    om = _dot_tn((pm * (1.0 / lm)).astype(BF16), mv_ref[0].astype(BF16))
    for h in range(MEM_HEADS):
        sl = slice(h * MEM_HEAD_DIM, (h + 1) * MEM_HEAD_DIM)
        m_ref[0, :, sl] = om[h * t:(h + 1) * t, sl]


def _sample_masks(cache_len, t):
    col = np.arange(LANES)
    g, tok = col // GROUP_LANES, (col % GROUP_LANES) % t
    used = (col < N_GROUPS * GROUP_LANES) & (col % GROUP_LANES < N_KV_HEADS * t)
    dil = np.array([d for _, d in DILATED_GROUPS] + [1] * (LANES // GROUP_LANES - N_GROUPS))[g]
    win = np.array([w for w, _ in DILATED_GROUPS] + [0] * (LANES // GROUP_LANES - N_GROUPS))[g]
    row = np.arange(cache_len)[:, None]
    dist = cache_len + tok[None, :] - row
    keep = used[None, :] & (dist % dil[None, :] == 0) & (dist <= win[None, :])
    rown = np.arange(LANES)[:, None]
    distn = tok[None, :] - rown
    keepn = used[None, :] & (rown < t) & (distn >= 0) & (distn % dil[None, :] == 0) & (distn <= win[None, :])
    to_bias = lambda k: jnp.asarray(np.where(k, 0.0, NEG), F32)
    return to_bias(keep), to_bias(keepn)


def _block_diag_queries(q, batch, t, n_groups):
    qt = q.reshape(batch, t, n_groups, N_KV_HEADS, HEAD_DIM).transpose(0, 3, 4, 2, 1)
    eye = jnp.eye(N_KV_HEADS, dtype=jnp.bool_)[None, :, None, None, :, None]
    wq = jnp.where(eye, qt[:, :, :, :, None, :], jnp.zeros((), q.dtype))
    wq = wq.reshape(batch, N_KV_HEADS * HEAD_DIM, n_groups, N_KV_HEADS * t)
    wq = jnp.pad(wq, ((0, 0), (0, 0), (0, 0), (0, GROUP_LANES - N_KV_HEADS * t)))
    wq = wq.reshape(batch, N_KV_HEADS * HEAD_DIM, n_groups * GROUP_LANES)
    return jnp.pad(wq, ((0, 0), (0, 0), (0, LANES - n_groups * GROUP_LANES)))


def _sample_attention(q, k_new, v_new, mq, cache_k, cache_v, mem_k, mem_v):
    batch, cache_len = cache_k.shape[0], cache_k.shape[1]
    t = k_new.shape[0] // batch
    n_mem = mem_k.shape[1]
    assert N_KV_HEADS * t <= GROUP_LANES and cache_len >= max(w for w, _ in DILATED_GROUPS)
    wq = _block_diag_queries(q, batch, t, N_GROUPS)
    wm = _block_diag_queries(mq, batch, t, 1)
    bias, biasn = _sample_masks(cache_len, t)
    biasm = jnp.asarray(np.where(np.arange(LANES) < MEM_HEADS * t, 0.0, NEG)[None, :], F32)
    per_b = lambda *shape: pl.BlockSpec((1,) + shape, lambda b: (b,) + (0,) * len(shape))
    kc = cache_k.reshape(batch, cache_len, ATTN_WIDTH)
    vc = cache_v.reshape(batch, cache_len, ATTN_WIDTH)
    kn = k_new.reshape(batch, t, ATTN_WIDTH)
    vn = v_new.reshape(batch, t, ATTN_WIDTH)
    win = jax.ShapeDtypeStruct((batch, cache_len, ATTN_WIDTH), F32)
    return pl.pallas_call(
        _sample_attn_kernel,
        grid=(batch,),
        in_specs=[
            per_b(cache_len, ATTN_WIDTH), per_b(cache_len, ATTN_WIDTH), per_b(t, ATTN_WIDTH), per_b(t, ATTN_WIDTH),
            per_b(ATTN_WIDTH, LANES), _const_spec((cache_len, LANES)), _const_spec((LANES, LANES)),
            per_b(n_mem, MEM_WIDTH), per_b(n_mem, MEM_WIDTH), per_b(MEM_WIDTH, LANES), _const_spec((1, LANES)),
        ],
        out_specs=(per_b(cache_len, ATTN_WIDTH), per_b(cache_len, ATTN_WIDTH), per_b(t, ATTN_WIDTH),
                   per_b(t, MEM_WIDTH)),
        out_shape=(win, win, jax.ShapeDtypeStruct((batch, t, ATTN_WIDTH), F32),
                   jax.ShapeDtypeStruct((batch, t, MEM_WIDTH), F32)),
        scratch_shapes=[pltpu.VMEM((LANES, ATTN_WIDTH), F32), pltpu.VMEM((LANES, ATTN_WIDTH), F32)],
        compiler_params=_params("parallel"),
        name="sample_attn",
    )(kc, vc, kn, vn, wq, bias, biasn, mem_k.reshape(batch, n_mem, MEM_WIDTH),
      mem_v.reshape(batch, n_mem, MEM_WIDTH), wm, biasm)


ROUTE_GROUP_LANE0 = 0
ROUTE_EXPERT_LANE0 = N_EXPERT_GROUPS


def _route(logits):
    lane = lax.broadcasted_iota(jnp.int32, logits.shape, 1).astype(F32)
    big = float(LANES)

    def masked_softmax(keep):
        z = jnp.where(keep, logits, NEG)
        e = jnp.where(keep, jnp.exp(z - jnp.max(z, axis=-1, keepdims=True)), 0.0)
        return e / jnp.sum(e, axis=-1, keepdims=True)

    def first_argmax(vals, keep):
        top = jnp.max(jnp.where(keep, vals, -1.0), axis=-1, keepdims=True)
        idx = jnp.min(jnp.where(jnp.logical_and(keep, vals == top), lane, big), axis=-1, keepdims=True)
        return top, idx

    is_group = lane < N_EXPERT_GROUPS
    pg = masked_softmax(is_group)
    pg_top, g_idx = first_argmax(pg, is_group)
    lo = ROUTE_EXPERT_LANE0 + g_idx * EXPERTS_PER_GROUP
    in_group = jnp.logical_and(lane >= lo, lane < lo + EXPERTS_PER_GROUP)
    pe = masked_softmax(in_group)
    p1, i1 = first_argmax(pe, in_group)
    p2, i2 = first_argmax(pe, jnp.logical_and(in_group, lane != i1))
    denom = p1 + p2
    w1 = pg_top * p1 / denom
    w2 = pg_top * p2 / denom
    e1 = i1 - ROUTE_EXPERT_LANE0
    e2 = i2 - ROUTE_EXPERT_LANE0
    return jnp.where(lane == 0, e1, jnp.where(lane == 1, e2, jnp.where(lane == 2, w1, jnp.where(lane == 3, w2, 0.0))))


def _merge_kernel(*refs, n_attn):
    x_ref, gmix_ref = refs[0], refs[1]
    o_refs = refs[2:2 + n_attn]
    lse_refs = refs[2 + n_attn:2 + 2 * n_attn] if n_attn > 1 else ()
    rest = refs[2 + n_attn + len(lse_refs):]
    (cc_ref, mo_ref, wgate_ref, wa_ref, wc_ref, wm_ref, wo_ref, gffn_ref, wr_ref, br_ref,
     x1_ref, h2_ref, route_ref) = rest
    d = x_ref.shape[1]
    x = x_ref[...]
    h = _rms(x, gmix_ref[...]).astype(BF16)

    if n_attn > 1:
        lses = [r[...] for r in lse_refs]
        mx = functools.reduce(jnp.maximum, lses)
        es = [jnp.exp(l - mx) for l in lses]
        inv = 1.0 / functools.reduce(jnp.add, es)
        for hd in range(N_KV_HEADS):
            sl = slice(hd * HEAD_DIM, (hd + 1) * HEAD_DIM)
            comb = sum((es[g][:, hd:hd + 1] * inv[:, hd:hd + 1]) * o_refs[g][:, sl] for g in range(n_attn))
            part = _dot(comb.astype(BF16), wa_ref[sl, :])
            a = part if hd == 0 else a + part
    else:
        a = _dot(o_refs[0][...].astype(BF16), wa_ref[...])
    c = _dot(cc_ref[...].astype(BF16), wc_ref[...])
    m = _dot(mo_ref[...].astype(BF16), wm_ref[...])
    z = jax.nn.sigmoid(_dot(h, wgate_ref[:, 0:d])) * a
    z = z + jax.nn.sigmoid(_dot(h, wgate_ref[:, d:2 * d])) * c
    z = z + jax.nn.sigmoid(_dot(h, wgate_ref[:, 2 * d:3 * d])) * m
    x1 = x + _dot(z.astype(BF16), wo_ref[...])
    x1_ref[...] = x1
    h2 = _rms(x1, gffn_ref[...])
    h2_ref[...] = h2
    logits = jnp.dot(h2, wr_ref[...], precision=lax.Precision.HIGHEST, preferred_element_type=F32) + br_ref[...]
    route_ref[...] = _route(logits)


def _merge(x, g_mix, attn_outs, attn_lses, cc, mo, w_gate, w_a, w_c, w_m, w_o, g_ffn, w_r, b_r):
    m, d = x.shape
    tm = min(ROW_TILE, m)
    row = lambda i: (i, 0)
    rows = lambda arr: pl.BlockSpec((tm, arr.shape[1]), row)
    n_attn = len(attn_outs)
    ins = [x, g_mix, *attn_outs, *attn_lses, cc, mo, w_gate, w_a, w_c, w_m, w_o, g_ffn, w_r, b_r]
    specs = ([rows(x), _const_spec(g_mix.shape)] + [rows(a) for a in attn_outs] + [rows(a) for a in attn_lses]
             + [rows(cc), rows(mo)] + [_const_spec(a.shape) for a in (w_gate, w_a, w_c, w_m, w_o, g_ffn, w_r, b_r)])
    out_shape = (jax.ShapeDtypeStruct((m, d), F32), jax.ShapeDtypeStruct((m, d), F32),
                 jax.ShapeDtypeStruct((m, LANES), F32))
    return pl.pallas_call(
        functools.partial(_merge_kernel, n_attn=n_attn),
        grid=(m // tm,),
        in_specs=specs,
        out_specs=tuple(pl.BlockSpec((tm, s.shape[1]), row) for s in out_shape),
        out_shape=out_shape,
        compiler_params=_params("parallel"),
        name="merge",
    )(*ins)


def _routing_tables(expert_ids, n_tiles):
    n = expert_ids.shape[0]
    flat = expert_ids.reshape(-1)
    order = jnp.argsort(flat, stable=True).astype(jnp.int32)
    sorted_e = flat[order]
    counts = jnp.zeros((N_EXPERTS,), jnp.int32).at[flat].add(1)
    tiles = (counts + MOE_TILE - 1) // MOE_TILE
    tile_end = jnp.cumsum(tiles)
    tile_start = tile_end - tiles
    start = jnp.cumsum(counts) - counts
    slot = tile_start[sorted_e] * MOE_TILE + (jnp.arange(TOP_K * n, dtype=jnp.int32) - start[sorted_e])
    src = jnp.zeros((n_tiles * MOE_TILE,), jnp.int32).at[slot].set(order // TOP_K)
    pos = jnp.zeros((TOP_K * n,), jnp.int32).at[order].set(slot)
    n_used = tile_end[-1]
    tile_ids = jnp.minimum(jnp.arange(n_tiles, dtype=jnp.int32), n_used - 1)
    tile_expert = jnp.searchsorted(tile_end, tile_ids, side="right").astype(jnp.int32)
    return src, pos, tile_expert, n_used.reshape(1).astype(jnp.int32)


def _gather_kernel(src_ref, nused_ref, h_hbm, o_ref, buf_ref, sem):
    t = pl.program_id(0)

    @pl.when(t < nused_ref[0])
    def _():
        base = t * MOE_TILE

        def issue(j, carry):
            tok = src_ref[base + j]
            pltpu.make_async_copy(h_hbm.at[pl.ds(tok, 1)], buf_ref.at[pl.ds(j, 1)], sem).start()
            return carry

        lax.fori_loop(0, MOE_TILE, issue, 0, unroll=8)
        pltpu.make_async_copy(h_hbm.at[pl.ds(0, MOE_TILE)], buf_ref, sem).wait()
        o_ref[...] = buf_ref[...].astype(BF16)

    @pl.when(t >= nused_ref[0])
    def _():
        o_ref[...] = jnp.zeros(o_ref.shape, BF16)


def _gather_rows(h2, src, n_used, n_tiles):
    d = h2.shape[1]
    return pl.pallas_call(
        _gather_kernel,
        grid_spec=pltpu.PrefetchScalarGridSpec(
            num_scalar_prefetch=2,
            grid=(n_tiles,),
            in_specs=[pl.BlockSpec(memory_space=pl.ANY)],
            out_specs=pl.BlockSpec((MOE_TILE, d), lambda t, *_: (t, 0)),
            scratch_shapes=[pltpu.VMEM((MOE_TILE, d), F32), pltpu.SemaphoreType.DMA(())],
        ),
        out_shape=jax.ShapeDtypeStruct((n_tiles * MOE_TILE, d), BF16),
        compiler_params=_params("arbitrary"),
        name="moe_gather",
    )(src, n_used, h2)


def _gmm_kernel(te_ref, nused_ref, x_ref, wg_ref, wu_ref, wd_ref, o_ref, wg_s, wu_s, wd_s):
    t = pl.program_id(0)
    used = t < nused_ref[0]
    new_expert = jnp.logical_or(t == 0, te_ref[t] != te_ref[jnp.maximum(t - 1, 0)])

    @pl.when(jnp.logical_and(used, new_expert))
    def _():
        wg_s[...] = wg_ref[0].astype(BF16)
        wu_s[...] = wu_ref[0].astype(BF16)
        wd_s[...] = wd_ref[0].astype(BF16)

    @pl.when(used)
    def _():
        x = x_ref[...]
        gate = _dot(x, wg_s[...])
        up = _dot(x, wu_s[...])
        hid = gate * jax.nn.sigmoid(gate) * up
        o_ref[...] = _dot(hid.astype(BF16), wd_s[...])

    @pl.when(jnp.logical_not(used))
    def _():
        o_ref[...] = jnp.zeros(o_ref.shape, F32)


def _grouped_mlp(xs, tile_expert, n_used, w_gate, w_up, w_down):
    n_tiles = xs.shape[0] // MOE_TILE
    d, ff = w_gate.shape[1], w_gate.shape[2]
    wspec = lambda a, b: pl.BlockSpec((1, a, b), lambda t, te, nu: (te[t], 0, 0))
    return pl.pallas_call(
        _gmm_kernel,
        grid_spec=pltpu.PrefetchScalarGridSpec(
            num_scalar_prefetch=2,
            grid=(n_tiles,),
            in_specs=[pl.BlockSpec((MOE_TILE, d), lambda t, *_: (t, 0)), wspec(d, ff), wspec(d, ff), wspec(ff, d)],
            out_specs=pl.BlockSpec((MOE_TILE, d), lambda t, *_: (t, 0)),
            scratch_shapes=[pltpu.VMEM((d, ff), BF16), pltpu.VMEM((d, ff), BF16), pltpu.VMEM((ff, d), BF16)],
        ),
        out_shape=jax.ShapeDtypeStruct((n_tiles * MOE_TILE, d), F32),
        compiler_params=_params("arbitrary"),
        name="moe_gmm",
    )(tile_expert, n_used, xs, w_gate, w_up, w_down)


def _combine_kernel(pos_ref, x1_ref, route_ref, ys_hbm, y_ref, buf_ref, sem, *, row0):
    tm = x1_ref.shape[0]
    base = (row0 + pl.program_id(0) * tm) * TOP_K

    def issue(j, carry):
        for k in range(TOP_K):
            slot = pos_ref[base + j * TOP_K + k]
            pltpu.make_async_copy(ys_hbm.at[pl.ds(slot, 1)], buf_ref.at[k, pl.ds(j, 1)], sem).start()
        return carry

    lax.fori_loop(0, tm, issue, 0, unroll=4)
    for k in range(TOP_K):
        pltpu.make_async_copy(ys_hbm.at[pl.ds(0, tm)], buf_ref.at[k], sem).wait()
    route = route_ref[...]
    y_ref[...] = x1_ref[...] + route[:, 2:3] * buf_ref[0] + route[:, 3:4] * buf_ref[1]


def _combine(x1, route, pos, ys, row0):
    m, d = x1.shape
    tm = min(ROW_TILE, m)
    row = lambda i, *_: (i, 0)
    return pl.pallas_call(
        functools.partial(_combine_kernel, row0=row0),
        grid_spec=pltpu.PrefetchScalarGridSpec(
            num_scalar_prefetch=1,
            grid=(m // tm,),
            in_specs=[pl.BlockSpec((tm, d), row), pl.BlockSpec((tm, LANES), row), pl.BlockSpec(memory_space=pl.ANY)],
            out_specs=pl.BlockSpec((tm, d), row),
            scratch_shapes=[pltpu.VMEM((TOP_K, tm, d), F32), pltpu.SemaphoreType.DMA(())],
        ),
        out_shape=jax.ShapeDtypeStruct((m, d), F32),
        compiler_params=_params("arbitrary"),
        name="moe_combine",
    )(pos, x1, route, ys)


def _layer(layer, x_prompt, x_sample, mem_prompt, cache_k, cache_v, state_conv, cache_mem_k, cache_mem_v, p):
    batch, seq, d = x_prompt.shape
    dec_batch, dec_seq, _ = x_sample.shape
    conv_ch = p["conv_w"].shape[-1]
    n_in = N_Q_HEADS * HEAD_DIM + 2 * ATTN_WIDTH + 2 * conv_ch + MEM_WIDTH
    past_len = cache_k.shape[2]

    row2 = lambda name: p[name][layer][None, :]
    w_in = p["w_in"][layer]
    w_main = w_in[:, :n_in].astype(BF16)
    w_gate = w_in[:, n_in:].astype(BF16)
    w_a, w_c, w_m, w_o = (p[n][layer].astype(BF16) for n in ("w_attn_proj", "w_conv_proj", "w_mem_proj", "w_out"))
    w_router = jnp.concatenate(
        [p["w_router_group"][layer], p["w_router_expert"][layer].transpose(1, 0, 2).reshape(d, N_EXPERTS)], axis=1)
    w_router = jnp.pad(w_router, ((0, 0), (0, LANES - w_router.shape[1])))
    b_router = jnp.concatenate([p["b_router_group"][layer], p["b_router_expert"][layer].reshape(-1)])
    b_router = jnp.pad(b_router, (0, LANES - b_router.shape[0]))[None, :]
    conv_args = (p["conv_w"][layer], row2("conv_b"), row2("conv_ln_g"), row2("conv_ln_b"))
    merge_w = (w_gate, w_a, w_c, w_m, w_o, row2("norm_ffn_g"), w_router, b_router)

    xp = x_prompt.reshape(batch * seq, d)
    tabs_p = _rope_tables(jnp.arange(seq, dtype=jnp.int32))
    q_p, k_p, v_p, u_p, mq_p = _in_proj(xp, row2("norm_mix_g"), w_main, row2("q_norm_g"), row2("k_norm_g"),
                                        row2("mq_norm_g"), tabs_p, conv_ch)
    attn = [_band_attention(q_p, k_p, v_p, batch, seq, g) for g in range(N_GROUPS)]
    mem_k_p, mem_v_p = _mem_kv(mem_prompt.reshape(-1, d), row2("mem_norm_g"), p["w_mem_kv"][layer].astype(BF16),
                               row2("mk_norm_g"))
    mo_p = _mem_attention(mq_p, mem_k_p, mem_v_p, batch, seq)
    cc_p = _conv_prompt(u_p, *conv_args, batch, seq)
    x1_p, h2_p, route_p = _merge(xp, row2("norm_mix_g"), [a[0] for a in attn], [a[1] for a in attn], cc_p, mo_p,
                                 *merge_w)

    xs = x_sample.reshape(dec_batch * dec_seq, d)
    tabs_s = _rope_tables(jnp.tile(past_len + jnp.arange(dec_seq, dtype=jnp.int32), dec_batch))
    q_s, k_s, v_s, u_s, mq_s = _in_proj(xs, row2("norm_mix_g"), w_main, row2("q_norm_g"), row2("k_norm_g"),
                                        row2("mq_norm_g"), tabs_s, conv_ch)
    k_win_s, v_win_s, a_s, mo_s = _sample_attention(q_s, k_s, v_s, mq_s, cache_k[layer], cache_v[layer],
                                                    cache_mem_k[layer], cache_mem_v[layer])
    cc_s, conv_state_s = _conv_sample(state_conv[layer], u_s.reshape(dec_batch, dec_seq, conv_ch), *conv_args)
    x1_s, h2_s, route_s = _merge(xs, row2("norm_mix_g"), [a_s.reshape(-1, ATTN_WIDTH)], [],
                                 cc_s.reshape(-1, conv_ch), mo_s.reshape(-1, MEM_WIDTH), *merge_w)

    n_p, n_s = xp.shape[0], xs.shape[0]
    h2 = jnp.concatenate([h2_p, h2_s], axis=0)
    expert_ids = jnp.concatenate([route_p[:, :TOP_K], route_s[:, :TOP_K]], axis=0).astype(jnp.int32)
    n_tok = n_p + n_s
    n_tiles = (TOP_K * n_tok + N_EXPERTS * (MOE_TILE - 1)) // MOE_TILE + 1
    src, pos, tile_expert, n_used = _routing_tables(expert_ids, n_tiles)
    gathered = _gather_rows(h2, src, n_used, n_tiles)
    ys = _grouped_mlp(gathered, tile_expert, n_used, p["w_expert_gate"][layer], p["w_expert_up"][layer],
                      p["w_expert_down"][layer])
    y_p = _combine(x1_p, route_p, pos, ys, 0)
    y_s = _combine(x1_s, route_s, pos, ys, n_p)

    state_p = (k_p.reshape(batch, seq, N_KV_HEADS, HEAD_DIM), v_p.reshape(batch, seq, N_KV_HEADS, HEAD_DIM),
               u_p.reshape(batch, seq, conv_ch)[:, seq - (CONV_WIDTH - 1):],
               mem_k_p.reshape(batch, -1, MEM_HEADS, MEM_HEAD_DIM), mem_v_p.reshape(batch, -1, MEM_HEADS, MEM_HEAD_DIM))
    state_s = (k_win_s.reshape(dec_batch, past_len, N_KV_HEADS, HEAD_DIM),
               v_win_s.reshape(dec_batch, past_len, N_KV_HEADS, HEAD_DIM), conv_state_s)
    return y_p.reshape(batch, seq, d), y_s.reshape(dec_batch, dec_seq, d), state_p, state_s


def kernel(x_prompt, x_sample, mem_prompt, cache_k, cache_v, state_conv, cache_mem_k, cache_mem_v, norm_mix_g, w_in, q_norm_g, k_norm_g, conv_w, conv_b, conv_ln_g, conv_ln_b, mem_norm_g, w_mem_kv, mq_norm_g, mk_norm_g, w_attn_proj, w_conv_proj, w_mem_proj, w_out, norm_ffn_g, w_router_group, b_router_group, w_router_expert, b_router_expert, w_expert_gate, w_expert_up, w_expert_down):
    p = dict(norm_mix_g=norm_mix_g, w_in=w_in, q_norm_g=q_norm_g, k_norm_g=k_norm_g, conv_w=conv_w, conv_b=conv_b,
             conv_ln_g=conv_ln_g, conv_ln_b=conv_ln_b, mem_norm_g=mem_norm_g, w_mem_kv=w_mem_kv, mq_norm_g=mq_norm_g,
             mk_norm_g=mk_norm_g, w_attn_proj=w_attn_proj, w_conv_proj=w_conv_proj, w_mem_proj=w_mem_proj,
             w_out=w_out, norm_ffn_g=norm_ffn_g, w_router_group=w_router_group, b_router_group=b_router_group,
             w_router_expert=w_router_expert, b_router_expert=b_router_expert, w_expert_gate=w_expert_gate,
             w_expert_up=w_expert_up, w_expert_down=w_expert_down)
    depth = w_in.shape[0]
    seq = x_prompt.shape[1]
    assert seq <= max(w for w, _ in DILATED_GROUPS)
    y_p, y_s = x_prompt, x_sample
    states_p, states_s = [], []
    for layer in range(depth):
        y_p, y_s, st_p, st_s = _layer(layer, y_p, y_s, mem_prompt, cache_k, cache_v, state_conv, cache_mem_k,
                                      cache_mem_v, p)
        states_p.append(st_p)
        states_s.append(st_s)
    stack = lambda states, i: jnp.stack([s[i] for s in states], axis=0)
    return (y_p, y_s, stack(states_p, 0), stack(states_p, 1), stack(states_p, 2), stack(states_p, 3),
            stack(states_p, 4), stack(states_s, 0), stack(states_s, 1), stack(states_s, 2))
```

```python
import functools

import jax
import jax.numpy as jnp
import numpy as np
from jax import lax
from jax.experimental import pallas as pl
from jax.experimental.pallas import tpu as pltpu

HEAD_DIM = 128
N_KV_HEADS = 4
DILATED_GROUPS = ((128, 1), (512, 4), (2048, 16))
N_GROUPS = len(DILATED_GROUPS)
N_Q_HEADS = N_GROUPS * N_KV_HEADS
ATTN_WIDTH = N_KV_HEADS * HEAD_DIM
BAND = 128
ROPE_THETA = 500000.0
ROT_DIM = HEAD_DIM // 4
CONV_WIDTH = 31
MEM_HEADS = 4
MEM_HEAD_DIM = 128
MEM_WIDTH = MEM_HEADS * MEM_HEAD_DIM
N_EXPERT_GROUPS = 4
EXPERTS_PER_GROUP = 8
N_EXPERTS = N_EXPERT_GROUPS * EXPERTS_PER_GROUP
TOP_K = 2
EPS = 1e-6
NEG = -1e30

LANES = 128
ROW_TILE = 256
MOE_TILE = 256
VMEM_LIMIT = 56 * 1024 * 1024

BF16 = jnp.bfloat16
F32 = jnp.float32


def _params(*sem):
    return pltpu.CompilerParams(dimension_semantics=sem, vmem_limit_bytes=VMEM_LIMIT)


def _dot(a, b):
    return jnp.dot(a, b, preferred_element_type=F32)


def _dot_nt(a, b):
    return lax.dot_general(a, b, (((1,), (1,)), ((), ())), preferred_element_type=F32)


def _dot_tn(a, b):
    return lax.dot_general(a, b, (((0,), (0,)), ((), ())), preferred_element_type=F32)


def _rms(x, g):
    return x * lax.rsqrt(jnp.mean(x * x, axis=-1, keepdims=True) + EPS) * g


def _const_spec(shape):
    nd = len(shape)
    return pl.BlockSpec(shape, lambda *_: (0,) * nd)


def _in_proj_kernel(x_ref, g_ref, w_ref, qg_ref, kg_ref, mqg_ref, rc_ref, ra_ref, rb_ref,
                    q_ref, k_ref, v_ref, u_ref, mq_ref):
    h = _rms(x_ref[...], g_ref[...]).astype(BF16)
    rc, ra, rb = rc_ref[...], ra_ref[...], rb_ref[...]

    def rope(y):
        return y * rc + pltpu.roll(y, LANES - ROT_DIM // 2, 1) * ra + pltpu.roll(y, ROT_DIM // 2, 1) * rb

    col = 0
    zq = _dot(h, w_ref[:, col:col + N_Q_HEADS * HEAD_DIM])
    for j in range(N_Q_HEADS):
        sl = slice(j * HEAD_DIM, (j + 1) * HEAD_DIM)
        q_ref[:, sl] = rope(_rms(zq[:, sl], qg_ref[...])).astype(BF16)
    col += N_Q_HEADS * HEAD_DIM
    tm = x_ref.shape[0]
    zk = _dot(h, w_ref[:, col:col + ATTN_WIDTH])
    for j in range(N_KV_HEADS):
        sl = slice(j * HEAD_DIM, (j + 1) * HEAD_DIM)
        k_ref[pl.ds(j, tm, stride=N_KV_HEADS), :] = rope(_rms(zk[:, sl], kg_ref[...]))
    col += ATTN_WIDTH
    zv = _dot(h, w_ref[:, col:col + ATTN_WIDTH])
    for j in range(N_KV_HEADS):
        v_ref[pl.ds(j, tm, stride=N_KV_HEADS), :] = zv[:, j * HEAD_DIM:(j + 1) * HEAD_DIM]
    col += ATTN_WIDTH
    conv_ch = u_ref.shape[-1]
    za = _dot(h, w_ref[:, col:col + conv_ch])
    zb = _dot(h, w_ref[:, col + conv_ch:col + 2 * conv_ch])
    u_ref[...] = za * jax.nn.sigmoid(zb)
    col += 2 * conv_ch
    zm = _dot(h, w_ref[:, col:col + MEM_WIDTH])
    for j in range(MEM_HEADS):
        sl = slice(j * MEM_HEAD_DIM, (j + 1) * MEM_HEAD_DIM)
        mq_ref[:, sl] = _rms(zm[:, sl], mqg_ref[...]).astype(BF16)


def _in_proj(x, g_mix, w_bf16, q_g, k_g, mq_g, rope_tabs, conv_ch):
    m, d = x.shape
    tm = min(ROW_TILE, m)
    n_tab_blocks = rope_tabs[0].shape[0] // tm
    row = lambda i: (i, 0)
    tab = lambda i: (i % n_tab_blocks, 0)
    ncols = w_bf16.shape[1]
    out_shape = (
        jax.ShapeDtypeStruct((m, N_Q_HEADS * HEAD_DIM), BF16),
        jax.ShapeDtypeStruct((m * N_KV_HEADS, HEAD_DIM), F32),
        jax.ShapeDtypeStruct((m * N_KV_HEADS, HEAD_DIM), F32),
        jax.ShapeDtypeStruct((m, conv_ch), F32),
        jax.ShapeDtypeStruct((m, MEM_WIDTH), BF16),
    )
    return pl.pallas_call(
        _in_proj_kernel,
        grid=(m // tm,),
        in_specs=[
            pl.BlockSpec((tm, d), row),
            _const_spec((1, d)),
            _const_spec((d, ncols)),
            _const_spec((1, HEAD_DIM)), _const_spec((1, HEAD_DIM)), _const_spec((1, MEM_HEAD_DIM)),
            pl.BlockSpec((tm, LANES), tab), pl.BlockSpec((tm, LANES), tab), pl.BlockSpec((tm, LANES), tab),
        ],
        out_specs=tuple(pl.BlockSpec((tm * s.shape[0] // m, s.shape[1]), row) for s in out_shape),
        out_shape=out_shape,
        compiler_params=_params("parallel"),
        name="in_proj",
    )(x, g_mix, w_bf16, q_g, k_g, mq_g, *rope_tabs)


def _rope_tables(pos):
    half = ROT_DIM // 2
    inv_freq = jnp.power(jnp.float32(ROPE_THETA), -jnp.arange(half, dtype=F32) * (2.0 / ROT_DIM))
    ang = pos.astype(F32)[:, None] * inv_freq[None, :]
    cos, sin = jnp.cos(ang), jnp.sin(ang)
    n = pos.shape[0]
    ones = jnp.ones((n, LANES - ROT_DIM), F32)
    zeros = jnp.zeros((n, LANES - half), F32)
    rc = jnp.concatenate([cos, cos, ones], axis=1)
    ra = jnp.concatenate([-sin, zeros], axis=1)
    rb = jnp.concatenate([jnp.zeros((n, half), F32), sin, jnp.zeros((n, LANES - ROT_DIM), F32)], axis=1)
    return rc, ra, rb


def _prompt_attn_kernel(q0_ref, q1_ref, q2_ref, k_ref, v_ref, o_ref, qf_ref, acc_ref, lse_ref):
    head = pl.program_id(1)
    seq = q0_ref.shape[0]
    q_refs = (q0_ref, q1_ref, q2_ref)
    for g in range(1, N_GROUPS):
        qf_ref[g - 1] = q_refs[g][...].astype(F32)
    iq = lax.broadcasted_iota(jnp.int32, (BAND, BAND), 0)
    ik = lax.broadcasted_iota(jnp.int32, (BAND, BAND), 1)
    keep_c = iq >= ik
    scale = HEAD_DIM ** -0.5

    for g, (_, dil) in enumerate(DILATED_GROUPS):
        nb = seq // (dil * BAND)

        def pos_rows(start, dil=dil):
            if dil == 1:
                return pl.ds(pl.multiple_of(start, BAND), BAND)
            return pl.ds(start, BAND, stride=dil)

        def kv_rows(ref, start, dil=dil):
            return ref[pl.ds(start * N_KV_HEADS + head, BAND, stride=dil * N_KV_HEADS), :].astype(BF16)

        def body(j, carry, g=g, dil=dil, nb=nb, pos_rows=pos_rows, kv_rows=kv_rows):
            r, blk = j // nb, j % nb
            start = blk * (BAND * dil) + r
            if dil == 1:
                q = q0_ref[pos_rows(start), :]
            else:
                q = qf_ref[g - 1, pos_rows(start), :].astype(BF16)
            s_c = jnp.where(keep_c, _dot_nt(q, kv_rows(k_ref, start)) * scale, NEG)
            m = jnp.max(s_c, axis=-1, keepdims=True)
            if nb > 1:
                prev = jnp.maximum(blk - 1, 0) * (BAND * dil) + r
                keep_p = jnp.logical_and(ik >= iq, blk > 0)
                s_p = jnp.where(keep_p, _dot_nt(q, kv_rows(k_ref, prev)) * scale, NEG)
                m = jnp.maximum(m, jnp.max(s_p, axis=-1, keepdims=True))
            p_c = jnp.exp(s_c - m)
            l = jnp.sum(p_c, axis=-1, keepdims=True)
            acc = _dot(p_c.astype(BF16), kv_rows(v_ref, start))
            if nb > 1:
                p_p = jnp.exp(s_p - m)
                l = l + jnp.sum(p_p, axis=-1, keepdims=True)
                acc = acc + _dot(p_p.astype(BF16), kv_rows(v_ref, prev))
            acc_ref[g, pos_rows(start), :] = acc * (1.0 / l)
            lse_ref[g, pos_rows(start), :] = jnp.broadcast_to(m + jnp.log(l), (BAND, LANES))
            return carry

        lax.fori_loop(0, dil * nb, body, 0)

    def combine(c, carry):
        rows = pl.ds(pl.multiple_of(c * BAND, BAND), BAND)
        lses = [lse_ref[g, rows, :] for g in range(N_GROUPS)]
        mx = functools.reduce(jnp.maximum, lses)
        ws = [jnp.exp(l - mx) for l in lses]
        out = functools.reduce(jnp.add, [w * acc_ref[g, rows, :] for g, w in enumerate(ws)])
        o_ref[rows, :] = (out * (1.0 / functools.reduce(jnp.add, ws))).astype(BF16)
        return carry

    lax.fori_loop(0, seq // BAND, combine, 0)


def _prompt_attention(q, k_flat, v_flat, batch, seq):
    for window, dil in DILATED_GROUPS:
        assert window // dil == BAND and seq % (dil * BAND) == 0
    qspec = lambda g: pl.BlockSpec((seq, HEAD_DIM), lambda b, h: (b, g * N_KV_HEADS + h))
    kvspec = pl.BlockSpec((seq * N_KV_HEADS, HEAD_DIM), lambda b, h: (b, 0))
    return pl.pallas_call(
        _prompt_attn_kernel,
        grid=(batch, N_KV_HEADS),
        in_specs=[qspec(0), qspec(1), qspec(2), kvspec, kvspec],
        out_specs=pl.BlockSpec((seq, HEAD_DIM), lambda b, h: (b, h)),
        out_shape=jax.ShapeDtypeStruct((batch * seq, ATTN_WIDTH), BF16),
        scratch_shapes=[pltpu.VMEM((N_GROUPS - 1, seq, HEAD_DIM), F32), pltpu.VMEM((N_GROUPS, seq, HEAD_DIM), F32),
                        pltpu.VMEM((N_GROUPS, seq, LANES), F32)],
        compiler_params=_params("parallel", "arbitrary"),
        name="prompt_attn",
    )(q, q, q, k_flat, v_flat)


def _mem_kv_kernel(x_ref, g_ref, w_ref, kg_ref, k_ref, v_ref):
    tm = x_ref.shape[0]
    h = _rms(x_ref[...], g_ref[...]).astype(BF16)
    zk = _dot(h, w_ref[:, :MEM_WIDTH])
    zv = _dot(h, w_ref[:, MEM_WIDTH:])
    for j in range(MEM_HEADS):
        sl = slice(j * MEM_HEAD_DIM, (j + 1) * MEM_HEAD_DIM)
        k_ref[pl.ds(j, tm, stride=MEM_HEADS), :] = _rms(zk[:, sl], kg_ref[...])
        v_ref[pl.ds(j, tm, stride=MEM_HEADS), :] = zv[:, sl]


def _mem_kv(mem, g, w_bf16, k_g):
    m, d = mem.shape
    tm = min(ROW_TILE, m)
    row = lambda i: (i, 0)
    shp = jax.ShapeDtypeStruct((m * MEM_HEADS, MEM_HEAD_DIM), F32)
    ospec = pl.BlockSpec((tm * MEM_HEADS, MEM_HEAD_DIM), row)
    return pl.pallas_call(
        _mem_kv_kernel,
        grid=(m // tm,),
        in_specs=[pl.BlockSpec((tm, d), row), _const_spec((1, d)), _const_spec((d, 2 * MEM_WIDTH)),
                  _const_spec((1, MEM_HEAD_DIM))],
        out_specs=(ospec, ospec),
        out_shape=(shp, shp),
        compiler_params=_params("parallel"),
        name="mem_kv",
    )(mem, g, w_bf16, k_g)


def _mem_attn_kernel(q_ref, k_ref, v_ref, o_ref):
    scale = MEM_HEAD_DIM ** -0.5
    n_mem = k_ref.shape[0] // MEM_HEADS
    for h in range(MEM_HEADS):
        sl = slice(h * MEM_HEAD_DIM, (h + 1) * MEM_HEAD_DIM)
        head_rows = pl.ds(h, n_mem, stride=MEM_HEADS)
        s = _dot_nt(q_ref[:, sl], k_ref[head_rows, :].astype(BF16)) * scale
        p = jnp.exp(s - jnp.max(s, axis=-1, keepdims=True))
        l = jnp.sum(p, axis=-1, keepdims=True)
        o_ref[:, sl] = (_dot(p.astype(BF16), v_ref[head_rows, :].astype(BF16)) * (1.0 / l)).astype(BF16)


def _mem_attention(mq, mem_k_flat, mem_v_flat, batch, seq):
    rows = mem_k_flat.shape[0] // batch
    tq = min(512, seq)
    nq = seq // tq
    kspec = pl.BlockSpec((rows, MEM_HEAD_DIM), lambda b, i: (b, 0))
    return pl.pallas_call(
        _mem_attn_kernel,
        grid=(batch, nq),
        in_specs=[pl.BlockSpec((tq, MEM_WIDTH), lambda b, i: (b * nq + i, 0)), kspec, kspec],
        out_specs=pl.BlockSpec((tq, MEM_WIDTH), lambda b, i: (b * nq + i, 0)),
        out_shape=jax.ShapeDtypeStruct((batch * seq, MEM_WIDTH), BF16),
        compiler_params=_params("parallel", "parallel"),
        name="mem_attn",
    )(mq, mem_k_flat, mem_v_flat)


CONV_HALO = 32
CONV_CHUNK = 32


def _ln_swish(c, g, b):
    mu = jnp.mean(c, axis=-1, keepdims=True)
    xc = c - mu
    y = xc * lax.rsqrt(jnp.mean(xc * xc, axis=-1, keepdims=True) + EPS) * g + b
    return y * jax.nn.sigmoid(y)


def _conv_prompt_kernel(halo_ref, u_ref, w_ref, b_ref, g_ref, beta_ref, o_ref, ext_ref):
    tc = u_ref.shape[0]
    first = pl.program_id(1) == 0
    ext_ref[0:CONV_HALO, :] = jnp.where(first, 0.0, halo_ref[...])
    ext_ref[CONV_HALO:, :] = u_ref[...]
    lead = CONV_HALO - (CONV_WIDTH - 1)
    for c0 in range(0, tc, CONV_CHUNK):
        acc = jnp.zeros((CONV_CHUNK, u_ref.shape[1]), F32) + b_ref[...]
        for w in range(CONV_WIDTH):
            acc = acc + w_ref[w:w + 1, :] * ext_ref[c0 + lead + w:c0 + lead + w + CONV_CHUNK, :]
        o_ref[c0:c0 + CONV_CHUNK, :] = _ln_swish(acc, g_ref[...], beta_ref[...]).astype(BF16)


def _conv_prompt(u, conv_w, conv_b, ln_g, ln_b, batch, seq):
    ch = u.shape[1]
    tc = min(ROW_TILE, seq)
    nt = seq // tc
    ratio = tc // CONV_HALO
    return pl.pallas_call(
        _conv_prompt_kernel,
        grid=(batch, nt),
        in_specs=[
            pl.BlockSpec((CONV_HALO, ch), lambda b, i: (jnp.maximum((b * nt + i) * ratio - 1, 0), 0)),
            pl.BlockSpec((tc, ch), lambda b, i: (b * nt + i, 0)),
            _const_spec((CONV_WIDTH, ch)), _const_spec((1, ch)), _const_spec((1, ch)), _const_spec((1, ch)),
        ],
        out_specs=pl.BlockSpec((tc, ch), lambda b, i: (b * nt + i, 0)),
        out_shape=jax.ShapeDtypeStruct((batch * seq, ch), BF16),
        scratch_shapes=[pltpu.VMEM((CONV_HALO + tc, ch), F32)],
        compiler_params=_params("parallel", "parallel"),
        name="conv_prompt",
    )(u, u, conv_w, conv_b, ln_g, ln_b)


def _conv_sample_kernel(state_ref, new_ref, w_ref, b_ref, g_ref, beta_ref, o_ref, state_out_ref, ext_ref):
    nb, ctx, ch = state_ref.shape
    t = new_ref.shape[1]
    ext_ref[:, 0:ctx, :] = state_ref[...]
    ext_ref[:, ctx:ctx + t, :] = new_ref[...]
    acc = jnp.zeros((nb, t, ch), F32) + b_ref[...]
    for w in range(CONV_WIDTH):
        acc = acc + w_ref[w:w + 1, :] * ext_ref[:, w:w + t, :]
    o_ref[...] = _ln_swish(acc, g_ref[...], beta_ref[...])
    state_out_ref[...] = ext_ref[:, t:t + ctx, :]


def _conv_sample(state, u_new, conv_w, conv_b, ln_g, ln_b):
    batch, ctx, ch = state.shape
    t = u_new.shape[1]
    nb = 8
    blk = lambda n: pl.BlockSpec((nb, n, ch), lambda i: (i, 0, 0))
    return pl.pallas_call(
        _conv_sample_kernel,
        grid=(batch // nb,),
        in_specs=[blk(ctx), blk(t), _const_spec((CONV_WIDTH, ch)), _const_spec((1, ch)), _const_spec((1, ch)),
                  _const_spec((1, ch))],
        out_specs=(blk(t), blk(ctx)),
        out_shape=(jax.ShapeDtypeStruct((batch, t, ch), F32), jax.ShapeDtypeStruct((batch, ctx, ch), F32)),
        scratch_shapes=[pltpu.VMEM((nb, ctx + t + 6, ch), F32)],
        compiler_params=_params("parallel"),
        name="conv_sample",
    )(state, u_new, conv_w, conv_b, ln_g, ln_b)


GROUP_LANES = 16


def _spread_groups(vec, combine):
    t = combine(combine(vec, pltpu.roll(vec, LANES - GROUP_LANES, 1)), pltpu.roll(vec, LANES - 2 * GROUP_LANES, 1))
    lane = lax.broadcasted_iota(jnp.int32, vec.shape, 1)
    return jnp.where(lane < GROUP_LANES, t,
                     jnp.where(lane < 2 * GROUP_LANES, pltpu.roll(t, GROUP_LANES, 1), pltpu.roll(t, 2 * GROUP_LANES, 1)))


SLAB_POS = DILATED_GROUPS[-1][1]
SLAB_ROWS = SLAB_POS * N_KV_HEADS
TAIL_POS = max(w for w, d in DILATED_GROUPS if d < SLAB_POS)


def _sample_attn_kernel(kc_ref, vc_ref, kn_ref, vn_ref, wq_ref, bias_d_ref, bias_t_ref, bias_n_ref, mk_ref, mv_ref,
                        wm_ref, bias_m_ref, kw_ref, vw_ref, a_ref, m_ref, kx_ref, vx_ref):
    n_slab = kc_ref.shape[1]
    new_rows = kn_ref.shape[1]
    t = new_rows // N_KV_HEADS
    tail_slabs = TAIL_POS // SLAB_POS
    n_used = N_GROUPS * GROUP_LANES
    lane = lax.broadcasted_iota(jnp.int32, (1, LANES), 1)

    for src, new, dst in ((kc_ref, kn_ref, kw_ref), (vc_ref, vn_ref, vw_ref)):
        dst[0, :, 0:SLAB_ROWS - new_rows, :] = src[0, :, new_rows:SLAB_ROWS, :]
        dst[0, 0:n_slab - 1, SLAB_ROWS - new_rows:SLAB_ROWS, :] = src[0, 1:n_slab, 0:new_rows, :]
        dst[0, n_slab - 1, SLAB_ROWS - new_rows:SLAB_ROWS, :] = new[0]

    kx_ref[...] = jnp.zeros(kx_ref.shape, F32)
    vx_ref[...] = jnp.zeros(vx_ref.shape, F32)
    kx_ref[0:new_rows, :] = kn_ref[0]
    vx_ref[0:new_rows, :] = vn_ref[0]

    def dilated(ref):
        return ref[0, :, 0:new_rows, :].reshape(n_slab * new_rows, HEAD_DIM).astype(BF16)

    def tail(ref):
        return ref[0, n_slab - tail_slabs:n_slab, :, :].reshape(tail_slabs * SLAB_ROWS, HEAD_DIM).astype(BF16)

    scale = HEAD_DIM ** -0.5
    wq = wq_ref[0]
    s_d = _dot(dilated(kc_ref), wq) * scale + bias_d_ref[...]
    s_t = _dot(tail(kc_ref), wq) * scale + bias_t_ref[...]
    s_n = _dot(kx_ref[...].astype(BF16), wq) * scale + bias_n_ref[...]
    col_max = lambda s: jnp.max(s, axis=0, keepdims=True)
    col_sum = lambda p: jnp.sum(p, axis=0, keepdims=True)
    m_col = jnp.maximum(jnp.maximum(col_max(s_d), col_max(s_t)), col_max(s_n))
    m_joint = jnp.where(lane < n_used, _spread_groups(m_col, jnp.maximum), 0.0)
    p_d, p_t, p_n = jnp.exp(s_d - m_joint), jnp.exp(s_t - m_joint), jnp.exp(s_n - m_joint)
    l_col = col_sum(p_d) + col_sum(p_t) + col_sum(p_n)
    inv = 1.0 / jnp.where(lane < n_used, _spread_groups(l_col, jnp.add), 1.0)
    o = (_dot_tn((p_d * inv).astype(BF16), dilated(vc_ref)) + _dot_tn((p_t * inv).astype(BF16), tail(vc_ref))
         + _dot_tn((p_n * inv).astype(BF16), vx_ref[...].astype(BF16)))
    for h in range(N_KV_HEADS):
        r = h * t
        a_ref[0, :, h * HEAD_DIM:(h + 1) * HEAD_DIM] = functools.reduce(
            jnp.add, [o[g * GROUP_LANES + r:g * GROUP_LANES + r + t, :] for g in range(N_GROUPS)])

    sm = _dot(mk_ref[0].astype(BF16), wm_ref[0]) * (MEM_HEAD_DIM ** -0.5) + bias_m_ref[...]
    pm = jnp.exp(sm - jnp.where(lane < MEM_HEADS * t, col_max(sm), 0.0))
    lm = jnp.where(lane < MEM_HEADS * t, col_sum(pm), 1.0)
    om = _dot_tn((pm * (1.0 / lm)).astype(BF16), mv_ref[0].astype(BF16))
    for h in range(MEM_HEADS):
        m_ref[0, :, h * MEM_HEAD_DIM:(h + 1) * MEM_HEAD_DIM] = om[h * t:(h + 1) * t, :]


def _sample_masks(cache_len, t, n_mem):
    col = np.arange(LANES)[None, :]
    g, c_head, c_tok = col // GROUP_LANES, (col % GROUP_LANES) // t, (col % GROUP_LANES) % t
    used = (col < N_GROUPS * GROUP_LANES) & (col % GROUP_LANES < N_KV_HEADS * t)
    pad = [1] * (LANES // GROUP_LANES - N_GROUPS)
    dil = np.array([d for _, d in DILATED_GROUPS] + pad)[g]
    win = np.array([w for w, _ in DILATED_GROUPS] + pad)[g]
    sparse = dil >= SLAB_POS

    def keep(pos, head, group_sel):
        dist = cache_len + c_tok - pos
        return used & group_sel & (head == c_head) & (dist >= 0) & (dist % dil == 0) & (dist <= win)

    n_slab = cache_len // SLAB_POS
    y = np.arange(n_slab * t * N_KV_HEADS)[:, None]
    keep_d = keep((y // (t * N_KV_HEADS)) * SLAB_POS + (y % (t * N_KV_HEADS)) // N_KV_HEADS, y % N_KV_HEADS, sparse)
    x = np.arange(TAIL_POS * N_KV_HEADS)[:, None]
    keep_t = keep(cache_len - TAIL_POS + x // N_KV_HEADS, x % N_KV_HEADS, ~sparse)
    z = np.arange(LANES)[:, None]
    keep_n = keep(cache_len + z // N_KV_HEADS, z % N_KV_HEADS, True) & (z < t * N_KV_HEADS)
    w = np.arange(n_mem * MEM_HEADS)[:, None]
    keep_m = (col < MEM_HEADS * t) & (w % MEM_HEADS == col // t)
    return tuple(jnp.asarray(np.where(k, 0.0, NEG), F32) for k in (keep_d, keep_t, keep_n, keep_m))


def _query_columns(q, batch, t, n_groups):
    qt = q.reshape(batch, t, n_groups, N_KV_HEADS, HEAD_DIM).transpose(0, 4, 2, 3, 1)
    qt = qt.reshape(batch, HEAD_DIM, n_groups, N_KV_HEADS * t)
    qt = jnp.pad(qt, ((0, 0), (0, 0), (0, 0), (0, GROUP_LANES - N_KV_HEADS * t)))
    qt = qt.reshape(batch, HEAD_DIM, n_groups * GROUP_LANES)
    return jnp.pad(qt, ((0, 0), (0, 0), (0, LANES - n_groups * GROUP_LANES)))


def _sample_attention(q, k_new, v_new, mq, cache_k, cache_v, mem_k, mem_v):
    batch, cache_len = cache_k.shape[0], cache_k.shape[1]
    new_rows = k_new.shape[0] // batch
    t = new_rows // N_KV_HEADS
    n_mem = mem_k.shape[1]
    assert N_KV_HEADS * t <= GROUP_LANES and t <= SLAB_POS and new_rows % 8 == 0
    assert cache_len % SLAB_POS == 0 and cache_len >= max(w for w, _ in DILATED_GROUPS)
    assert all(d == SLAB_POS or w <= TAIL_POS for w, d in DILATED_GROUPS)
    n_slab = cache_len // SLAB_POS
    wq = _query_columns(q, batch, t, N_GROUPS)
    wm = _query_columns(mq, batch, t, 1)
    bias_d, bias_t, bias_n, bias_m = _sample_masks(cache_len, t, n_mem)
    per_b = lambda *shape: pl.BlockSpec((1,) + shape, lambda b: (b,) + (0,) * len(shape))
    slabs = lambda c: c.reshape(batch, n_slab, SLAB_ROWS, HEAD_DIM)
    mem_rows = n_mem * MEM_HEADS
    win = jax.ShapeDtypeStruct((batch, n_slab, SLAB_ROWS, HEAD_DIM), F32)
    k_win, v_win, a, m = pl.pallas_call(
        _sample_attn_kernel,
        grid=(batch,),
        in_specs=[
            per_b(n_slab, SLAB_ROWS, HEAD_DIM), per_b(n_slab, SLAB_ROWS, HEAD_DIM),
            per_b(new_rows, HEAD_DIM), per_b(new_rows, HEAD_DIM), per_b(HEAD_DIM, LANES),
            _const_spec(bias_d.shape), _const_spec(bias_t.shape), _const_spec(bias_n.shape),
            per_b(mem_rows, MEM_HEAD_DIM), per_b(mem_rows, MEM_HEAD_DIM), per_b(MEM_HEAD_DIM, LANES),
            _const_spec(bias_m.shape),
        ],
        out_specs=(per_b(n_slab, SLAB_ROWS, HEAD_DIM), per_b(n_slab, SLAB_ROWS, HEAD_DIM), per_b(t, ATTN_WIDTH),
                   per_b(t, MEM_WIDTH)),
        out_shape=(win, win, jax.ShapeDtypeStruct((batch, t, ATTN_WIDTH), F32),
                   jax.ShapeDtypeStruct((batch, t, MEM_WIDTH), F32)),
        scratch_shapes=[pltpu.VMEM((LANES, HEAD_DIM), F32), pltpu.VMEM((LANES, HEAD_DIM), F32)],
        compiler_params=_params("parallel"),
        name="sample_attn",
    )(slabs(cache_k), slabs(cache_v), k_new.reshape(batch, new_rows, HEAD_DIM), v_new.reshape(batch, new_rows, HEAD_DIM),
      wq, bias_d, bias_t, bias_n, mem_k.reshape(batch, mem_rows, MEM_HEAD_DIM),
      mem_v.reshape(batch, mem_rows, MEM_HEAD_DIM), wm, bias_m)
    return k_win.reshape(cache_k.shape), v_win.reshape(cache_v.shape), a, m


ROUTE_GROUP_LANE0 = 0
ROUTE_EXPERT_LANE0 = N_EXPERT_GROUPS


def _route(logits):
    lane = lax.broadcasted_iota(jnp.int32, logits.shape, 1).astype(F32)
    big = float(LANES)

    def masked_softmax(keep):
        z = jnp.where(keep, logits, NEG)
        e = jnp.where(keep, jnp.exp(z - jnp.max(z, axis=-1, keepdims=True)), 0.0)
        return e / jnp.sum(e, axis=-1, keepdims=True)

    def first_argmax(vals, keep):
        top = jnp.max(jnp.where(keep, vals, -1.0), axis=-1, keepdims=True)
        idx = jnp.min(jnp.where(jnp.logical_and(keep, vals == top), lane, big), axis=-1, keepdims=True)
        return top, idx

    is_group = lane < N_EXPERT_GROUPS
    pg = masked_softmax(is_group)
    pg_top, g_idx = first_argmax(pg, is_group)
    lo = ROUTE_EXPERT_LANE0 + g_idx * EXPERTS_PER_GROUP
    in_group = jnp.logical_and(lane >= lo, lane < lo + EXPERTS_PER_GROUP)
    pe = masked_softmax(in_group)
    p1, i1 = first_argmax(pe, in_group)
    p2, i2 = first_argmax(pe, jnp.logical_and(in_group, lane != i1))
    denom = p1 + p2
    w1 = pg_top * p1 / denom
    w2 = pg_top * p2 / denom
    e1 = i1 - ROUTE_EXPERT_LANE0
    e2 = i2 - ROUTE_EXPERT_LANE0
    return jnp.where(lane == 0, e1, jnp.where(lane == 1, e2, jnp.where(lane == 2, w1, jnp.where(lane == 3, w2, 0.0))))


def _merge_kernel(x_ref, gmix_ref, ao_ref, cc_ref, mo_ref, wgate_ref, wa_ref, wc_ref, wm_ref, wo_ref, gffn_ref, wr_ref,
                  br_ref, x1_ref, h2_ref, route_ref):
    d = x_ref.shape[1]
    x = x_ref[...]
    h = _rms(x, gmix_ref[...]).astype(BF16)
    a = _dot(ao_ref[...].astype(BF16), wa_ref[...])
    c = _dot(cc_ref[...].astype(BF16), wc_ref[...])
    m = _dot(mo_ref[...].astype(BF16), wm_ref[...])
    z = jax.nn.sigmoid(_dot(h, wgate_ref[:, 0:d])) * a
    z = z + jax.nn.sigmoid(_dot(h, wgate_ref[:, d:2 * d])) * c
    z = z + jax.nn.sigmoid(_dot(h, wgate_ref[:, 2 * d:3 * d])) * m
    x1 = x + _dot(z.astype(BF16), wo_ref[...])
    x1_ref[...] = x1
    h2 = _rms(x1, gffn_ref[...])
    h2_ref[...] = h2
    logits = jnp.dot(h2, wr_ref[...], precision=lax.Precision.HIGHEST, preferred_element_type=F32) + br_ref[...]
    route_ref[...] = _route(logits)


def _merge(x, g_mix, ao, cc, mo, w_gate, w_a, w_c, w_m, w_o, g_ffn, w_r, b_r):
    m, d = x.shape
    tm = min(ROW_TILE, m)
    row = lambda i: (i, 0)
    rows = lambda arr: pl.BlockSpec((tm, arr.shape[1]), row)
    ins = [x, g_mix, ao, cc, mo, w_gate, w_a, w_c, w_m, w_o, g_ffn, w_r, b_r]
    specs = ([rows(x), _const_spec(g_mix.shape), rows(ao), rows(cc), rows(mo)]
             + [_const_spec(a.shape) for a in (w_gate, w_a, w_c, w_m, w_o, g_ffn, w_r, b_r)])
    out_shape = (jax.ShapeDtypeStruct((m, d), F32), jax.ShapeDtypeStruct((m, d), F32),
                 jax.ShapeDtypeStruct((m, LANES), F32))
    return pl.pallas_call(
        _merge_kernel,
        grid=(m // tm,),
        in_specs=specs,
        out_specs=tuple(pl.BlockSpec((tm, s.shape[1]), row) for s in out_shape),
        out_shape=out_shape,
        compiler_params=_params("parallel"),
        name="merge",
    )(*ins)


def _routing_tables(expert_ids, n_tiles):
    n = expert_ids.shape[0]
    flat = expert_ids.reshape(-1)
    order = jnp.argsort(flat, stable=True).astype(jnp.int32)
    sorted_e = flat[order]
    counts = jnp.zeros((N_EXPERTS,), jnp.int32).at[flat].add(1)
    tiles = (counts + MOE_TILE - 1) // MOE_TILE
    tile_end = jnp.cumsum(tiles)
    tile_start = tile_end - tiles
    start = jnp.cumsum(counts) - counts
    slot = tile_start[sorted_e] * MOE_TILE + (jnp.arange(TOP_K * n, dtype=jnp.int32) - start[sorted_e])
    src = jnp.zeros((n_tiles * MOE_TILE,), jnp.int32).at[slot].set(order // TOP_K)
    pos = jnp.zeros((TOP_K * n,), jnp.int32).at[order].set(slot)
    n_used = tile_end[-1]
    tile_ids = jnp.minimum(jnp.arange(n_tiles, dtype=jnp.int32), n_used - 1)
    tile_expert = jnp.searchsorted(tile_end, tile_ids, side="right").astype(jnp.int32)
    return src, pos, tile_expert, n_used.reshape(1).astype(jnp.int32)


def _gather_kernel(src_ref, nused_ref, h_hbm, o_ref, buf_ref, sem):
    t = pl.program_id(0)

    @pl.when(t < nused_ref[0])
    def _():
        base = t * MOE_TILE

        def issue(j, carry):
            tok = src_ref[base + j]
            pltpu.make_async_copy(h_hbm.at[pl.ds(tok, 1)], buf_ref.at[pl.ds(j, 1)], sem).start()
            return carry

        lax.fori_loop(0, MOE_TILE, issue, 0, unroll=8)
        pltpu.make_async_copy(h_hbm.at[pl.ds(0, MOE_TILE)], buf_ref, sem).wait()
        o_ref[...] = buf_ref[...].astype(BF16)

    @pl.when(t >= nused_ref[0])
    def _():
        o_ref[...] = jnp.zeros(o_ref.shape, BF16)


def _gather_rows(h2, src, n_used, n_tiles):
    d = h2.shape[1]
    return pl.pallas_call(
        _gather_kernel,
        grid_spec=pltpu.PrefetchScalarGridSpec(
            num_scalar_prefetch=2,
            grid=(n_tiles,),
            in_specs=[pl.BlockSpec(memory_space=pl.ANY)],
            out_specs=pl.BlockSpec((MOE_TILE, d), lambda t, *_: (t, 0)),
            scratch_shapes=[pltpu.VMEM((MOE_TILE, d), F32), pltpu.SemaphoreType.DMA(())],
        ),
        out_shape=jax.ShapeDtypeStruct((n_tiles * MOE_TILE, d), BF16),
        compiler_params=_params("arbitrary"),
        name="moe_gather",
    )(src, n_used, h2)


def _gmm_kernel(te_ref, nused_ref, x_ref, wg_ref, wu_ref, wd_ref, o_ref, wg_s, wu_s, wd_s):
    t = pl.program_id(0)
    used = t < nused_ref[0]
    new_expert = jnp.logical_or(t == 0, te_ref[t] != te_ref[jnp.maximum(t - 1, 0)])

    @pl.when(jnp.logical_and(used, new_expert))
    def _():
        wg_s[...] = wg_ref[0].astype(BF16)
        wu_s[...] = wu_ref[0].astype(BF16)
        wd_s[...] = wd_ref[0].astype(BF16)

    @pl.when(used)
    def _():
        x = x_ref[...]
        gate = _dot(x, wg_s[...])
        up = _dot(x, wu_s[...])
        hid = gate * jax.nn.sigmoid(gate) * up
        o_ref[...] = _dot(hid.astype(BF16), wd_s[...])

    @pl.when(jnp.logical_not(used))
    def _():
        o_ref[...] = jnp.zeros(o_ref.shape, F32)


def _grouped_mlp(xs, tile_expert, n_used, w_gate, w_up, w_down):
    n_tiles = xs.shape[0] // MOE_TILE
    d, ff = w_gate.shape[1], w_gate.shape[2]
    wspec = lambda a, b: pl.BlockSpec((1, a, b), lambda t, te, nu: (te[t], 0, 0))
    return pl.pallas_call(
        _gmm_kernel,
        grid_spec=pltpu.PrefetchScalarGridSpec(
            num_scalar_prefetch=2,
            grid=(n_tiles,),
            in_specs=[pl.BlockSpec((MOE_TILE, d), lambda t, *_: (t, 0)), wspec(d, ff), wspec(d, ff), wspec(ff, d)],
            out_specs=pl.BlockSpec((MOE_TILE, d), lambda t, *_: (t, 0)),
            scratch_shapes=[pltpu.VMEM((d, ff), BF16), pltpu.VMEM((d, ff), BF16), pltpu.VMEM((ff, d), BF16)],
        ),
        out_shape=jax.ShapeDtypeStruct((n_tiles * MOE_TILE, d), F32),
        compiler_params=_params("arbitrary"),
        name="moe_gmm",
    )(tile_expert, n_used, xs, w_gate, w_up, w_down)


def _combine_kernel(pos_ref, x1_ref, route_ref, ys_hbm, y_ref, buf_ref, sem, *, row0):
    tm = x1_ref.shape[0]
    base = (row0 + pl.program_id(0) * tm) * TOP_K

    def issue(j, carry):
        for k in range(TOP_K):
            slot = pos_ref[base + j * TOP_K + k]
            pltpu.make_async_copy(ys_hbm.at[pl.ds(slot, 1)], buf_ref.at[k, pl.ds(j, 1)], sem).start()
        return carry

    lax.fori_loop(0, tm, issue, 0, unroll=4)
    for k in range(TOP_K):
        pltpu.make_async_copy(ys_hbm.at[pl.ds(0, tm)], buf_ref.at[k], sem).wait()
    route = route_ref[...]
    y_ref[...] = x1_ref[...] + route[:, 2:3] * buf_ref[0] + route[:, 3:4] * buf_ref[1]


def _combine(x1, route, pos, ys, row0):
    m, d = x1.shape
    tm = min(ROW_TILE, m)
    row = lambda i, *_: (i, 0)
    return pl.pallas_call(
        functools.partial(_combine_kernel, row0=row0),
        grid_spec=pltpu.PrefetchScalarGridSpec(
            num_scalar_prefetch=1,
            grid=(m // tm,),
            in_specs=[pl.BlockSpec((tm, d), row), pl.BlockSpec((tm, LANES), row), pl.BlockSpec(memory_space=pl.ANY)],
            out_specs=pl.BlockSpec((tm, d), row),
            scratch_shapes=[pltpu.VMEM((TOP_K, tm, d), F32), pltpu.SemaphoreType.DMA(())],
        ),
        out_shape=jax.ShapeDtypeStruct((m, d), F32),
        compiler_params=_params("arbitrary"),
        name="moe_combine",
    )(pos, x1, route, ys)


def _layer(layer, x_prompt, x_sample, mem_prompt, cache_k, cache_v, state_conv, cache_mem_k, cache_mem_v, p):
    batch, seq, d = x_prompt.shape
    dec_batch, dec_seq, _ = x_sample.shape
    conv_ch = p["conv_w"].shape[-1]
    n_in = N_Q_HEADS * HEAD_DIM + 2 * ATTN_WIDTH + 2 * conv_ch + MEM_WIDTH
    past_len = cache_k.shape[2]

    row2 = lambda name: p[name][layer][None, :]
    w_in = p["w_in"][layer]
    w_main = w_in[:, :n_in].astype(BF16)
    w_gate = w_in[:, n_in:].astype(BF16)
    w_a, w_c, w_m, w_o = (p[n][layer].astype(BF16) for n in ("w_attn_proj", "w_conv_proj", "w_mem_proj", "w_out"))
    w_router = jnp.concatenate(
        [p["w_router_group"][layer], p["w_router_expert"][layer].transpose(1, 0, 2).reshape(d, N_EXPERTS)], axis=1)
    w_router = jnp.pad(w_router, ((0, 0), (0, LANES - w_router.shape[1])))
    b_router = jnp.concatenate([p["b_router_group"][layer], p["b_router_expert"][layer].reshape(-1)])
    b_router = jnp.pad(b_router, (0, LANES - b_router.shape[0]))[None, :]
    conv_args = (p["conv_w"][layer], row2("conv_b"), row2("conv_ln_g"), row2("conv_ln_b"))
    merge_w = (w_gate, w_a, w_c, w_m, w_o, row2("norm_ffn_g"), w_router, b_router)

    xp = x_prompt.reshape(batch * seq, d)
    tabs_p = _rope_tables(jnp.arange(seq, dtype=jnp.int32))
    q_p, k_p, v_p, u_p, mq_p = _in_proj(xp, row2("norm_mix_g"), w_main, row2("q_norm_g"), row2("k_norm_g"),
                                        row2("mq_norm_g"), tabs_p, conv_ch)
    ao_p = _prompt_attention(q_p, k_p, v_p, batch, seq)
    mem_k_p, mem_v_p = _mem_kv(mem_prompt.reshape(-1, d), row2("mem_norm_g"), p["w_mem_kv"][layer].astype(BF16),
                               row2("mk_norm_g"))
    mo_p = _mem_attention(mq_p, mem_k_p, mem_v_p, batch, seq)
    cc_p = _conv_prompt(u_p, *conv_args, batch, seq)
    x1_p, h2_p, route_p = _merge(xp, row2("norm_mix_g"), ao_p, cc_p, mo_p, *merge_w)

    xs = x_sample.reshape(dec_batch * dec_seq, d)
    tabs_s = _rope_tables(jnp.tile(past_len + jnp.arange(dec_seq, dtype=jnp.int32), dec_batch))
    q_s, k_s, v_s, u_s, mq_s = _in_proj(xs, row2("norm_mix_g"), w_main, row2("q_norm_g"), row2("k_norm_g"),
                                        row2("mq_norm_g"), tabs_s, conv_ch)
    k_win_s, v_win_s, a_s, mo_s = _sample_attention(q_s, k_s, v_s, mq_s, cache_k[layer], cache_v[layer],
                                                    cache_mem_k[layer], cache_mem_v[layer])
    cc_s, conv_state_s = _conv_sample(state_conv[layer], u_s.reshape(dec_batch, dec_seq, conv_ch), *conv_args)
    x1_s, h2_s, route_s = _merge(xs, row2("norm_mix_g"), a_s.reshape(-1, ATTN_WIDTH), cc_s.reshape(-1, conv_ch),
                                 mo_s.reshape(-1, MEM_WIDTH), *merge_w)

    n_p, n_s = xp.shape[0], xs.shape[0]
    h2 = jnp.concatenate([h2_p, h2_s], axis=0)
    expert_ids = jnp.concatenate([route_p[:, :TOP_K], route_s[:, :TOP_K]], axis=0).astype(jnp.int32)
    n_tok = n_p + n_s
    n_tiles = (TOP_K * n_tok + N_EXPERTS * (MOE_TILE - 1)) // MOE_TILE + 1
    src, pos, tile_expert, n_used = _routing_tables(expert_ids, n_tiles)
    gathered = _gather_rows(h2, src, n_used, n_tiles)
    ys = _grouped_mlp(gathered, tile_expert, n_used, p["w_expert_gate"][layer], p["w_expert_up"][layer],
                      p["w_expert_down"][layer])
    y_p = _combine(x1_p, route_p, pos, ys, 0)
    y_s = _combine(x1_s, route_s, pos, ys, n_p)

    state_p = (k_p.reshape(batch, seq, N_KV_HEADS, HEAD_DIM), v_p.reshape(batch, seq, N_KV_HEADS, HEAD_DIM),
               u_p.reshape(batch, seq, conv_ch)[:, seq - (CONV_WIDTH - 1):],
               mem_k_p.reshape(batch, -1, MEM_HEADS, MEM_HEAD_DIM), mem_v_p.reshape(batch, -1, MEM_HEADS, MEM_HEAD_DIM))
    state_s = (k_win_s, v_win_s, conv_state_s)
    return y_p.reshape(batch, seq, d), y_s.reshape(dec_batch, dec_seq, d), state_p, state_s


def kernel(x_prompt, x_sample, mem_prompt, cache_k, cache_v, state_conv, cache_mem_k, cache_mem_v, norm_mix_g, w_in, q_norm_g, k_norm_g, conv_w, conv_b, conv_ln_g, conv_ln_b, mem_norm_g, w_mem_kv, mq_norm_g, mk_norm_g, w_attn_proj, w_conv_proj, w_mem_proj, w_out, norm_ffn_g, w_router_group, b_router_group, w_router_expert, b_router_expert, w_expert_gate, w_expert_up, w_expert_down):
    p = dict(norm_mix_g=norm_mix_g, w_in=w_in, q_norm_g=q_norm_g, k_norm_g=k_norm_g, conv_w=conv_w, conv_b=conv_b,
             conv_ln_g=conv_ln_g, conv_ln_b=conv_ln_b, mem_norm_g=mem_norm_g, w_mem_kv=w_mem_kv, mq_norm_g=mq_norm_g,
             mk_norm_g=mk_norm_g, w_attn_proj=w_attn_proj, w_conv_proj=w_conv_proj, w_mem_proj=w_mem_proj,
             w_out=w_out, norm_ffn_g=norm_ffn_g, w_router_group=w_router_group, b_router_group=b_router_group,
             w_router_expert=w_router_expert, b_router_expert=b_router_expert, w_expert_gate=w_expert_gate,
             w_expert_up=w_expert_up, w_expert_down=w_expert_down)
    depth = w_in.shape[0]
    seq = x_prompt.shape[1]
    assert seq <= max(w for w, _ in DILATED_GROUPS)
    y_p, y_s = x_prompt, x_sample
    states_p, states_s = [], []
    for layer in range(depth):
        y_p, y_s, st_p, st_s = _layer(layer, y_p, y_s, mem_prompt, cache_k, cache_v, state_conv, cache_mem_k,
                                      cache_mem_v, p)
        states_p.append(st_p)
        states_s.append(st_s)
    stack = lambda states, i: jnp.stack([s[i] for s in states], axis=0)
    return (y_p, y_s, stack(states_p, 0), stack(states_p, 1), stack(states_p, 2), stack(states_p, 3),
            stack(states_p, 4), stack(states_s, 0), stack(states_s, 1), stack(states_s, 2))
```

```python
import functools

import jax
import jax.numpy as jnp
import numpy as np
from jax import lax
from jax.experimental import pallas as pl
from jax.experimental.pallas import tpu as pltpu

HEAD_DIM = 128
N_KV_HEADS = 4
DILATED_GROUPS = ((128, 1), (512, 4), (2048, 16))
N_GROUPS = len(DILATED_GROUPS)
N_Q_HEADS = N_GROUPS * N_KV_HEADS
ATTN_WIDTH = N_KV_HEADS * HEAD_DIM
BAND = 128
ATTN_UNROLL = 8
ROPE_THETA = 500000.0
ROT_DIM = HEAD_DIM // 4
CONV_WIDTH = 31
MEM_HEADS = 4
MEM_HEAD_DIM = 128
MEM_WIDTH = MEM_HEADS * MEM_HEAD_DIM
N_EXPERT_GROUPS = 4
EXPERTS_PER_GROUP = 8
N_EXPERTS = N_EXPERT_GROUPS * EXPERTS_PER_GROUP
TOP_K = 2
EPS = 1e-6
NEG = -1e30

LANES = 128
ROW_TILE = 256
MOE_TILE = 256
VMEM_LIMIT = 56 * 1024 * 1024

BF16 = jnp.bfloat16
F32 = jnp.float32


def _params(*sem):
    return pltpu.CompilerParams(dimension_semantics=sem, vmem_limit_bytes=VMEM_LIMIT)


def _dot(a, b):
    return jnp.dot(a, b, preferred_element_type=F32)


def _dot_nt(a, b):
    return lax.dot_general(a, b, (((1,), (1,)), ((), ())), preferred_element_type=F32)


def _dot_tn(a, b):
    return lax.dot_general(a, b, (((0,), (0,)), ((), ())), preferred_element_type=F32)


def _rms(x, g):
    return x * lax.rsqrt(jnp.mean(x * x, axis=-1, keepdims=True) + EPS) * g


def _const_spec(shape):
    nd = len(shape)
    return pl.BlockSpec(shape, lambda *_: (0,) * nd)


def _in_proj_kernel(x_ref, g_ref, w_ref, qg_ref, kg_ref, mqg_ref, rc_ref, ra_ref, rb_ref,
                    q_ref, k_ref, v_ref, u_ref, mq_ref):
    h = _rms(x_ref[...], g_ref[...]).astype(BF16)
    rc, ra, rb = rc_ref[...], ra_ref[...], rb_ref[...]

    def rope(y):
        return y * rc + pltpu.roll(y, LANES - ROT_DIM // 2, 1) * ra + pltpu.roll(y, ROT_DIM // 2, 1) * rb

    col = 0
    zq = _dot(h, w_ref[:, col:col + N_Q_HEADS * HEAD_DIM])
    for j in range(N_Q_HEADS):
        sl = slice(j * HEAD_DIM, (j + 1) * HEAD_DIM)
        q_ref[:, sl] = rope(_rms(zq[:, sl], qg_ref[...])).astype(BF16)
    col += N_Q_HEADS * HEAD_DIM
    tm = x_ref.shape[0]
    zk = _dot(h, w_ref[:, col:col + ATTN_WIDTH])
    for j in range(N_KV_HEADS):
        sl = slice(j * HEAD_DIM, (j + 1) * HEAD_DIM)
        k_ref[pl.ds(j, tm, stride=N_KV_HEADS), :] = rope(_rms(zk[:, sl], kg_ref[...]))
    col += ATTN_WIDTH
    zv = _dot(h, w_ref[:, col:col + ATTN_WIDTH])
    for j in range(N_KV_HEADS):
        v_ref[pl.ds(j, tm, stride=N_KV_HEADS), :] = zv[:, j * HEAD_DIM:(j + 1) * HEAD_DIM]
    col += ATTN_WIDTH
    conv_ch = u_ref.shape[-1]
    za = _dot(h, w_ref[:, col:col + conv_ch])
    zb = _dot(h, w_ref[:, col + conv_ch:col + 2 * conv_ch])
    u_ref[...] = za * jax.nn.sigmoid(zb)
    col += 2 * conv_ch
    zm = _dot(h, w_ref[:, col:col + MEM_WIDTH])
    for j in range(MEM_HEADS):
        sl = slice(j * MEM_HEAD_DIM, (j + 1) * MEM_HEAD_DIM)
        mq_ref[:, sl] = _rms(zm[:, sl], mqg_ref[...]).astype(BF16)


def _in_proj(x, g_mix, w_bf16, q_g, k_g, mq_g, rope_tabs, conv_ch):
    m, d = x.shape
    tm = min(ROW_TILE, m)
    n_tab_blocks = rope_tabs[0].shape[0] // tm
    row = lambda i: (i, 0)
    tab = lambda i: (i % n_tab_blocks, 0)
    ncols = w_bf16.shape[1]
    out_shape = (
        jax.ShapeDtypeStruct((m, N_Q_HEADS * HEAD_DIM), BF16),
        jax.ShapeDtypeStruct((m * N_KV_HEADS, HEAD_DIM), F32),
        jax.ShapeDtypeStruct((m * N_KV_HEADS, HEAD_DIM), F32),
        jax.ShapeDtypeStruct((m, conv_ch), F32),
        jax.ShapeDtypeStruct((m, MEM_WIDTH), BF16),
    )
    return pl.pallas_call(
        _in_proj_kernel,
        grid=(m // tm,),
        in_specs=[
            pl.BlockSpec((tm, d), row),
            _const_spec((1, d)),
            _const_spec((d, ncols)),
            _const_spec((1, HEAD_DIM)), _const_spec((1, HEAD_DIM)), _const_spec((1, MEM_HEAD_DIM)),
            pl.BlockSpec((tm, LANES), tab), pl.BlockSpec((tm, LANES), tab), pl.BlockSpec((tm, LANES), tab),
        ],
        out_specs=tuple(pl.BlockSpec((tm * s.shape[0] // m, s.shape[1]), row) for s in out_shape),
        out_shape=out_shape,
        compiler_params=_params("parallel"),
        name="in_proj",
    )(x, g_mix, w_bf16, q_g, k_g, mq_g, *rope_tabs)


def _rope_tables(pos):
    half = ROT_DIM // 2
    inv_freq = jnp.power(jnp.float32(ROPE_THETA), -jnp.arange(half, dtype=F32) * (2.0 / ROT_DIM))
    ang = pos.astype(F32)[:, None] * inv_freq[None, :]
    cos, sin = jnp.cos(ang), jnp.sin(ang)
    n = pos.shape[0]
    ones = jnp.ones((n, LANES - ROT_DIM), F32)
    zeros = jnp.zeros((n, LANES - half), F32)
    rc = jnp.concatenate([cos, cos, ones], axis=1)
    ra = jnp.concatenate([-sin, zeros], axis=1)
    rb = jnp.concatenate([jnp.zeros((n, half), F32), sin, jnp.zeros((n, LANES - ROT_DIM), F32)], axis=1)
    return rc, ra, rb


def _to_residue_layout(dst, src, classes_src, ratio):
    len_src = src.shape[0] // classes_src
    len_dst = len_src // ratio
    for c_src in range(classes_src):
        for a in range(ratio):
            c = c_src + classes_src * a
            dst[c * len_dst:(c + 1) * len_dst, :] = src[pl.ds(c_src * len_src + a, len_dst, stride=ratio), :]


def _prompt_attn_kernel(q0_ref, q1_ref, q2_ref, k_ref, v_ref, o_ref, perm_ref, tmp_ref, acc_ref, lse_ref):
    head = pl.program_id(1)
    seq = q0_ref.shape[0]
    q_refs = (q0_ref, q1_ref, q2_ref)
    dils = [d for _, d in DILATED_GROUPS]
    for t, ref in ((1, k_ref), (2, v_ref)):
        perm_ref[0, t] = ref[pl.ds(head, seq, stride=N_KV_HEADS), :]
        for g in range(1, N_GROUPS):
            _to_residue_layout(perm_ref.at[g, t], perm_ref.at[g - 1, t], dils[g - 1], dils[g] // dils[g - 1])
    for g in range(1, N_GROUPS):
        tmp_ref[0] = q_refs[g][...].astype(F32)
        for step in range(1, g + 1):
            dst = perm_ref.at[g, 0] if step == g else tmp_ref.at[step % 2]
            _to_residue_layout(dst, tmp_ref.at[(step - 1) % 2], dils[step - 1], dils[step] // dils[step - 1])
    iq = lax.broadcasted_iota(jnp.int32, (BAND, BAND), 0)
    ik = lax.broadcasted_iota(jnp.int32, (BAND, BAND), 1)
    keep_c = iq >= ik
    scale = HEAD_DIM ** -0.5

    for g, dil in enumerate(dils):
        nb = seq // (dil * BAND)

        def body(jj, carry, g=g, dil=dil, nb=nb):
            blocks = []
            for u in range(ATTN_UNROLL):
                j = jj * ATTN_UNROLL + u
                rows = pl.ds(pl.multiple_of(j * BAND, BAND), BAND)
                prev = pl.ds(pl.multiple_of(jnp.maximum(j - 1, 0) * BAND, BAND), BAND)
                q = q0_ref[rows, :] if g == 0 else perm_ref[g, 0, rows, :].astype(BF16)
                s_c = jnp.where(keep_c, _dot_nt(q, perm_ref[g, 1, rows, :].astype(BF16)) * scale, NEG)
                s_p = None
                if nb > 1:
                    keep_p = jnp.logical_and(ik >= iq, j % nb > 0)
                    s_p = jnp.where(keep_p, _dot_nt(q, perm_ref[g, 1, prev, :].astype(BF16)) * scale, NEG)
                blocks.append((j, rows, prev, s_c, s_p))
            probs = []
            for j, rows, prev, s_c, s_p in blocks:
                m = jnp.max(s_c, axis=-1, keepdims=True)
                if nb > 1:
                    m = jnp.maximum(m, jnp.max(s_p, axis=-1, keepdims=True))
                p_c = jnp.exp(s_c - m)
                l = jnp.sum(p_c, axis=-1, keepdims=True)
                p_p = None
                if nb > 1:
                    p_p = jnp.exp(s_p - m)
                    l = l + jnp.sum(p_p, axis=-1, keepdims=True)
                probs.append((m, l, p_c.astype(BF16), None if p_p is None else p_p.astype(BF16)))
            for (j, rows, prev, _, _), (m, l, p_c, p_p) in zip(blocks, probs):
                acc = _dot(p_c, perm_ref[g, 2, rows, :].astype(BF16))
                if nb > 1:
                    acc = acc + _dot(p_p, perm_ref[g, 2, prev, :].astype(BF16))
                out_rows = rows if dil == 1 else pl.ds((j % nb) * (BAND * dil) + j // nb, BAND, stride=dil)
                acc_ref[g, out_rows, :] = acc * (1.0 / l)
                lse_ref[g, out_rows, :] = jnp.broadcast_to(m + jnp.log(l), (BAND, LANES))
            return carry

        assert (dil * nb) % ATTN_UNROLL == 0
        lax.fori_loop(0, dil * nb // ATTN_UNROLL, body, 0)

    def combine(c, carry):
        rows = pl.ds(pl.multiple_of(c * BAND, BAND), BAND)
        lses = [lse_ref[g, rows, :] for g in range(N_GROUPS)]
        mx = functools.reduce(jnp.maximum, lses)
        ws = [jnp.exp(l - mx) for l in lses]
        out = functools.reduce(jnp.add, [w * acc_ref[g, rows, :] for g, w in enumerate(ws)])
        o_ref[rows, :] = (out * (1.0 / functools.reduce(jnp.add, ws))).astype(BF16)
        return carry

    lax.fori_loop(0, seq // BAND, combine, 0)


def _prompt_attention(q, k_flat, v_flat, batch, seq):
    for window, dil in DILATED_GROUPS:
        assert window // dil == BAND and seq % (dil * BAND) == 0
    qspec = lambda g: pl.BlockSpec((seq, HEAD_DIM), lambda b, h: (b, g * N_KV_HEADS + h))
    kvspec = pl.BlockSpec((seq * N_KV_HEADS, HEAD_DIM), lambda b, h: (b, 0))
    return pl.pallas_call(
        _prompt_attn_kernel,
        grid=(batch, N_KV_HEADS),
        in_specs=[qspec(0), qspec(1), qspec(2), kvspec, kvspec],
        out_specs=pl.BlockSpec((seq, HEAD_DIM), lambda b, h: (b, h)),
        out_shape=jax.ShapeDtypeStruct((batch * seq, ATTN_WIDTH), BF16),
        scratch_shapes=[pltpu.VMEM((N_GROUPS, 3, seq, HEAD_DIM), F32), pltpu.VMEM((2, seq, HEAD_DIM), F32),
                        pltpu.VMEM((N_GROUPS, seq, HEAD_DIM), F32), pltpu.VMEM((N_GROUPS, seq, LANES), F32)],
        compiler_params=_params("parallel", "arbitrary"),
        name="prompt_attn",
    )(q, q, q, k_flat, v_flat)


def _mem_kv_kernel(x_ref, g_ref, w_ref, kg_ref, k_ref, v_ref):
    tm = x_ref.shape[0]
    h = _rms(x_ref[...], g_ref[...]).astype(BF16)
    zk = _dot(h, w_ref[:, :MEM_WIDTH])
    zv = _dot(h, w_ref[:, MEM_WIDTH:])
    for j in range(MEM_HEADS):
        sl = slice(j * MEM_HEAD_DIM, (j + 1) * MEM_HEAD_DIM)
        k_ref[pl.ds(j, tm, stride=MEM_HEADS), :] = _rms(zk[:, sl], kg_ref[...])
        v_ref[pl.ds(j, tm, stride=MEM_HEADS), :] = zv[:, sl]


def _mem_kv(mem, g, w_bf16, k_g):
    m, d = mem.shape
    tm = min(ROW_TILE, m)
    row = lambda i: (i, 0)
    shp = jax.ShapeDtypeStruct((m * MEM_HEADS, MEM_HEAD_DIM), F32)
    ospec = pl.BlockSpec((tm * MEM_HEADS, MEM_HEAD_DIM), row)
    return pl.pallas_call(
        _mem_kv_kernel,
        grid=(m // tm,),
        in_specs=[pl.BlockSpec((tm, d), row), _const_spec((1, d)), _const_spec((d, 2 * MEM_WIDTH)),
                  _const_spec((1, MEM_HEAD_DIM))],
        out_specs=(ospec, ospec),
        out_shape=(shp, shp),
        compiler_params=_params("parallel"),
        name="mem_kv",
    )(mem, g, w_bf16, k_g)


def _mem_attn_kernel(q_ref, k_ref, v_ref, o_ref):
    scale = MEM_HEAD_DIM ** -0.5
    n_mem = k_ref.shape[0] // MEM_HEADS
    for h in range(MEM_HEADS):
        sl = slice(h * MEM_HEAD_DIM, (h + 1) * MEM_HEAD_DIM)
        head_rows = pl.ds(h, n_mem, stride=MEM_HEADS)
        s = _dot_nt(q_ref[:, sl], k_ref[head_rows, :].astype(BF16)) * scale
        p = jnp.exp(s - jnp.max(s, axis=-1, keepdims=True))
        l = jnp.sum(p, axis=-1, keepdims=True)
        o_ref[:, sl] = (_dot(p.astype(BF16), v_ref[head_rows, :].astype(BF16)) * (1.0 / l)).astype(BF16)


def _mem_attention(mq, mem_k_flat, mem_v_flat, batch, seq):
    rows = mem_k_flat.shape[0] // batch
    tq = min(512, seq)
    nq = seq // tq
    kspec = pl.BlockSpec((rows, MEM_HEAD_DIM), lambda b, i: (b, 0))
    return pl.pallas_call(
        _mem_attn_kernel,
        grid=(batch, nq),
        in_specs=[pl.BlockSpec((tq, MEM_WIDTH), lambda b, i: (b * nq + i, 0)), kspec, kspec],
        out_specs=pl.BlockSpec((tq, MEM_WIDTH), lambda b, i: (b * nq + i, 0)),
        out_shape=jax.ShapeDtypeStruct((batch * seq, MEM_WIDTH), BF16),
        compiler_params=_params("parallel", "parallel"),
        name="mem_attn",
    )(mq, mem_k_flat, mem_v_flat)


CONV_HALO = 32
CONV_CHUNK = 32
CONV_ACC_ROWS = 128
CONV_TAP_UNROLL = 8


def _ln_swish(c, g, b):
    mu = jnp.mean(c, axis=-1, keepdims=True)
    xc = c - mu
    y = xc * lax.rsqrt(jnp.mean(xc * xc, axis=-1, keepdims=True) + EPS) * g + b
    return y * jax.nn.sigmoid(y)


def _conv_prompt_kernel(halo_ref, u_ref, w_ref, b_ref, g_ref, beta_ref, o_ref, ext_ref, conv_ref, wb_ref):
    tc, ch = u_ref.shape
    first = pl.program_id(1) == 0
    lane_tiles = [slice(j * LANES, (j + 1) * LANES) for j in range(ch // LANES)]
    for j, sl in enumerate(lane_tiles):
        halo = jnp.where(first, 0.0, halo_ref[:, sl])
        for dup in range(2):
            ext_ref[j, pl.ds(dup, CONV_HALO, stride=2), :] = halo
            ext_ref[j, pl.ds(2 * CONV_HALO + dup, tc, stride=2), :] = u_ref[:, sl]
    for w in range(CONV_WIDTH):
        wb_ref[w] = jnp.broadcast_to(w_ref[w:w + 1, :], (8, ch))
    lead = CONV_HALO - (CONV_WIDTH - 1)
    rows = CONV_ACC_ROWS
    for j, sl in enumerate(lane_tiles):
        for r0 in range(0, tc, rows):
            def tap(w, acc, j=j, sl=sl, r0=r0):
                win = ext_ref[j, pl.ds(2 * (r0 + lead + w), rows, stride=2), :].reshape(rows // 8, 8, LANES)
                return acc + wb_ref[w, :, sl][None] * win

            acc = lax.fori_loop(0, CONV_WIDTH, tap, jnp.zeros((rows // 8, 8, LANES), F32) + b_ref[:, sl],
                                unroll=CONV_TAP_UNROLL)
            conv_ref[r0:r0 + rows, sl] = acc.reshape(rows, LANES)
    for c0 in range(0, tc, CONV_CHUNK):
        o_ref[c0:c0 + CONV_CHUNK, :] = _ln_swish(conv_ref[c0:c0 + CONV_CHUNK, :], g_ref[...], beta_ref[...]).astype(BF16)


def _conv_prompt(u, conv_w, conv_b, ln_g, ln_b, batch, seq):
    ch = u.shape[1]
    tc = min(ROW_TILE, seq)
    nt = seq // tc
    ratio = tc // CONV_HALO
    return pl.pallas_call(
        _conv_prompt_kernel,
        grid=(batch, nt),
        in_specs=[
            pl.BlockSpec((CONV_HALO, ch), lambda b, i: (jnp.maximum((b * nt + i) * ratio - 1, 0), 0)),
            pl.BlockSpec((tc, ch), lambda b, i: (b * nt + i, 0)),
            _const_spec((CONV_WIDTH, ch)), _const_spec((1, ch)), _const_spec((1, ch)), _const_spec((1, ch)),
        ],
        out_specs=pl.BlockSpec((tc, ch), lambda b, i: (b * nt + i, 0)),
        out_shape=jax.ShapeDtypeStruct((batch * seq, ch), BF16),
        scratch_shapes=[pltpu.VMEM((ch // LANES, 2 * (CONV_HALO + tc), LANES), F32), pltpu.VMEM((tc, ch), F32),
                        pltpu.VMEM((CONV_WIDTH, 8, ch), F32)],
        compiler_params=_params("parallel", "parallel"),
        name="conv_prompt",
    )(u, u, conv_w, conv_b, ln_g, ln_b)


def _conv_sample_kernel(state_ref, new_ref, w_ref, b_ref, g_ref, beta_ref, o_ref, state_out_ref, ext_ref):
    nb, ctx, ch = state_ref.shape
    t = new_ref.shape[1]
    ext_ref[:, 0:ctx, :] = state_ref[...]
    ext_ref[:, ctx:ctx + t, :] = new_ref[...]
    acc = jnp.zeros((nb, t, ch), F32) + b_ref[...]
    for w in range(CONV_WIDTH):
        acc = acc + w_ref[w:w + 1, :] * ext_ref[:, w:w + t, :]
    o_ref[...] = _ln_swish(acc, g_ref[...], beta_ref[...])
    state_out_ref[...] = ext_ref[:, t:t + ctx, :]


def _conv_sample(state, u_new, conv_w, conv_b, ln_g, ln_b):
    batch, ctx, ch = state.shape
    t = u_new.shape[1]
    nb = 8
    blk = lambda n: pl.BlockSpec((nb, n, ch), lambda i: (i, 0, 0))
    return pl.pallas_call(
        _conv_sample_kernel,
        grid=(batch // nb,),
        in_specs=[blk(ctx), blk(t), _const_spec((CONV_WIDTH, ch)), _const_spec((1, ch)), _const_spec((1, ch)),
                  _const_spec((1, ch))],
        out_specs=(blk(t), blk(ctx)),
        out_shape=(jax.ShapeDtypeStruct((batch, t, ch), F32), jax.ShapeDtypeStruct((batch, ctx, ch), F32)),
        scratch_shapes=[pltpu.VMEM((nb, ctx + t + 6, ch), F32)],
        compiler_params=_params("parallel"),
        name="conv_sample",
    )(state, u_new, conv_w, conv_b, ln_g, ln_b)


GROUP_LANES = 16


def _spread_groups(vec, combine):
    t = combine(combine(vec, pltpu.roll(vec, LANES - GROUP_LANES, 1)), pltpu.roll(vec, LANES - 2 * GROUP_LANES, 1))
    lane = lax.broadcasted_iota(jnp.int32, vec.shape, 1)
    return jnp.where(lane < GROUP_LANES, t,
                     jnp.where(lane < 2 * GROUP_LANES, pltpu.roll(t, GROUP_LANES, 1), pltpu.roll(t, 2 * GROUP_LANES, 1)))


SLAB_POS = DILATED_GROUPS[-1][1]
SLAB_ROWS = SLAB_POS * N_KV_HEADS
TAIL_POS = max(w for w, d in DILATED_GROUPS if d < SLAB_POS)


def _sample_attn_kernel(kc_ref, vc_ref, kn_ref, vn_ref, wq_ref, bias_d_ref, bias_t_ref, bias_n_ref, mk_ref, mv_ref,
                        wm_ref, bias_m_ref, kw_ref, vw_ref, a_ref, m_ref, kx_ref, vx_ref):
    n_slab = kc_ref.shape[1]
    new_rows = kn_ref.shape[1]
    t = new_rows // N_KV_HEADS
    tail_slabs = TAIL_POS // SLAB_POS
    n_used = N_GROUPS * GROUP_LANES
    lane = lax.broadcasted_iota(jnp.int32, (1, LANES), 1)

    for src, new, dst in ((kc_ref, kn_ref, kw_ref), (vc_ref, vn_ref, vw_ref)):
        dst[0, :, 0:SLAB_ROWS - new_rows, :] = src[0, :, new_rows:SLAB_ROWS, :]
        dst[0, 0:n_slab - 1, SLAB_ROWS - new_rows:SLAB_ROWS, :] = src[0, 1:n_slab, 0:new_rows, :]
        dst[0, n_slab - 1, SLAB_ROWS - new_rows:SLAB_ROWS, :] = new[0]

    kx_ref[...] = jnp.zeros(kx_ref.shape, F32)
    vx_ref[...] = jnp.zeros(vx_ref.shape, F32)
    kx_ref[0:new_rows, :] = kn_ref[0]
    vx_ref[0:new_rows, :] = vn_ref[0]

    def dilated(ref):
        return ref[0, :, 0:new_rows, :].reshape(n_slab * new_rows, HEAD_DIM).astype(BF16)

    def tail(ref):
        return ref[0, n_slab - tail_slabs:n_slab, :, :].reshape(tail_slabs * SLAB_ROWS, HEAD_DIM).astype(BF16)

    scale = HEAD_DIM ** -0.5
    wq = wq_ref[0]
    s_d = _dot(dilated(kc_ref), wq) * scale + bias_d_ref[...]
    s_t = _dot(tail(kc_ref), wq) * scale + bias_t_ref[...]
    s_n = _dot(kx_ref[...].astype(BF16), wq) * scale + bias_n_ref[...]
    col_max = lambda s: jnp.max(s, axis=0, keepdims=True)
    col_sum = lambda p: jnp.sum(p, axis=0, keepdims=True)
    m_col = jnp.maximum(jnp.maximum(col_max(s_d), col_max(s_t)), col_max(s_n))
    m_joint = jnp.where(lane < n_used, _spread_groups(m_col, jnp.maximum), 0.0)
    p_d, p_t, p_n = jnp.exp(s_d - m_joint), jnp.exp(s_t - m_joint), jnp.exp(s_n - m_joint)
    l_col = col_sum(p_d) + col_sum(p_t) + col_sum(p_n)
    inv = 1.0 / jnp.where(lane < n_used, _spread_groups(l_col, jnp.add), 1.0)
    o = (_dot_tn((p_d * inv).astype(BF16), dilated(vc_ref)) + _dot_tn((p_t * inv).astype(BF16), tail(vc_ref))
         + _dot_tn((p_n * inv).astype(BF16), vx_ref[...].astype(BF16)))
    for h in range(N_KV_HEADS):
        r = h * t
        a_ref[0, :, h * HEAD_DIM:(h + 1) * HEAD_DIM] = functools.reduce(
            jnp.add, [o[g * GROUP_LANES + r:g * GROUP_LANES + r + t, :] for g in range(N_GROUPS)])

    sm = _dot(mk_ref[0].astype(BF16), wm_ref[0]) * (MEM_HEAD_DIM ** -0.5) + bias_m_ref[...]
    pm = jnp.exp(sm - jnp.where(lane < MEM_HEADS * t, col_max(sm), 0.0))
    lm = jnp.where(lane < MEM_HEADS * t, col_sum(pm), 1.0)
    om = _dot_tn((pm * (1.0 / lm)).astype(BF16), mv_ref[0].astype(BF16))
    for h in range(MEM_HEADS):
        m_ref[0, :, h * MEM_HEAD_DIM:(h + 1) * MEM_HEAD_DIM] = om[h * t:(h + 1) * t, :]


def _sample_masks(cache_len, t, n_mem):
    col = np.arange(LANES)[None, :]
    g, c_head, c_tok = col // GROUP_LANES, (col % GROUP_LANES) // t, (col % GROUP_LANES) % t
    used = (col < N_GROUPS * GROUP_LANES) & (col % GROUP_LANES < N_KV_HEADS * t)
    pad = [1] * (LANES // GROUP_LANES - N_GROUPS)
    dil = np.array([d for _, d in DILATED_GROUPS] + pad)[g]
    win = np.array([w for w, _ in DILATED_GROUPS] + pad)[g]
    sparse = dil >= SLAB_POS

    def keep(pos, head, group_sel):
        dist = cache_len + c_tok - pos
        return used & group_sel & (head == c_head) & (dist >= 0) & (dist % dil == 0) & (dist <= win)

    n_slab = cache_len // SLAB_POS
    y = np.arange(n_slab * t * N_KV_HEADS)[:, None]
    keep_d = keep((y // (t * N_KV_HEADS)) * SLAB_POS + (y % (t * N_KV_HEADS)) // N_KV_HEADS, y % N_KV_HEADS, sparse)
    x = np.arange(TAIL_POS * N_KV_HEADS)[:, None]
    keep_t = keep(cache_len - TAIL_POS + x // N_KV_HEADS, x % N_KV_HEADS, ~sparse)
    z = np.arange(LANES)[:, None]
    keep_n = keep(cache_len + z // N_KV_HEADS, z % N_KV_HEADS, True) & (z < t * N_KV_HEADS)
    w = np.arange(n_mem * MEM_HEADS)[:, None]
    keep_m = (col < MEM_HEADS * t) & (w % MEM_HEADS == col // t)
    return tuple(jnp.asarray(np.where(k, 0.0, NEG), F32) for k in (keep_d, keep_t, keep_n, keep_m))


def _query_columns(q, batch, t, n_groups):
    qt = q.reshape(batch, t, n_groups, N_KV_HEADS, HEAD_DIM).transpose(0, 4, 2, 3, 1)
    qt = qt.reshape(batch, HEAD_DIM, n_groups, N_KV_HEADS * t)
    qt = jnp.pad(qt, ((0, 0), (0, 0), (0, 0), (0, GROUP_LANES - N_KV_HEADS * t)))
    qt = qt.reshape(batch, HEAD_DIM, n_groups * GROUP_LANES)
    return jnp.pad(qt, ((0, 0), (0, 0), (0, LANES - n_groups * GROUP_LANES)))


def _sample_attention(q, k_new, v_new, mq, cache_k, cache_v, mem_k, mem_v):
    batch, cache_len = cache_k.shape[0], cache_k.shape[1]
    new_rows = k_new.shape[0] // batch
    t = new_rows // N_KV_HEADS
    n_mem = mem_k.shape[1]
    assert N_KV_HEADS * t <= GROUP_LANES and t <= SLAB_POS and new_rows % 8 == 0
    assert cache_len % SLAB_POS == 0 and cache_len >= max(w for w, _ in DILATED_GROUPS)
    assert all(d == SLAB_POS or w <= TAIL_POS for w, d in DILATED_GROUPS)
    n_slab = cache_len // SLAB_POS
    wq = _query_columns(q, batch, t, N_GROUPS)
    wm = _query_columns(mq, batch, t, 1)
    bias_d, bias_t, bias_n, bias_m = _sample_masks(cache_len, t, n_mem)
    per_b = lambda *shape: pl.BlockSpec((1,) + shape, lambda b: (b,) + (0,) * len(shape))
    slabs = lambda c: c.reshape(batch, n_slab, SLAB_ROWS, HEAD_DIM)
    mem_rows = n_mem * MEM_HEADS
    win = jax.ShapeDtypeStruct((batch, n_slab, SLAB_ROWS, HEAD_DIM), F32)
    k_win, v_win, a, m = pl.pallas_call(
        _sample_attn_kernel,
        grid=(batch,),
        in_specs=[
            per_b(n_slab, SLAB_ROWS, HEAD_DIM), per_b(n_slab, SLAB_ROWS, HEAD_DIM),
            per_b(new_rows, HEAD_DIM), per_b(new_rows, HEAD_DIM), per_b(HEAD_DIM, LANES),
            _const_spec(bias_d.shape), _const_spec(bias_t.shape), _const_spec(bias_n.shape),
            per_b(mem_rows, MEM_HEAD_DIM), per_b(mem_rows, MEM_HEAD_DIM), per_b(MEM_HEAD_DIM, LANES),
            _const_spec(bias_m.shape),
        ],
        out_specs=(per_b(n_slab, SLAB_ROWS, HEAD_DIM), per_b(n_slab, SLAB_ROWS, HEAD_DIM), per_b(t, ATTN_WIDTH),
                   per_b(t, MEM_WIDTH)),
        out_shape=(win, win, jax.ShapeDtypeStruct((batch, t, ATTN_WIDTH), F32),
                   jax.ShapeDtypeStruct((batch, t, MEM_WIDTH), F32)),
        scratch_shapes=[pltpu.VMEM((LANES, HEAD_DIM), F32), pltpu.VMEM((LANES, HEAD_DIM), F32)],
        compiler_params=_params("parallel"),
        name="sample_attn",
    )(slabs(cache_k), slabs(cache_v), k_new.reshape(batch, new_rows, HEAD_DIM), v_new.reshape(batch, new_rows, HEAD_DIM),
      wq, bias_d, bias_t, bias_n, mem_k.reshape(batch, mem_rows, MEM_HEAD_DIM),
      mem_v.reshape(batch, mem_rows, MEM_HEAD_DIM), wm, bias_m)
    return k_win.reshape(cache_k.shape), v_win.reshape(cache_v.shape), a, m


ROUTE_GROUP_LANE0 = 0
ROUTE_EXPERT_LANE0 = N_EXPERT_GROUPS


def _route(logits):
    lane = lax.broadcasted_iota(jnp.int32, logits.shape, 1).astype(F32)
    big = float(LANES)

    def masked_softmax(keep):
        z = jnp.where(keep, logits, NEG)
        e = jnp.where(keep, jnp.exp(z - jnp.max(z, axis=-1, keepdims=True)), 0.0)
        return e / jnp.sum(e, axis=-1, keepdims=True)

    def first_argmax(vals, keep):
        top = jnp.max(jnp.where(keep, vals, -1.0), axis=-1, keepdims=True)
        idx = jnp.min(jnp.where(jnp.logical_and(keep, vals == top), lane, big), axis=-1, keepdims=True)
        return top, idx

    is_group = lane < N_EXPERT_GROUPS
    pg = masked_softmax(is_group)
    pg_top, g_idx = first_argmax(pg, is_group)
    lo = ROUTE_EXPERT_LANE0 + g_idx * EXPERTS_PER_GROUP
    in_group = jnp.logical_and(lane >= lo, lane < lo + EXPERTS_PER_GROUP)
    pe = masked_softmax(in_group)
    p1, i1 = first_argmax(pe, in_group)
    p2, i2 = first_argmax(pe, jnp.logical_and(in_group, lane != i1))
    denom = p1 + p2
    w1 = pg_top * p1 / denom
    w2 = pg_top * p2 / denom
    return lane, i1 - ROUTE_EXPERT_LANE0, i2 - ROUTE_EXPERT_LANE0, w1, w2


def _merge_kernel(x_ref, gmix_ref, ao_ref, cc_ref, mo_ref, wgate_ref, wa_ref, wc_ref, wm_ref, wo_ref, gffn_ref, wr_ref,
                  br_ref, cnt_in_ref, x1_ref, h2_ref, route_ref, cnt_out_ref, cnt_ref):
    @pl.when(pl.program_id(0) == 0)
    def _():
        cnt_ref[...] = cnt_in_ref[...]

    d = x_ref.shape[1]
    x = x_ref[...]
    h = _rms(x, gmix_ref[...]).astype(BF16)
    a = _dot(ao_ref[...].astype(BF16), wa_ref[...])
    c = _dot(cc_ref[...].astype(BF16), wc_ref[...])
    m = _dot(mo_ref[...].astype(BF16), wm_ref[...])
    z = jax.nn.sigmoid(_dot(h, wgate_ref[:, 0:d])) * a
    z = z + jax.nn.sigmoid(_dot(h, wgate_ref[:, d:2 * d])) * c
    z = z + jax.nn.sigmoid(_dot(h, wgate_ref[:, 2 * d:3 * d])) * m
    x1 = x + _dot(z.astype(BF16), wo_ref[...])
    x1_ref[...] = x1
    h2 = _rms(x1, gffn_ref[...])
    h2_ref[...] = h2
    logits = jnp.dot(h2, wr_ref[...], precision=lax.Precision.HIGHEST, preferred_element_type=F32) + br_ref[...]
    lane, e1, e2, w1, w2 = _route(logits)

    tm = x_ref.shape[0]
    hit1 = jnp.where(lane == e1, 1.0, 0.0)
    hit2 = jnp.where(lane == e2, 1.0, 0.0)
    hits = hit1 + hit2
    earlier = (lax.broadcasted_iota(jnp.int32, (tm, tm), 1) < lax.broadcasted_iota(jnp.int32, (tm, tm), 0))
    before = _dot(jnp.where(earlier, 1.0, 0.0).astype(BF16), hits.astype(BF16)) + cnt_ref[...]
    rank1 = jnp.sum(hit1 * before, axis=-1, keepdims=True)
    rank2 = jnp.sum(hit2 * before, axis=-1, keepdims=True)
    cnt_ref[...] = cnt_ref[...] + jnp.sum(hits, axis=0, keepdims=True)
    cnt_out_ref[...] = cnt_ref[...]
    route = jnp.zeros(logits.shape, F32)
    for i, val in enumerate((e1, e2, w1, w2, rank1, rank2)):
        route = jnp.where(lane == i, val, route)
    route_ref[...] = route


ROUTE_E, ROUTE_W, ROUTE_RANK = 0, 2, 4


def _merge(x, g_mix, ao, cc, mo, w_gate, w_a, w_c, w_m, w_o, g_ffn, w_r, b_r, counts):
    m, d = x.shape
    tm = min(ROW_TILE, m)
    row = lambda i: (i, 0)
    rows = lambda arr: pl.BlockSpec((tm, arr.shape[1]), row)
    ins = [x, g_mix, ao, cc, mo, w_gate, w_a, w_c, w_m, w_o, g_ffn, w_r, b_r, counts]
    specs = ([rows(x), _const_spec(g_mix.shape), rows(ao), rows(cc), rows(mo)]
             + [_const_spec(a.shape) for a in (w_gate, w_a, w_c, w_m, w_o, g_ffn, w_r, b_r, counts)])
    out_shape = (jax.ShapeDtypeStruct((m, d), F32), jax.ShapeDtypeStruct((m, d), F32),
                 jax.ShapeDtypeStruct((m, LANES), F32), jax.ShapeDtypeStruct((1, LANES), F32))
    return pl.pallas_call(
        _merge_kernel,
        grid=(m // tm,),
        in_specs=specs,
        out_specs=(rows(out_shape[0]), rows(out_shape[1]), rows(out_shape[2]), _const_spec((1, LANES))),
        out_shape=out_shape,
        scratch_shapes=[pltpu.VMEM((1, LANES), F32)],
        compiler_params=_params("arbitrary"),
        name="merge",
    )(*ins)


def _routing_tables(counts, routes, n_tiles):
    cnt = counts[0, :N_EXPERTS].astype(jnp.int32)
    tiles = (cnt + MOE_TILE - 1) // MOE_TILE
    tile_end = jnp.cumsum(tiles)
    row_start = (tile_end - tiles) * MOE_TILE
    n_used = tile_end[-1]
    tile_ids = jnp.minimum(jnp.arange(n_tiles, dtype=jnp.int32), n_used - 1)
    tile_expert = jnp.sum(tile_ids[:, None] >= tile_end[None, :], axis=1).astype(jnp.int32)
    ids = routes[:, ROUTE_E:ROUTE_E + TOP_K].astype(jnp.int32)
    start_of = jnp.sum(jnp.where(ids[:, :, None] == jnp.arange(N_EXPERTS)[None, None, :], row_start[None, None, :], 0),
                       axis=-1)
    pos = start_of + routes[:, ROUTE_RANK:ROUTE_RANK + TOP_K].astype(jnp.int32)
    return pos.reshape(-1), tile_expert, n_used.reshape(1).astype(jnp.int32)


def _dispatch_kernel(pos_ref, h_ref, xs_in, xs_out, sem, *, row0):
    del xs_in
    tm = h_ref.shape[0]
    base = (row0 + pl.program_id(0) * tm) * TOP_K

    def issue(j, carry):
        for k in range(TOP_K):
            slot = pos_ref[base + j * TOP_K + k]
            pltpu.make_async_copy(h_ref.at[pl.ds(j, 1)], xs_out.at[pl.ds(slot, 1)], sem).start()
        return carry

    lax.fori_loop(0, tm, issue, 0, unroll=8)
    for k in range(TOP_K):
        pltpu.make_async_copy(h_ref, xs_out.at[pl.ds(0, tm)], sem).wait()


def _dispatch(h2, pos, xs, row0):
    m, d = h2.shape
    tm = min(ROW_TILE, m)
    return pl.pallas_call(
        functools.partial(_dispatch_kernel, row0=row0),
        grid_spec=pltpu.PrefetchScalarGridSpec(
            num_scalar_prefetch=1,
            grid=(m // tm,),
            in_specs=[pl.BlockSpec((tm, d), lambda i, *_: (i, 0)), pl.BlockSpec(memory_space=pl.ANY)],
            out_specs=pl.BlockSpec(memory_space=pl.ANY),
            scratch_shapes=[pltpu.SemaphoreType.DMA(())],
        ),
        out_shape=jax.ShapeDtypeStruct(xs.shape, xs.dtype),
        input_output_aliases={2: 0},
        compiler_params=_params("arbitrary"),
        name="moe_dispatch",
    )(pos, h2, xs)


def _gmm_kernel(te_ref, nused_ref, x_ref, wg_ref, wu_ref, wd_ref, o_ref, wg_s, wu_s, wd_s):
    t = pl.program_id(0)
    used = t < nused_ref[0]
    new_expert = jnp.logical_or(t == 0, te_ref[t] != te_ref[jnp.maximum(t - 1, 0)])

    @pl.when(jnp.logical_and(used, new_expert))
    def _():
        wg_s[...] = wg_ref[0].astype(BF16)
        wu_s[...] = wu_ref[0].astype(BF16)
        wd_s[...] = wd_ref[0].astype(BF16)

    @pl.when(used)
    def _():
        x = x_ref[...].astype(BF16)
        gate = _dot(x, wg_s[...])
        up = _dot(x, wu_s[...])
        hid = gate * jax.nn.sigmoid(gate) * up
        o_ref[...] = _dot(hid.astype(BF16), wd_s[...])

    @pl.when(jnp.logical_not(used))
    def _():
        o_ref[...] = jnp.zeros(o_ref.shape, F32)


def _grouped_mlp(xs, tile_expert, n_used, w_gate, w_up, w_down):
    n_tiles = xs.shape[0] // MOE_TILE
    d, ff = w_gate.shape[1], w_gate.shape[2]
    wspec = lambda a, b: pl.BlockSpec((1, a, b), lambda t, te, nu: (te[t], 0, 0))
    return pl.pallas_call(
        _gmm_kernel,
        grid_spec=pltpu.PrefetchScalarGridSpec(
            num_scalar_prefetch=2,
            grid=(n_tiles,),
            in_specs=[pl.BlockSpec((MOE_TILE, d), lambda t, te, nu: (jnp.minimum(t, nu[0] - 1), 0)),
                      wspec(d, ff), wspec(d, ff), wspec(ff, d)],
            out_specs=pl.BlockSpec((MOE_TILE, d), lambda t, *_: (t, 0)),
            scratch_shapes=[pltpu.VMEM((d, ff), BF16), pltpu.VMEM((d, ff), BF16), pltpu.VMEM((ff, d), BF16)],
        ),
        out_shape=jax.ShapeDtypeStruct((n_tiles * MOE_TILE, d), F32),
        compiler_params=_params("arbitrary"),
        name="moe_gmm",
    )(tile_expert, n_used, xs, w_gate, w_up, w_down)


def _combine_kernel(pos_ref, x1_ref, route_ref, ys_hbm, y_ref, buf_ref, sem, *, row0):
    tm = x1_ref.shape[0]
    base = (row0 + pl.program_id(0) * tm) * TOP_K

    def issue(j, carry):
        for k in range(TOP_K):
            slot = pos_ref[base + j * TOP_K + k]
            pltpu.make_async_copy(ys_hbm.at[pl.ds(slot, 1)], buf_ref.at[k, pl.ds(j, 1)], sem).start()
        return carry

    lax.fori_loop(0, tm, issue, 0, unroll=4)
    for k in range(TOP_K):
        pltpu.make_async_copy(ys_hbm.at[pl.ds(0, tm)], buf_ref.at[k], sem).wait()
    route = route_ref[...]
    y_ref[...] = x1_ref[...] + route[:, 2:3] * buf_ref[0] + route[:, 3:4] * buf_ref[1]


def _combine(x1, route, pos, ys, row0):
    m, d = x1.shape
    tm = min(ROW_TILE, m)
    row = lambda i, *_: (i, 0)
    return pl.pallas_call(
        functools.partial(_combine_kernel, row0=row0),
        grid_spec=pltpu.PrefetchScalarGridSpec(
            num_scalar_prefetch=1,
            grid=(m // tm,),
            in_specs=[pl.BlockSpec((tm, d), row), pl.BlockSpec((tm, LANES), row), pl.BlockSpec(memory_space=pl.ANY)],
            out_specs=pl.BlockSpec((tm, d), row),
            scratch_shapes=[pltpu.VMEM((TOP_K, tm, d), F32), pltpu.SemaphoreType.DMA(())],
        ),
        out_shape=jax.ShapeDtypeStruct((m, d), F32),
        compiler_params=_params("arbitrary"),
        name="moe_combine",
    )(pos, x1, route, ys)


def _layer(layer, x_prompt, x_sample, mem_prompt, cache_k, cache_v, state_conv, cache_mem_k, cache_mem_v, p):
    batch, seq, d = x_prompt.shape
    dec_batch, dec_seq, _ = x_sample.shape
    conv_ch = p["conv_w"].shape[-1]
    n_in = N_Q_HEADS * HEAD_DIM + 2 * ATTN_WIDTH + 2 * conv_ch + MEM_WIDTH
    past_len = cache_k.shape[2]

    row2 = lambda name: p[name][layer][None, :]
    w_in = p["w_in"][layer]
    w_main = w_in[:, :n_in].astype(BF16)
    w_gate = w_in[:, n_in:].astype(BF16)
    w_a, w_c, w_m, w_o = (p[n][layer].astype(BF16) for n in ("w_attn_proj", "w_conv_proj", "w_mem_proj", "w_out"))
    w_router = jnp.concatenate(
        [p["w_router_group"][layer], p["w_router_expert"][layer].transpose(1, 0, 2).reshape(d, N_EXPERTS)], axis=1)
    w_router = jnp.pad(w_router, ((0, 0), (0, LANES - w_router.shape[1])))
    b_router = jnp.concatenate([p["b_router_group"][layer], p["b_router_expert"][layer].reshape(-1)])
    b_router = jnp.pad(b_router, (0, LANES - b_router.shape[0]))[None, :]
    conv_args = (p["conv_w"][layer], row2("conv_b"), row2("conv_ln_g"), row2("conv_ln_b"))
    merge_w = (w_gate, w_a, w_c, w_m, w_o, row2("norm_ffn_g"), w_router, b_router)

    xp = x_prompt.reshape(batch * seq, d)
    tabs_p = _rope_tables(jnp.arange(seq, dtype=jnp.int32))
    q_p, k_p, v_p, u_p, mq_p = _in_proj(xp, row2("norm_mix_g"), w_main, row2("q_norm_g"), row2("k_norm_g"),
                                        row2("mq_norm_g"), tabs_p, conv_ch)
    ao_p = _prompt_attention(q_p, k_p, v_p, batch, seq)
    mem_k_p, mem_v_p = _mem_kv(mem_prompt.reshape(-1, d), row2("mem_norm_g"), p["w_mem_kv"][layer].astype(BF16),
                               row2("mk_norm_g"))
    mo_p = _mem_attention(mq_p, mem_k_p, mem_v_p, batch, seq)
    cc_p = _conv_prompt(u_p, *conv_args, batch, seq)
    x1_p, h2_p, route_p, counts = _merge(xp, row2("norm_mix_g"), ao_p, cc_p, mo_p, *merge_w,
                                         jnp.zeros((1, LANES), F32))

    xs = x_sample.reshape(dec_batch * dec_seq, d)
    tabs_s = _rope_tables(jnp.tile(past_len + jnp.arange(dec_seq, dtype=jnp.int32), dec_batch))
    q_s, k_s, v_s, u_s, mq_s = _in_proj(xs, row2("norm_mix_g"), w_main, row2("q_norm_g"), row2("k_norm_g"),
                                        row2("mq_norm_g"), tabs_s, conv_ch)
    k_win_s, v_win_s, a_s, mo_s = _sample_attention(q_s, k_s, v_s, mq_s, cache_k[layer], cache_v[layer],
                                                    cache_mem_k[layer], cache_mem_v[layer])
    cc_s, conv_state_s = _conv_sample(state_conv[layer], u_s.reshape(dec_batch, dec_seq, conv_ch), *conv_args)
    x1_s, h2_s, route_s, counts = _merge(xs, row2("norm_mix_g"), a_s.reshape(-1, ATTN_WIDTH),
                                         cc_s.reshape(-1, conv_ch), mo_s.reshape(-1, MEM_WIDTH), *merge_w, counts)

    n_p, n_s = xp.shape[0], xs.shape[0]
    n_tok = n_p + n_s
    n_tiles = (TOP_K * n_tok + N_EXPERTS * (MOE_TILE - 1)) // MOE_TILE + 1
    pos, tile_expert, n_used = _routing_tables(counts, jnp.concatenate([route_p, route_s], axis=0), n_tiles)
    slots = _dispatch(h2_p, pos, jnp.zeros((n_tiles * MOE_TILE, d), F32), 0)
    slots = _dispatch(h2_s, pos, slots, n_p)
    ys = _grouped_mlp(slots, tile_expert, n_used, p["w_expert_gate"][layer], p["w_expert_up"][layer],
                      p["w_expert_down"][layer])
    y_p = _combine(x1_p, route_p, pos, ys, 0)
    y_s = _combine(x1_s, route_s, pos, ys, n_p)

    state_p = (k_p.reshape(batch, seq, N_KV_HEADS, HEAD_DIM), v_p.reshape(batch, seq, N_KV_HEADS, HEAD_DIM),
               u_p.reshape(batch, seq, conv_ch)[:, seq - (CONV_WIDTH - 1):],
               mem_k_p.reshape(batch, -1, MEM_HEADS, MEM_HEAD_DIM), mem_v_p.reshape(batch, -1, MEM_HEADS, MEM_HEAD_DIM))
    state_s = (k_win_s, v_win_s, conv_state_s)
    return y_p.reshape(batch, seq, d), y_s.reshape(dec_batch, dec_seq, d), state_p, state_s


def kernel(x_prompt, x_sample, mem_prompt, cache_k, cache_v, state_conv, cache_mem_k, cache_mem_v, norm_mix_g, w_in, q_norm_g, k_norm_g, conv_w, conv_b, conv_ln_g, conv_ln_b, mem_norm_g, w_mem_kv, mq_norm_g, mk_norm_g, w_attn_proj, w_conv_proj, w_mem_proj, w_out, norm_ffn_g, w_router_group, b_router_group, w_router_expert, b_router_expert, w_expert_gate, w_expert_up, w_expert_down):
    p = dict(norm_mix_g=norm_mix_g, w_in=w_in, q_norm_g=q_norm_g, k_norm_g=k_norm_g, conv_w=conv_w, conv_b=conv_b,
             conv_ln_g=conv_ln_g, conv_ln_b=conv_ln_b, mem_norm_g=mem_norm_g, w_mem_kv=w_mem_kv, mq_norm_g=mq_norm_g,
             mk_norm_g=mk_norm_g, w_attn_proj=w_attn_proj, w_conv_proj=w_conv_proj, w_mem_proj=w_mem_proj,
             w_out=w_out, norm_ffn_g=norm_ffn_g, w_router_group=w_router_group, b_router_group=b_router_group,
             w_router_expert=w_router_expert, b_router_expert=b_router_expert, w_expert_gate=w_expert_gate,
             w_expert_up=w_expert_up, w_expert_down=w_expert_down)
    depth = w_in.shape[0]
    seq = x_prompt.shape[1]
    assert seq <= max(w for w, _ in DILATED_GROUPS)
    y_p, y_s = x_prompt, x_sample
    states_p, states_s = [], []
    for layer in range(depth):
        y_p, y_s, st_p, st_s = _layer(layer, y_p, y_s, mem_prompt, cache_k, cache_v, state_conv, cache_mem_k,
                                      cache_mem_v, p)
        states_p.append(st_p)
        states_s.append(st_s)
    stack = lambda states, i: jnp.stack([s[i] for s in states], axis=0)
    return (y_p, y_s, stack(states_p, 0), stack(states_p, 1), stack(states_p, 2), stack(states_p, 3),
            stack(states_p, 4), stack(states_s, 0), stack(states_s, 1), stack(states_s, 2))
```

```python
import functools

import jax
import jax.numpy as jnp
import numpy as np
from jax import lax
from jax.experimental import pallas as pl
from jax.experimental.pallas import tpu as pltpu

HEAD_DIM = 128
N_KV_HEADS = 4
DILATED_GROUPS = ((128, 1), (512, 4), (2048, 16))
N_GROUPS = len(DILATED_GROUPS)
N_Q_HEADS = N_GROUPS * N_KV_HEADS
ATTN_WIDTH = N_KV_HEADS * HEAD_DIM
BAND = 128
ATTN_UNROLL = 8
ROPE_THETA = 500000.0
ROT_DIM = HEAD_DIM // 4
CONV_WIDTH = 31
MEM_HEADS = 4
MEM_HEAD_DIM = 128
MEM_WIDTH = MEM_HEADS * MEM_HEAD_DIM
N_EXPERT_GROUPS = 4
EXPERTS_PER_GROUP = 8
N_EXPERTS = N_EXPERT_GROUPS * EXPERTS_PER_GROUP
TOP_K = 2
EPS = 1e-6
NEG = -1e30

LANES = 128
ROW_TILE = 256
MERGE_TILE = 512
IN_PROJ_TILE = 256
MOE_TILE = 256
VMEM_LIMIT = 56 * 1024 * 1024

BF16 = jnp.bfloat16
F32 = jnp.float32


def _params(*sem):
    return pltpu.CompilerParams(dimension_semantics=sem, vmem_limit_bytes=VMEM_LIMIT)


def _dot(a, b):
    return jnp.dot(a, b, preferred_element_type=F32)


def _dot_nt(a, b):
    return lax.dot_general(a, b, (((1,), (1,)), ((), ())), preferred_element_type=F32)


def _dot_tn(a, b):
    return lax.dot_general(a, b, (((0,), (0,)), ((), ())), preferred_element_type=F32)


def _rms(x, g):
    return x * lax.rsqrt(jnp.mean(x * x, axis=-1, keepdims=True) + EPS) * g


def _pack_halves(x):
    c = x.shape[1] // 2
    hi = lax.bitcast_convert_type(x[:, :c].astype(F32), jnp.uint32)
    lo = lax.bitcast_convert_type(x[:, c:].astype(F32), jnp.uint32)
    return hi | (lo >> 16)


def _unpack_halves(p):
    hi = lax.bitcast_convert_type(p & jnp.uint32(0xFFFF0000), F32).astype(BF16)
    lo = lax.bitcast_convert_type(p << 16, F32).astype(BF16)
    return hi, lo


def _const_spec(shape):
    nd = len(shape)
    return pl.BlockSpec(shape, lambda *_: (0,) * nd)


ROLL_CHUNKS = 8


def _window_copies(cache_ref, new_ref, win_ref, sems):
    batch, rows, _ = cache_ref.shape
    new_rows = new_ref.shape[1]
    keep = rows - new_rows
    step = batch // ROLL_CHUNKS
    copies = [pltpu.make_async_copy(cache_ref.at[pl.ds(i * step, step), pl.ds(new_rows, keep), :],
                                    win_ref.at[pl.ds(i * step, step), pl.ds(0, keep), :], sems.at[i])
              for i in range(ROLL_CHUNKS)]
    copies.append(pltpu.make_async_copy(new_ref, win_ref.at[:, pl.ds(keep, new_rows), :], sems.at[ROLL_CHUNKS]))
    return copies


def _background_roll(cache_ref, new_ref, win_ref, sems, n_axes):
    copies = _window_copies(cache_ref, new_ref, win_ref, sems)
    axes = range(n_axes)
    first = functools.reduce(jnp.logical_and, [pl.program_id(a) == 0 for a in axes])
    last = functools.reduce(jnp.logical_and, [pl.program_id(a) == pl.num_programs(a) - 1 for a in axes])

    @pl.when(first)
    def _():
        for c in copies:
            c.start()

    def finish():
        @pl.when(last)
        def _():
            for c in copies:
                c.wait()

    return finish


def _roll_specs(cache, new):
    any_spec = pl.BlockSpec(memory_space=pl.ANY)
    return ([any_spec, any_spec], any_spec, jax.ShapeDtypeStruct(cache.shape, cache.dtype),
            pltpu.SemaphoreType.DMA((ROLL_CHUNKS + 1,)))


def _in_proj_kernel(x_ref, g_ref, w_ref, qg_ref, kg_ref, mqg_ref, rc_ref, ra_ref, rb_ref, *rest):
    if len(rest) > 5:
        cache_ref, new_ref, q_ref, k_ref, v_ref, u_ref, mq_ref, win_ref, sems = rest
        finish_roll = _background_roll(cache_ref, new_ref, win_ref, sems, 1)
    else:
        q_ref, k_ref, v_ref, u_ref, mq_ref = rest
        finish_roll = lambda: None
    h = _rms(x_ref[...], g_ref[...]).astype(BF16)
    rc, ra, rb = rc_ref[...], ra_ref[...], rb_ref[...]

    def rope(y):
        return y * rc + pltpu.roll(y, LANES - ROT_DIM // 2, 1) * ra + pltpu.roll(y, ROT_DIM // 2, 1) * rb

    col = 0
    zq = _dot(h, w_ref[:, col:col + N_Q_HEADS * HEAD_DIM])
    for j in range(N_Q_HEADS):
        sl = slice(j * HEAD_DIM, (j + 1) * HEAD_DIM)
        q_ref[:, sl] = rope(_rms(zq[:, sl], qg_ref[...])).astype(BF16)
    col += N_Q_HEADS * HEAD_DIM
    tm = x_ref.shape[0]
    zk = _dot(h, w_ref[:, col:col + ATTN_WIDTH])
    for j in range(N_KV_HEADS):
        sl = slice(j * HEAD_DIM, (j + 1) * HEAD_DIM)
        k_ref[pl.ds(j, tm, stride=N_KV_HEADS), :] = rope(_rms(zk[:, sl], kg_ref[...]))
    col += ATTN_WIDTH
    zv = _dot(h, w_ref[:, col:col + ATTN_WIDTH])
    for j in range(N_KV_HEADS):
        v_ref[pl.ds(j, tm, stride=N_KV_HEADS), :] = zv[:, j * HEAD_DIM:(j + 1) * HEAD_DIM]
    col += ATTN_WIDTH
    conv_ch = u_ref.shape[-1]
    za = _dot(h, w_ref[:, col:col + conv_ch])
    zb = _dot(h, w_ref[:, col + conv_ch:col + 2 * conv_ch])
    u_ref[...] = za * jax.nn.sigmoid(zb)
    col += 2 * conv_ch
    zm = _dot(h, w_ref[:, col:col + MEM_WIDTH])
    for j in range(MEM_HEADS):
        sl = slice(j * MEM_HEAD_DIM, (j + 1) * MEM_HEAD_DIM)
        mq_ref[:, sl] = _rms(zm[:, sl], mqg_ref[...]).astype(BF16)
    finish_roll()


def _in_proj(x, g_mix, w_bf16, q_g, k_g, mq_g, rope_tabs, conv_ch, roll=None):
    m, d = x.shape
    tm = min(IN_PROJ_TILE, m)
    n_tab_blocks = rope_tabs[0].shape[0] // tm
    row = lambda i: (i, 0)
    tab = lambda i: (i % n_tab_blocks, 0)
    ncols = w_bf16.shape[1]
    out_shape = (
        jax.ShapeDtypeStruct((m, N_Q_HEADS * HEAD_DIM), BF16),
        jax.ShapeDtypeStruct((m * N_KV_HEADS, HEAD_DIM), F32),
        jax.ShapeDtypeStruct((m * N_KV_HEADS, HEAD_DIM), F32),
        jax.ShapeDtypeStruct((m, conv_ch), F32),
        jax.ShapeDtypeStruct((m, MEM_WIDTH), BF16),
    )
    in_specs = [
        pl.BlockSpec((tm, d), row),
        _const_spec((1, d)),
        _const_spec((d, ncols)),
        _const_spec((1, HEAD_DIM)), _const_spec((1, HEAD_DIM)), _const_spec((1, MEM_HEAD_DIM)),
        pl.BlockSpec((tm, LANES), tab), pl.BlockSpec((tm, LANES), tab), pl.BlockSpec((tm, LANES), tab),
    ]
    out_specs = [pl.BlockSpec((tm * s.shape[0] // m, s.shape[1]), row) for s in out_shape]
    operands = [x, g_mix, w_bf16, q_g, k_g, mq_g, *rope_tabs]
    scratch = []
    if roll is not None:
        roll_in, roll_out, roll_shape, roll_sems = _roll_specs(*roll)
        in_specs, out_specs, out_shape = in_specs + roll_in, out_specs + [roll_out], out_shape + (roll_shape,)
        operands, scratch = operands + list(roll), [roll_sems]
    return pl.pallas_call(
        _in_proj_kernel,
        grid=(m // tm,),
        in_specs=in_specs,
        out_specs=tuple(out_specs),
        out_shape=out_shape,
        scratch_shapes=scratch,
        compiler_params=_params("arbitrary"),
        name="in_proj",
    )(*operands)


def _rope_tables(pos):
    half = ROT_DIM // 2
    inv_freq = jnp.power(jnp.float32(ROPE_THETA), -jnp.arange(half, dtype=F32) * (2.0 / ROT_DIM))
    ang = pos.astype(F32)[:, None] * inv_freq[None, :]
    cos, sin = jnp.cos(ang), jnp.sin(ang)
    n = pos.shape[0]
    ones = jnp.ones((n, LANES - ROT_DIM), F32)
    zeros = jnp.zeros((n, LANES - half), F32)
    rc = jnp.concatenate([cos, cos, ones], axis=1)
    ra = jnp.concatenate([-sin, zeros], axis=1)
    rb = jnp.concatenate([jnp.zeros((n, half), F32), sin, jnp.zeros((n, LANES - ROT_DIM), F32)], axis=1)
    return rc, ra, rb


def _to_residue_layout(dst, src, classes_src, ratio):
    len_src = src.shape[0] // classes_src
    len_dst = len_src // ratio
    for c_src in range(classes_src):
        for a in range(ratio):
            c = c_src + classes_src * a
            dst[c * len_dst:(c + 1) * len_dst, :] = src[pl.ds(c_src * len_src + a, len_dst, stride=ratio), :]


def _prompt_attn_kernel(q0_ref, q1_ref, q2_ref, k_ref, v_ref, *rest):
    if len(rest) > 5:
        cache_ref, new_ref, o_ref, win_ref, perm_ref, tmp_ref, acc_ref, lse_ref, sems = rest
        finish_roll = _background_roll(cache_ref, new_ref, win_ref, sems, 2)
    else:
        o_ref, perm_ref, tmp_ref, acc_ref, lse_ref = rest
        finish_roll = lambda: None
    head = pl.program_id(1)
    seq = q0_ref.shape[0]
    q_refs = (q0_ref, q1_ref, q2_ref)
    dils = [d for _, d in DILATED_GROUPS]
    for t, ref in ((1, k_ref), (2, v_ref)):
        perm_ref[0, t] = ref[pl.ds(head, seq, stride=N_KV_HEADS), :]
        for g in range(1, N_GROUPS):
            _to_residue_layout(perm_ref.at[g, t], perm_ref.at[g - 1, t], dils[g - 1], dils[g] // dils[g - 1])
    for g in range(1, N_GROUPS):
        tmp_ref[0] = q_refs[g][...].astype(F32)
        for step in range(1, g + 1):
            dst = perm_ref.at[g, 0] if step == g else tmp_ref.at[step % 2]
            _to_residue_layout(dst, tmp_ref.at[(step - 1) % 2], dils[step - 1], dils[step] // dils[step - 1])
    iq = lax.broadcasted_iota(jnp.int32, (BAND, BAND), 0)
    ik = lax.broadcasted_iota(jnp.int32, (BAND, BAND), 1)
    keep_c = iq >= ik
    scale = HEAD_DIM ** -0.5

    for g, dil in enumerate(dils):
        nb = seq // (dil * BAND)

        def body(jj, carry, g=g, dil=dil, nb=nb):
            blocks = []
            for u in range(ATTN_UNROLL):
                j = jj * ATTN_UNROLL + u
                rows = pl.ds(pl.multiple_of(j * BAND, BAND), BAND)
                prev = pl.ds(pl.multiple_of(jnp.maximum(j - 1, 0) * BAND, BAND), BAND)
                q = q0_ref[rows, :] if g == 0 else perm_ref[g, 0, rows, :].astype(BF16)
                s_c = jnp.where(keep_c, _dot_nt(q, perm_ref[g, 1, rows, :].astype(BF16)) * scale, NEG)
                s_p = None
                if nb > 1:
                    keep_p = jnp.logical_and(ik >= iq, j % nb > 0)
                    s_p = jnp.where(keep_p, _dot_nt(q, perm_ref[g, 1, prev, :].astype(BF16)) * scale, NEG)
                blocks.append((j, rows, prev, s_c, s_p))
            probs = []
            for j, rows, prev, s_c, s_p in blocks:
                m = jnp.max(s_c, axis=-1, keepdims=True)
                if nb > 1:
                    m = jnp.maximum(m, jnp.max(s_p, axis=-1, keepdims=True))
                p_c = jnp.exp(s_c - m)
                l = jnp.sum(p_c, axis=-1, keepdims=True)
                p_p = None
                if nb > 1:
                    p_p = jnp.exp(s_p - m)
                    l = l + jnp.sum(p_p, axis=-1, keepdims=True)
                probs.append((m, l, p_c.astype(BF16), None if p_p is None else p_p.astype(BF16)))
            for (j, rows, prev, _, _), (m, l, p_c, p_p) in zip(blocks, probs):
                acc = _dot(p_c, perm_ref[g, 2, rows, :].astype(BF16))
                if nb > 1:
                    acc = acc + _dot(p_p, perm_ref[g, 2, prev, :].astype(BF16))
                out_rows = rows if dil == 1 else pl.ds((j % nb) * (BAND * dil) + j // nb, BAND, stride=dil)
                acc_ref[g, out_rows, :] = acc * (1.0 / l)
                lse_ref[g, out_rows, :] = jnp.broadcast_to(m + jnp.log(l), (BAND, LANES))
            return carry

        assert (dil * nb) % ATTN_UNROLL == 0
        lax.fori_loop(0, dil * nb // ATTN_UNROLL, body, 0)

    def combine(c, carry):
        rows = pl.ds(pl.multiple_of(c * BAND, BAND), BAND)
        lses = [lse_ref[g, rows, :] for g in range(N_GROUPS)]
        mx = functools.reduce(jnp.maximum, lses)
        ws = [jnp.exp(l - mx) for l in lses]
        out = functools.reduce(jnp.add, [w * acc_ref[g, rows, :] for g, w in enumerate(ws)])
        o_ref[rows, :] = (out * (1.0 / functools.reduce(jnp.add, ws))).astype(BF16)
        return carry

    lax.fori_loop(0, seq // BAND, combine, 0)
    finish_roll()


def _prompt_attention(q, k_flat, v_flat, batch, seq, roll=None):
    for window, dil in DILATED_GROUPS:
        assert window // dil == BAND and seq % (dil * BAND) == 0
    qspec = lambda g: pl.BlockSpec((seq, HEAD_DIM), lambda b, h: (b, g * N_KV_HEADS + h))
    kvspec = pl.BlockSpec((seq * N_KV_HEADS, HEAD_DIM), lambda b, h: (b, 0))
    in_specs = [qspec(0), qspec(1), qspec(2), kvspec, kvspec]
    out_specs = [pl.BlockSpec((seq, HEAD_DIM), lambda b, h: (b, h))]
    out_shape = [jax.ShapeDtypeStruct((batch * seq, ATTN_WIDTH), BF16)]
    operands = [q, q, q, k_flat, v_flat]
    scratch = [pltpu.VMEM((N_GROUPS, 3, seq, HEAD_DIM), F32), pltpu.VMEM((2, seq, HEAD_DIM), F32),
               pltpu.VMEM((N_GROUPS, seq, HEAD_DIM), F32), pltpu.VMEM((N_GROUPS, seq, LANES), F32)]
    if roll is not None:
        roll_in, roll_out, roll_shape, roll_sems = _roll_specs(*roll)
        in_specs, out_specs, out_shape = in_specs + roll_in, out_specs + [roll_out], out_shape + [roll_shape]
        operands, scratch = operands + list(roll), scratch + [roll_sems]
    out = pl.pallas_call(
        _prompt_attn_kernel,
        grid=(batch, N_KV_HEADS),
        in_specs=in_specs,
        out_specs=tuple(out_specs),
        out_shape=tuple(out_shape),
        scratch_shapes=scratch,
        compiler_params=_params("arbitrary", "arbitrary"),
        name="prompt_attn",
    )(*operands)
    return out if roll is not None else out[0]


def _mem_kv_kernel(x_ref, g_ref, w_ref, kg_ref, k_ref, v_ref):
    tm = x_ref.shape[0]
    h = _rms(x_ref[...], g_ref[...]).astype(BF16)
    zk = _dot(h, w_ref[:, :MEM_WIDTH])
    zv = _dot(h, w_ref[:, MEM_WIDTH:])
    for j in range(MEM_HEADS):
        sl = slice(j * MEM_HEAD_DIM, (j + 1) * MEM_HEAD_DIM)
        k_ref[pl.ds(j, tm, stride=MEM_HEADS), :] = _rms(zk[:, sl], kg_ref[...])
        v_ref[pl.ds(j, tm, stride=MEM_HEADS), :] = zv[:, sl]


def _mem_kv(mem, g, w_bf16, k_g):
    m, d = mem.shape
    tm = min(ROW_TILE, m)
    row = lambda i: (i, 0)
    shp = jax.ShapeDtypeStruct((m * MEM_HEADS, MEM_HEAD_DIM), F32)
    ospec = pl.BlockSpec((tm * MEM_HEADS, MEM_HEAD_DIM), row)
    return pl.pallas_call(
        _mem_kv_kernel,
        grid=(m // tm,),
        in_specs=[pl.BlockSpec((tm, d), row), _const_spec((1, d)), _const_spec((d, 2 * MEM_WIDTH)),
                  _const_spec((1, MEM_HEAD_DIM))],
        out_specs=(ospec, ospec),
        out_shape=(shp, shp),
        compiler_params=_params("parallel"),
        name="mem_kv",
    )(mem, g, w_bf16, k_g)


def _mem_attn_kernel(q_ref, k_ref, v_ref, o_ref):
    scale = MEM_HEAD_DIM ** -0.5
    n_mem = k_ref.shape[0] // MEM_HEADS
    for h in range(MEM_HEADS):
        sl = slice(h * MEM_HEAD_DIM, (h + 1) * MEM_HEAD_DIM)
        head_rows = pl.ds(h, n_mem, stride=MEM_HEADS)
        s = _dot_nt(q_ref[:, sl], k_ref[head_rows, :].astype(BF16)) * scale
        p = jnp.exp(s - jnp.max(s, axis=-1, keepdims=True))
        l = jnp.sum(p, axis=-1, keepdims=True)
        o_ref[:, sl] = (_dot(p.astype(BF16), v_ref[head_rows, :].astype(BF16)) * (1.0 / l)).astype(BF16)


def _mem_attention(mq, mem_k_flat, mem_v_flat, batch, seq):
    rows = mem_k_flat.shape[0] // batch
    tq = min(512, seq)
    nq = seq // tq
    kspec = pl.BlockSpec((rows, MEM_HEAD_DIM), lambda b, i: (b, 0))
    return pl.pallas_call(
        _mem_attn_kernel,
        grid=(batch, nq),
        in_specs=[pl.BlockSpec((tq, MEM_WIDTH), lambda b, i: (b * nq + i, 0)), kspec, kspec],
        out_specs=pl.BlockSpec((tq, MEM_WIDTH), lambda b, i: (b * nq + i, 0)),
        out_shape=jax.ShapeDtypeStruct((batch * seq, MEM_WIDTH), BF16),
        compiler_params=_params("parallel", "parallel"),
        name="mem_attn",
    )(mq, mem_k_flat, mem_v_flat)


CONV_HALO = 32
CONV_CHUNK = 32
CONV_ACC_ROWS = 128
CONV_TAP_UNROLL = 8


def _ln_swish(c, g, b):
    mu = jnp.mean(c, axis=-1, keepdims=True)
    xc = c - mu
    y = xc * lax.rsqrt(jnp.mean(xc * xc, axis=-1, keepdims=True) + EPS) * g + b
    return y * jax.nn.sigmoid(y)


def _conv_prompt_kernel(halo_ref, u_ref, w_ref, b_ref, g_ref, beta_ref, o_ref, ext_ref, conv_ref, wb_ref):
    tc, ch = u_ref.shape
    first = pl.program_id(1) == 0
    lane_tiles = [slice(j * LANES, (j + 1) * LANES) for j in range(ch // LANES)]
    for j, sl in enumerate(lane_tiles):
        halo = jnp.where(first, 0.0, halo_ref[:, sl])
        for dup in range(2):
            ext_ref[j, pl.ds(dup, CONV_HALO, stride=2), :] = halo
            ext_ref[j, pl.ds(2 * CONV_HALO + dup, tc, stride=2), :] = u_ref[:, sl]
    for w in range(CONV_WIDTH):
        wb_ref[w] = jnp.broadcast_to(w_ref[w:w + 1, :], (8, ch))
    lead = CONV_HALO - (CONV_WIDTH - 1)
    rows = CONV_ACC_ROWS
    for j, sl in enumerate(lane_tiles):
        for r0 in range(0, tc, rows):
            def tap(w, acc, j=j, sl=sl, r0=r0):
                win = ext_ref[j, pl.ds(2 * (r0 + lead + w), rows, stride=2), :].reshape(rows // 8, 8, LANES)
                return acc + wb_ref[w, :, sl][None] * win

            acc = lax.fori_loop(0, CONV_WIDTH, tap, jnp.zeros((rows // 8, 8, LANES), F32) + b_ref[:, sl],
                                unroll=CONV_TAP_UNROLL)
            conv_ref[r0:r0 + rows, sl] = acc.reshape(rows, LANES)
    for c0 in range(0, tc, CONV_CHUNK):
        o_ref[c0:c0 + CONV_CHUNK, :] = _ln_swish(conv_ref[c0:c0 + CONV_CHUNK, :], g_ref[...], beta_ref[...]).astype(BF16)


def _conv_prompt(u, conv_w, conv_b, ln_g, ln_b, batch, seq):
    ch = u.shape[1]
    tc = min(ROW_TILE, seq)
    nt = seq // tc
    ratio = tc // CONV_HALO
    return pl.pallas_call(
        _conv_prompt_kernel,
        grid=(batch, nt),
        in_specs=[
            pl.BlockSpec((CONV_HALO, ch), lambda b, i: (jnp.maximum((b * nt + i) * ratio - 1, 0), 0)),
            pl.BlockSpec((tc, ch), lambda b, i: (b * nt + i, 0)),
            _const_spec((CONV_WIDTH, ch)), _const_spec((1, ch)), _const_spec((1, ch)), _const_spec((1, ch)),
        ],
        out_specs=pl.BlockSpec((tc, ch), lambda b, i: (b * nt + i, 0)),
        out_shape=jax.ShapeDtypeStruct((batch * seq, ch), BF16),
        scratch_shapes=[pltpu.VMEM((ch // LANES, 2 * (CONV_HALO + tc), LANES), F32), pltpu.VMEM((tc, ch), F32),
                        pltpu.VMEM((CONV_WIDTH, 8, ch), F32)],
        compiler_params=_params("parallel", "parallel"),
        name="conv_prompt",
    )(u, u, conv_w, conv_b, ln_g, ln_b)


def _conv_sample_kernel(state_ref, new_ref, w_ref, b_ref, g_ref, beta_ref, o_ref, state_out_ref, ext_ref):
    nb, ctx, ch = state_ref.shape
    t = new_ref.shape[1]
    ext_ref[:, 0:ctx, :] = state_ref[...]
    ext_ref[:, ctx:ctx + t, :] = new_ref[...]
    acc = jnp.zeros((nb, t, ch), F32) + b_ref[...]
    for w in range(CONV_WIDTH):
        acc = acc + w_ref[w:w + 1, :] * ext_ref[:, w:w + t, :]
    o_ref[...] = _ln_swish(acc, g_ref[...], beta_ref[...])
    state_out_ref[...] = ext_ref[:, t:t + ctx, :]


def _conv_sample(state, u_new, conv_w, conv_b, ln_g, ln_b):
    batch, ctx, ch = state.shape
    t = u_new.shape[1]
    nb = 8
    blk = lambda n: pl.BlockSpec((nb, n, ch), lambda i: (i, 0, 0))
    return pl.pallas_call(
        _conv_sample_kernel,
        grid=(batch // nb,),
        in_specs=[blk(ctx), blk(t), _const_spec((CONV_WIDTH, ch)), _const_spec((1, ch)), _const_spec((1, ch)),
                  _const_spec((1, ch))],
        out_specs=(blk(t), blk(ctx)),
        out_shape=(jax.ShapeDtypeStruct((batch, t, ch), F32), jax.ShapeDtypeStruct((batch, ctx, ch), F32)),
        scratch_shapes=[pltpu.VMEM((nb, ctx + t + 6, ch), F32)],
        compiler_params=_params("parallel"),
        name="conv_sample",
    )(state, u_new, conv_w, conv_b, ln_g, ln_b)


GROUP_LANES = 16


def _spread_groups(vec, combine):
    t = combine(combine(vec, pltpu.roll(vec, LANES - GROUP_LANES, 1)), pltpu.roll(vec, LANES - 2 * GROUP_LANES, 1))
    lane = lax.broadcasted_iota(jnp.int32, vec.shape, 1)
    return jnp.where(lane < GROUP_LANES, t,
                     jnp.where(lane < 2 * GROUP_LANES, pltpu.roll(t, GROUP_LANES, 1), pltpu.roll(t, 2 * GROUP_LANES, 1)))


SLAB_POS = DILATED_GROUPS[-1][1]
SLAB_ROWS = SLAB_POS * N_KV_HEADS
TAIL_POS = max(w for w, d in DILATED_GROUPS if d < SLAB_POS)


def _sample_attn_kernel(kd_ref, kt_ref, vd_ref, vt_ref, kn_ref, vn_ref, wq_ref, bias_d_ref, bias_t_ref, bias_n_ref,
                        mk_ref, mv_ref, wm_ref, bias_m_ref, a_ref, m_ref, kx_ref, vx_ref):
    new_rows = kn_ref.shape[1]
    t = new_rows // N_KV_HEADS
    n_used = N_GROUPS * GROUP_LANES
    lane = lax.broadcasted_iota(jnp.int32, (1, LANES), 1)

    kx_ref[...] = jnp.zeros(kx_ref.shape, F32)
    vx_ref[...] = jnp.zeros(vx_ref.shape, F32)
    kx_ref[0:new_rows, :] = kn_ref[0]
    vx_ref[0:new_rows, :] = vn_ref[0]

    def rows(ref):
        return ref[0].reshape(ref.shape[1] * ref.shape[2], HEAD_DIM).astype(BF16)

    scale = HEAD_DIM ** -0.5
    wq = wq_ref[0]
    s_d = _dot(rows(kd_ref), wq) * scale + bias_d_ref[...]
    s_t = _dot(rows(kt_ref), wq) * scale + bias_t_ref[...]
    s_n = _dot(kx_ref[...].astype(BF16), wq) * scale + bias_n_ref[...]
    col_max = lambda s: jnp.max(s, axis=0, keepdims=True)
    col_sum = lambda p: jnp.sum(p, axis=0, keepdims=True)
    m_col = jnp.maximum(jnp.maximum(col_max(s_d), col_max(s_t)), col_max(s_n))
    m_joint = jnp.where(lane < n_used, _spread_groups(m_col, jnp.maximum), 0.0)
    p_d, p_t, p_n = jnp.exp(s_d - m_joint), jnp.exp(s_t - m_joint), jnp.exp(s_n - m_joint)
    l_col = col_sum(p_d) + col_sum(p_t) + col_sum(p_n)
    inv = 1.0 / jnp.where(lane < n_used, _spread_groups(l_col, jnp.add), 1.0)
    o = (_dot_tn((p_d * inv).astype(BF16), rows(vd_ref)) + _dot_tn((p_t * inv).astype(BF16), rows(vt_ref))
         + _dot_tn((p_n * inv).astype(BF16), vx_ref[...].astype(BF16)))
    for h in range(N_KV_HEADS):
        r = h * t
        a_ref[0, :, h * HEAD_DIM:(h + 1) * HEAD_DIM] = functools.reduce(
            jnp.add, [o[g * GROUP_LANES + r:g * GROUP_LANES + r + t, :] for g in range(N_GROUPS)])

    sm = _dot(mk_ref[0].astype(BF16), wm_ref[0]) * (MEM_HEAD_DIM ** -0.5) + bias_m_ref[...]
    pm = jnp.exp(sm - jnp.where(lane < MEM_HEADS * t, col_max(sm), 0.0))
    lm = jnp.where(lane < MEM_HEADS * t, col_sum(pm), 1.0)
    om = _dot_tn((pm * (1.0 / lm)).astype(BF16), mv_ref[0].astype(BF16))
    for h in range(MEM_HEADS):
        m_ref[0, :, h * MEM_HEAD_DIM:(h + 1) * MEM_HEAD_DIM] = om[h * t:(h + 1) * t, :]


def _sample_masks(cache_len, t, n_mem):
    col = np.arange(LANES)[None, :]
    g, c_head, c_tok = col // GROUP_LANES, (col % GROUP_LANES) // t, (col % GROUP_LANES) % t
    used = (col < N_GROUPS * GROUP_LANES) & (col % GROUP_LANES < N_KV_HEADS * t)
    pad = [1] * (LANES // GROUP_LANES - N_GROUPS)
    dil = np.array([d for _, d in DILATED_GROUPS] + pad)[g]
    win = np.array([w for w, _ in DILATED_GROUPS] + pad)[g]
    sparse = dil >= SLAB_POS

    def keep(pos, head, group_sel):
        dist = cache_len + c_tok - pos
        return used & group_sel & (head == c_head) & (dist >= 0) & (dist % dil == 0) & (dist <= win)

    n_slab = cache_len // SLAB_POS
    y = np.arange(n_slab * t * N_KV_HEADS)[:, None]
    keep_d = keep((y // (t * N_KV_HEADS)) * SLAB_POS + (y % (t * N_KV_HEADS)) // N_KV_HEADS, y % N_KV_HEADS, sparse)
    x = np.arange(TAIL_POS * N_KV_HEADS)[:, None]
    keep_t = keep(cache_len - TAIL_POS + x // N_KV_HEADS, x % N_KV_HEADS, ~sparse)
    z = np.arange(LANES)[:, None]
    keep_n = keep(cache_len + z // N_KV_HEADS, z % N_KV_HEADS, True) & (z < t * N_KV_HEADS)
    w = np.arange(n_mem * MEM_HEADS)[:, None]
    keep_m = (col < MEM_HEADS * t) & (w % MEM_HEADS == col // t)
    return tuple(jnp.asarray(np.where(k, 0.0, NEG), F32) for k in (keep_d, keep_t, keep_n, keep_m))


def _query_columns(q, batch, t, n_groups):
    qt = q.reshape(batch, t, n_groups, N_KV_HEADS, HEAD_DIM).transpose(0, 4, 2, 3, 1)
    qt = qt.reshape(batch, HEAD_DIM, n_groups, N_KV_HEADS * t)
    qt = jnp.pad(qt, ((0, 0), (0, 0), (0, 0), (0, GROUP_LANES - N_KV_HEADS * t)))
    qt = qt.reshape(batch, HEAD_DIM, n_groups * GROUP_LANES)
    return jnp.pad(qt, ((0, 0), (0, 0), (0, LANES - n_groups * GROUP_LANES)))


def _sample_attention(q, k_new, v_new, mq, cache_k, cache_v, mem_k, mem_v):
    batch, cache_len = cache_k.shape[0], cache_k.shape[1]
    new_rows = k_new.shape[0] // batch
    t = new_rows // N_KV_HEADS
    n_mem = mem_k.shape[1]
    assert N_KV_HEADS * t <= GROUP_LANES and t <= SLAB_POS and new_rows % 8 == 0
    assert cache_len % SLAB_POS == 0 and cache_len >= max(w for w, _ in DILATED_GROUPS)
    assert all(d == SLAB_POS or w <= TAIL_POS for w, d in DILATED_GROUPS)
    n_slab = cache_len // SLAB_POS
    wq = _query_columns(q, batch, t, N_GROUPS)
    wm = _query_columns(mq, batch, t, 1)
    bias_d, bias_t, bias_n, bias_m = _sample_masks(cache_len, t, n_mem)
    per_b = lambda *shape: pl.BlockSpec((1,) + shape, lambda b: (b,) + (0,) * len(shape))
    slabs = lambda c: c.reshape(batch, n_slab, SLAB_ROWS, HEAD_DIM)
    tail_slabs = TAIL_POS // SLAB_POS
    assert n_slab % tail_slabs == 0
    dil_spec = per_b(n_slab, new_rows, HEAD_DIM)
    tail_spec = pl.BlockSpec((1, tail_slabs, SLAB_ROWS, HEAD_DIM), lambda b: (b, n_slab // tail_slabs - 1, 0, 0))
    mem_rows = n_mem * MEM_HEADS
    return pl.pallas_call(
        _sample_attn_kernel,
        grid=(batch,),
        in_specs=[
            dil_spec, tail_spec, dil_spec, tail_spec,
            per_b(new_rows, HEAD_DIM), per_b(new_rows, HEAD_DIM), per_b(HEAD_DIM, LANES),
            _const_spec(bias_d.shape), _const_spec(bias_t.shape), _const_spec(bias_n.shape),
            per_b(mem_rows, MEM_HEAD_DIM), per_b(mem_rows, MEM_HEAD_DIM), per_b(MEM_HEAD_DIM, LANES),
            _const_spec(bias_m.shape),
        ],
        out_specs=(per_b(t, ATTN_WIDTH), per_b(t, MEM_WIDTH)),
        out_shape=(jax.ShapeDtypeStruct((batch, t, ATTN_WIDTH), F32), jax.ShapeDtypeStruct((batch, t, MEM_WIDTH), F32)),
        scratch_shapes=[pltpu.VMEM((LANES, HEAD_DIM), F32), pltpu.VMEM((LANES, HEAD_DIM), F32)],
        compiler_params=_params("parallel"),
        name="sample_attn",
    )(slabs(cache_k), slabs(cache_k), slabs(cache_v), slabs(cache_v), k_new.reshape(batch, new_rows, HEAD_DIM),
      v_new.reshape(batch, new_rows, HEAD_DIM), wq, bias_d, bias_t, bias_n,
      mem_k.reshape(batch, mem_rows, MEM_HEAD_DIM), mem_v.reshape(batch, mem_rows, MEM_HEAD_DIM), wm, bias_m)


ROUTE_GROUP_LANE0 = 0
ROUTE_EXPERT_LANE0 = N_EXPERT_GROUPS


def _route(logits):
    lane = lax.broadcasted_iota(jnp.int32, logits.shape, 1).astype(F32)
    big = float(LANES)

    def masked_softmax(keep):
        z = jnp.where(keep, logits, NEG)
        e = jnp.where(keep, jnp.exp(z - jnp.max(z, axis=-1, keepdims=True)), 0.0)
        return e / jnp.sum(e, axis=-1, keepdims=True)

    def first_argmax(vals, keep):
        top = jnp.max(jnp.where(keep, vals, -1.0), axis=-1, keepdims=True)
        idx = jnp.min(jnp.where(jnp.logical_and(keep, vals == top), lane, big), axis=-1, keepdims=True)
        return top, idx

    is_group = lane < N_EXPERT_GROUPS
    pg = masked_softmax(is_group)
    pg_top, g_idx = first_argmax(pg, is_group)
    lo = ROUTE_EXPERT_LANE0 + g_idx * EXPERTS_PER_GROUP
    in_group = jnp.logical_and(lane >= lo, lane < lo + EXPERTS_PER_GROUP)
    pe = masked_softmax(in_group)
    p1, i1 = first_argmax(pe, in_group)
    p2, i2 = first_argmax(pe, jnp.logical_and(in_group, lane != i1))
    denom = p1 + p2
    w1 = pg_top * p1 / denom
    w2 = pg_top * p2 / denom
    return lane, i1 - ROUTE_EXPERT_LANE0, i2 - ROUTE_EXPERT_LANE0, w1, w2


def _merge_kernel(x_ref, gmix_ref, ao_ref, cc_ref, mo_ref, wgate_ref, wa_ref, wc_ref, wm_ref, wo_ref, gffn_ref, wr_ref,
                  br_ref, cnt_in_ref, x1_ref, h2_ref, route_ref, cnt_out_ref, cnt_ref):
    @pl.when(pl.program_id(0) == 0)
    def _():
        cnt_ref[...] = cnt_in_ref[...]

    d = x_ref.shape[1]
    x = x_ref[...]
    h = _rms(x, gmix_ref[...]).astype(BF16)
    a = _dot(ao_ref[...].astype(BF16), wa_ref[...])
    c = _dot(cc_ref[...].astype(BF16), wc_ref[...])
    m = _dot(mo_ref[...].astype(BF16), wm_ref[...])
    z = jax.nn.sigmoid(_dot(h, wgate_ref[:, 0:d])) * a
    z = z + jax.nn.sigmoid(_dot(h, wgate_ref[:, d:2 * d])) * c
    z = z + jax.nn.sigmoid(_dot(h, wgate_ref[:, 2 * d:3 * d])) * m
    x1 = x + _dot(z.astype(BF16), wo_ref[...])
    x1_ref[...] = x1
    h2 = _rms(x1, gffn_ref[...]).astype(BF16)
    h2_ref[...] = _pack_halves(h2)
    logits = _dot(h2, wr_ref[...]) + br_ref[...]
    lane, e1, e2, w1, w2 = _route(logits)

    tm = x_ref.shape[0]
    hit1 = jnp.where(lane == e1, 1.0, 0.0)
    hit2 = jnp.where(lane == e2, 1.0, 0.0)
    hits = hit1 + hit2
    earlier = (lax.broadcasted_iota(jnp.int32, (tm, tm), 1) < lax.broadcasted_iota(jnp.int32, (tm, tm), 0))
    before = _dot(jnp.where(earlier, 1.0, 0.0).astype(BF16), hits.astype(BF16)) + cnt_ref[...]
    rank1 = jnp.sum(hit1 * before, axis=-1, keepdims=True)
    rank2 = jnp.sum(hit2 * before, axis=-1, keepdims=True)
    cnt_ref[...] = cnt_ref[...] + jnp.sum(hits, axis=0, keepdims=True)
    cnt_out_ref[...] = cnt_ref[...]
    route = jnp.zeros(logits.shape, F32)
    for i, val in enumerate((e1, e2, w1, w2, rank1, rank2)):
        route = jnp.where(lane == i, val, route)
    route_ref[...] = route


ROUTE_E, ROUTE_W, ROUTE_RANK = 0, 2, 4


def _merge(x, g_mix, ao, cc, mo, w_gate, w_a, w_c, w_m, w_o, g_ffn, w_r, b_r, counts):
    m, d = x.shape
    tm = min(MERGE_TILE, m)
    row = lambda i: (i, 0)
    rows = lambda arr: pl.BlockSpec((tm, arr.shape[1]), row)
    ins = [x, g_mix, ao, cc, mo, w_gate, w_a, w_c, w_m, w_o, g_ffn, w_r, b_r, counts]
    specs = ([rows(x), _const_spec(g_mix.shape), rows(ao), rows(cc), rows(mo)]
             + [_const_spec(a.shape) for a in (w_gate, w_a, w_c, w_m, w_o, g_ffn, w_r, b_r, counts)])
    out_shape = (jax.ShapeDtypeStruct((m, d), F32), jax.ShapeDtypeStruct((m, d // 2), jnp.uint32),
                 jax.ShapeDtypeStruct((m, LANES), F32), jax.ShapeDtypeStruct((1, LANES), F32))
    return pl.pallas_call(
        _merge_kernel,
        grid=(m // tm,),
        in_specs=specs,
        out_specs=(rows(out_shape[0]), rows(out_shape[1]), rows(out_shape[2]), _const_spec((1, LANES))),
        out_shape=out_shape,
        scratch_shapes=[pltpu.VMEM((1, LANES), F32)],
        compiler_params=_params("arbitrary"),
        name="merge",
    )(*ins)


def _routing_tables(counts, routes, n_tiles):
    cnt = counts[0, :N_EXPERTS].astype(jnp.int32)
    tiles = (cnt + MOE_TILE - 1) // MOE_TILE
    tile_end = jnp.cumsum(tiles)
    row_start = (tile_end - tiles) * MOE_TILE
    n_used = tile_end[-1]
    tile_ids = jnp.minimum(jnp.arange(n_tiles, dtype=jnp.int32), n_used - 1)
    tile_expert = jnp.sum(tile_ids[:, None] >= tile_end[None, :], axis=1).astype(jnp.int32)
    ids = routes[:, ROUTE_E:ROUTE_E + TOP_K].astype(jnp.int32)
    start_of = jnp.sum(jnp.where(ids[:, :, None] == jnp.arange(N_EXPERTS)[None, None, :], row_start[None, None, :], 0),
                       axis=-1)
    pos = start_of + routes[:, ROUTE_RANK:ROUTE_RANK + TOP_K].astype(jnp.int32)
    return pos.reshape(-1), tile_expert, n_used.reshape(1).astype(jnp.int32)


def _dispatch_kernel(pos_ref, h_ref, xs_in, xs_out, sem, *, row0):
    del xs_in
    tm = h_ref.shape[0]
    base = (row0 + pl.program_id(0) * tm) * TOP_K

    def issue(j, carry):
        for k in range(TOP_K):
            slot = pos_ref[base + j * TOP_K + k]
            pltpu.make_async_copy(h_ref.at[pl.ds(j, 1)], xs_out.at[pl.ds(slot, 1)], sem).start()
        return carry

    lax.fori_loop(0, tm, issue, 0, unroll=8)
    for k in range(TOP_K):
        pltpu.make_async_copy(h_ref, xs_out.at[pl.ds(0, tm)], sem).wait()


def _dispatch(h2, pos, xs, row0):
    m, d = h2.shape
    tm = min(ROW_TILE, m)
    return pl.pallas_call(
        functools.partial(_dispatch_kernel, row0=row0),
        grid_spec=pltpu.PrefetchScalarGridSpec(
            num_scalar_prefetch=1,
            grid=(m // tm,),
            in_specs=[pl.BlockSpec((tm, d), lambda i, *_: (i, 0)), pl.BlockSpec(memory_space=pl.ANY)],
            out_specs=pl.BlockSpec(memory_space=pl.ANY),
            scratch_shapes=[pltpu.SemaphoreType.DMA(())],
        ),
        out_shape=jax.ShapeDtypeStruct(xs.shape, xs.dtype),
        input_output_aliases={2: 0},
        compiler_params=_params("arbitrary"),
        name="moe_dispatch",
    )(pos, h2, xs)


def _gmm_kernel(te_ref, nused_ref, x_ref, wg_ref, wu_ref, wd_ref, o_ref, wg_s, wu_s, wd_s):
    t = pl.program_id(0)
    used = t < nused_ref[0]
    new_expert = jnp.logical_or(t == 0, te_ref[t] != te_ref[jnp.maximum(t - 1, 0)])

    @pl.when(jnp.logical_and(used, new_expert))
    def _():
        wg_s[...] = wg_ref[0].astype(BF16)
        wu_s[...] = wu_ref[0].astype(BF16)
        wd_s[...] = wd_ref[0].astype(BF16)

    @pl.when(used)
    def _():
        x_hi, x_lo = _unpack_halves(x_ref[...])
        half = x_ref.shape[1]
        gate = _dot(x_hi, wg_s[0:half, :]) + _dot(x_lo, wg_s[half:, :])
        up = _dot(x_hi, wu_s[0:half, :]) + _dot(x_lo, wu_s[half:, :])
        hid = gate * jax.nn.sigmoid(gate) * up
        o_ref[...] = _dot(hid.astype(BF16), wd_s[...])

    @pl.when(jnp.logical_not(used))
    def _():
        o_ref[...] = jnp.zeros(o_ref.shape, F32)


def _grouped_mlp(xs, tile_expert, n_used, w_gate, w_up, w_down):
    n_tiles = xs.shape[0] // MOE_TILE
    d, ff = w_gate.shape[1], w_gate.shape[2]
    wspec = lambda a, b: pl.BlockSpec((1, a, b), lambda t, te, nu: (te[t], 0, 0))
    return pl.pallas_call(
        _gmm_kernel,
        grid_spec=pltpu.PrefetchScalarGridSpec(
            num_scalar_prefetch=2,
            grid=(n_tiles,),
            in_specs=[pl.BlockSpec((MOE_TILE, xs.shape[1]), lambda t, te, nu: (jnp.minimum(t, nu[0] - 1), 0)),
                      wspec(d, ff), wspec(d, ff), wspec(ff, d)],
            out_specs=pl.BlockSpec((MOE_TILE, d), lambda t, *_: (t, 0)),
            scratch_shapes=[pltpu.VMEM((d, ff), BF16), pltpu.VMEM((d, ff), BF16), pltpu.VMEM((ff, d), BF16)],
        ),
        out_shape=jax.ShapeDtypeStruct((n_tiles * MOE_TILE, d), F32),
        compiler_params=_params("arbitrary"),
        name="moe_gmm",
    )(tile_expert, n_used, xs, w_gate, w_up, w_down)


def _combine_kernel(pos_ref, x1_ref, route_ref, ys_hbm, y_ref, buf_ref, sem, *, row0):
    tm = x1_ref.shape[0]
    base = (row0 + pl.program_id(0) * tm) * TOP_K

    def issue(j, carry):
        for k in range(TOP_K):
            slot = pos_ref[base + j * TOP_K + k]
            pltpu.make_async_copy(ys_hbm.at[pl.ds(slot, 1)], buf_ref.at[k, pl.ds(j, 1)], sem).start()
        return carry

    lax.fori_loop(0, tm, issue, 0, unroll=4)
    for k in range(TOP_K):
        pltpu.make_async_copy(ys_hbm.at[pl.ds(0, tm)], buf_ref.at[k], sem).wait()
    route = route_ref[...]
    y_ref[...] = x1_ref[...] + route[:, 2:3] * buf_ref[0] + route[:, 3:4] * buf_ref[1]


def _combine(x1, route, pos, ys, row0):
    m, d = x1.shape
    tm = min(ROW_TILE, m)
    row = lambda i, *_: (i, 0)
    return pl.pallas_call(
        functools.partial(_combine_kernel, row0=row0),
        grid_spec=pltpu.PrefetchScalarGridSpec(
            num_scalar_prefetch=1,
            grid=(m // tm,),
            in_specs=[pl.BlockSpec((tm, d), row), pl.BlockSpec((tm, LANES), row), pl.BlockSpec(memory_space=pl.ANY)],
            out_specs=pl.BlockSpec((tm, d), row),
            scratch_shapes=[pltpu.VMEM((TOP_K, tm, d), F32), pltpu.SemaphoreType.DMA(())],
        ),
        out_shape=jax.ShapeDtypeStruct((m, d), F32),
        compiler_params=_params("arbitrary"),
        name="moe_combine",
    )(pos, x1, route, ys)


def _layer(layer, x_prompt, x_sample, mem_prompt, cache_k, cache_v, state_conv, cache_mem_k, cache_mem_v, p):
    batch, seq, d = x_prompt.shape
    dec_batch, dec_seq, _ = x_sample.shape
    conv_ch = p["conv_w"].shape[-1]
    n_in = N_Q_HEADS * HEAD_DIM + 2 * ATTN_WIDTH + 2 * conv_ch + MEM_WIDTH
    past_len = cache_k.shape[2]

    row2 = lambda name: p[name][layer][None, :]
    w_in = p["w_in"][layer]
    w_main = w_in[:, :n_in].astype(BF16)
    w_gate = w_in[:, n_in:].astype(BF16)
    w_a, w_c, w_m, w_o = (p[n][layer].astype(BF16) for n in ("w_attn_proj", "w_conv_proj", "w_mem_proj", "w_out"))
    w_router = jnp.concatenate(
        [p["w_router_group"][layer], p["w_router_expert"][layer].transpose(1, 0, 2).reshape(d, N_EXPERTS)], axis=1)
    w_router = jnp.pad(w_router, ((0, 0), (0, LANES - w_router.shape[1]))).astype(BF16)
    b_router = jnp.concatenate([p["b_router_group"][layer], p["b_router_expert"][layer].reshape(-1)])
    b_router = jnp.pad(b_router, (0, LANES - b_router.shape[0]))[None, :]
    conv_args = (p["conv_w"][layer], row2("conv_b"), row2("conv_ln_g"), row2("conv_ln_b"))
    merge_w = (w_gate, w_a, w_c, w_m, w_o, row2("norm_ffn_g"), w_router, b_router)

    xs = x_sample.reshape(dec_batch * dec_seq, d)
    tabs_s = _rope_tables(jnp.tile(past_len + jnp.arange(dec_seq, dtype=jnp.int32), dec_batch))
    q_s, k_s, v_s, u_s, mq_s = _in_proj(xs, row2("norm_mix_g"), w_main, row2("q_norm_g"), row2("k_norm_g"),
                                        row2("mq_norm_g"), tabs_s, conv_ch)
    flat_rows = lambda c: c.reshape(dec_batch, -1, HEAD_DIM)

    xp = x_prompt.reshape(batch * seq, d)
    tabs_p = _rope_tables(jnp.arange(seq, dtype=jnp.int32))
    q_p, k_p, v_p, u_p, mq_p, k_win_s = _in_proj(xp, row2("norm_mix_g"), w_main, row2("q_norm_g"), row2("k_norm_g"),
                                                 row2("mq_norm_g"), tabs_p, conv_ch,
                                                 roll=(flat_rows(cache_k[layer]), flat_rows(k_s)))
    ao_p, v_win_s = _prompt_attention(q_p, k_p, v_p, batch, seq, roll=(flat_rows(cache_v[layer]), flat_rows(v_s)))
    mem_k_p, mem_v_p = _mem_kv(mem_prompt.reshape(-1, d), row2("mem_norm_g"), p["w_mem_kv"][layer].astype(BF16),
                               row2("mk_norm_g"))
    mo_p = _mem_attention(mq_p, mem_k_p, mem_v_p, batch, seq)
    cc_p = _conv_prompt(u_p, *conv_args, batch, seq)
    x1_p, h2_p, route_p, counts = _merge(xp, row2("norm_mix_g"), ao_p, cc_p, mo_p, *merge_w,
                                         jnp.zeros((1, LANES), F32))

    a_s, mo_s = _sample_attention(q_s, k_s, v_s, mq_s, cache_k[layer], cache_v[layer], cache_mem_k[layer],
                                  cache_mem_v[layer])
    cc_s, conv_state_s = _conv_sample(state_conv[layer], u_s.reshape(dec_batch, dec_seq, conv_ch), *conv_args)
    x1_s, h2_s, route_s, counts = _merge(xs, row2("norm_mix_g"), a_s.reshape(-1, ATTN_WIDTH),
                                         cc_s.reshape(-1, conv_ch), mo_s.reshape(-1, MEM_WIDTH), *merge_w, counts)

    n_p, n_s = xp.shape[0], xs.shape[0]
    n_tok = n_p + n_s
    n_tiles = (TOP_K * n_tok + N_EXPERTS * (MOE_TILE - 1)) // MOE_TILE + 1
    pos, tile_expert, n_used = _routing_tables(counts, jnp.concatenate([route_p, route_s], axis=0), n_tiles)
    slots = _dispatch(h2_p, pos, jnp.zeros((n_tiles * MOE_TILE, h2_p.shape[1]), h2_p.dtype), 0)
    slots = _dispatch(h2_s, pos, slots, n_p)
    ys = _grouped_mlp(slots, tile_expert, n_used, p["w_expert_gate"][layer], p["w_expert_up"][layer],
                      p["w_expert_down"][layer])
    y_p = _combine(x1_p, route_p, pos, ys, 0)
    y_s = _combine(x1_s, route_s, pos, ys, n_p)

    state_p = (k_p.reshape(batch, seq, N_KV_HEADS, HEAD_DIM), v_p.reshape(batch, seq, N_KV_HEADS, HEAD_DIM),
               u_p.reshape(batch, seq, conv_ch)[:, seq - (CONV_WIDTH - 1):],
               mem_k_p.reshape(batch, -1, MEM_HEADS, MEM_HEAD_DIM), mem_v_p.reshape(batch, -1, MEM_HEADS, MEM_HEAD_DIM))
    state_s = (k_win_s.reshape(cache_k[layer].shape), v_win_s.reshape(cache_v[layer].shape), conv_state_s)
    return y_p.reshape(batch, seq, d), y_s.reshape(dec_batch, dec_seq, d), state_p, state_s


def kernel(x_prompt, x_sample, mem_prompt, cache_k, cache_v, state_conv, cache_mem_k, cache_mem_v, norm_mix_g, w_in, q_norm_g, k_norm_g, conv_w, conv_b, conv_ln_g, conv_ln_b, mem_norm_g, w_mem_kv, mq_norm_g, mk_norm_g, w_attn_proj, w_conv_proj, w_mem_proj, w_out, norm_ffn_g, w_router_group, b_router_group, w_router_expert, b_router_expert, w_expert_gate, w_expert_up, w_expert_down):
    p = dict(norm_mix_g=norm_mix_g, w_in=w_in, q_norm_g=q_norm_g, k_norm_g=k_norm_g, conv_w=conv_w, conv_b=conv_b,
             conv_ln_g=conv_ln_g, conv_ln_b=conv_ln_b, mem_norm_g=mem_norm_g, w_mem_kv=w_mem_kv, mq_norm_g=mq_norm_g,
             mk_norm_g=mk_norm_g, w_attn_proj=w_attn_proj, w_conv_proj=w_conv_proj, w_mem_proj=w_mem_proj,
             w_out=w_out, norm_ffn_g=norm_ffn_g, w_router_group=w_router_group, b_router_group=b_router_group,
             w_router_expert=w_router_expert, b_router_expert=b_router_expert, w_expert_gate=w_expert_gate,
             w_expert_up=w_expert_up, w_expert_down=w_expert_down)
    depth = w_in.shape[0]
    seq = x_prompt.shape[1]
    assert seq <= max(w for w, _ in DILATED_GROUPS)
    y_p, y_s = x_prompt, x_sample
    states_p, states_s = [], []
    for layer in range(depth):
        y_p, y_s, st_p, st_s = _layer(layer, y_p, y_s, mem_prompt, cache_k, cache_v, state_conv, cache_mem_k,
                                      cache_mem_v, p)
        states_p.append(st_p)
        states_s.append(st_s)
    stack = lambda states, i: jnp.stack([s[i] for s in states], axis=0)
    return (y_p, y_s, stack(states_p, 0), stack(states_p, 1), stack(states_p, 2), stack(states_p, 3),
            stack(states_p, 4), stack(states_s, 0), stack(states_s, 1), stack(states_s, 2))
```

```python
import functools

import jax
import jax.numpy as jnp
import numpy as np
from jax import lax
from jax.experimental import pallas as pl
from jax.experimental.pallas import tpu as pltpu

HEAD_DIM = 128
N_KV_HEADS = 4
DILATED_GROUPS = ((128, 1), (512, 4), (2048, 16))
N_GROUPS = len(DILATED_GROUPS)
N_Q_HEADS = N_GROUPS * N_KV_HEADS
ATTN_WIDTH = N_KV_HEADS * HEAD_DIM
BAND = 128
ATTN_UNROLL = 8
ROPE_THETA = 500000.0
ROT_DIM = HEAD_DIM // 4
CONV_WIDTH = 31
MEM_HEADS = 4
MEM_HEAD_DIM = 128
MEM_WIDTH = MEM_HEADS * MEM_HEAD_DIM
N_EXPERT_GROUPS = 4
EXPERTS_PER_GROUP = 8
N_EXPERTS = N_EXPERT_GROUPS * EXPERTS_PER_GROUP
TOP_K = 2
EPS = 1e-6
NEG = -1e30

LANES = 128
ROW_TILE = 256
MERGE_TILE = 512
IN_PROJ_TILE = 256
MOE_TILE = 256
VMEM_LIMIT = 56 * 1024 * 1024

BF16 = jnp.bfloat16
F32 = jnp.float32


def _params(*sem):
    return pltpu.CompilerParams(dimension_semantics=sem, vmem_limit_bytes=VMEM_LIMIT)


def _dot(a, b):
    return jnp.dot(a, b, preferred_element_type=F32)


def _dot_nt(a, b):
    return lax.dot_general(a, b, (((1,), (1,)), ((), ())), preferred_element_type=F32)


def _dot_tn(a, b):
    return lax.dot_general(a, b, (((0,), (0,)), ((), ())), preferred_element_type=F32)


def _rms(x, g):
    return x * lax.rsqrt(jnp.mean(x * x, axis=-1, keepdims=True) + EPS) * g


def _pack_halves(x):
    c = x.shape[1] // 2
    hi = lax.bitcast_convert_type(x[:, :c].astype(F32), jnp.uint32)
    lo = lax.bitcast_convert_type(x[:, c:].astype(F32), jnp.uint32)
    return hi | (lo >> 16)


def _unpack_halves(p):
    hi = lax.bitcast_convert_type(p & jnp.uint32(0xFFFF0000), F32).astype(BF16)
    lo = lax.bitcast_convert_type(p << 16, F32).astype(BF16)
    return hi, lo


def _const_spec(shape):
    nd = len(shape)
    return pl.BlockSpec(shape, lambda *_: (0,) * nd)


def _in_proj_kernel(x_ref, g_ref, w_ref, qg_ref, kg_ref, mqg_ref, rc_ref, ra_ref, rb_ref,
                    q_ref, k_ref, v_ref, u_ref, mq_ref):
    h = _rms(x_ref[...], g_ref[...]).astype(BF16)
    rc, ra, rb = rc_ref[...], ra_ref[...], rb_ref[...]

    def rope(y):
        return y * rc + pltpu.roll(y, LANES - ROT_DIM // 2, 1) * ra + pltpu.roll(y, ROT_DIM // 2, 1) * rb

    col = 0
    zq = _dot(h, w_ref[:, col:col + N_Q_HEADS * HEAD_DIM])
    for j in range(N_Q_HEADS):
        sl = slice(j * HEAD_DIM, (j + 1) * HEAD_DIM)
        q_ref[:, sl] = rope(_rms(zq[:, sl], qg_ref[...])).astype(BF16)
    col += N_Q_HEADS * HEAD_DIM
    tm = x_ref.shape[0]
    zk = _dot(h, w_ref[:, col:col + ATTN_WIDTH])
    for j in range(N_KV_HEADS):
        sl = slice(j * HEAD_DIM, (j + 1) * HEAD_DIM)
        k_ref[pl.ds(j, tm, stride=N_KV_HEADS), :] = rope(_rms(zk[:, sl], kg_ref[...]))
    col += ATTN_WIDTH
    zv = _dot(h, w_ref[:, col:col + ATTN_WIDTH])
    for j in range(N_KV_HEADS):
        v_ref[pl.ds(j, tm, stride=N_KV_HEADS), :] = zv[:, j * HEAD_DIM:(j + 1) * HEAD_DIM]
    col += ATTN_WIDTH
    conv_ch = u_ref.shape[-1]
    za = _dot(h, w_ref[:, col:col + conv_ch])
    zb = _dot(h, w_ref[:, col + conv_ch:col + 2 * conv_ch])
    u_ref[...] = za * jax.nn.sigmoid(zb)
    col += 2 * conv_ch
    zm = _dot(h, w_ref[:, col:col + MEM_WIDTH])
    for j in range(MEM_HEADS):
        sl = slice(j * MEM_HEAD_DIM, (j + 1) * MEM_HEAD_DIM)
        mq_ref[:, sl] = _rms(zm[:, sl], mqg_ref[...]).astype(BF16)


def _in_proj(x, g_mix, w_bf16, q_g, k_g, mq_g, rope_tabs, conv_ch):
    m, d = x.shape
    tm = min(IN_PROJ_TILE, m)
    n_tab_blocks = rope_tabs[0].shape[0] // tm
    row = lambda i: (i, 0)
    tab = lambda i: (i % n_tab_blocks, 0)
    ncols = w_bf16.shape[1]
    out_shape = (
        jax.ShapeDtypeStruct((m, N_Q_HEADS * HEAD_DIM), BF16),
        jax.ShapeDtypeStruct((m * N_KV_HEADS, HEAD_DIM), F32),
        jax.ShapeDtypeStruct((m * N_KV_HEADS, HEAD_DIM), F32),
        jax.ShapeDtypeStruct((m, conv_ch), F32),
        jax.ShapeDtypeStruct((m, MEM_WIDTH), BF16),
    )
    return pl.pallas_call(
        _in_proj_kernel,
        grid=(m // tm,),
        in_specs=[
            pl.BlockSpec((tm, d), row),
            _const_spec((1, d)),
            _const_spec((d, ncols)),
            _const_spec((1, HEAD_DIM)), _const_spec((1, HEAD_DIM)), _const_spec((1, MEM_HEAD_DIM)),
            pl.BlockSpec((tm, LANES), tab), pl.BlockSpec((tm, LANES), tab), pl.BlockSpec((tm, LANES), tab),
        ],
        out_specs=tuple(pl.BlockSpec((tm * s.shape[0] // m, s.shape[1]), row) for s in out_shape),
        out_shape=out_shape,
        compiler_params=_params("parallel"),
        name="in_proj",
    )(x, g_mix, w_bf16, q_g, k_g, mq_g, *rope_tabs)


def _rope_tables(pos):
    half = ROT_DIM // 2
    inv_freq = jnp.power(jnp.float32(ROPE_THETA), -jnp.arange(half, dtype=F32) * (2.0 / ROT_DIM))
    ang = pos.astype(F32)[:, None] * inv_freq[None, :]
    cos, sin = jnp.cos(ang), jnp.sin(ang)
    n = pos.shape[0]
    ones = jnp.ones((n, LANES - ROT_DIM), F32)
    zeros = jnp.zeros((n, LANES - half), F32)
    rc = jnp.concatenate([cos, cos, ones], axis=1)
    ra = jnp.concatenate([-sin, zeros], axis=1)
    rb = jnp.concatenate([jnp.zeros((n, half), F32), sin, jnp.zeros((n, LANES - ROT_DIM), F32)], axis=1)
    return rc, ra, rb


def _to_residue_layout(dst, src, classes_src, ratio):
    len_src = src.shape[0] // classes_src
    len_dst = len_src // ratio
    for c_src in range(classes_src):
        for a in range(ratio):
            c = c_src + classes_src * a
            dst[c * len_dst:(c + 1) * len_dst, :] = src[pl.ds(c_src * len_src + a, len_dst, stride=ratio), :]


def _prompt_attn_kernel(q0_ref, q1_ref, q2_ref, k_ref, v_ref, o_ref, perm_ref, tmp_ref, acc_ref, lse_ref):
    head = pl.program_id(1)
    seq = q0_ref.shape[0]
    q_refs = (q0_ref, q1_ref, q2_ref)
    dils = [d for _, d in DILATED_GROUPS]
    for t, ref in ((1, k_ref), (2, v_ref)):
        perm_ref[0, t] = ref[pl.ds(head, seq, stride=N_KV_HEADS), :]
        for g in range(1, N_GROUPS):
            _to_residue_layout(perm_ref.at[g, t], perm_ref.at[g - 1, t], dils[g - 1], dils[g] // dils[g - 1])
    for g in range(1, N_GROUPS):
        tmp_ref[0] = q_refs[g][...].astype(F32)
        for step in range(1, g + 1):
            dst = perm_ref.at[g, 0] if step == g else tmp_ref.at[step % 2]
            _to_residue_layout(dst, tmp_ref.at[(step - 1) % 2], dils[step - 1], dils[step] // dils[step - 1])
    iq = lax.broadcasted_iota(jnp.int32, (BAND, BAND), 0)
    ik = lax.broadcasted_iota(jnp.int32, (BAND, BAND), 1)
    keep_c = iq >= ik
    scale = HEAD_DIM ** -0.5

    for g, dil in enumerate(dils):
        nb = seq // (dil * BAND)

        def body(jj, carry, g=g, dil=dil, nb=nb):
            blocks = []
            for u in range(ATTN_UNROLL):
                j = jj * ATTN_UNROLL + u
                rows = pl.ds(pl.multiple_of(j * BAND, BAND), BAND)
                prev = pl.ds(pl.multiple_of(jnp.maximum(j - 1, 0) * BAND, BAND), BAND)
                q = q0_ref[rows, :] if g == 0 else perm_ref[g, 0, rows, :].astype(BF16)
                s_c = jnp.where(keep_c, _dot_nt(q, perm_ref[g, 1, rows, :].astype(BF16)) * scale, NEG)
                s_p = None
                if nb > 1:
                    keep_p = jnp.logical_and(ik >= iq, j % nb > 0)
                    s_p = jnp.where(keep_p, _dot_nt(q, perm_ref[g, 1, prev, :].astype(BF16)) * scale, NEG)
                blocks.append((j, rows, prev, s_c, s_p))
            probs = []
            for j, rows, prev, s_c, s_p in blocks:
                m = jnp.max(s_c, axis=-1, keepdims=True)
                if nb > 1:
                    m = jnp.maximum(m, jnp.max(s_p, axis=-1, keepdims=True))
                p_c = jnp.exp(s_c - m)
                l = jnp.sum(p_c, axis=-1, keepdims=True)
                p_p = None
                if nb > 1:
                    p_p = jnp.exp(s_p - m)
                    l = l + jnp.sum(p_p, axis=-1, keepdims=True)
                probs.append((m, l, p_c.astype(BF16), None if p_p is None else p_p.astype(BF16)))
            for (j, rows, prev, _, _), (m, l, p_c, p_p) in zip(blocks, probs):
                acc = _dot(p_c, perm_ref[g, 2, rows, :].astype(BF16))
                if nb > 1:
                    acc = acc + _dot(p_p, perm_ref[g, 2, prev, :].astype(BF16))
                out_rows = rows if dil == 1 else pl.ds((j % nb) * (BAND * dil) + j // nb, BAND, stride=dil)
                acc_ref[g, out_rows, :] = acc * (1.0 / l)
                lse_ref[g, out_rows, :] = jnp.broadcast_to(m + jnp.log(l), (BAND, LANES))
            return carry

        assert (dil * nb) % ATTN_UNROLL == 0
        lax.fori_loop(0, dil * nb // ATTN_UNROLL, body, 0)

    def combine(c, carry):
        rows = pl.ds(pl.multiple_of(c * BAND, BAND), BAND)
        lses = [lse_ref[g, rows, :] for g in range(N_GROUPS)]
        mx = functools.reduce(jnp.maximum, lses)
        ws = [jnp.exp(l - mx) for l in lses]
        out = functools.reduce(jnp.add, [w * acc_ref[g, rows, :] for g, w in enumerate(ws)])
        o_ref[rows, :] = (out * (1.0 / functools.reduce(jnp.add, ws))).astype(BF16)
        return carry

    lax.fori_loop(0, seq // BAND, combine, 0)


def _prompt_attention(q, k_flat, v_flat, batch, seq):
    for window, dil in DILATED_GROUPS:
        assert window // dil == BAND and seq % (dil * BAND) == 0
    qspec = lambda g: pl.BlockSpec((seq, HEAD_DIM), lambda b, h: (b, g * N_KV_HEADS + h))
    kvspec = pl.BlockSpec((seq * N_KV_HEADS, HEAD_DIM), lambda b, h: (b, 0))
    return pl.pallas_call(
        _prompt_attn_kernel,
        grid=(batch, N_KV_HEADS),
        in_specs=[qspec(0), qspec(1), qspec(2), kvspec, kvspec],
        out_specs=pl.BlockSpec((seq, HEAD_DIM), lambda b, h: (b, h)),
        out_shape=jax.ShapeDtypeStruct((batch * seq, ATTN_WIDTH), BF16),
        scratch_shapes=[pltpu.VMEM((N_GROUPS, 3, seq, HEAD_DIM), F32), pltpu.VMEM((2, seq, HEAD_DIM), F32),
                        pltpu.VMEM((N_GROUPS, seq, HEAD_DIM), F32), pltpu.VMEM((N_GROUPS, seq, LANES), F32)],
        compiler_params=_params("parallel", "arbitrary"),
        name="prompt_attn",
    )(q, q, q, k_flat, v_flat)


def _mem_kv_kernel(x_ref, g_ref, w_ref, kg_ref, k_ref, v_ref):
    tm = x_ref.shape[0]
    h = _rms(x_ref[...], g_ref[...]).astype(BF16)
    zk = _dot(h, w_ref[:, :MEM_WIDTH])
    zv = _dot(h, w_ref[:, MEM_WIDTH:])
    for j in range(MEM_HEADS):
        sl = slice(j * MEM_HEAD_DIM, (j + 1) * MEM_HEAD_DIM)
        k_ref[pl.ds(j, tm, stride=MEM_HEADS), :] = _rms(zk[:, sl], kg_ref[...])
        v_ref[pl.ds(j, tm, stride=MEM_HEADS), :] = zv[:, sl]


def _mem_kv(mem, g, w_bf16, k_g):
    m, d = mem.shape
    tm = min(ROW_TILE, m)
    row = lambda i: (i, 0)
    shp = jax.ShapeDtypeStruct((m * MEM_HEADS, MEM_HEAD_DIM), F32)
    ospec = pl.BlockSpec((tm * MEM_HEADS, MEM_HEAD_DIM), row)
    return pl.pallas_call(
        _mem_kv_kernel,
        grid=(m // tm,),
        in_specs=[pl.BlockSpec((tm, d), row), _const_spec((1, d)), _const_spec((d, 2 * MEM_WIDTH)),
                  _const_spec((1, MEM_HEAD_DIM))],
        out_specs=(ospec, ospec),
        out_shape=(shp, shp),
        compiler_params=_params("parallel"),
        name="mem_kv",
    )(mem, g, w_bf16, k_g)


def _mem_attn_kernel(q_ref, k_ref, v_ref, o_ref):
    scale = MEM_HEAD_DIM ** -0.5
    n_mem = k_ref.shape[0] // MEM_HEADS
    for h in range(MEM_HEADS):
        sl = slice(h * MEM_HEAD_DIM, (h + 1) * MEM_HEAD_DIM)
        head_rows = pl.ds(h, n_mem, stride=MEM_HEADS)
        s = _dot_nt(q_ref[:, sl], k_ref[head_rows, :].astype(BF16)) * scale
        p = jnp.exp(s - jnp.max(s, axis=-1, keepdims=True))
        l = jnp.sum(p, axis=-1, keepdims=True)
        o_ref[:, sl] = (_dot(p.astype(BF16), v_ref[head_rows, :].astype(BF16)) * (1.0 / l)).astype(BF16)


def _mem_attention(mq, mem_k_flat, mem_v_flat, batch, seq):
    rows = mem_k_flat.shape[0] // batch
    tq = min(512, seq)
    nq = seq // tq
    kspec = pl.BlockSpec((rows, MEM_HEAD_DIM), lambda b, i: (b, 0))
    return pl.pallas_call(
        _mem_attn_kernel,
        grid=(batch, nq),
        in_specs=[pl.BlockSpec((tq, MEM_WIDTH), lambda b, i: (b * nq + i, 0)), kspec, kspec],
        out_specs=pl.BlockSpec((tq, MEM_WIDTH), lambda b, i: (b * nq + i, 0)),
        out_shape=jax.ShapeDtypeStruct((batch * seq, MEM_WIDTH), BF16),
        compiler_params=_params("parallel", "parallel"),
        name="mem_attn",
    )(mq, mem_k_flat, mem_v_flat)


CONV_HALO = 32
CONV_CHUNK = 32
CONV_ACC_ROWS = 128
CONV_TAP_UNROLL = 8


def _ln_swish(c, g, b):
    mu = jnp.mean(c, axis=-1, keepdims=True)
    xc = c - mu
    y = xc * lax.rsqrt(jnp.mean(xc * xc, axis=-1, keepdims=True) + EPS) * g + b
    return y * jax.nn.sigmoid(y)


def _conv_prompt_kernel(halo_ref, u_ref, w_ref, b_ref, g_ref, beta_ref, o_ref, ext_ref, conv_ref, wb_ref):
    tc, ch = u_ref.shape
    first = pl.program_id(1) == 0
    lane_tiles = [slice(j * LANES, (j + 1) * LANES) for j in range(ch // LANES)]
    for j, sl in enumerate(lane_tiles):
        halo = jnp.where(first, 0.0, halo_ref[:, sl])
        for dup in range(2):
            ext_ref[j, pl.ds(dup, CONV_HALO, stride=2), :] = halo
            ext_ref[j, pl.ds(2 * CONV_HALO + dup, tc, stride=2), :] = u_ref[:, sl]
    for w in range(CONV_WIDTH):
        wb_ref[w] = jnp.broadcast_to(w_ref[w:w + 1, :], (8, ch))
    lead = CONV_HALO - (CONV_WIDTH - 1)
    rows = CONV_ACC_ROWS
    for j, sl in enumerate(lane_tiles):
        for r0 in range(0, tc, rows):
            def tap(w, acc, j=j, sl=sl, r0=r0):
                win = ext_ref[j, pl.ds(2 * (r0 + lead + w), rows, stride=2), :].reshape(rows // 8, 8, LANES)
                return acc + wb_ref[w, :, sl][None] * win

            acc = lax.fori_loop(0, CONV_WIDTH, tap, jnp.zeros((rows // 8, 8, LANES), F32) + b_ref[:, sl],
                                unroll=CONV_TAP_UNROLL)
            conv_ref[r0:r0 + rows, sl] = acc.reshape(rows, LANES)
    for c0 in range(0, tc, CONV_CHUNK):
        o_ref[c0:c0 + CONV_CHUNK, :] = _ln_swish(conv_ref[c0:c0 + CONV_CHUNK, :], g_ref[...], beta_ref[...]).astype(BF16)


def _conv_prompt(u, conv_w, conv_b, ln_g, ln_b, batch, seq):
    ch = u.shape[1]
    tc = min(ROW_TILE, seq)
    nt = seq // tc
    ratio = tc // CONV_HALO
    return pl.pallas_call(
        _conv_prompt_kernel,
        grid=(batch, nt),
        in_specs=[
            pl.BlockSpec((CONV_HALO, ch), lambda b, i: (jnp.maximum((b * nt + i) * ratio - 1, 0), 0)),
            pl.BlockSpec((tc, ch), lambda b, i: (b * nt + i, 0)),
            _const_spec((CONV_WIDTH, ch)), _const_spec((1, ch)), _const_spec((1, ch)), _const_spec((1, ch)),
        ],
        out_specs=pl.BlockSpec((tc, ch), lambda b, i: (b * nt + i, 0)),
        out_shape=jax.ShapeDtypeStruct((batch * seq, ch), BF16),
        scratch_shapes=[pltpu.VMEM((ch // LANES, 2 * (CONV_HALO + tc), LANES), F32), pltpu.VMEM((tc, ch), F32),
                        pltpu.VMEM((CONV_WIDTH, 8, ch), F32)],
        compiler_params=_params("parallel", "parallel"),
        name="conv_prompt",
    )(u, u, conv_w, conv_b, ln_g, ln_b)


def _conv_sample_kernel(state_ref, new_ref, w_ref, b_ref, g_ref, beta_ref, o_ref, state_out_ref, ext_ref):
    nb, ctx, ch = state_ref.shape
    t = new_ref.shape[1]
    ext_ref[:, 0:ctx, :] = state_ref[...]
    ext_ref[:, ctx:ctx + t, :] = new_ref[...]
    acc = jnp.zeros((nb, t, ch), F32) + b_ref[...]
    for w in range(CONV_WIDTH):
        acc = acc + w_ref[w:w + 1, :] * ext_ref[:, w:w + t, :]
    o_ref[...] = _ln_swish(acc, g_ref[...], beta_ref[...])
    state_out_ref[...] = ext_ref[:, t:t + ctx, :]


def _conv_sample(state, u_new, conv_w, conv_b, ln_g, ln_b):
    batch, ctx, ch = state.shape
    t = u_new.shape[1]
    nb = 8
    blk = lambda n: pl.BlockSpec((nb, n, ch), lambda i: (i, 0, 0))
    return pl.pallas_call(
        _conv_sample_kernel,
        grid=(batch // nb,),
        in_specs=[blk(ctx), blk(t), _const_spec((CONV_WIDTH, ch)), _const_spec((1, ch)), _const_spec((1, ch)),
                  _const_spec((1, ch))],
        out_specs=(blk(t), blk(ctx)),
        out_shape=(jax.ShapeDtypeStruct((batch, t, ch), F32), jax.ShapeDtypeStruct((batch, ctx, ch), F32)),
        scratch_shapes=[pltpu.VMEM((nb, ctx + t + 6, ch), F32)],
        compiler_params=_params("parallel"),
        name="conv_sample",
    )(state, u_new, conv_w, conv_b, ln_g, ln_b)


GROUP_LANES = 16


def _spread_groups(vec, combine):
    t = combine(combine(vec, pltpu.roll(vec, LANES - GROUP_LANES, 1)), pltpu.roll(vec, LANES - 2 * GROUP_LANES, 1))
    lane = lax.broadcasted_iota(jnp.int32, vec.shape, 1)
    return jnp.where(lane < GROUP_LANES, t,
                     jnp.where(lane < 2 * GROUP_LANES, pltpu.roll(t, GROUP_LANES, 1), pltpu.roll(t, 2 * GROUP_LANES, 1)))


SLAB_POS = DILATED_GROUPS[-1][1]
SLAB_ROWS = SLAB_POS * N_KV_HEADS
TAIL_POS = max(w for w, d in DILATED_GROUPS if d < SLAB_POS)


def _sample_attn_kernel(kc_ref, vc_ref, kn_ref, vn_ref, wq_ref, bias_d_ref, bias_t_ref, bias_n_ref, mk_ref, mv_ref,
                        wm_ref, bias_m_ref, kw_ref, vw_ref, a_ref, m_ref, kx_ref, vx_ref):
    n_slab = kc_ref.shape[1]
    new_rows = kn_ref.shape[1]
    t = new_rows // N_KV_HEADS
    tail_slabs = TAIL_POS // SLAB_POS
    n_used = N_GROUPS * GROUP_LANES
    lane = lax.broadcasted_iota(jnp.int32, (1, LANES), 1)

    for src, new, dst in ((kc_ref, kn_ref, kw_ref), (vc_ref, vn_ref, vw_ref)):
        dst[0, :, 0:SLAB_ROWS - new_rows, :] = src[0, :, new_rows:SLAB_ROWS, :]
        dst[0, 0:n_slab - 1, SLAB_ROWS - new_rows:SLAB_ROWS, :] = src[0, 1:n_slab, 0:new_rows, :]
        dst[0, n_slab - 1, SLAB_ROWS - new_rows:SLAB_ROWS, :] = new[0]

    kx_ref[...] = jnp.zeros(kx_ref.shape, F32)
    vx_ref[...] = jnp.zeros(vx_ref.shape, F32)
    kx_ref[0:new_rows, :] = kn_ref[0]
    vx_ref[0:new_rows, :] = vn_ref[0]

    def dilated(ref):
        return ref[0, :, 0:new_rows, :].reshape(n_slab * new_rows, HEAD_DIM).astype(BF16)

    def tail(ref):
        return ref[0, n_slab - tail_slabs:n_slab, :, :].reshape(tail_slabs * SLAB_ROWS, HEAD_DIM).astype(BF16)

    scale = HEAD_DIM ** -0.5
    wq = wq_ref[0]
    s_d = _dot(dilated(kc_ref), wq) * scale + bias_d_ref[...]
    s_t = _dot(tail(kc_ref), wq) * scale + bias_t_ref[...]
    s_n = _dot(kx_ref[...].astype(BF16), wq) * scale + bias_n_ref[...]
    col_max = lambda s: jnp.max(s, axis=0, keepdims=True)
    col_sum = lambda p: jnp.sum(p, axis=0, keepdims=True)
    m_col = jnp.maximum(jnp.maximum(col_max(s_d), col_max(s_t)), col_max(s_n))
    m_joint = jnp.where(lane < n_used, _spread_groups(m_col, jnp.maximum), 0.0)
    p_d, p_t, p_n = jnp.exp(s_d - m_joint), jnp.exp(s_t - m_joint), jnp.exp(s_n - m_joint)
    l_col = col_sum(p_d) + col_sum(p_t) + col_sum(p_n)
    inv = 1.0 / jnp.where(lane < n_used, _spread_groups(l_col, jnp.add), 1.0)
    o = (_dot_tn((p_d * inv).astype(BF16), dilated(vc_ref)) + _dot_tn((p_t * inv).astype(BF16), tail(vc_ref))
         + _dot_tn((p_n * inv).astype(BF16), vx_ref[...].astype(BF16)))
    for h in range(N_KV_HEADS):
        r = h * t
        a_ref[0, :, h * HEAD_DIM:(h + 1) * HEAD_DIM] = functools.reduce(
            jnp.add, [o[g * GROUP_LANES + r:g * GROUP_LANES + r + t, :] for g in range(N_GROUPS)])

    sm = _dot(mk_ref[0].astype(BF16), wm_ref[0]) * (MEM_HEAD_DIM ** -0.5) + bias_m_ref[...]
    pm = jnp.exp(sm - jnp.where(lane < MEM_HEADS * t, col_max(sm), 0.0))
    lm = jnp.where(lane < MEM_HEADS * t, col_sum(pm), 1.0)
    om = _dot_tn((pm * (1.0 / lm)).astype(BF16), mv_ref[0].astype(BF16))
    for h in range(MEM_HEADS):
        m_ref[0, :, h * MEM_HEAD_DIM:(h + 1) * MEM_HEAD_DIM] = om[h * t:(h + 1) * t, :]


def _sample_masks(cache_len, t, n_mem):
    col = np.arange(LANES)[None, :]
    g, c_head, c_tok = col // GROUP_LANES, (col % GROUP_LANES) // t, (col % GROUP_LANES) % t
    used = (col < N_GROUPS * GROUP_LANES) & (col % GROUP_LANES < N_KV_HEADS * t)
    pad = [1] * (LANES // GROUP_LANES - N_GROUPS)
    dil = np.array([d for _, d in DILATED_GROUPS] + pad)[g]
    win = np.array([w for w, _ in DILATED_GROUPS] + pad)[g]
    sparse = dil >= SLAB_POS

    def keep(pos, head, group_sel):
        dist = cache_len + c_tok - pos
        return used & group_sel & (head == c_head) & (dist >= 0) & (dist % dil == 0) & (dist <= win)

    n_slab = cache_len // SLAB_POS
    y = np.arange(n_slab * t * N_KV_HEADS)[:, None]
    keep_d = keep((y // (t * N_KV_HEADS)) * SLAB_POS + (y % (t * N_KV_HEADS)) // N_KV_HEADS, y % N_KV_HEADS, sparse)
    x = np.arange(TAIL_POS * N_KV_HEADS)[:, None]
    keep_t = keep(cache_len - TAIL_POS + x // N_KV_HEADS, x % N_KV_HEADS, ~sparse)
    z = np.arange(LANES)[:, None]
    keep_n = keep(cache_len + z // N_KV_HEADS, z % N_KV_HEADS, True) & (z < t * N_KV_HEADS)
    w = np.arange(n_mem * MEM_HEADS)[:, None]
    keep_m = (col < MEM_HEADS * t) & (w % MEM_HEADS == col // t)
    return tuple(jnp.asarray(np.where(k, 0.0, NEG), F32) for k in (keep_d, keep_t, keep_n, keep_m))


def _query_columns(q, batch, t, n_groups):
    qt = q.reshape(batch, t, n_groups, N_KV_HEADS, HEAD_DIM).transpose(0, 4, 2, 3, 1)
    qt = qt.reshape(batch, HEAD_DIM, n_groups, N_KV_HEADS * t)
    qt = jnp.pad(qt, ((0, 0), (0, 0), (0, 0), (0, GROUP_LANES - N_KV_HEADS * t)))
    qt = qt.reshape(batch, HEAD_DIM, n_groups * GROUP_LANES)
    return jnp.pad(qt, ((0, 0), (0, 0), (0, LANES - n_groups * GROUP_LANES)))


def _sample_attention(q, k_new, v_new, mq, cache_k, cache_v, mem_k, mem_v):
    batch, cache_len = cache_k.shape[0], cache_k.shape[1]
    new_rows = k_new.shape[0] // batch
    t = new_rows // N_KV_HEADS
    n_mem = mem_k.shape[1]
    assert N_KV_HEADS * t <= GROUP_LANES and t <= SLAB_POS and new_rows % 8 == 0
    assert cache_len % SLAB_POS == 0 and cache_len >= max(w for w, _ in DILATED_GROUPS)
    assert all(d == SLAB_POS or w <= TAIL_POS for w, d in DILATED_GROUPS)
    n_slab = cache_len // SLAB_POS
    wq = _query_columns(q, batch, t, N_GROUPS)
    wm = _query_columns(mq, batch, t, 1)
    bias_d, bias_t, bias_n, bias_m = _sample_masks(cache_len, t, n_mem)
    per_b = lambda *shape: pl.BlockSpec((1,) + shape, lambda b: (b,) + (0,) * len(shape))
    slabs = lambda c: c.reshape(batch, n_slab, SLAB_ROWS, HEAD_DIM)
    mem_rows = n_mem * MEM_HEADS
    win = jax.ShapeDtypeStruct((batch, n_slab, SLAB_ROWS, HEAD_DIM), F32)
    k_win, v_win, a, m = pl.pallas_call(
        _sample_attn_kernel,
        grid=(batch,),
        in_specs=[
            per_b(n_slab, SLAB_ROWS, HEAD_DIM), per_b(n_slab, SLAB_ROWS, HEAD_DIM),
            per_b(new_rows, HEAD_DIM), per_b(new_rows, HEAD_DIM), per_b(HEAD_DIM, LANES),
            _const_spec(bias_d.shape), _const_spec(bias_t.shape), _const_spec(bias_n.shape),
            per_b(mem_rows, MEM_HEAD_DIM), per_b(mem_rows, MEM_HEAD_DIM), per_b(MEM_HEAD_DIM, LANES),
            _const_spec(bias_m.shape),
        ],
        out_specs=(per_b(n_slab, SLAB_ROWS, HEAD_DIM), per_b(n_slab, SLAB_ROWS, HEAD_DIM), per_b(t, ATTN_WIDTH),
                   per_b(t, MEM_WIDTH)),
        out_shape=(win, win, jax.ShapeDtypeStruct((batch, t, ATTN_WIDTH), F32),
                   jax.ShapeDtypeStruct((batch, t, MEM_WIDTH), F32)),
        scratch_shapes=[pltpu.VMEM((LANES, HEAD_DIM), F32), pltpu.VMEM((LANES, HEAD_DIM), F32)],
        compiler_params=_params("parallel"),
        name="sample_attn",
    )(slabs(cache_k), slabs(cache_v), k_new.reshape(batch, new_rows, HEAD_DIM), v_new.reshape(batch, new_rows, HEAD_DIM),
      wq, bias_d, bias_t, bias_n, mem_k.reshape(batch, mem_rows, MEM_HEAD_DIM),
      mem_v.reshape(batch, mem_rows, MEM_HEAD_DIM), wm, bias_m)
    return k_win.reshape(cache_k.shape), v_win.reshape(cache_v.shape), a, m


ROUTE_GROUP_LANE0 = 0
ROUTE_EXPERT_LANE0 = N_EXPERT_GROUPS


def _route(logits):
    lane = lax.broadcasted_iota(jnp.int32, logits.shape, 1).astype(F32)
    big = float(LANES)

    def masked_softmax(keep):
        z = jnp.where(keep, logits, NEG)
        e = jnp.where(keep, jnp.exp(z - jnp.max(z, axis=-1, keepdims=True)), 0.0)
        return e / jnp.sum(e, axis=-1, keepdims=True)

    def first_argmax(vals, keep):
        top = jnp.max(jnp.where(keep, vals, -1.0), axis=-1, keepdims=True)
        idx = jnp.min(jnp.where(jnp.logical_and(keep, vals == top), lane, big), axis=-1, keepdims=True)
        return top, idx

    is_group = lane < N_EXPERT_GROUPS
    pg = masked_softmax(is_group)
    pg_top, g_idx = first_argmax(pg, is_group)
    lo = ROUTE_EXPERT_LANE0 + g_idx * EXPERTS_PER_GROUP
    in_group = jnp.logical_and(lane >= lo, lane < lo + EXPERTS_PER_GROUP)
    pe = masked_softmax(in_group)
    p1, i1 = first_argmax(pe, in_group)
    p2, i2 = first_argmax(pe, jnp.logical_and(in_group, lane != i1))
    denom = p1 + p2
    w1 = pg_top * p1 / denom
    w2 = pg_top * p2 / denom
    return lane, i1 - ROUTE_EXPERT_LANE0, i2 - ROUTE_EXPERT_LANE0, w1, w2


def _merge_kernel(x_ref, gmix_ref, ao_ref, cc_ref, mo_ref, wgate_ref, wa_ref, wc_ref, wm_ref, wo_ref, gffn_ref, wr_ref,
                  br_ref, cnt_in_ref, x1_ref, h2_ref, route_ref, cnt_out_ref, cnt_ref):
    @pl.when(pl.program_id(0) == 0)
    def _():
        cnt_ref[...] = cnt_in_ref[...]

    d = x_ref.shape[1]
    x = x_ref[...]
    h = _rms(x, gmix_ref[...]).astype(BF16)
    a = _dot(ao_ref[...].astype(BF16), wa_ref[...])
    c = _dot(cc_ref[...].astype(BF16), wc_ref[...])
    m = _dot(mo_ref[...].astype(BF16), wm_ref[...])
    z = jax.nn.sigmoid(_dot(h, wgate_ref[:, 0:d])) * a
    z = z + jax.nn.sigmoid(_dot(h, wgate_ref[:, d:2 * d])) * c
    z = z + jax.nn.sigmoid(_dot(h, wgate_ref[:, 2 * d:3 * d])) * m
    x1 = x + _dot(z.astype(BF16), wo_ref[...])
    x1_ref[...] = x1
    h2 = _rms(x1, gffn_ref[...]).astype(BF16)
    h2_ref[...] = _pack_halves(h2)
    logits = _dot(h2, wr_ref[...]) + br_ref[...]
    lane, e1, e2, w1, w2 = _route(logits)

    tm = x_ref.shape[0]
    hit1 = jnp.where(lane == e1, 1.0, 0.0)
    hit2 = jnp.where(lane == e2, 1.0, 0.0)
    hits = hit1 + hit2
    earlier = (lax.broadcasted_iota(jnp.int32, (tm, tm), 1) < lax.broadcasted_iota(jnp.int32, (tm, tm), 0))
    before = _dot(jnp.where(earlier, 1.0, 0.0).astype(BF16), hits.astype(BF16)) + cnt_ref[...]
    rank1 = jnp.sum(hit1 * before, axis=-1, keepdims=True)
    rank2 = jnp.sum(hit2 * before, axis=-1, keepdims=True)
    cnt_ref[...] = cnt_ref[...] + jnp.sum(hits, axis=0, keepdims=True)
    cnt_out_ref[...] = cnt_ref[...]
    route = jnp.zeros(logits.shape, F32)
    for i, val in enumerate((e1, e2, w1, w2, rank1, rank2)):
        route = jnp.where(lane == i, val, route)
    route_ref[...] = route


ROUTE_E, ROUTE_W, ROUTE_RANK = 0, 2, 4


def _merge(x, g_mix, ao, cc, mo, w_gate, w_a, w_c, w_m, w_o, g_ffn, w_r, b_r, counts):
    m, d = x.shape
    tm = min(MERGE_TILE, m)
    row = lambda i: (i, 0)
    rows = lambda arr: pl.BlockSpec((tm, arr.shape[1]), row)
    ins = [x, g_mix, ao, cc, mo, w_gate, w_a, w_c, w_m, w_o, g_ffn, w_r, b_r, counts]
    specs = ([rows(x), _const_spec(g_mix.shape), rows(ao), rows(cc), rows(mo)]
             + [_const_spec(a.shape) for a in (w_gate, w_a, w_c, w_m, w_o, g_ffn, w_r, b_r, counts)])
    out_shape = (jax.ShapeDtypeStruct((m, d), F32), jax.ShapeDtypeStruct((m, d // 2), jnp.uint32),
                 jax.ShapeDtypeStruct((m, LANES), F32), jax.ShapeDtypeStruct((1, LANES), F32))
    return pl.pallas_call(
        _merge_kernel,
        grid=(m // tm,),
        in_specs=specs,
        out_specs=(rows(out_shape[0]), rows(out_shape[1]), rows(out_shape[2]), _const_spec((1, LANES))),
        out_shape=out_shape,
        scratch_shapes=[pltpu.VMEM((1, LANES), F32)],
        compiler_params=_params("arbitrary"),
        name="merge",
    )(*ins)


def _routing_tables(counts, routes, n_tiles):
    cnt = counts[0, :N_EXPERTS].astype(jnp.int32)
    tiles = (cnt + MOE_TILE - 1) // MOE_TILE
    tile_end = jnp.cumsum(tiles)
    row_start = (tile_end - tiles) * MOE_TILE
    n_used = tile_end[-1]
    tile_ids = jnp.minimum(jnp.arange(n_tiles, dtype=jnp.int32), n_used - 1)
    tile_expert = jnp.sum(tile_ids[:, None] >= tile_end[None, :], axis=1).astype(jnp.int32)
    ids = routes[:, ROUTE_E:ROUTE_E + TOP_K].astype(jnp.int32)
    start_of = jnp.sum(jnp.where(ids[:, :, None] == jnp.arange(N_EXPERTS)[None, None, :], row_start[None, None, :], 0),
                       axis=-1)
    pos = start_of + routes[:, ROUTE_RANK:ROUTE_RANK + TOP_K].astype(jnp.int32)
    return pos.reshape(-1), tile_expert, n_used.reshape(1).astype(jnp.int32)


def _dispatch_kernel(pos_ref, h_ref, xs_in, xs_out, sem, *, row0):
    del xs_in
    tm = h_ref.shape[0]
    base = (row0 + pl.program_id(0) * tm) * TOP_K

    def issue(j, carry):
        for k in range(TOP_K):
            slot = pos_ref[base + j * TOP_K + k]
            pltpu.make_async_copy(h_ref.at[pl.ds(j, 1)], xs_out.at[pl.ds(slot, 1)], sem).start()
        return carry

    lax.fori_loop(0, tm, issue, 0, unroll=8)
    for k in range(TOP_K):
        pltpu.make_async_copy(h_ref, xs_out.at[pl.ds(0, tm)], sem).wait()


def _dispatch(h2, pos, xs, row0):
    m, d = h2.shape
    tm = min(ROW_TILE, m)
    return pl.pallas_call(
        functools.partial(_dispatch_kernel, row0=row0),
        grid_spec=pltpu.PrefetchScalarGridSpec(
            num_scalar_prefetch=1,
            grid=(m // tm,),
            in_specs=[pl.BlockSpec((tm, d), lambda i, *_: (i, 0)), pl.BlockSpec(memory_space=pl.ANY)],
            out_specs=pl.BlockSpec(memory_space=pl.ANY),
            scratch_shapes=[pltpu.SemaphoreType.DMA(())],
        ),
        out_shape=jax.ShapeDtypeStruct(xs.shape, xs.dtype),
        input_output_aliases={2: 0},
        compiler_params=_params("arbitrary"),
        name="moe_dispatch",
    )(pos, h2, xs)


def _gmm_kernel(te_ref, nused_ref, x_ref, wg_ref, wu_ref, wd_ref, o_ref, wg_s, wu_s, wd_s):
    t = pl.program_id(0)
    used = t < nused_ref[0]
    new_expert = jnp.logical_or(t == 0, te_ref[t] != te_ref[jnp.maximum(t - 1, 0)])

    @pl.when(jnp.logical_and(used, new_expert))
    def _():
        wg_s[...] = wg_ref[0].astype(BF16)
        wu_s[...] = wu_ref[0].astype(BF16)
        wd_s[...] = wd_ref[0].astype(BF16)

    @pl.when(used)
    def _():
        x_hi, x_lo = _unpack_halves(x_ref[...])
        half = x_ref.shape[1]
        gate = _dot(x_hi, wg_s[0:half, :]) + _dot(x_lo, wg_s[half:, :])
        up = _dot(x_hi, wu_s[0:half, :]) + _dot(x_lo, wu_s[half:, :])
        hid = gate * jax.nn.sigmoid(gate) * up
        o_ref[...] = _dot(hid.astype(BF16), wd_s[...])

    @pl.when(jnp.logical_not(used))
    def _():
        o_ref[...] = jnp.zeros(o_ref.shape, F32)


def _grouped_mlp(xs, tile_expert, n_used, w_gate, w_up, w_down):
    n_tiles = xs.shape[0] // MOE_TILE
    d, ff = w_gate.shape[1], w_gate.shape[2]
    wspec = lambda a, b: pl.BlockSpec((1, a, b), lambda t, te, nu: (te[t], 0, 0))
    return pl.pallas_call(
        _gmm_kernel,
        grid_spec=pltpu.PrefetchScalarGridSpec(
            num_scalar_prefetch=2,
            grid=(n_tiles,),
            in_specs=[pl.BlockSpec((MOE_TILE, xs.shape[1]), lambda t, te, nu: (jnp.minimum(t, nu[0] - 1), 0)),
                      wspec(d, ff), wspec(d, ff), wspec(ff, d)],
            out_specs=pl.BlockSpec((MOE_TILE, d), lambda t, *_: (t, 0)),
            scratch_shapes=[pltpu.VMEM((d, ff), BF16), pltpu.VMEM((d, ff), BF16), pltpu.VMEM((ff, d), BF16)],
        ),
        out_shape=jax.ShapeDtypeStruct((n_tiles * MOE_TILE, d), F32),
        compiler_params=_params("arbitrary"),
        name="moe_gmm",
    )(tile_expert, n_used, xs, w_gate, w_up, w_down)


def _combine_kernel(pos_ref, x1_ref, route_ref, ys_hbm, y_ref, buf_ref, sems, *, row0):
    tm = x1_ref.shape[0]
    i = pl.program_id(0)

    def gather(tile, half):
        base = (row0 + tile * tm) * TOP_K

        def issue(j, carry):
            for k in range(TOP_K):
                slot = pos_ref[base + j * TOP_K + k]
                pltpu.make_async_copy(ys_hbm.at[pl.ds(slot, 1)], buf_ref.at[half, k, pl.ds(j, 1)],
                                      sems.at[half]).start()
            return carry

        lax.fori_loop(0, tm, issue, 0, unroll=4)

    @pl.when(i == 0)
    def _():
        gather(0, 0)

    @pl.when(i + 1 < pl.num_programs(0))
    def _():
        gather(i + 1, (i + 1) % 2)

    half = i % 2
    for k in range(TOP_K):
        pltpu.make_async_copy(ys_hbm.at[pl.ds(0, tm)], buf_ref.at[half, k], sems.at[half]).wait()
    route = route_ref[...]
    y_ref[...] = (x1_ref[...] + route[:, ROUTE_W:ROUTE_W + 1] * buf_ref[half, 0]
                  + route[:, ROUTE_W + 1:ROUTE_W + 2] * buf_ref[half, 1])


def _combine(x1, route, pos, ys, row0):
    m, d = x1.shape
    tm = min(ROW_TILE, m)
    row = lambda i, *_: (i, 0)
    return pl.pallas_call(
        functools.partial(_combine_kernel, row0=row0),
        grid_spec=pltpu.PrefetchScalarGridSpec(
            num_scalar_prefetch=1,
            grid=(m // tm,),
            in_specs=[pl.BlockSpec((tm, d), row), pl.BlockSpec((tm, LANES), row), pl.BlockSpec(memory_space=pl.ANY)],
            out_specs=pl.BlockSpec((tm, d), row),
            scratch_shapes=[pltpu.VMEM((2, TOP_K, tm, d), F32), pltpu.SemaphoreType.DMA((2,))],
        ),
        out_shape=jax.ShapeDtypeStruct((m, d), F32),
        compiler_params=_params("arbitrary"),
        name="moe_combine",
    )(pos, x1, route, ys)


def _layer(layer, x_prompt, x_sample, mem_prompt, cache_k, cache_v, state_conv, cache_mem_k, cache_mem_v, p):
    batch, seq, d = x_prompt.shape
    dec_batch, dec_seq, _ = x_sample.shape
    conv_ch = p["conv_w"].shape[-1]
    n_in = N_Q_HEADS * HEAD_DIM + 2 * ATTN_WIDTH + 2 * conv_ch + MEM_WIDTH
    past_len = cache_k.shape[2]

    row2 = lambda name: p[name][layer][None, :]
    w_in = p["w_in"][layer]
    w_main = w_in[:, :n_in].astype(BF16)
    w_gate = w_in[:, n_in:].astype(BF16)
    w_a, w_c, w_m, w_o = (p[n][layer].astype(BF16) for n in ("w_attn_proj", "w_conv_proj", "w_mem_proj", "w_out"))
    w_router = jnp.concatenate(
        [p["w_router_group"][layer], p["w_router_expert"][layer].transpose(1, 0, 2).reshape(d, N_EXPERTS)], axis=1)
    w_router = jnp.pad(w_router, ((0, 0), (0, LANES - w_router.shape[1]))).astype(BF16)
    b_router = jnp.concatenate([p["b_router_group"][layer], p["b_router_expert"][layer].reshape(-1)])
    b_router = jnp.pad(b_router, (0, LANES - b_router.shape[0]))[None, :]
    conv_args = (p["conv_w"][layer], row2("conv_b"), row2("conv_ln_g"), row2("conv_ln_b"))
    merge_w = (w_gate, w_a, w_c, w_m, w_o, row2("norm_ffn_g"), w_router, b_router)

    xp = x_prompt.reshape(batch * seq, d)
    tabs_p = _rope_tables(jnp.arange(seq, dtype=jnp.int32))
    q_p, k_p, v_p, u_p, mq_p = _in_proj(xp, row2("norm_mix_g"), w_main, row2("q_norm_g"), row2("k_norm_g"),
                                        row2("mq_norm_g"), tabs_p, conv_ch)
    ao_p = _prompt_attention(q_p, k_p, v_p, batch, seq)
    mem_k_p, mem_v_p = _mem_kv(mem_prompt.reshape(-1, d), row2("mem_norm_g"), p["w_mem_kv"][layer].astype(BF16),
                               row2("mk_norm_g"))
    mo_p = _mem_attention(mq_p, mem_k_p, mem_v_p, batch, seq)
    cc_p = _conv_prompt(u_p, *conv_args, batch, seq)
    x1_p, h2_p, route_p, counts = _merge(xp, row2("norm_mix_g"), ao_p, cc_p, mo_p, *merge_w,
                                         jnp.zeros((1, LANES), F32))

    xs = x_sample.reshape(dec_batch * dec_seq, d)
    tabs_s = _rope_tables(jnp.tile(past_len + jnp.arange(dec_seq, dtype=jnp.int32), dec_batch))
    q_s, k_s, v_s, u_s, mq_s = _in_proj(xs, row2("norm_mix_g"), w_main, row2("q_norm_g"), row2("k_norm_g"),
                                        row2("mq_norm_g"), tabs_s, conv_ch)
    k_win_s, v_win_s, a_s, mo_s = _sample_attention(q_s, k_s, v_s, mq_s, cache_k[layer], cache_v[layer],
                                                    cache_mem_k[layer], cache_mem_v[layer])
    cc_s, conv_state_s = _conv_sample(state_conv[layer], u_s.reshape(dec_batch, dec_seq, conv_ch), *conv_args)
    x1_s, h2_s, route_s, counts = _merge(xs, row2("norm_mix_g"), a_s.reshape(-1, ATTN_WIDTH),
                                         cc_s.reshape(-1, conv_ch), mo_s.reshape(-1, MEM_WIDTH), *merge_w, counts)

    n_p, n_s = xp.shape[0], xs.shape[0]
    n_tok = n_p + n_s
    n_tiles = (TOP_K * n_tok + N_EXPERTS * (MOE_TILE - 1)) // MOE_TILE + 1
    pos, tile_expert, n_used = _routing_tables(counts, jnp.concatenate([route_p, route_s], axis=0), n_tiles)
    slots = _dispatch(h2_p, pos, jnp.zeros((n_tiles * MOE_TILE, h2_p.shape[1]), h2_p.dtype), 0)
    slots = _dispatch(h2_s, pos, slots, n_p)
    ys = _grouped_mlp(slots, tile_expert, n_used, p["w_expert_gate"][layer], p["w_expert_up"][layer],
                      p["w_expert_down"][layer])
    y_p = _combine(x1_p, route_p, pos, ys, 0)
    y_s = _combine(x1_s, route_s, pos, ys, n_p)

    state_p = (k_p.reshape(batch, seq, N_KV_HEADS, HEAD_DIM), v_p.reshape(batch, seq, N_KV_HEADS, HEAD_DIM),
               u_p.reshape(batch, seq, conv_ch)[:, seq - (CONV_WIDTH - 1):],
               mem_k_p.reshape(batch, -1, MEM_HEADS, MEM_HEAD_DIM), mem_v_p.reshape(batch, -1, MEM_HEADS, MEM_HEAD_DIM))
    state_s = (k_win_s, v_win_s, conv_state_s)
    return y_p.reshape(batch, seq, d), y_s.reshape(dec_batch, dec_seq, d), state_p, state_s


def kernel(x_prompt, x_sample, mem_prompt, cache_k, cache_v, state_conv, cache_mem_k, cache_mem_v, norm_mix_g, w_in, q_norm_g, k_norm_g, conv_w, conv_b, conv_ln_g, conv_ln_b, mem_norm_g, w_mem_kv, mq_norm_g, mk_norm_g, w_attn_proj, w_conv_proj, w_mem_proj, w_out, norm_ffn_g, w_router_group, b_router_group, w_router_expert, b_router_expert, w_expert_gate, w_expert_up, w_expert_down):
    p = dict(norm_mix_g=norm_mix_g, w_in=w_in, q_norm_g=q_norm_g, k_norm_g=k_norm_g, conv_w=conv_w, conv_b=conv_b,
             conv_ln_g=conv_ln_g, conv_ln_b=conv_ln_b, mem_norm_g=mem_norm_g, w_mem_kv=w_mem_kv, mq_norm_g=mq_norm_g,
             mk_norm_g=mk_norm_g, w_attn_proj=w_attn_proj, w_conv_proj=w_conv_proj, w_mem_proj=w_mem_proj,
             w_out=w_out, norm_ffn_g=norm_ffn_g, w_router_group=w_router_group, b_router_group=b_router_group,
             w_router_expert=w_router_expert, b_router_expert=b_router_expert, w_expert_gate=w_expert_gate,
             w_expert_up=w_expert_up, w_expert_down=w_expert_down)
    depth = w_in.shape[0]
    seq = x_prompt.shape[1]
    assert seq <= max(w for w, _ in DILATED_GROUPS)
    y_p, y_s = x_prompt, x_sample
    states_p, states_s = [], []
    for layer in range(depth):
        y_p, y_s, st_p, st_s = _layer(layer, y_p, y_s, mem_prompt, cache_k, cache_v, state_conv, cache_mem_k,
                                      cache_mem_v, p)
        states_p.append(st_p)
        states_s.append(st_s)
    stack = lambda states, i: jnp.stack([s[i] for s in states], axis=0)
    return (y_p, y_s, stack(states_p, 0), stack(states_p, 1), stack(states_p, 2), stack(states_p, 3),
            stack(states_p, 4), stack(states_s, 0), stack(states_s, 1), stack(states_s, 2))
```

```python
import functools

import jax
import jax.numpy as jnp
import numpy as np
from jax import lax
from jax.experimental import pallas as pl
from jax.experimental.pallas import tpu as pltpu

HEAD_DIM = 128
N_KV_HEADS = 4
DILATED_GROUPS = ((128, 1), (512, 4), (2048, 16))
N_GROUPS = len(DILATED_GROUPS)
N_Q_HEADS = N_GROUPS * N_KV_HEADS
ATTN_WIDTH = N_KV_HEADS * HEAD_DIM
BAND = 128
ATTN_UNROLL = 8
ROPE_THETA = 500000.0
ROT_DIM = HEAD_DIM // 4
CONV_WIDTH = 31
MEM_HEADS = 4
MEM_HEAD_DIM = 128
MEM_WIDTH = MEM_HEADS * MEM_HEAD_DIM
N_EXPERT_GROUPS = 4
EXPERTS_PER_GROUP = 8
N_EXPERTS = N_EXPERT_GROUPS * EXPERTS_PER_GROUP
TOP_K = 2
EPS = 1e-6
NEG = -1e30

LANES = 128
ROW_TILE = 256
MERGE_TILE = 512
IN_PROJ_TILE = 256
MOE_TILE = 256
VMEM_LIMIT = 56 * 1024 * 1024

BF16 = jnp.bfloat16
F32 = jnp.float32


def _params(*sem):
    return pltpu.CompilerParams(dimension_semantics=sem, vmem_limit_bytes=VMEM_LIMIT)


def _dot(a, b):
    return jnp.dot(a, b, preferred_element_type=F32)


def _dot_nt(a, b):
    return lax.dot_general(a, b, (((1,), (1,)), ((), ())), preferred_element_type=F32)


def _dot_tn(a, b):
    return lax.dot_general(a, b, (((0,), (0,)), ((), ())), preferred_element_type=F32)


def _rms(x, g):
    return x * lax.rsqrt(jnp.mean(x * x, axis=-1, keepdims=True) + EPS) * g


def _pack_halves(x):
    c = x.shape[1] // 2
    hi = lax.bitcast_convert_type(x[:, :c].astype(F32), jnp.uint32)
    lo = lax.bitcast_convert_type(x[:, c:].astype(F32), jnp.uint32)
    return hi | (lo >> 16)


def _unpack_halves(p):
    hi = lax.bitcast_convert_type(p & jnp.uint32(0xFFFF0000), F32).astype(BF16)
    lo = lax.bitcast_convert_type(p << 16, F32).astype(BF16)
    return hi, lo


def _const_spec(shape):
    nd = len(shape)
    return pl.BlockSpec(shape, lambda *_: (0,) * nd)


def _in_proj_kernel(x_ref, g_ref, w_ref, qg_ref, kg_ref, mqg_ref, rc_ref, ra_ref, rb_ref,
                    q_ref, k_ref, v_ref, u_ref, mq_ref):
    h = _rms(x_ref[...], g_ref[...]).astype(BF16)
    rc, ra, rb = rc_ref[...], ra_ref[...], rb_ref[...]

    def rope(y):
        return y * rc + pltpu.roll(y, LANES - ROT_DIM // 2, 1) * ra + pltpu.roll(y, ROT_DIM // 2, 1) * rb

    col = 0
    zq = _dot(h, w_ref[:, col:col + N_Q_HEADS * HEAD_DIM])
    for j in range(N_Q_HEADS):
        sl = slice(j * HEAD_DIM, (j + 1) * HEAD_DIM)
        q_ref[:, sl] = rope(_rms(zq[:, sl], qg_ref[...])).astype(BF16)
    col += N_Q_HEADS * HEAD_DIM
    tm = x_ref.shape[0]
    zk = _dot(h, w_ref[:, col:col + ATTN_WIDTH])
    for j in range(N_KV_HEADS):
        sl = slice(j * HEAD_DIM, (j + 1) * HEAD_DIM)
        k_ref[pl.ds(j, tm, stride=N_KV_HEADS), :] = rope(_rms(zk[:, sl], kg_ref[...]))
    col += ATTN_WIDTH
    zv = _dot(h, w_ref[:, col:col + ATTN_WIDTH])
    for j in range(N_KV_HEADS):
        v_ref[pl.ds(j, tm, stride=N_KV_HEADS), :] = zv[:, j * HEAD_DIM:(j + 1) * HEAD_DIM]
    col += ATTN_WIDTH
    conv_ch = u_ref.shape[-1]
    za = _dot(h, w_ref[:, col:col + conv_ch])
    zb = _dot(h, w_ref[:, col + conv_ch:col + 2 * conv_ch])
    u_ref[...] = za * jax.nn.sigmoid(zb)
    col += 2 * conv_ch
    zm = _dot(h, w_ref[:, col:col + MEM_WIDTH])
    for j in range(MEM_HEADS):
        sl = slice(j * MEM_HEAD_DIM, (j + 1) * MEM_HEAD_DIM)
        mq_ref[:, sl] = _rms(zm[:, sl], mqg_ref[...]).astype(BF16)


def _in_proj(x, g_mix, w_bf16, q_g, k_g, mq_g, rope_tabs, conv_ch):
    m, d = x.shape
    tm = min(IN_PROJ_TILE, m)
    n_tab_blocks = rope_tabs[0].shape[0] // tm
    row = lambda i: (i, 0)
    tab = lambda i: (i % n_tab_blocks, 0)
    ncols = w_bf16.shape[1]
    out_shape = (
        jax.ShapeDtypeStruct((m, N_Q_HEADS * HEAD_DIM), BF16),
        jax.ShapeDtypeStruct((m * N_KV_HEADS, HEAD_DIM), F32),
        jax.ShapeDtypeStruct((m * N_KV_HEADS, HEAD_DIM), F32),
        jax.ShapeDtypeStruct((m, conv_ch), F32),
        jax.ShapeDtypeStruct((m, MEM_WIDTH), BF16),
    )
    return pl.pallas_call(
        _in_proj_kernel,
        grid=(m // tm,),
        in_specs=[
            pl.BlockSpec((tm, d), row),
            _const_spec((1, d)),
            _const_spec((d, ncols)),
            _const_spec((1, HEAD_DIM)), _const_spec((1, HEAD_DIM)), _const_spec((1, MEM_HEAD_DIM)),
            pl.BlockSpec((tm, LANES), tab), pl.BlockSpec((tm, LANES), tab), pl.BlockSpec((tm, LANES), tab),
        ],
        out_specs=tuple(pl.BlockSpec((tm * s.shape[0] // m, s.shape[1]), row) for s in out_shape),
        out_shape=out_shape,
        compiler_params=_params("parallel"),
        name="in_proj",
    )(x, g_mix, w_bf16, q_g, k_g, mq_g, *rope_tabs)


def _rope_tables(pos):
    half = ROT_DIM // 2
    inv_freq = jnp.power(jnp.float32(ROPE_THETA), -jnp.arange(half, dtype=F32) * (2.0 / ROT_DIM))
    ang = pos.astype(F32)[:, None] * inv_freq[None, :]
    cos, sin = jnp.cos(ang), jnp.sin(ang)
    n = pos.shape[0]
    ones = jnp.ones((n, LANES - ROT_DIM), F32)
    zeros = jnp.zeros((n, LANES - half), F32)
    rc = jnp.concatenate([cos, cos, ones], axis=1)
    ra = jnp.concatenate([-sin, zeros], axis=1)
    rb = jnp.concatenate([jnp.zeros((n, half), F32), sin, jnp.zeros((n, LANES - ROT_DIM), F32)], axis=1)
    return rc, ra, rb


def _to_residue_layout(dst, src, classes_src, ratio):
    len_src = src.shape[0] // classes_src
    len_dst = len_src // ratio
    for c_src in range(classes_src):
        for a in range(ratio):
            c = c_src + classes_src * a
            dst[c * len_dst:(c + 1) * len_dst, :] = src[pl.ds(c_src * len_src + a, len_dst, stride=ratio), :]


def _prompt_attn_kernel(q0_ref, q1_ref, q2_ref, k_ref, v_ref, o_ref, perm_ref, tmp_ref, acc_ref, lse_ref):
    head = pl.program_id(1)
    seq = q0_ref.shape[0]
    q_refs = (q0_ref, q1_ref, q2_ref)
    dils = [d for _, d in DILATED_GROUPS]
    for t, ref in ((1, k_ref), (2, v_ref)):
        perm_ref[0, t] = ref[pl.ds(head, seq, stride=N_KV_HEADS), :]
        for g in range(1, N_GROUPS):
            _to_residue_layout(perm_ref.at[g, t], perm_ref.at[g - 1, t], dils[g - 1], dils[g] // dils[g - 1])
    for g in range(1, N_GROUPS):
        tmp_ref[0] = q_refs[g][...].astype(F32)
        for step in range(1, g + 1):
            dst = perm_ref.at[g, 0] if step == g else tmp_ref.at[step % 2]
            _to_residue_layout(dst, tmp_ref.at[(step - 1) % 2], dils[step - 1], dils[step] // dils[step - 1])
    iq = lax.broadcasted_iota(jnp.int32, (BAND, BAND), 0)
    ik = lax.broadcasted_iota(jnp.int32, (BAND, BAND), 1)
    keep_c = iq >= ik
    scale = HEAD_DIM ** -0.5

    for g, dil in enumerate(dils):
        nb = seq // (dil * BAND)

        def body(jj, carry, g=g, dil=dil, nb=nb):
            blocks = []
            for u in range(ATTN_UNROLL):
                j = jj * ATTN_UNROLL + u
                rows = pl.ds(pl.multiple_of(j * BAND, BAND), BAND)
                prev = pl.ds(pl.multiple_of(jnp.maximum(j - 1, 0) * BAND, BAND), BAND)
                q = q0_ref[rows, :] if g == 0 else perm_ref[g, 0, rows, :].astype(BF16)
                s_c = jnp.where(keep_c, _dot_nt(q, perm_ref[g, 1, rows, :].astype(BF16)) * scale, NEG)
                s_p = None
                if nb > 1:
                    keep_p = jnp.logical_and(ik >= iq, j % nb > 0)
                    s_p = jnp.where(keep_p, _dot_nt(q, perm_ref[g, 1, prev, :].astype(BF16)) * scale, NEG)
                blocks.append((j, rows, prev, s_c, s_p))
            probs = []
            for j, rows, prev, s_c, s_p in blocks:
                m = jnp.max(s_c if nb == 1 else jnp.maximum(s_c, s_p), axis=-1, keepdims=True)
                p_c = jnp.exp(s_c - m)
                p_p = None if nb == 1 else jnp.exp(s_p - m)
                l = jnp.sum(p_c if nb == 1 else p_c + p_p, axis=-1, keepdims=True)
                probs.append((m, l, p_c.astype(BF16), None if p_p is None else p_p.astype(BF16)))
            for (j, rows, prev, _, _), (m, l, p_c, p_p) in zip(blocks, probs):
                acc = _dot(p_c, perm_ref[g, 2, rows, :].astype(BF16))
                if nb > 1:
                    acc = acc + _dot(p_p, perm_ref[g, 2, prev, :].astype(BF16))
                out_rows = rows if dil == 1 else pl.ds((j % nb) * (BAND * dil) + j // nb, BAND, stride=dil)
                acc_ref[g, out_rows, :] = acc * (1.0 / l)
                lse_ref[g, out_rows, :] = jnp.broadcast_to(m + jnp.log(l), (BAND, LANES))
            return carry

        assert (dil * nb) % ATTN_UNROLL == 0
        lax.fori_loop(0, dil * nb // ATTN_UNROLL, body, 0)

    def combine(c, carry):
        rows = pl.ds(pl.multiple_of(c * BAND, BAND), BAND)
        lses = [lse_ref[g, rows, :] for g in range(N_GROUPS)]
        mx = functools.reduce(jnp.maximum, lses)
        ws = [jnp.exp(l - mx) for l in lses]
        out = functools.reduce(jnp.add, [w * acc_ref[g, rows, :] for g, w in enumerate(ws)])
        o_ref[rows, :] = (out * (1.0 / functools.reduce(jnp.add, ws))).astype(BF16)
        return carry

    lax.fori_loop(0, seq // BAND, combine, 0)


def _prompt_attention(q, k_flat, v_flat, batch, seq):
    for window, dil in DILATED_GROUPS:
        assert window // dil == BAND and seq % (dil * BAND) == 0
    qspec = lambda g: pl.BlockSpec((seq, HEAD_DIM), lambda b, h: (b, g * N_KV_HEADS + h))
    kvspec = pl.BlockSpec((seq * N_KV_HEADS, HEAD_DIM), lambda b, h: (b, 0))
    return pl.pallas_call(
        _prompt_attn_kernel,
        grid=(batch, N_KV_HEADS),
        in_specs=[qspec(0), qspec(1), qspec(2), kvspec, kvspec],
        out_specs=pl.BlockSpec((seq, HEAD_DIM), lambda b, h: (b, h)),
        out_shape=jax.ShapeDtypeStruct((batch * seq, ATTN_WIDTH), BF16),
        scratch_shapes=[pltpu.VMEM((N_GROUPS, 3, seq, HEAD_DIM), F32), pltpu.VMEM((2, seq, HEAD_DIM), F32),
                        pltpu.VMEM((N_GROUPS, seq, HEAD_DIM), F32), pltpu.VMEM((N_GROUPS, seq, LANES), F32)],
        compiler_params=_params("parallel", "arbitrary"),
        name="prompt_attn",
    )(q, q, q, k_flat, v_flat)


def _mem_kv_kernel(x_ref, g_ref, w_ref, kg_ref, k_ref, v_ref):
    tm = x_ref.shape[0]
    h = _rms(x_ref[...], g_ref[...]).astype(BF16)
    zk = _dot(h, w_ref[:, :MEM_WIDTH])
    zv = _dot(h, w_ref[:, MEM_WIDTH:])
    for j in range(MEM_HEADS):
        sl = slice(j * MEM_HEAD_DIM, (j + 1) * MEM_HEAD_DIM)
        k_ref[pl.ds(j, tm, stride=MEM_HEADS), :] = _rms(zk[:, sl], kg_ref[...])
        v_ref[pl.ds(j, tm, stride=MEM_HEADS), :] = zv[:, sl]


def _mem_kv(mem, g, w_bf16, k_g):
    m, d = mem.shape
    tm = min(ROW_TILE, m)
    row = lambda i: (i, 0)
    shp = jax.ShapeDtypeStruct((m * MEM_HEADS, MEM_HEAD_DIM), F32)
    ospec = pl.BlockSpec((tm * MEM_HEADS, MEM_HEAD_DIM), row)
    return pl.pallas_call(
        _mem_kv_kernel,
        grid=(m // tm,),
        in_specs=[pl.BlockSpec((tm, d), row), _const_spec((1, d)), _const_spec((d, 2 * MEM_WIDTH)),
                  _const_spec((1, MEM_HEAD_DIM))],
        out_specs=(ospec, ospec),
        out_shape=(shp, shp),
        compiler_params=_params("parallel"),
        name="mem_kv",
    )(mem, g, w_bf16, k_g)


def _mem_attn_kernel(q_ref, k_ref, v_ref, o_ref):
    scale = MEM_HEAD_DIM ** -0.5
    n_mem = k_ref.shape[0] // MEM_HEADS
    for h in range(MEM_HEADS):
        sl = slice(h * MEM_HEAD_DIM, (h + 1) * MEM_HEAD_DIM)
        head_rows = pl.ds(h, n_mem, stride=MEM_HEADS)
        s = _dot_nt(q_ref[:, sl], k_ref[head_rows, :].astype(BF16)) * scale
        p = jnp.exp(s - jnp.max(s, axis=-1, keepdims=True))
        l = jnp.sum(p, axis=-1, keepdims=True)
        o_ref[:, sl] = (_dot(p.astype(BF16), v_ref[head_rows, :].astype(BF16)) * (1.0 / l)).astype(BF16)


def _mem_attention(mq, mem_k_flat, mem_v_flat, batch, seq):
    rows = mem_k_flat.shape[0] // batch
    tq = min(512, seq)
    nq = seq // tq
    kspec = pl.BlockSpec((rows, MEM_HEAD_DIM), lambda b, i: (b, 0))
    return pl.pallas_call(
        _mem_attn_kernel,
        grid=(batch, nq),
        in_specs=[pl.BlockSpec((tq, MEM_WIDTH), lambda b, i: (b * nq + i, 0)), kspec, kspec],
        out_specs=pl.BlockSpec((tq, MEM_WIDTH), lambda b, i: (b * nq + i, 0)),
        out_shape=jax.ShapeDtypeStruct((batch * seq, MEM_WIDTH), BF16),
        compiler_params=_params("parallel", "parallel"),
        name="mem_attn",
    )(mq, mem_k_flat, mem_v_flat)


CONV_HALO = 32
CONV_CHUNK = 32
CONV_ACC_ROWS = 128
CONV_TAP_UNROLL = 8


def _ln_swish(c, g, b):
    mu = jnp.mean(c, axis=-1, keepdims=True)
    xc = c - mu
    y = xc * lax.rsqrt(jnp.mean(xc * xc, axis=-1, keepdims=True) + EPS) * g + b
    return y * jax.nn.sigmoid(y)


def _conv_prompt_kernel(halo_ref, u_ref, w_ref, b_ref, g_ref, beta_ref, o_ref, ext_ref, conv_ref, wb_ref):
    tc, ch = u_ref.shape
    first = pl.program_id(1) == 0
    lane_tiles = [slice(j * LANES, (j + 1) * LANES) for j in range(ch // LANES)]
    for j, sl in enumerate(lane_tiles):
        halo = jnp.where(first, 0.0, halo_ref[:, sl])
        for dup in range(2):
            ext_ref[j, pl.ds(dup, CONV_HALO, stride=2), :] = halo
            ext_ref[j, pl.ds(2 * CONV_HALO + dup, tc, stride=2), :] = u_ref[:, sl]
    for w in range(CONV_WIDTH):
        wb_ref[w] = jnp.broadcast_to(w_ref[w:w + 1, :], (8, ch))
    lead = CONV_HALO - (CONV_WIDTH - 1)
    rows = CONV_ACC_ROWS
    for j, sl in enumerate(lane_tiles):
        for r0 in range(0, tc, rows):
            def tap(w, acc, j=j, sl=sl, r0=r0):
                win = ext_ref[j, pl.ds(2 * (r0 + lead + w), rows, stride=2), :].reshape(rows // 8, 8, LANES)
                return acc + wb_ref[w, :, sl][None] * win

            acc = lax.fori_loop(0, CONV_WIDTH, tap, jnp.zeros((rows // 8, 8, LANES), F32) + b_ref[:, sl],
                                unroll=CONV_TAP_UNROLL)
            conv_ref[r0:r0 + rows, sl] = acc.reshape(rows, LANES)
    for c0 in range(0, tc, CONV_CHUNK):
        o_ref[c0:c0 + CONV_CHUNK, :] = _ln_swish(conv_ref[c0:c0 + CONV_CHUNK, :], g_ref[...], beta_ref[...]).astype(BF16)


def _conv_prompt(u, conv_w, conv_b, ln_g, ln_b, batch, seq):
    ch = u.shape[1]
    tc = min(ROW_TILE, seq)
    nt = seq // tc
    ratio = tc // CONV_HALO
    return pl.pallas_call(
        _conv_prompt_kernel,
        grid=(batch, nt),
        in_specs=[
            pl.BlockSpec((CONV_HALO, ch), lambda b, i: (jnp.maximum((b * nt + i) * ratio - 1, 0), 0)),
            pl.BlockSpec((tc, ch), lambda b, i: (b * nt + i, 0)),
            _const_spec((CONV_WIDTH, ch)), _const_spec((1, ch)), _const_spec((1, ch)), _const_spec((1, ch)),
        ],
        out_specs=pl.BlockSpec((tc, ch), lambda b, i: (b * nt + i, 0)),
        out_shape=jax.ShapeDtypeStruct((batch * seq, ch), BF16),
        scratch_shapes=[pltpu.VMEM((ch // LANES, 2 * (CONV_HALO + tc), LANES), F32), pltpu.VMEM((tc, ch), F32),
                        pltpu.VMEM((CONV_WIDTH, 8, ch), F32)],
        compiler_params=_params("parallel", "parallel"),
        name="conv_prompt",
    )(u, u, conv_w, conv_b, ln_g, ln_b)


def _conv_sample_kernel(state_ref, new_ref, w_ref, b_ref, g_ref, beta_ref, o_ref, state_out_ref, ext_ref):
    nb, ctx, ch = state_ref.shape
    t = new_ref.shape[1]
    ext_ref[:, 0:ctx, :] = state_ref[...]
    ext_ref[:, ctx:ctx + t, :] = new_ref[...]
    acc = jnp.zeros((nb, t, ch), F32) + b_ref[...]
    for w in range(CONV_WIDTH):
        acc = acc + w_ref[w:w + 1, :] * ext_ref[:, w:w + t, :]
    o_ref[...] = _ln_swish(acc, g_ref[...], beta_ref[...])
    state_out_ref[...] = ext_ref[:, t:t + ctx, :]


def _conv_sample(state, u_new, conv_w, conv_b, ln_g, ln_b):
    batch, ctx, ch = state.shape
    t = u_new.shape[1]
    nb = 8
    blk = lambda n: pl.BlockSpec((nb, n, ch), lambda i: (i, 0, 0))
    return pl.pallas_call(
        _conv_sample_kernel,
        grid=(batch // nb,),
        in_specs=[blk(ctx), blk(t), _const_spec((CONV_WIDTH, ch)), _const_spec((1, ch)), _const_spec((1, ch)),
                  _const_spec((1, ch))],
        out_specs=(blk(t), blk(ctx)),
        out_shape=(jax.ShapeDtypeStruct((batch, t, ch), F32), jax.ShapeDtypeStruct((batch, ctx, ch), F32)),
        scratch_shapes=[pltpu.VMEM((nb, ctx + t + 6, ch), F32)],
        compiler_params=_params("parallel"),
        name="conv_sample",
    )(state, u_new, conv_w, conv_b, ln_g, ln_b)


GROUP_LANES = 16


def _spread_groups(vec, combine):
    t = combine(combine(vec, pltpu.roll(vec, LANES - GROUP_LANES, 1)), pltpu.roll(vec, LANES - 2 * GROUP_LANES, 1))
    lane = lax.broadcasted_iota(jnp.int32, vec.shape, 1)
    return jnp.where(lane < GROUP_LANES, t,
                     jnp.where(lane < 2 * GROUP_LANES, pltpu.roll(t, GROUP_LANES, 1), pltpu.roll(t, 2 * GROUP_LANES, 1)))


SLAB_POS = DILATED_GROUPS[-1][1]
SLAB_ROWS = SLAB_POS * N_KV_HEADS
TAIL_POS = max(w for w, d in DILATED_GROUPS if d < SLAB_POS)


def _sample_attn_kernel(kc_ref, vc_ref, kn_ref, vn_ref, wq_ref, bias_d_ref, bias_t_ref, bias_n_ref, mk_ref, mv_ref,
                        wm_ref, bias_m_ref, kw_ref, vw_ref, a_ref, m_ref, kx_ref, vx_ref):
    n_slab = kc_ref.shape[1]
    new_rows = kn_ref.shape[1]
    t = new_rows // N_KV_HEADS
    tail_slabs = TAIL_POS // SLAB_POS
    n_used = N_GROUPS * GROUP_LANES
    lane = lax.broadcasted_iota(jnp.int32, (1, LANES), 1)

    for src, new, dst in ((kc_ref, kn_ref, kw_ref), (vc_ref, vn_ref, vw_ref)):
        dst[0, :, 0:SLAB_ROWS - new_rows, :] = src[0, :, new_rows:SLAB_ROWS, :]
        dst[0, 0:n_slab - 1, SLAB_ROWS - new_rows:SLAB_ROWS, :] = src[0, 1:n_slab, 0:new_rows, :]
        dst[0, n_slab - 1, SLAB_ROWS - new_rows:SLAB_ROWS, :] = new[0]

    kx_ref[...] = jnp.zeros(kx_ref.shape, F32)
    vx_ref[...] = jnp.zeros(vx_ref.shape, F32)
    kx_ref[0:new_rows, :] = kn_ref[0]
    vx_ref[0:new_rows, :] = vn_ref[0]

    def dilated(ref):
        return ref[0, :, 0:new_rows, :].reshape(n_slab * new_rows, HEAD_DIM).astype(BF16)

    def tail(ref):
        return ref[0, n_slab - tail_slabs:n_slab, :, :].reshape(tail_slabs * SLAB_ROWS, HEAD_DIM).astype(BF16)

    scale = HEAD_DIM ** -0.5
    wq = wq_ref[0]
    s_d = _dot(dilated(kc_ref), wq) * scale + bias_d_ref[...]
    s_t = _dot(tail(kc_ref), wq) * scale + bias_t_ref[...]
    s_n = _dot(kx_ref[...].astype(BF16), wq) * scale + bias_n_ref[...]
    col_max = lambda s: jnp.max(s, axis=0, keepdims=True)
    col_sum = lambda p: jnp.sum(p, axis=0, keepdims=True)
    m_col = jnp.maximum(jnp.maximum(col_max(s_d), col_max(s_t)), col_max(s_n))
    m_joint = jnp.where(lane < n_used, _spread_groups(m_col, jnp.maximum), 0.0)
    p_d, p_t, p_n = jnp.exp(s_d - m_joint), jnp.exp(s_t - m_joint), jnp.exp(s_n - m_joint)
    l_col = col_sum(p_d) + col_sum(p_t) + col_sum(p_n)
    inv = 1.0 / jnp.where(lane < n_used, _spread_groups(l_col, jnp.add), 1.0)
    o = (_dot_tn((p_d * inv).astype(BF16), dilated(vc_ref)) + _dot_tn((p_t * inv).astype(BF16), tail(vc_ref))
         + _dot_tn((p_n * inv).astype(BF16), vx_ref[...].astype(BF16)))
    for h in range(N_KV_HEADS):
        r = h * t
        a_ref[0, :, h * HEAD_DIM:(h + 1) * HEAD_DIM] = functools.reduce(
            jnp.add, [o[g * GROUP_LANES + r:g * GROUP_LANES + r + t, :] for g in range(N_GROUPS)])

    sm = _dot(mk_ref[0].astype(BF16), wm_ref[0]) * (MEM_HEAD_DIM ** -0.5) + bias_m_ref[...]
    pm = jnp.exp(sm - jnp.where(lane < MEM_HEADS * t, col_max(sm), 0.0))
    lm = jnp.where(lane < MEM_HEADS * t, col_sum(pm), 1.0)
    om = _dot_tn((pm * (1.0 / lm)).astype(BF16), mv_ref[0].astype(BF16))
    for h in range(MEM_HEADS):
        m_ref[0, :, h * MEM_HEAD_DIM:(h + 1) * MEM_HEAD_DIM] = om[h * t:(h + 1) * t, :]


def _sample_masks(cache_len, t, n_mem):
    col = np.arange(LANES)[None, :]
    g, c_head, c_tok = col // GROUP_LANES, (col % GROUP_LANES) // t, (col % GROUP_LANES) % t
    used = (col < N_GROUPS * GROUP_LANES) & (col % GROUP_LANES < N_KV_HEADS * t)
    pad = [1] * (LANES // GROUP_LANES - N_GROUPS)
    dil = np.array([d for _, d in DILATED_GROUPS] + pad)[g]
    win = np.array([w for w, _ in DILATED_GROUPS] + pad)[g]
    sparse = dil >= SLAB_POS

    def keep(pos, head, group_sel):
        dist = cache_len + c_tok - pos
        return used & group_sel & (head == c_head) & (dist >= 0) & (dist % dil == 0) & (dist <= win)

    n_slab = cache_len // SLAB_POS
    y = np.arange(n_slab * t * N_KV_HEADS)[:, None]
    keep_d = keep((y // (t * N_KV_HEADS)) * SLAB_POS + (y % (t * N_KV_HEADS)) // N_KV_HEADS, y % N_KV_HEADS, sparse)
    x = np.arange(TAIL_POS * N_KV_HEADS)[:, None]
    keep_t = keep(cache_len - TAIL_POS + x // N_KV_HEADS, x % N_KV_HEADS, ~sparse)
    z = np.arange(LANES)[:, None]
    keep_n = keep(cache_len + z // N_KV_HEADS, z % N_KV_HEADS, True) & (z < t * N_KV_HEADS)
    w = np.arange(n_mem * MEM_HEADS)[:, None]
    keep_m = (col < MEM_HEADS * t) & (w % MEM_HEADS == col // t)
    return tuple(jnp.asarray(np.where(k, 0.0, NEG), F32) for k in (keep_d, keep_t, keep_n, keep_m))


def _query_columns(q, batch, t, n_groups):
    qt = q.reshape(batch, t, n_groups, N_KV_HEADS, HEAD_DIM).transpose(0, 4, 2, 3, 1)
    qt = qt.reshape(batch, HEAD_DIM, n_groups, N_KV_HEADS * t)
    qt = jnp.pad(qt, ((0, 0), (0, 0), (0, 0), (0, GROUP_LANES - N_KV_HEADS * t)))
    qt = qt.reshape(batch, HEAD_DIM, n_groups * GROUP_LANES)
    return jnp.pad(qt, ((0, 0), (0, 0), (0, LANES - n_groups * GROUP_LANES)))


def _sample_attention(q, k_new, v_new, mq, cache_k, cache_v, mem_k, mem_v):
    batch, cache_len = cache_k.shape[0], cache_k.shape[1]
    new_rows = k_new.shape[0] // batch
    t = new_rows // N_KV_HEADS
    n_mem = mem_k.shape[1]
    assert N_KV_HEADS * t <= GROUP_LANES and t <= SLAB_POS and new_rows % 8 == 0
    assert cache_len % SLAB_POS == 0 and cache_len >= max(w for w, _ in DILATED_GROUPS)
    assert all(d == SLAB_POS or w <= TAIL_POS for w, d in DILATED_GROUPS)
    n_slab = cache_len // SLAB_POS
    wq = _query_columns(q, batch, t, N_GROUPS)
    wm = _query_columns(mq, batch, t, 1)
    bias_d, bias_t, bias_n, bias_m = _sample_masks(cache_len, t, n_mem)
    per_b = lambda *shape: pl.BlockSpec((1,) + shape, lambda b: (b,) + (0,) * len(shape))
    slabs = lambda c: c.reshape(batch, n_slab, SLAB_ROWS, HEAD_DIM)
    mem_rows = n_mem * MEM_HEADS
    win = jax.ShapeDtypeStruct((batch, n_slab, SLAB_ROWS, HEAD_DIM), F32)
    k_win, v_win, a, m = pl.pallas_call(
        _sample_attn_kernel,
        grid=(batch,),
        in_specs=[
            per_b(n_slab, SLAB_ROWS, HEAD_DIM), per_b(n_slab, SLAB_ROWS, HEAD_DIM),
            per_b(new_rows, HEAD_DIM), per_b(new_rows, HEAD_DIM), per_b(HEAD_DIM, LANES),
            _const_spec(bias_d.shape), _const_spec(bias_t.shape), _const_spec(bias_n.shape),
            per_b(mem_rows, MEM_HEAD_DIM), per_b(mem_rows, MEM_HEAD_DIM), per_b(MEM_HEAD_DIM, LANES),
            _const_spec(bias_m.shape),
        ],
        out_specs=(per_b(n_slab, SLAB_ROWS, HEAD_DIM), per_b(n_slab, SLAB_ROWS, HEAD_DIM), per_b(t, ATTN_WIDTH),
                   per_b(t, MEM_WIDTH)),
        out_shape=(win, win, jax.ShapeDtypeStruct((batch, t, ATTN_WIDTH), F32),
                   jax.ShapeDtypeStruct((batch, t, MEM_WIDTH), F32)),
        scratch_shapes=[pltpu.VMEM((LANES, HEAD_DIM), F32), pltpu.VMEM((LANES, HEAD_DIM), F32)],
        compiler_params=_params("parallel"),
        name="sample_attn",
    )(slabs(cache_k), slabs(cache_v), k_new.reshape(batch, new_rows, HEAD_DIM), v_new.reshape(batch, new_rows, HEAD_DIM),
      wq, bias_d, bias_t, bias_n, mem_k.reshape(batch, mem_rows, MEM_HEAD_DIM),
      mem_v.reshape(batch, mem_rows, MEM_HEAD_DIM), wm, bias_m)
    return k_win.reshape(cache_k.shape), v_win.reshape(cache_v.shape), a, m


ROUTE_GROUP_LANE0 = 0
ROUTE_EXPERT_LANE0 = N_EXPERT_GROUPS


def _route(logits):
    lane = lax.broadcasted_iota(jnp.int32, logits.shape, 1).astype(F32)
    big = float(LANES)

    def masked_softmax(keep):
        z = jnp.where(keep, logits, NEG)
        e = jnp.where(keep, jnp.exp(z - jnp.max(z, axis=-1, keepdims=True)), 0.0)
        return e / jnp.sum(e, axis=-1, keepdims=True)

    def first_argmax(vals, keep):
        top = jnp.max(jnp.where(keep, vals, -1.0), axis=-1, keepdims=True)
        idx = jnp.min(jnp.where(jnp.logical_and(keep, vals == top), lane, big), axis=-1, keepdims=True)
        return top, idx

    is_group = lane < N_EXPERT_GROUPS
    pg = masked_softmax(is_group)
    pg_top, g_idx = first_argmax(pg, is_group)
    lo = ROUTE_EXPERT_LANE0 + g_idx * EXPERTS_PER_GROUP
    in_group = jnp.logical_and(lane >= lo, lane < lo + EXPERTS_PER_GROUP)
    pe = masked_softmax(in_group)
    p1, i1 = first_argmax(pe, in_group)
    p2, i2 = first_argmax(pe, jnp.logical_and(in_group, lane != i1))
    denom = p1 + p2
    w1 = pg_top * p1 / denom
    w2 = pg_top * p2 / denom
    return lane, i1 - ROUTE_EXPERT_LANE0, i2 - ROUTE_EXPERT_LANE0, w1, w2


def _merge_kernel(x_ref, gmix_ref, ao_ref, cc_ref, mo_ref, wgate_ref, wa_ref, wc_ref, wm_ref, wo_ref, gffn_ref, wr_ref,
                  br_ref, cnt_in_ref, x1_ref, h2_ref, route_ref, cnt_out_ref, cnt_ref):
    @pl.when(pl.program_id(0) == 0)
    def _():
        cnt_ref[...] = cnt_in_ref[...]

    d = x_ref.shape[1]
    x = x_ref[...]
    h = _rms(x, gmix_ref[...]).astype(BF16)
    a = _dot(ao_ref[...].astype(BF16), wa_ref[...])
    c = _dot(cc_ref[...].astype(BF16), wc_ref[...])
    m = _dot(mo_ref[...].astype(BF16), wm_ref[...])
    z = jax.nn.sigmoid(_dot(h, wgate_ref[:, 0:d])) * a
    z = z + jax.nn.sigmoid(_dot(h, wgate_ref[:, d:2 * d])) * c
    z = z + jax.nn.sigmoid(_dot(h, wgate_ref[:, 2 * d:3 * d])) * m
    x1 = x + _dot(z.astype(BF16), wo_ref[...])
    x1_ref[...] = x1
    h2 = _rms(x1, gffn_ref[...]).astype(BF16)
    h2_ref[...] = _pack_halves(h2)
    logits = _dot(h2, wr_ref[...]) + br_ref[...]
    lane, e1, e2, w1, w2 = _route(logits)

    tm = x_ref.shape[0]
    hit1 = jnp.where(lane == e1, 1.0, 0.0)
    hit2 = jnp.where(lane == e2, 1.0, 0.0)
    hits = hit1 + hit2
    earlier = (lax.broadcasted_iota(jnp.int32, (tm, tm), 1) < lax.broadcasted_iota(jnp.int32, (tm, tm), 0))
    before = _dot(jnp.where(earlier, 1.0, 0.0).astype(BF16), hits.astype(BF16)) + cnt_ref[...]
    rank1 = jnp.sum(hit1 * before, axis=-1, keepdims=True)
    rank2 = jnp.sum(hit2 * before, axis=-1, keepdims=True)
    cnt_ref[...] = cnt_ref[...] + jnp.sum(hits, axis=0, keepdims=True)
    cnt_out_ref[...] = cnt_ref[...]
    route = jnp.zeros(logits.shape, F32)
    for i, val in enumerate((e1, e2, w1, w2, rank1, rank2)):
        route = jnp.where(lane == i, val, route)
    route_ref[...] = route


ROUTE_E, ROUTE_W, ROUTE_RANK = 0, 2, 4


def _merge(x, g_mix, ao, cc, mo, w_gate, w_a, w_c, w_m, w_o, g_ffn, w_r, b_r, counts):
    m, d = x.shape
    tm = min(MERGE_TILE, m)
    row = lambda i: (i, 0)
    rows = lambda arr: pl.BlockSpec((tm, arr.shape[1]), row)
    ins = [x, g_mix, ao, cc, mo, w_gate, w_a, w_c, w_m, w_o, g_ffn, w_r, b_r, counts]
    specs = ([rows(x), _const_spec(g_mix.shape), rows(ao), rows(cc), rows(mo)]
             + [_const_spec(a.shape) for a in (w_gate, w_a, w_c, w_m, w_o, g_ffn, w_r, b_r, counts)])
    out_shape = (jax.ShapeDtypeStruct((m, d), F32), jax.ShapeDtypeStruct((m, d // 2), jnp.uint32),
                 jax.ShapeDtypeStruct((m, LANES), F32), jax.ShapeDtypeStruct((1, LANES), F32))
    return pl.pallas_call(
        _merge_kernel,
        grid=(m // tm,),
        in_specs=specs,
        out_specs=(rows(out_shape[0]), rows(out_shape[1]), rows(out_shape[2]), _const_spec((1, LANES))),
        out_shape=out_shape,
        scratch_shapes=[pltpu.VMEM((1, LANES), F32)],
        compiler_params=_params("arbitrary"),
        name="merge",
    )(*ins)


def _routing_tables(counts, routes):
    cnt = counts[0, :N_EXPERTS].astype(jnp.int32)
    tiles = (cnt + MOE_TILE - 1) // MOE_TILE
    tile_start = jnp.cumsum(tiles) - tiles
    row_start = tile_start * MOE_TILE
    ids = routes[:, ROUTE_E:ROUTE_E + TOP_K].astype(jnp.int32)
    start_of = jnp.sum(jnp.where(ids[:, :, None] == jnp.arange(N_EXPERTS)[None, None, :], row_start[None, None, :], 0),
                       axis=-1)
    pos = start_of + routes[:, ROUTE_RANK:ROUTE_RANK + TOP_K].astype(jnp.int32)
    return pos.reshape(-1), tile_start.astype(jnp.int32), tiles.astype(jnp.int32)


def _dispatch_kernel(pos_ref, h_ref, xs_in, xs_out, sem, *, row0):
    del xs_in
    tm = h_ref.shape[0]
    base = (row0 + pl.program_id(0) * tm) * TOP_K

    def issue(j, carry):
        for k in range(TOP_K):
            slot = pos_ref[base + j * TOP_K + k]
            pltpu.make_async_copy(h_ref.at[pl.ds(j, 1)], xs_out.at[pl.ds(slot, 1)], sem).start()
        return carry

    lax.fori_loop(0, tm, issue, 0, unroll=8)
    for k in range(TOP_K):
        pltpu.make_async_copy(h_ref, xs_out.at[pl.ds(0, tm)], sem).wait()


def _dispatch(h2, pos, xs, row0):
    m, d = h2.shape
    tm = min(ROW_TILE, m)
    return pl.pallas_call(
        functools.partial(_dispatch_kernel, row0=row0),
        grid_spec=pltpu.PrefetchScalarGridSpec(
            num_scalar_prefetch=1,
            grid=(m // tm,),
            in_specs=[pl.BlockSpec((tm, d), lambda i, *_: (i, 0)), pl.BlockSpec(memory_space=pl.ANY)],
            out_specs=pl.BlockSpec(memory_space=pl.ANY),
            scratch_shapes=[pltpu.SemaphoreType.DMA(())],
        ),
        out_shape=jax.ShapeDtypeStruct(xs.shape, xs.dtype),
        input_output_aliases={2: 0},
        compiler_params=_params("arbitrary"),
        name="moe_dispatch",
    )(pos, h2, xs)


def _gmm_kernel(start_ref, count_ref, xs_hbm, wg_ref, wu_ref, wd_ref, ys_hbm, xbuf, obuf, wg_s, wu_s, wd_s,
                in_sems, out_sems):
    e = pl.program_id(0)
    first, n = start_ref[e], count_ref[e]

    def rows(i):
        return pl.ds(pl.multiple_of((first + i) * MOE_TILE, MOE_TILE), MOE_TILE)

    def in_copy(i, slot):
        return pltpu.make_async_copy(xs_hbm.at[rows(i)], xbuf.at[slot], in_sems.at[slot])

    def out_copy(i, slot):
        return pltpu.make_async_copy(obuf.at[slot], ys_hbm.at[rows(i)], out_sems.at[slot])

    @pl.when(n > 0)
    def _():
        in_copy(0, 0).start()
        wg_s[...] = wg_ref[0].astype(BF16)
        wu_s[...] = wu_ref[0].astype(BF16)
        wd_s[...] = wd_ref[0].astype(BF16)
        half = xbuf.shape[2]

        def tile(i, carry):
            slot = i % 2
            in_copy(i, slot).wait()

            @pl.when(i + 1 < n)
            def _():
                in_copy(i + 1, 1 - slot).start()

            @pl.when(i >= 2)
            def _():
                out_copy(i - 2, slot).wait()

            x_hi, x_lo = _unpack_halves(xbuf[slot])
            gate = _dot(x_hi, wg_s[0:half, :]) + _dot(x_lo, wg_s[half:, :])
            up = _dot(x_hi, wu_s[0:half, :]) + _dot(x_lo, wu_s[half:, :])
            hid = gate * jax.nn.sigmoid(gate) * up
            obuf[slot] = _dot(hid.astype(BF16), wd_s[...])
            out_copy(i, slot).start()
            return carry

        lax.fori_loop(0, n, tile, 0)

        @pl.when(n >= 2)
        def _():
            out_copy(n - 2, n % 2).wait()

        out_copy(n - 1, (n - 1) % 2).wait()

    @pl.when(e == pl.num_programs(0) - 1)
    def _():
        obuf[0] = jnp.zeros(obuf.shape[1:], F32)

        def fill(t, carry):
            dst = ys_hbm.at[pl.ds(pl.multiple_of(t * MOE_TILE, MOE_TILE), MOE_TILE)]
            copy = pltpu.make_async_copy(obuf.at[0], dst, out_sems.at[0])
            copy.start()
            copy.wait()
            return carry

        lax.fori_loop(first + n, ys_hbm.shape[0] // MOE_TILE, fill, 0)


def _grouped_mlp(xs, tile_start, tile_count, w_gate, w_up, w_down):
    d, ff = w_gate.shape[1], w_gate.shape[2]
    wspec = lambda a, b: pl.BlockSpec((1, a, b), lambda e, *_: (e, 0, 0))
    any_spec = pl.BlockSpec(memory_space=pl.ANY)
    return pl.pallas_call(
        _gmm_kernel,
        grid_spec=pltpu.PrefetchScalarGridSpec(
            num_scalar_prefetch=2,
            grid=(N_EXPERTS,),
            in_specs=[any_spec, wspec(d, ff), wspec(d, ff), wspec(ff, d)],
            out_specs=any_spec,
            scratch_shapes=[pltpu.VMEM((2, MOE_TILE, xs.shape[1]), xs.dtype), pltpu.VMEM((2, MOE_TILE, d), F32),
                            pltpu.VMEM((d, ff), BF16), pltpu.VMEM((d, ff), BF16), pltpu.VMEM((ff, d), BF16),
                            pltpu.SemaphoreType.DMA((2,)), pltpu.SemaphoreType.DMA((2,))],
        ),
        out_shape=jax.ShapeDtypeStruct((xs.shape[0], d), F32),
        compiler_params=_params("arbitrary"),
        name="moe_gmm",
    )(tile_start, tile_count, xs, w_gate, w_up, w_down)


def _combine_kernel(pos_ref, x1_ref, route_ref, ys_hbm, y_ref, buf_ref, sems, *, row0):
    tm = x1_ref.shape[0]
    i = pl.program_id(0)

    def gather(tile, half):
        base = (row0 + tile * tm) * TOP_K

        def issue(j, carry):
            for k in range(TOP_K):
                slot = pos_ref[base + j * TOP_K + k]
                pltpu.make_async_copy(ys_hbm.at[pl.ds(slot, 1)], buf_ref.at[half, k, pl.ds(j, 1)],
                                      sems.at[half]).start()
            return carry

        lax.fori_loop(0, tm, issue, 0, unroll=4)

    @pl.when(i == 0)
    def _():
        gather(0, 0)

    @pl.when(i + 1 < pl.num_programs(0))
    def _():
        gather(i + 1, (i + 1) % 2)

    half = i % 2
    for k in range(TOP_K):
        pltpu.make_async_copy(ys_hbm.at[pl.ds(0, tm)], buf_ref.at[half, k], sems.at[half]).wait()
    route = route_ref[...]
    y_ref[...] = (x1_ref[...] + route[:, ROUTE_W:ROUTE_W + 1] * buf_ref[half, 0]
                  + route[:, ROUTE_W + 1:ROUTE_W + 2] * buf_ref[half, 1])


def _combine(x1, route, pos, ys, row0):
    m, d = x1.shape
    tm = min(ROW_TILE, m)
    row = lambda i, *_: (i, 0)
    return pl.pallas_call(
        functools.partial(_combine_kernel, row0=row0),
        grid_spec=pltpu.PrefetchScalarGridSpec(
            num_scalar_prefetch=1,
            grid=(m // tm,),
            in_specs=[pl.BlockSpec((tm, d), row), pl.BlockSpec((tm, LANES), row), pl.BlockSpec(memory_space=pl.ANY)],
            out_specs=pl.BlockSpec((tm, d), row),
            scratch_shapes=[pltpu.VMEM((2, TOP_K, tm, d), F32), pltpu.SemaphoreType.DMA((2,))],
        ),
        out_shape=jax.ShapeDtypeStruct((m, d), F32),
        compiler_params=_params("arbitrary"),
        name="moe_combine",
    )(pos, x1, route, ys)


def _layer(layer, x_prompt, x_sample, mem_prompt, cache_k, cache_v, state_conv, cache_mem_k, cache_mem_v, p):
    batch, seq, d = x_prompt.shape
    dec_batch, dec_seq, _ = x_sample.shape
    conv_ch = p["conv_w"].shape[-1]
    n_in = N_Q_HEADS * HEAD_DIM + 2 * ATTN_WIDTH + 2 * conv_ch + MEM_WIDTH
    past_len = cache_k.shape[2]

    row2 = lambda name: p[name][layer][None, :]
    w_in = p["w_in"][layer]
    w_main = w_in[:, :n_in].astype(BF16)
    w_gate = w_in[:, n_in:].astype(BF16)
    w_a, w_c, w_m, w_o = (p[n][layer].astype(BF16) for n in ("w_attn_proj", "w_conv_proj", "w_mem_proj", "w_out"))
    w_router = jnp.concatenate(
        [p["w_router_group"][layer], p["w_router_expert"][layer].transpose(1, 0, 2).reshape(d, N_EXPERTS)], axis=1)
    w_router = jnp.pad(w_router, ((0, 0), (0, LANES - w_router.shape[1]))).astype(BF16)
    b_router = jnp.concatenate([p["b_router_group"][layer], p["b_router_expert"][layer].reshape(-1)])
    b_router = jnp.pad(b_router, (0, LANES - b_router.shape[0]))[None, :]
    conv_args = (p["conv_w"][layer], row2("conv_b"), row2("conv_ln_g"), row2("conv_ln_b"))
    merge_w = (w_gate, w_a, w_c, w_m, w_o, row2("norm_ffn_g"), w_router, b_router)

    xp = x_prompt.reshape(batch * seq, d)
    tabs_p = _rope_tables(jnp.arange(seq, dtype=jnp.int32))
    q_p, k_p, v_p, u_p, mq_p = _in_proj(xp, row2("norm_mix_g"), w_main, row2("q_norm_g"), row2("k_norm_g"),
                                        row2("mq_norm_g"), tabs_p, conv_ch)
    ao_p = _prompt_attention(q_p, k_p, v_p, batch, seq)
    mem_k_p, mem_v_p = _mem_kv(mem_prompt.reshape(-1, d), row2("mem_norm_g"), p["w_mem_kv"][layer].astype(BF16),
                               row2("mk_norm_g"))
    mo_p = _mem_attention(mq_p, mem_k_p, mem_v_p, batch, seq)
    cc_p = _conv_prompt(u_p, *conv_args, batch, seq)
    x1_p, h2_p, route_p, counts = _merge(xp, row2("norm_mix_g"), ao_p, cc_p, mo_p, *merge_w,
                                         jnp.zeros((1, LANES), F32))

    xs = x_sample.reshape(dec_batch * dec_seq, d)
    tabs_s = _rope_tables(jnp.tile(past_len + jnp.arange(dec_seq, dtype=jnp.int32), dec_batch))
    q_s, k_s, v_s, u_s, mq_s = _in_proj(xs, row2("norm_mix_g"), w_main, row2("q_norm_g"), row2("k_norm_g"),
                                        row2("mq_norm_g"), tabs_s, conv_ch)
    k_win_s, v_win_s, a_s, mo_s = _sample_attention(q_s, k_s, v_s, mq_s, cache_k[layer], cache_v[layer],
                                                    cache_mem_k[layer], cache_mem_v[layer])
    cc_s, conv_state_s = _conv_sample(state_conv[layer], u_s.reshape(dec_batch, dec_seq, conv_ch), *conv_args)
    x1_s, h2_s, route_s, counts = _merge(xs, row2("norm_mix_g"), a_s.reshape(-1, ATTN_WIDTH),
                                         cc_s.reshape(-1, conv_ch), mo_s.reshape(-1, MEM_WIDTH), *merge_w, counts)

    n_p, n_s = xp.shape[0], xs.shape[0]
    n_tok = n_p + n_s
    n_tiles = (TOP_K * n_tok + N_EXPERTS * (MOE_TILE - 1)) // MOE_TILE + 1
    pos, tile_start, tile_count = _routing_tables(counts, jnp.concatenate([route_p, route_s], axis=0))
    slots = _dispatch(h2_p, pos, jnp.zeros((n_tiles * MOE_TILE, h2_p.shape[1]), h2_p.dtype), 0)
    slots = _dispatch(h2_s, pos, slots, n_p)
    ys = _grouped_mlp(slots, tile_start, tile_count, p["w_expert_gate"][layer], p["w_expert_up"][layer],
                      p["w_expert_down"][layer])
    y_p = _combine(x1_p, route_p, pos, ys, 0)
    y_s = _combine(x1_s, route_s, pos, ys, n_p)

    state_p = (k_p.reshape(batch, seq, N_KV_HEADS, HEAD_DIM), v_p.reshape(batch, seq, N_KV_HEADS, HEAD_DIM),
               u_p.reshape(batch, seq, conv_ch)[:, seq - (CONV_WIDTH - 1):],
               mem_k_p.reshape(batch, -1, MEM_HEADS, MEM_HEAD_DIM), mem_v_p.reshape(batch, -1, MEM_HEADS, MEM_HEAD_DIM))
    state_s = (k_win_s, v_win_s, conv_state_s)
    return y_p.reshape(batch, seq, d), y_s.reshape(dec_batch, dec_seq, d), state_p, state_s


def kernel(x_prompt, x_sample, mem_prompt, cache_k, cache_v, state_conv, cache_mem_k, cache_mem_v, norm_mix_g, w_in, q_norm_g, k_norm_g, conv_w, conv_b, conv_ln_g, conv_ln_b, mem_norm_g, w_mem_kv, mq_norm_g, mk_norm_g, w_attn_proj, w_conv_proj, w_mem_proj, w_out, norm_ffn_g, w_router_group, b_router_group, w_router_expert, b_router_expert, w_expert_gate, w_expert_up, w_expert_down):
    p = dict(norm_mix_g=norm_mix_g, w_in=w_in, q_norm_g=q_norm_g, k_norm_g=k_norm_g, conv_w=conv_w, conv_b=conv_b,
             conv_ln_g=conv_ln_g, conv_ln_b=conv_ln_b, mem_norm_g=mem_norm_g, w_mem_kv=w_mem_kv, mq_norm_g=mq_norm_g,
             mk_norm_g=mk_norm_g, w_attn_proj=w_attn_proj, w_conv_proj=w_conv_proj, w_mem_proj=w_mem_proj,
             w_out=w_out, norm_ffn_g=norm_ffn_g, w_router_group=w_router_group, b_router_group=b_router_group,
             w_router_expert=w_router_expert, b_router_expert=b_router_expert, w_expert_gate=w_expert_gate,
             w_expert_up=w_expert_up, w_expert_down=w_expert_down)
    depth = w_in.shape[0]
    seq = x_prompt.shape[1]
    assert seq <= max(w for w, _ in DILATED_GROUPS)
    y_p, y_s = x_prompt, x_sample
    states_p, states_s = [], []
    for layer in range(depth):
        y_p, y_s, st_p, st_s = _layer(layer, y_p, y_s, mem_prompt, cache_k, cache_v, state_conv, cache_mem_k,
                                      cache_mem_v, p)
        states_p.append(st_p)
        states_s.append(st_s)
    stack = lambda states, i: jnp.stack([s[i] for s in states], axis=0)
    return (y_p, y_s, stack(states_p, 0), stack(states_p, 1), stack(states_p, 2), stack(states_p, 3),
            stack(states_p, 4), stack(states_s, 0), stack(states_s, 1), stack(states_s, 2))
```

```python
import functools

import jax
import jax.numpy as jnp
import numpy as np
from jax import lax
from jax.experimental import pallas as pl
from jax.experimental.pallas import tpu as pltpu

HEAD_DIM = 128
N_KV_HEADS = 4
DILATED_GROUPS = ((128, 1), (512, 4), (2048, 16))
N_GROUPS = len(DILATED_GROUPS)
N_Q_HEADS = N_GROUPS * N_KV_HEADS
ATTN_WIDTH = N_KV_HEADS * HEAD_DIM
BAND = 128
ATTN_UNROLL = 8
ROPE_THETA = 500000.0
ROT_DIM = HEAD_DIM // 4
CONV_WIDTH = 31
MEM_HEADS = 4
MEM_HEAD_DIM = 128
MEM_WIDTH = MEM_HEADS * MEM_HEAD_DIM
N_EXPERT_GROUPS = 4
EXPERTS_PER_GROUP = 8
N_EXPERTS = N_EXPERT_GROUPS * EXPERTS_PER_GROUP
TOP_K = 2
EPS = 1e-6
NEG = -1e30

LANES = 128
ROW_TILE = 256
MERGE_TILE = 512
IN_PROJ_TILE = 256
MOE_TILE = 256
GMM_BUFS = 4
VMEM_LIMIT = 56 * 1024 * 1024

BF16 = jnp.bfloat16
F32 = jnp.float32


def _params(*sem):
    return pltpu.CompilerParams(dimension_semantics=sem, vmem_limit_bytes=VMEM_LIMIT)


def _dot(a, b):
    return jnp.dot(a, b, preferred_element_type=F32)


def _dot_nt(a, b):
    return lax.dot_general(a, b, (((1,), (1,)), ((), ())), preferred_element_type=F32)


def _dot_tn(a, b):
    return lax.dot_general(a, b, (((0,), (0,)), ((), ())), preferred_element_type=F32)


def _rms(x, g):
    return x * lax.rsqrt(jnp.mean(x * x, axis=-1, keepdims=True) + EPS) * g


def _pack_halves(x):
    c = x.shape[1] // 2
    hi = lax.bitcast_convert_type(x[:, :c].astype(F32), jnp.uint32)
    lo = lax.bitcast_convert_type(x[:, c:].astype(F32), jnp.uint32)
    return hi | (lo >> 16)


def _unpack_halves(p):
    hi = lax.bitcast_convert_type(p & jnp.uint32(0xFFFF0000), F32).astype(BF16)
    lo = lax.bitcast_convert_type(p << 16, F32).astype(BF16)
    return hi, lo


def _const_spec(shape):
    nd = len(shape)
    return pl.BlockSpec(shape, lambda *_: (0,) * nd)


def _in_proj_kernel(x_ref, g_ref, w_ref, qg_ref, kg_ref, mqg_ref, rc_ref, ra_ref, rb_ref,
                    q_ref, k_ref, v_ref, u_ref, mq_ref):
    h = _rms(x_ref[...], g_ref[...]).astype(BF16)
    rc, ra, rb = rc_ref[...], ra_ref[...], rb_ref[...]

    def rope(y):
        return y * rc + pltpu.roll(y, LANES - ROT_DIM // 2, 1) * ra + pltpu.roll(y, ROT_DIM // 2, 1) * rb

    col = 0
    zq = _dot(h, w_ref[:, col:col + N_Q_HEADS * HEAD_DIM])
    for j in range(N_Q_HEADS):
        sl = slice(j * HEAD_DIM, (j + 1) * HEAD_DIM)
        q_ref[:, sl] = rope(_rms(zq[:, sl], qg_ref[...])).astype(BF16)
    col += N_Q_HEADS * HEAD_DIM
    tm = x_ref.shape[0]
    zk = _dot(h, w_ref[:, col:col + ATTN_WIDTH])
    for j in range(N_KV_HEADS):
        sl = slice(j * HEAD_DIM, (j + 1) * HEAD_DIM)
        k_ref[pl.ds(j, tm, stride=N_KV_HEADS), :] = rope(_rms(zk[:, sl], kg_ref[...]))
    col += ATTN_WIDTH
    zv = _dot(h, w_ref[:, col:col + ATTN_WIDTH])
    for j in range(N_KV_HEADS):
        v_ref[pl.ds(j, tm, stride=N_KV_HEADS), :] = zv[:, j * HEAD_DIM:(j + 1) * HEAD_DIM]
    col += ATTN_WIDTH
    conv_ch = u_ref.shape[-1]
    za = _dot(h, w_ref[:, col:col + conv_ch])
    zb = _dot(h, w_ref[:, col + conv_ch:col + 2 * conv_ch])
    u_ref[...] = za * jax.nn.sigmoid(zb)
    col += 2 * conv_ch
    zm = _dot(h, w_ref[:, col:col + MEM_WIDTH])
    for j in range(MEM_HEADS):
        sl = slice(j * MEM_HEAD_DIM, (j + 1) * MEM_HEAD_DIM)
        mq_ref[:, sl] = _rms(zm[:, sl], mqg_ref[...]).astype(BF16)


def _in_proj(x, g_mix, w_bf16, q_g, k_g, mq_g, rope_tabs, conv_ch):
    m, d = x.shape
    tm = min(IN_PROJ_TILE, m)
    n_tab_blocks = rope_tabs[0].shape[0] // tm
    row = lambda i: (i, 0)
    tab = lambda i: (i % n_tab_blocks, 0)
    ncols = w_bf16.shape[1]
    out_shape = (
        jax.ShapeDtypeStruct((m, N_Q_HEADS * HEAD_DIM), BF16),
        jax.ShapeDtypeStruct((m * N_KV_HEADS, HEAD_DIM), F32),
        jax.ShapeDtypeStruct((m * N_KV_HEADS, HEAD_DIM), F32),
        jax.ShapeDtypeStruct((m, conv_ch), F32),
        jax.ShapeDtypeStruct((m, MEM_WIDTH), BF16),
    )
    return pl.pallas_call(
        _in_proj_kernel,
        grid=(m // tm,),
        in_specs=[
            pl.BlockSpec((tm, d), row),
            _const_spec((1, d)),
            _const_spec((d, ncols)),
            _const_spec((1, HEAD_DIM)), _const_spec((1, HEAD_DIM)), _const_spec((1, MEM_HEAD_DIM)),
            pl.BlockSpec((tm, LANES), tab), pl.BlockSpec((tm, LANES), tab), pl.BlockSpec((tm, LANES), tab),
        ],
        out_specs=tuple(pl.BlockSpec((tm * s.shape[0] // m, s.shape[1]), row) for s in out_shape),
        out_shape=out_shape,
        compiler_params=_params("parallel"),
        name="in_proj",
    )(x, g_mix, w_bf16, q_g, k_g, mq_g, *rope_tabs)


def _rope_tables(pos):
    half = ROT_DIM // 2
    inv_freq = jnp.power(jnp.float32(ROPE_THETA), -jnp.arange(half, dtype=F32) * (2.0 / ROT_DIM))
    ang = pos.astype(F32)[:, None] * inv_freq[None, :]
    cos, sin = jnp.cos(ang), jnp.sin(ang)
    n = pos.shape[0]
    ones = jnp.ones((n, LANES - ROT_DIM), F32)
    zeros = jnp.zeros((n, LANES - half), F32)
    rc = jnp.concatenate([cos, cos, ones], axis=1)
    ra = jnp.concatenate([-sin, zeros], axis=1)
    rb = jnp.concatenate([jnp.zeros((n, half), F32), sin, jnp.zeros((n, LANES - ROT_DIM), F32)], axis=1)
    return rc, ra, rb


def _to_residue_layout(dst, src, classes_src, ratio):
    len_src = src.shape[0] // classes_src
    len_dst = len_src // ratio
    for c_src in range(classes_src):
        for a in range(ratio):
            c = c_src + classes_src * a
            dst[c * len_dst:(c + 1) * len_dst, :] = src[pl.ds(c_src * len_src + a, len_dst, stride=ratio), :]


def _prompt_attn_kernel(q0_ref, q1_ref, q2_ref, k_ref, v_ref, o_ref, perm_ref, tmp_ref, acc_ref, lse_ref):
    head = pl.program_id(1)
    seq = q0_ref.shape[0]
    q_refs = (q0_ref, q1_ref, q2_ref)
    dils = [d for _, d in DILATED_GROUPS]
    for t, ref in ((1, k_ref), (2, v_ref)):
        perm_ref[0, t] = ref[pl.ds(head, seq, stride=N_KV_HEADS), :]
        for g in range(1, N_GROUPS):
            _to_residue_layout(perm_ref.at[g, t], perm_ref.at[g - 1, t], dils[g - 1], dils[g] // dils[g - 1])
    for g in range(1, N_GROUPS):
        tmp_ref[0] = q_refs[g][...].astype(F32)
        for step in range(1, g + 1):
            dst = perm_ref.at[g, 0] if step == g else tmp_ref.at[step % 2]
            _to_residue_layout(dst, tmp_ref.at[(step - 1) % 2], dils[step - 1], dils[step] // dils[step - 1])
    iq = lax.broadcasted_iota(jnp.int32, (BAND, BAND), 0)
    ik = lax.broadcasted_iota(jnp.int32, (BAND, BAND), 1)
    keep_c = iq >= ik
    scale = HEAD_DIM ** -0.5

    for g, dil in enumerate(dils):
        nb = seq // (dil * BAND)

        def body(jj, carry, g=g, dil=dil, nb=nb):
            blocks = []
            for u in range(ATTN_UNROLL):
                j = jj * ATTN_UNROLL + u
                rows = pl.ds(pl.multiple_of(j * BAND, BAND), BAND)
                prev = pl.ds(pl.multiple_of(jnp.maximum(j - 1, 0) * BAND, BAND), BAND)
                q = q0_ref[rows, :] if g == 0 else perm_ref[g, 0, rows, :].astype(BF16)
                s_c = jnp.where(keep_c, _dot_nt(q, perm_ref[g, 1, rows, :].astype(BF16)) * scale, NEG)
                s_p = None
                if nb > 1:
                    keep_p = jnp.logical_and(ik >= iq, j % nb > 0)
                    s_p = jnp.where(keep_p, _dot_nt(q, perm_ref[g, 1, prev, :].astype(BF16)) * scale, NEG)
                blocks.append((j, rows, prev, s_c, s_p))
            probs = []
            for j, rows, prev, s_c, s_p in blocks:
                m = jnp.max(s_c if nb == 1 else jnp.maximum(s_c, s_p), axis=-1, keepdims=True)
                p_c = jnp.exp(s_c - m)
                p_p = None if nb == 1 else jnp.exp(s_p - m)
                l = jnp.sum(p_c if nb == 1 else p_c + p_p, axis=-1, keepdims=True)
                probs.append((m, l, p_c.astype(BF16), None if p_p is None else p_p.astype(BF16)))
            for (j, rows, prev, _, _), (m, l, p_c, p_p) in zip(blocks, probs):
                acc = _dot(p_c, perm_ref[g, 2, rows, :].astype(BF16))
                if nb > 1:
                    acc = acc + _dot(p_p, perm_ref[g, 2, prev, :].astype(BF16))
                out_rows = rows if dil == 1 else pl.ds((j % nb) * (BAND * dil) + j // nb, BAND, stride=dil)
                acc_ref[g, out_rows, :] = acc * (1.0 / l)
                lse_ref[g, out_rows, :] = jnp.broadcast_to(m + jnp.log(l), (BAND, LANES))
            return carry

        assert (dil * nb) % ATTN_UNROLL == 0
        lax.fori_loop(0, dil * nb // ATTN_UNROLL, body, 0)

    def combine(c, carry):
        rows = pl.ds(pl.multiple_of(c * BAND, BAND), BAND)
        lses = [lse_ref[g, rows, :] for g in range(N_GROUPS)]
        mx = functools.reduce(jnp.maximum, lses)
        ws = [jnp.exp(l - mx) for l in lses]
        out = functools.reduce(jnp.add, [w * acc_ref[g, rows, :] for g, w in enumerate(ws)])
        o_ref[rows, :] = (out * (1.0 / functools.reduce(jnp.add, ws))).astype(BF16)
        return carry

    lax.fori_loop(0, seq // BAND, combine, 0)


def _prompt_attention(q, k_flat, v_flat, batch, seq):
    for window, dil in DILATED_GROUPS:
        assert window // dil == BAND and seq % (dil * BAND) == 0
    qspec = lambda g: pl.BlockSpec((seq, HEAD_DIM), lambda b, h: (b, g * N_KV_HEADS + h))
    kvspec = pl.BlockSpec((seq * N_KV_HEADS, HEAD_DIM), lambda b, h: (b, 0))
    return pl.pallas_call(
        _prompt_attn_kernel,
        grid=(batch, N_KV_HEADS),
        in_specs=[qspec(0), qspec(1), qspec(2), kvspec, kvspec],
        out_specs=pl.BlockSpec((seq, HEAD_DIM), lambda b, h: (b, h)),
        out_shape=jax.ShapeDtypeStruct((batch * seq, ATTN_WIDTH), BF16),
        scratch_shapes=[pltpu.VMEM((N_GROUPS, 3, seq, HEAD_DIM), F32), pltpu.VMEM((2, seq, HEAD_DIM), F32),
                        pltpu.VMEM((N_GROUPS, seq, HEAD_DIM), F32), pltpu.VMEM((N_GROUPS, seq, LANES), F32)],
        compiler_params=_params("parallel", "arbitrary"),
        name="prompt_attn",
    )(q, q, q, k_flat, v_flat)


def _mem_kv_kernel(x_ref, g_ref, w_ref, kg_ref, k_ref, v_ref):
    tm = x_ref.shape[0]
    h = _rms(x_ref[...], g_ref[...]).astype(BF16)
    zk = _dot(h, w_ref[:, :MEM_WIDTH])
    zv = _dot(h, w_ref[:, MEM_WIDTH:])
    for j in range(MEM_HEADS):
        sl = slice(j * MEM_HEAD_DIM, (j + 1) * MEM_HEAD_DIM)
        k_ref[pl.ds(j, tm, stride=MEM_HEADS), :] = _rms(zk[:, sl], kg_ref[...])
        v_ref[pl.ds(j, tm, stride=MEM_HEADS), :] = zv[:, sl]


def _mem_kv(mem, g, w_bf16, k_g):
    m, d = mem.shape
    tm = min(ROW_TILE, m)
    row = lambda i: (i, 0)
    shp = jax.ShapeDtypeStruct((m * MEM_HEADS, MEM_HEAD_DIM), F32)
    ospec = pl.BlockSpec((tm * MEM_HEADS, MEM_HEAD_DIM), row)
    return pl.pallas_call(
        _mem_kv_kernel,
        grid=(m // tm,),
        in_specs=[pl.BlockSpec((tm, d), row), _const_spec((1, d)), _const_spec((d, 2 * MEM_WIDTH)),
                  _const_spec((1, MEM_HEAD_DIM))],
        out_specs=(ospec, ospec),
        out_shape=(shp, shp),
        compiler_params=_params("parallel"),
        name="mem_kv",
    )(mem, g, w_bf16, k_g)


def _mem_attn_kernel(q_ref, k_ref, v_ref, o_ref):
    scale = MEM_HEAD_DIM ** -0.5
    n_mem = k_ref.shape[0] // MEM_HEADS
    for h in range(MEM_HEADS):
        sl = slice(h * MEM_HEAD_DIM, (h + 1) * MEM_HEAD_DIM)
        head_rows = pl.ds(h, n_mem, stride=MEM_HEADS)
        s = _dot_nt(q_ref[:, sl], k_ref[head_rows, :].astype(BF16)) * scale
        p = jnp.exp(s - jnp.max(s, axis=-1, keepdims=True))
        l = jnp.sum(p, axis=-1, keepdims=True)
        o_ref[:, sl] = (_dot(p.astype(BF16), v_ref[head_rows, :].astype(BF16)) * (1.0 / l)).astype(BF16)


def _mem_attention(mq, mem_k_flat, mem_v_flat, batch, seq):
    rows = mem_k_flat.shape[0] // batch
    tq = min(512, seq)
    nq = seq // tq
    kspec = pl.BlockSpec((rows, MEM_HEAD_DIM), lambda b, i: (b, 0))
    return pl.pallas_call(
        _mem_attn_kernel,
        grid=(batch, nq),
        in_specs=[pl.BlockSpec((tq, MEM_WIDTH), lambda b, i: (b * nq + i, 0)), kspec, kspec],
        out_specs=pl.BlockSpec((tq, MEM_WIDTH), lambda b, i: (b * nq + i, 0)),
        out_shape=jax.ShapeDtypeStruct((batch * seq, MEM_WIDTH), BF16),
        compiler_params=_params("parallel", "parallel"),
        name="mem_attn",
    )(mq, mem_k_flat, mem_v_flat)


CONV_HALO = 32
CONV_CHUNK = 32
CONV_ACC_ROWS = 128
CONV_TAP_UNROLL = 8


def _ln_swish(c, g, b):
    mu = jnp.mean(c, axis=-1, keepdims=True)
    xc = c - mu
    y = xc * lax.rsqrt(jnp.mean(xc * xc, axis=-1, keepdims=True) + EPS) * g + b
    return y * jax.nn.sigmoid(y)


def _conv_prompt_kernel(halo_ref, u_ref, w_ref, b_ref, g_ref, beta_ref, o_ref, ext_ref, conv_ref, wb_ref):
    tc, ch = u_ref.shape
    first = pl.program_id(1) == 0
    lane_tiles = [slice(j * LANES, (j + 1) * LANES) for j in range(ch // LANES)]
    for j, sl in enumerate(lane_tiles):
        halo = jnp.where(first, 0.0, halo_ref[:, sl])
        for dup in range(2):
            ext_ref[j, pl.ds(dup, CONV_HALO, stride=2), :] = halo
            ext_ref[j, pl.ds(2 * CONV_HALO + dup, tc, stride=2), :] = u_ref[:, sl]
    for w in range(CONV_WIDTH):
        wb_ref[w] = jnp.broadcast_to(w_ref[w:w + 1, :], (8, ch))
    lead = CONV_HALO - (CONV_WIDTH - 1)
    rows = CONV_ACC_ROWS
    for j, sl in enumerate(lane_tiles):
        for r0 in range(0, tc, rows):
            def tap(w, acc, j=j, sl=sl, r0=r0):
                win = ext_ref[j, pl.ds(2 * (r0 + lead + w), rows, stride=2), :].reshape(rows // 8, 8, LANES)
                return acc + wb_ref[w, :, sl][None] * win

            acc = lax.fori_loop(0, CONV_WIDTH, tap, jnp.zeros((rows // 8, 8, LANES), F32) + b_ref[:, sl],
                                unroll=CONV_TAP_UNROLL)
            conv_ref[r0:r0 + rows, sl] = acc.reshape(rows, LANES)
    for c0 in range(0, tc, CONV_CHUNK):
        o_ref[c0:c0 + CONV_CHUNK, :] = _ln_swish(conv_ref[c0:c0 + CONV_CHUNK, :], g_ref[...], beta_ref[...]).astype(BF16)


def _conv_prompt(u, conv_w, conv_b, ln_g, ln_b, batch, seq):
    ch = u.shape[1]
    tc = min(ROW_TILE, seq)
    nt = seq // tc
    ratio = tc // CONV_HALO
    return pl.pallas_call(
        _conv_prompt_kernel,
        grid=(batch, nt),
        in_specs=[
            pl.BlockSpec((CONV_HALO, ch), lambda b, i: (jnp.maximum((b * nt + i) * ratio - 1, 0), 0)),
            pl.BlockSpec((tc, ch), lambda b, i: (b * nt + i, 0)),
            _const_spec((CONV_WIDTH, ch)), _const_spec((1, ch)), _const_spec((1, ch)), _const_spec((1, ch)),
        ],
        out_specs=pl.BlockSpec((tc, ch), lambda b, i: (b * nt + i, 0)),
        out_shape=jax.ShapeDtypeStruct((batch * seq, ch), BF16),
        scratch_shapes=[pltpu.VMEM((ch // LANES, 2 * (CONV_HALO + tc), LANES), F32), pltpu.VMEM((tc, ch), F32),
                        pltpu.VMEM((CONV_WIDTH, 8, ch), F32)],
        compiler_params=_params("parallel", "parallel"),
        name="conv_prompt",
    )(u, u, conv_w, conv_b, ln_g, ln_b)


def _conv_sample_kernel(state_ref, new_ref, w_ref, b_ref, g_ref, beta_ref, o_ref, state_out_ref, ext_ref):
    nb, ctx, ch = state_ref.shape
    t = new_ref.shape[1]
    ext_ref[:, 0:ctx, :] = state_ref[...]
    ext_ref[:, ctx:ctx + t, :] = new_ref[...]
    acc = jnp.zeros((nb, t, ch), F32) + b_ref[...]
    for w in range(CONV_WIDTH):
        acc = acc + w_ref[w:w + 1, :] * ext_ref[:, w:w + t, :]
    o_ref[...] = _ln_swish(acc, g_ref[...], beta_ref[...])
    state_out_ref[...] = ext_ref[:, t:t + ctx, :]


def _conv_sample(state, u_new, conv_w, conv_b, ln_g, ln_b):
    batch, ctx, ch = state.shape
    t = u_new.shape[1]
    nb = 8
    blk = lambda n: pl.BlockSpec((nb, n, ch), lambda i: (i, 0, 0))
    return pl.pallas_call(
        _conv_sample_kernel,
        grid=(batch // nb,),
        in_specs=[blk(ctx), blk(t), _const_spec((CONV_WIDTH, ch)), _const_spec((1, ch)), _const_spec((1, ch)),
                  _const_spec((1, ch))],
        out_specs=(blk(t), blk(ctx)),
        out_shape=(jax.ShapeDtypeStruct((batch, t, ch), F32), jax.ShapeDtypeStruct((batch, ctx, ch), F32)),
        scratch_shapes=[pltpu.VMEM((nb, ctx + t + 6, ch), F32)],
        compiler_params=_params("parallel"),
        name="conv_sample",
    )(state, u_new, conv_w, conv_b, ln_g, ln_b)


GROUP_LANES = 16


def _spread_groups(vec, combine):
    t = combine(combine(vec, pltpu.roll(vec, LANES - GROUP_LANES, 1)), pltpu.roll(vec, LANES - 2 * GROUP_LANES, 1))
    lane = lax.broadcasted_iota(jnp.int32, vec.shape, 1)
    return jnp.where(lane < GROUP_LANES, t,
                     jnp.where(lane < 2 * GROUP_LANES, pltpu.roll(t, GROUP_LANES, 1), pltpu.roll(t, 2 * GROUP_LANES, 1)))


SLAB_POS = DILATED_GROUPS[-1][1]
SLAB_ROWS = SLAB_POS * N_KV_HEADS
TAIL_POS = max(w for w, d in DILATED_GROUPS if d < SLAB_POS)


def _sample_attn_kernel(kc_ref, vc_ref, kn_ref, vn_ref, wq_ref, bias_d_ref, bias_t_ref, bias_n_ref, mk_ref, mv_ref,
                        wm_ref, bias_m_ref, kw_ref, vw_ref, a_ref, m_ref, kx_ref, vx_ref):
    n_slab = kc_ref.shape[1]
    new_rows = kn_ref.shape[1]
    t = new_rows // N_KV_HEADS
    tail_slabs = TAIL_POS // SLAB_POS
    n_used = N_GROUPS * GROUP_LANES
    lane = lax.broadcasted_iota(jnp.int32, (1, LANES), 1)

    for src, new, dst in ((kc_ref, kn_ref, kw_ref), (vc_ref, vn_ref, vw_ref)):
        dst[0, :, 0:SLAB_ROWS - new_rows, :] = src[0, :, new_rows:SLAB_ROWS, :]
        dst[0, 0:n_slab - 1, SLAB_ROWS - new_rows:SLAB_ROWS, :] = src[0, 1:n_slab, 0:new_rows, :]
        dst[0, n_slab - 1, SLAB_ROWS - new_rows:SLAB_ROWS, :] = new[0]

    kx_ref[...] = jnp.zeros(kx_ref.shape, F32)
    vx_ref[...] = jnp.zeros(vx_ref.shape, F32)
    kx_ref[0:new_rows, :] = kn_ref[0]
    vx_ref[0:new_rows, :] = vn_ref[0]

    def dilated(ref):
        return ref[0, :, 0:new_rows, :].reshape(n_slab * new_rows, HEAD_DIM).astype(BF16)

    def tail(ref):
        return ref[0, n_slab - tail_slabs:n_slab, :, :].reshape(tail_slabs * SLAB_ROWS, HEAD_DIM).astype(BF16)

    scale = HEAD_DIM ** -0.5
    wq = wq_ref[0]
    s_d = _dot(dilated(kc_ref), wq) * scale + bias_d_ref[...]
    s_t = _dot(tail(kc_ref), wq) * scale + bias_t_ref[...]
    s_n = _dot(kx_ref[...].astype(BF16), wq) * scale + bias_n_ref[...]
    col_max = lambda s: jnp.max(s, axis=0, keepdims=True)
    col_sum = lambda p: jnp.sum(p, axis=0, keepdims=True)
    m_col = jnp.maximum(jnp.maximum(col_max(s_d), col_max(s_t)), col_max(s_n))
    m_joint = jnp.where(lane < n_used, _spread_groups(m_col, jnp.maximum), 0.0)
    p_d, p_t, p_n = jnp.exp(s_d - m_joint), jnp.exp(s_t - m_joint), jnp.exp(s_n - m_joint)
    l_col = col_sum(p_d) + col_sum(p_t) + col_sum(p_n)
    inv = 1.0 / jnp.where(lane < n_used, _spread_groups(l_col, jnp.add), 1.0)
    o = (_dot_tn((p_d * inv).astype(BF16), dilated(vc_ref)) + _dot_tn((p_t * inv).astype(BF16), tail(vc_ref))
         + _dot_tn((p_n * inv).astype(BF16), vx_ref[...].astype(BF16)))
    for h in range(N_KV_HEADS):
        r = h * t
        a_ref[0, :, h * HEAD_DIM:(h + 1) * HEAD_DIM] = functools.reduce(
            jnp.add, [o[g * GROUP_LANES + r:g * GROUP_LANES + r + t, :] for g in range(N_GROUPS)])

    sm = _dot(mk_ref[0].astype(BF16), wm_ref[0]) * (MEM_HEAD_DIM ** -0.5) + bias_m_ref[...]
    pm = jnp.exp(sm - jnp.where(lane < MEM_HEADS * t, col_max(sm), 0.0))
    lm = jnp.where(lane < MEM_HEADS * t, col_sum(pm), 1.0)
    om = _dot_tn((pm * (1.0 / lm)).astype(BF16), mv_ref[0].astype(BF16))
    for h in range(MEM_HEADS):
        m_ref[0, :, h * MEM_HEAD_DIM:(h + 1) * MEM_HEAD_DIM] = om[h * t:(h + 1) * t, :]


def _sample_masks(cache_len, t, n_mem):
    col = np.arange(LANES)[None, :]
    g, c_head, c_tok = col // GROUP_LANES, (col % GROUP_LANES) // t, (col % GROUP_LANES) % t
    used = (col < N_GROUPS * GROUP_LANES) & (col % GROUP_LANES < N_KV_HEADS * t)
    pad = [1] * (LANES // GROUP_LANES - N_GROUPS)
    dil = np.array([d for _, d in DILATED_GROUPS] + pad)[g]
    win = np.array([w for w, _ in DILATED_GROUPS] + pad)[g]
    sparse = dil >= SLAB_POS

    def keep(pos, head, group_sel):
        dist = cache_len + c_tok - pos
        return used & group_sel & (head == c_head) & (dist >= 0) & (dist % dil == 0) & (dist <= win)

    n_slab = cache_len // SLAB_POS
    y = np.arange(n_slab * t * N_KV_HEADS)[:, None]
    keep_d = keep((y // (t * N_KV_HEADS)) * SLAB_POS + (y % (t * N_KV_HEADS)) // N_KV_HEADS, y % N_KV_HEADS, sparse)
    x = np.arange(TAIL_POS * N_KV_HEADS)[:, None]
    keep_t = keep(cache_len - TAIL_POS + x // N_KV_HEADS, x % N_KV_HEADS, ~sparse)
    z = np.arange(LANES)[:, None]
    keep_n = keep(cache_len + z // N_KV_HEADS, z % N_KV_HEADS, True) & (z < t * N_KV_HEADS)
    w = np.arange(n_mem * MEM_HEADS)[:, None]
    keep_m = (col < MEM_HEADS * t) & (w % MEM_HEADS == col // t)
    return tuple(jnp.asarray(np.where(k, 0.0, NEG), F32) for k in (keep_d, keep_t, keep_n, keep_m))


def _query_columns(q, batch, t, n_groups):
    qt = q.reshape(batch, t, n_groups, N_KV_HEADS, HEAD_DIM).transpose(0, 4, 2, 3, 1)
    qt = qt.reshape(batch, HEAD_DIM, n_groups, N_KV_HEADS * t)
    qt = jnp.pad(qt, ((0, 0), (0, 0), (0, 0), (0, GROUP_LANES - N_KV_HEADS * t)))
    qt = qt.reshape(batch, HEAD_DIM, n_groups * GROUP_LANES)
    return jnp.pad(qt, ((0, 0), (0, 0), (0, LANES - n_groups * GROUP_LANES)))


def _sample_attention(q, k_new, v_new, mq, cache_k, cache_v, mem_k, mem_v):
    batch, cache_len = cache_k.shape[0], cache_k.shape[1]
    new_rows = k_new.shape[0] // batch
    t = new_rows // N_KV_HEADS
    n_mem = mem_k.shape[1]
    assert N_KV_HEADS * t <= GROUP_LANES and t <= SLAB_POS and new_rows % 8 == 0
    assert cache_len % SLAB_POS == 0 and cache_len >= max(w for w, _ in DILATED_GROUPS)
    assert all(d == SLAB_POS or w <= TAIL_POS for w, d in DILATED_GROUPS)
    n_slab = cache_len // SLAB_POS
    wq = _query_columns(q, batch, t, N_GROUPS)
    wm = _query_columns(mq, batch, t, 1)
    bias_d, bias_t, bias_n, bias_m = _sample_masks(cache_len, t, n_mem)
    per_b = lambda *shape: pl.BlockSpec((1,) + shape, lambda b: (b,) + (0,) * len(shape))
    slabs = lambda c: c.reshape(batch, n_slab, SLAB_ROWS, HEAD_DIM)
    mem_rows = n_mem * MEM_HEADS
    win = jax.ShapeDtypeStruct((batch, n_slab, SLAB_ROWS, HEAD_DIM), F32)
    k_win, v_win, a, m = pl.pallas_call(
        _sample_attn_kernel,
        grid=(batch,),
        in_specs=[
            per_b(n_slab, SLAB_ROWS, HEAD_DIM), per_b(n_slab, SLAB_ROWS, HEAD_DIM),
            per_b(new_rows, HEAD_DIM), per_b(new_rows, HEAD_DIM), per_b(HEAD_DIM, LANES),
            _const_spec(bias_d.shape), _const_spec(bias_t.shape), _const_spec(bias_n.shape),
            per_b(mem_rows, MEM_HEAD_DIM), per_b(mem_rows, MEM_HEAD_DIM), per_b(MEM_HEAD_DIM, LANES),
            _const_spec(bias_m.shape),
        ],
        out_specs=(per_b(n_slab, SLAB_ROWS, HEAD_DIM), per_b(n_slab, SLAB_ROWS, HEAD_DIM), per_b(t, ATTN_WIDTH),
                   per_b(t, MEM_WIDTH)),
        out_shape=(win, win, jax.ShapeDtypeStruct((batch, t, ATTN_WIDTH), F32),
                   jax.ShapeDtypeStruct((batch, t, MEM_WIDTH), F32)),
        scratch_shapes=[pltpu.VMEM((LANES, HEAD_DIM), F32), pltpu.VMEM((LANES, HEAD_DIM), F32)],
        compiler_params=_params("parallel"),
        name="sample_attn",
    )(slabs(cache_k), slabs(cache_v), k_new.reshape(batch, new_rows, HEAD_DIM), v_new.reshape(batch, new_rows, HEAD_DIM),
      wq, bias_d, bias_t, bias_n, mem_k.reshape(batch, mem_rows, MEM_HEAD_DIM),
      mem_v.reshape(batch, mem_rows, MEM_HEAD_DIM), wm, bias_m)
    return k_win.reshape(cache_k.shape), v_win.reshape(cache_v.shape), a, m


ROUTE_GROUP_LANE0 = 0
ROUTE_EXPERT_LANE0 = N_EXPERT_GROUPS


def _route(logits):
    lane = lax.broadcasted_iota(jnp.int32, logits.shape, 1).astype(F32)
    big = float(LANES)

    def masked_softmax(keep):
        z = jnp.where(keep, logits, NEG)
        e = jnp.where(keep, jnp.exp(z - jnp.max(z, axis=-1, keepdims=True)), 0.0)
        return e / jnp.sum(e, axis=-1, keepdims=True)

    def first_argmax(vals, keep):
        top = jnp.max(jnp.where(keep, vals, -1.0), axis=-1, keepdims=True)
        idx = jnp.min(jnp.where(jnp.logical_and(keep, vals == top), lane, big), axis=-1, keepdims=True)
        return top, idx

    is_group = lane < N_EXPERT_GROUPS
    pg = masked_softmax(is_group)
    pg_top, g_idx = first_argmax(pg, is_group)
    lo = ROUTE_EXPERT_LANE0 + g_idx * EXPERTS_PER_GROUP
    in_group = jnp.logical_and(lane >= lo, lane < lo + EXPERTS_PER_GROUP)
    pe = masked_softmax(in_group)
    p1, i1 = first_argmax(pe, in_group)
    p2, i2 = first_argmax(pe, jnp.logical_and(in_group, lane != i1))
    denom = p1 + p2
    w1 = pg_top * p1 / denom
    w2 = pg_top * p2 / denom
    return lane, i1 - ROUTE_EXPERT_LANE0, i2 - ROUTE_EXPERT_LANE0, w1, w2


def _merge_kernel(x_ref, gmix_ref, ao_ref, cc_ref, mo_ref, wgate_ref, wa_ref, wc_ref, wm_ref, wo_ref, gffn_ref, wr_ref,
                  br_ref, cnt_in_ref, x1_ref, h2_ref, route_ref, cnt_out_ref, cnt_ref):
    @pl.when(pl.program_id(0) == 0)
    def _():
        cnt_ref[...] = cnt_in_ref[...]

    d = x_ref.shape[1]
    x = x_ref[...]
    h = _rms(x, gmix_ref[...]).astype(BF16)
    a = _dot(ao_ref[...].astype(BF16), wa_ref[...])
    c = _dot(cc_ref[...].astype(BF16), wc_ref[...])
    m = _dot(mo_ref[...].astype(BF16), wm_ref[...])
    z = jax.nn.sigmoid(_dot(h, wgate_ref[:, 0:d])) * a
    z = z + jax.nn.sigmoid(_dot(h, wgate_ref[:, d:2 * d])) * c
    z = z + jax.nn.sigmoid(_dot(h, wgate_ref[:, 2 * d:3 * d])) * m
    x1 = x + _dot(z.astype(BF16), wo_ref[...])
    x1_ref[...] = x1
    h2 = _rms(x1, gffn_ref[...]).astype(BF16)
    h2_ref[...] = _pack_halves(h2)
    logits = _dot(h2, wr_ref[...]) + br_ref[...]
    lane, e1, e2, w1, w2 = _route(logits)

    tm = x_ref.shape[0]
    hit1 = jnp.where(lane == e1, 1.0, 0.0)
    hit2 = jnp.where(lane == e2, 1.0, 0.0)
    hits = hit1 + hit2
    earlier = (lax.broadcasted_iota(jnp.int32, (tm, tm), 1) < lax.broadcasted_iota(jnp.int32, (tm, tm), 0))
    before = _dot(jnp.where(earlier, 1.0, 0.0).astype(BF16), hits.astype(BF16)) + cnt_ref[...]
    rank1 = jnp.sum(hit1 * before, axis=-1, keepdims=True)
    rank2 = jnp.sum(hit2 * before, axis=-1, keepdims=True)
    cnt_ref[...] = cnt_ref[...] + jnp.sum(hits, axis=0, keepdims=True)
    cnt_out_ref[...] = cnt_ref[...]
    route = jnp.zeros(logits.shape, F32)
    for i, val in enumerate((e1, e2, w1, w2, rank1, rank2)):
        route = jnp.where(lane == i, val, route)
    route_ref[...] = route


ROUTE_E, ROUTE_W, ROUTE_RANK = 0, 2, 4


def _merge(x, g_mix, ao, cc, mo, w_gate, w_a, w_c, w_m, w_o, g_ffn, w_r, b_r, counts):
    m, d = x.shape
    tm = min(MERGE_TILE, m)
    row = lambda i: (i, 0)
    rows = lambda arr: pl.BlockSpec((tm, arr.shape[1]), row)
    ins = [x, g_mix, ao, cc, mo, w_gate, w_a, w_c, w_m, w_o, g_ffn, w_r, b_r, counts]
    specs = ([rows(x), _const_spec(g_mix.shape), rows(ao), rows(cc), rows(mo)]
             + [_const_spec(a.shape) for a in (w_gate, w_a, w_c, w_m, w_o, g_ffn, w_r, b_r, counts)])
    out_shape = (jax.ShapeDtypeStruct((m, d), F32), jax.ShapeDtypeStruct((m, d // 2), jnp.uint32),
                 jax.ShapeDtypeStruct((m, LANES), F32), jax.ShapeDtypeStruct((1, LANES), F32))
    return pl.pallas_call(
        _merge_kernel,
        grid=(m // tm,),
        in_specs=specs,
        out_specs=(rows(out_shape[0]), rows(out_shape[1]), rows(out_shape[2]), _const_spec((1, LANES))),
        out_shape=out_shape,
        scratch_shapes=[pltpu.VMEM((1, LANES), F32)],
        compiler_params=_params("arbitrary"),
        name="merge",
    )(*ins)


def _routing_tables(counts, routes):
    cnt = counts[0, :N_EXPERTS].astype(jnp.int32)
    tiles = (cnt + MOE_TILE - 1) // MOE_TILE
    tile_start = jnp.cumsum(tiles) - tiles
    row_start = tile_start * MOE_TILE
    ids = routes[:, ROUTE_E:ROUTE_E + TOP_K].astype(jnp.int32)
    start_of = jnp.sum(jnp.where(ids[:, :, None] == jnp.arange(N_EXPERTS)[None, None, :], row_start[None, None, :], 0),
                       axis=-1)
    pos = start_of + routes[:, ROUTE_RANK:ROUTE_RANK + TOP_K].astype(jnp.int32)
    return pos.reshape(-1), tile_start.astype(jnp.int32), tiles.astype(jnp.int32)


def _dispatch_kernel(pos_ref, h_ref, xs_in, xs_out, sem, *, row0):
    del xs_in
    tm = h_ref.shape[0]
    base = (row0 + pl.program_id(0) * tm) * TOP_K

    def issue(j, carry):
        for k in range(TOP_K):
            slot = pos_ref[base + j * TOP_K + k]
            pltpu.make_async_copy(h_ref.at[pl.ds(j, 1)], xs_out.at[pl.ds(slot, 1)], sem).start()
        return carry

    lax.fori_loop(0, tm, issue, 0, unroll=8)
    for k in range(TOP_K):
        pltpu.make_async_copy(h_ref, xs_out.at[pl.ds(0, tm)], sem).wait()


def _dispatch(h2, pos, xs, row0):
    m, d = h2.shape
    tm = min(ROW_TILE, m)
    return pl.pallas_call(
        functools.partial(_dispatch_kernel, row0=row0),
        grid_spec=pltpu.PrefetchScalarGridSpec(
            num_scalar_prefetch=1,
            grid=(m // tm,),
            in_specs=[pl.BlockSpec((tm, d), lambda i, *_: (i, 0)), pl.BlockSpec(memory_space=pl.ANY)],
            out_specs=pl.BlockSpec(memory_space=pl.ANY),
            scratch_shapes=[pltpu.SemaphoreType.DMA(())],
        ),
        out_shape=jax.ShapeDtypeStruct(xs.shape, xs.dtype),
        input_output_aliases={2: 0},
        compiler_params=_params("arbitrary"),
        name="moe_dispatch",
    )(pos, h2, xs)


def _gmm_kernel(start_ref, count_ref, xs_hbm, wg_ref, wu_ref, wd_ref, ys_hbm, xbuf, obuf, wg_s, wu_s, wd_s,
                in_sems, out_sems):
    e = pl.program_id(0)
    nb = GMM_BUFS
    first, n = start_ref[e], count_ref[e]

    def copies(expert_first):
        def rows(i):
            return pl.ds(pl.multiple_of((expert_first + i) * MOE_TILE, MOE_TILE), MOE_TILE)

        def in_copy(i):
            return pltpu.make_async_copy(xs_hbm.at[rows(i)], xbuf.at[i % nb], in_sems.at[i % nb])

        def out_copy(i):
            return pltpu.make_async_copy(obuf.at[i % nb], ys_hbm.at[rows(i)], out_sems.at[i % nb])

        return in_copy, out_copy

    in_copy, out_copy = copies(first)

    def start_first_reads(expert):
        copy, _ = copies(start_ref[expert])
        lax.fori_loop(0, jnp.minimum(count_ref[expert], nb - 1), lambda i, c: (copy(i).start(), c)[1], 0)

    @pl.when(e == 0)
    def _():
        start_first_reads(0)

    @pl.when(n > 0)
    def _():
        wg_s[...] = wg_ref[0].astype(BF16)
        wu_s[...] = wu_ref[0].astype(BF16)
        wd_s[...] = wd_ref[0].astype(BF16)
        half = xbuf.shape[2]

        def tile(i, carry):
            slot = i % nb
            in_copy(i).wait()

            @pl.when(i + nb - 1 < n)
            def _():
                in_copy(i + nb - 1).start()

            @pl.when(i >= nb)
            def _():
                out_copy(i - nb).wait()

            x_hi, x_lo = _unpack_halves(xbuf[slot])
            gate = _dot(x_hi, wg_s[0:half, :]) + _dot(x_lo, wg_s[half:, :])
            up = _dot(x_hi, wu_s[0:half, :]) + _dot(x_lo, wu_s[half:, :])
            hid = gate * jax.nn.sigmoid(gate) * up
            obuf[slot] = _dot(hid.astype(BF16), wd_s[...])
            out_copy(i).start()
            return carry

        lax.fori_loop(0, n, tile, 0)
        lax.fori_loop(jnp.maximum(n - nb, 0), n, lambda i, c: (out_copy(i).wait(), c)[1], 0)

    @pl.when(e + 1 < pl.num_programs(0))
    def _():
        start_first_reads(e + 1)

    @pl.when(e == pl.num_programs(0) - 1)
    def _():
        obuf[0] = jnp.zeros(obuf.shape[1:], F32)

        def fill(t, carry):
            dst = ys_hbm.at[pl.ds(pl.multiple_of(t * MOE_TILE, MOE_TILE), MOE_TILE)]
            copy = pltpu.make_async_copy(obuf.at[0], dst, out_sems.at[0])
            copy.start()
            copy.wait()
            return carry

        lax.fori_loop(first + n, ys_hbm.shape[0] // MOE_TILE, fill, 0)


def _grouped_mlp(xs, tile_start, tile_count, w_gate, w_up, w_down):
    d, ff = w_gate.shape[1], w_gate.shape[2]
    wspec = lambda a, b: pl.BlockSpec((1, a, b), lambda e, *_: (e, 0, 0))
    any_spec = pl.BlockSpec(memory_space=pl.ANY)
    return pl.pallas_call(
        _gmm_kernel,
        grid_spec=pltpu.PrefetchScalarGridSpec(
            num_scalar_prefetch=2,
            grid=(N_EXPERTS,),
            in_specs=[any_spec, wspec(d, ff), wspec(d, ff), wspec(ff, d)],
            out_specs=any_spec,
            scratch_shapes=[pltpu.VMEM((GMM_BUFS, MOE_TILE, xs.shape[1]), xs.dtype),
                            pltpu.VMEM((GMM_BUFS, MOE_TILE, d), F32),
                            pltpu.VMEM((d, ff), BF16), pltpu.VMEM((d, ff), BF16), pltpu.VMEM((ff, d), BF16),
                            pltpu.SemaphoreType.DMA((GMM_BUFS,)), pltpu.SemaphoreType.DMA((GMM_BUFS,))],
        ),
        out_shape=jax.ShapeDtypeStruct((xs.shape[0], d), F32),
        compiler_params=_params("arbitrary"),
        name="moe_gmm",
    )(tile_start, tile_count, xs, w_gate, w_up, w_down)


def _combine_kernel(pos_ref, x1_ref, route_ref, ys_hbm, y_ref, buf_ref, sems, *, row0):
    tm = x1_ref.shape[0]
    i = pl.program_id(0)

    def gather(tile, half):
        base = (row0 + tile * tm) * TOP_K

        def issue(j, carry):
            for k in range(TOP_K):
                slot = pos_ref[base + j * TOP_K + k]
                pltpu.make_async_copy(ys_hbm.at[pl.ds(slot, 1)], buf_ref.at[half, k, pl.ds(j, 1)],
                                      sems.at[half]).start()
            return carry

        lax.fori_loop(0, tm, issue, 0, unroll=4)

    @pl.when(i == 0)
    def _():
        gather(0, 0)

    @pl.when(i + 1 < pl.num_programs(0))
    def _():
        gather(i + 1, (i + 1) % 2)

    half = i % 2
    for k in range(TOP_K):
        pltpu.make_async_copy(ys_hbm.at[pl.ds(0, tm)], buf_ref.at[half, k], sems.at[half]).wait()
    route = route_ref[...]
    y_ref[...] = (x1_ref[...] + route[:, ROUTE_W:ROUTE_W + 1] * buf_ref[half, 0]
                  + route[:, ROUTE_W + 1:ROUTE_W + 2] * buf_ref[half, 1])


def _combine(x1, route, pos, ys, row0):
    m, d = x1.shape
    tm = min(ROW_TILE, m)
    row = lambda i, *_: (i, 0)
    return pl.pallas_call(
        functools.partial(_combine_kernel, row0=row0),
        grid_spec=pltpu.PrefetchScalarGridSpec(
            num_scalar_prefetch=1,
            grid=(m // tm,),
            in_specs=[pl.BlockSpec((tm, d), row), pl.BlockSpec((tm, LANES), row), pl.BlockSpec(memory_space=pl.ANY)],
            out_specs=pl.BlockSpec((tm, d), row),
            scratch_shapes=[pltpu.VMEM((2, TOP_K, tm, d), F32), pltpu.SemaphoreType.DMA((2,))],
        ),
        out_shape=jax.ShapeDtypeStruct((m, d), F32),
        compiler_params=_params("arbitrary"),
        name="moe_combine",
    )(pos, x1, route, ys)


def _layer(layer, x_prompt, x_sample, mem_prompt, cache_k, cache_v, state_conv, cache_mem_k, cache_mem_v, p):
    batch, seq, d = x_prompt.shape
    dec_batch, dec_seq, _ = x_sample.shape
    conv_ch = p["conv_w"].shape[-1]
    n_in = N_Q_HEADS * HEAD_DIM + 2 * ATTN_WIDTH + 2 * conv_ch + MEM_WIDTH
    past_len = cache_k.shape[2]

    row2 = lambda name: p[name][layer][None, :]
    w_in = p["w_in"][layer]
    w_main = w_in[:, :n_in].astype(BF16)
    w_gate = w_in[:, n_in:].astype(BF16)
    w_a, w_c, w_m, w_o = (p[n][layer].astype(BF16) for n in ("w_attn_proj", "w_conv_proj", "w_mem_proj", "w_out"))
    w_router = jnp.concatenate(
        [p["w_router_group"][layer], p["w_router_expert"][layer].transpose(1, 0, 2).reshape(d, N_EXPERTS)], axis=1)
    w_router = jnp.pad(w_router, ((0, 0), (0, LANES - w_router.shape[1]))).astype(BF16)
    b_router = jnp.concatenate([p["b_router_group"][layer], p["b_router_expert"][layer].reshape(-1)])
    b_router = jnp.pad(b_router, (0, LANES - b_router.shape[0]))[None, :]
    conv_args = (p["conv_w"][layer], row2("conv_b"), row2("conv_ln_g"), row2("conv_ln_b"))
    merge_w = (w_gate, w_a, w_c, w_m, w_o, row2("norm_ffn_g"), w_router, b_router)

    xp = x_prompt.reshape(batch * seq, d)
    tabs_p = _rope_tables(jnp.arange(seq, dtype=jnp.int32))
    q_p, k_p, v_p, u_p, mq_p = _in_proj(xp, row2("norm_mix_g"), w_main, row2("q_norm_g"), row2("k_norm_g"),
                                        row2("mq_norm_g"), tabs_p, conv_ch)
    ao_p = _prompt_attention(q_p, k_p, v_p, batch, seq)
    mem_k_p, mem_v_p = _mem_kv(mem_prompt.reshape(-1, d), row2("mem_norm_g"), p["w_mem_kv"][layer].astype(BF16),
                               row2("mk_norm_g"))
    mo_p = _mem_attention(mq_p, mem_k_p, mem_v_p, batch, seq)
    cc_p = _conv_prompt(u_p, *conv_args, batch, seq)
    x1_p, h2_p, route_p, counts = _merge(xp, row2("norm_mix_g"), ao_p, cc_p, mo_p, *merge_w,
                                         jnp.zeros((1, LANES), F32))

    xs = x_sample.reshape(dec_batch * dec_seq, d)
    tabs_s = _rope_tables(jnp.tile(past_len + jnp.arange(dec_seq, dtype=jnp.int32), dec_batch))
    q_s, k_s, v_s, u_s, mq_s = _in_proj(xs, row2("norm_mix_g"), w_main, row2("q_norm_g"), row2("k_norm_g"),
                                        row2("mq_norm_g"), tabs_s, conv_ch)
    k_win_s, v_win_s, a_s, mo_s = _sample_attention(q_s, k_s, v_s, mq_s, cache_k[layer], cache_v[layer],
                                                    cache_mem_k[layer], cache_mem_v[layer])
    cc_s, conv_state_s = _conv_sample(state_conv[layer], u_s.reshape(dec_batch, dec_seq, conv_ch), *conv_args)
    x1_s, h2_s, route_s, counts = _merge(xs, row2("norm_mix_g"), a_s.reshape(-1, ATTN_WIDTH),
                                         cc_s.reshape(-1, conv_ch), mo_s.reshape(-1, MEM_WIDTH), *merge_w, counts)

    n_p, n_s = xp.shape[0], xs.shape[0]
    n_tok = n_p + n_s
    n_tiles = (TOP_K * n_tok + N_EXPERTS * (MOE_TILE - 1)) // MOE_TILE + 1
    pos, tile_start, tile_count = _routing_tables(counts, jnp.concatenate([route_p, route_s], axis=0))
    slots = _dispatch(h2_p, pos, jnp.zeros((n_tiles * MOE_TILE, h2_p.shape[1]), h2_p.dtype), 0)
    slots = _dispatch(h2_s, pos, slots, n_p)
    ys = _grouped_mlp(slots, tile_start, tile_count, p["w_expert_gate"][layer], p["w_expert_up"][layer],
                      p["w_expert_down"][layer])
    y_p = _combine(x1_p, route_p, pos, ys, 0)
    y_s = _combine(x1_s, route_s, pos, ys, n_p)

    state_p = (k_p.reshape(batch, seq, N_KV_HEADS, HEAD_DIM), v_p.reshape(batch, seq, N_KV_HEADS, HEAD_DIM),
               u_p.reshape(batch, seq, conv_ch)[:, seq - (CONV_WIDTH - 1):],
               mem_k_p.reshape(batch, -1, MEM_HEADS, MEM_HEAD_DIM), mem_v_p.reshape(batch, -1, MEM_HEADS, MEM_HEAD_DIM))
    state_s = (k_win_s, v_win_s, conv_state_s)
    return y_p.reshape(batch, seq, d), y_s.reshape(dec_batch, dec_seq, d), state_p, state_s


def kernel(x_prompt, x_sample, mem_prompt, cache_k, cache_v, state_conv, cache_mem_k, cache_mem_v, norm_mix_g, w_in, q_norm_g, k_norm_g, conv_w, conv_b, conv_ln_g, conv_ln_b, mem_norm_g, w_mem_kv, mq_norm_g, mk_norm_g, w_attn_proj, w_conv_proj, w_mem_proj, w_out, norm_ffn_g, w_router_group, b_router_group, w_router_expert, b_router_expert, w_expert_gate, w_expert_up, w_expert_down):
    p = dict(norm_mix_g=norm_mix_g, w_in=w_in, q_norm_g=q_norm_g, k_norm_g=k_norm_g, conv_w=conv_w, conv_b=conv_b,
             conv_ln_g=conv_ln_g, conv_ln_b=conv_ln_b, mem_norm_g=mem_norm_g, w_mem_kv=w_mem_kv, mq_norm_g=mq_norm_g,
             mk_norm_g=mk_norm_g, w_attn_proj=w_attn_proj, w_conv_proj=w_conv_proj, w_mem_proj=w_mem_proj,
             w_out=w_out, norm_ffn_g=norm_ffn_g, w_router_group=w_router_group, b_router_group=b_router_group,
             w_router_expert=w_router_expert, b_router_expert=b_router_expert, w_expert_gate=w_expert_gate,
             w_expert_up=w_expert_up, w_expert_down=w_expert_down)
    depth = w_in.shape[0]
    seq = x_prompt.shape[1]
    assert seq <= max(w for w, _ in DILATED_GROUPS)
    y_p, y_s = x_prompt, x_sample
    states_p, states_s = [], []
    for layer in range(depth):
        y_p, y_s, st_p, st_s = _layer(layer, y_p, y_s, mem_prompt, cache_k, cache_v, state_conv, cache_mem_k,
                                      cache_mem_v, p)
        states_p.append(st_p)
        states_s.append(st_s)
    stack = lambda states, i: jnp.stack([s[i] for s in states], axis=0)
    return (y_p, y_s, stack(states_p, 0), stack(states_p, 1), stack(states_p, 2), stack(states_p, 3),
            stack(states_p, 4), stack(states_s, 0), stack(states_s, 1), stack(states_s, 2))
```

```python
import functools

import jax
import jax.numpy as jnp
import numpy as np
from jax import lax
from jax.experimental import pallas as pl
from jax.experimental.pallas import tpu as pltpu

HEAD_DIM = 128
N_KV_HEADS = 4
DILATED_GROUPS = ((128, 1), (512, 4), (2048, 16))
N_GROUPS = len(DILATED_GROUPS)
N_Q_HEADS = N_GROUPS * N_KV_HEADS
ATTN_WIDTH = N_KV_HEADS * HEAD_DIM
BAND = 128
ATTN_UNROLL = 8
ROPE_THETA = 500000.0
ROT_DIM = HEAD_DIM // 4
CONV_WIDTH = 31
MEM_HEADS = 4
MEM_HEAD_DIM = 128
MEM_WIDTH = MEM_HEADS * MEM_HEAD_DIM
N_EXPERT_GROUPS = 4
EXPERTS_PER_GROUP = 8
N_EXPERTS = N_EXPERT_GROUPS * EXPERTS_PER_GROUP
TOP_K = 2
EPS = 1e-6
NEG = -1e30

LANES = 128
ROW_TILE = 256
MERGE_TILE = 512
IN_PROJ_TILE = 256
MOE_TILE = 256
GMM_BUFS = 4
VMEM_LIMIT = 56 * 1024 * 1024

BF16 = jnp.bfloat16
F32 = jnp.float32


def _params(*sem):
    return pltpu.CompilerParams(dimension_semantics=sem, vmem_limit_bytes=VMEM_LIMIT)


def _dot(a, b):
    return jnp.dot(a, b, preferred_element_type=F32)


def _dot_nt(a, b):
    return lax.dot_general(a, b, (((1,), (1,)), ((), ())), preferred_element_type=F32)


def _dot_tn(a, b):
    return lax.dot_general(a, b, (((0,), (0,)), ((), ())), preferred_element_type=F32)


def _rms(x, g):
    return x * lax.rsqrt(jnp.mean(x * x, axis=-1, keepdims=True) + EPS) * g


def _pack_halves(x):
    c = x.shape[1] // 2
    hi = lax.bitcast_convert_type(x[:, :c].astype(F32), jnp.uint32)
    lo = lax.bitcast_convert_type(x[:, c:].astype(F32), jnp.uint32)
    return hi | (lo >> 16)


def _unpack_halves(p):
    hi = lax.bitcast_convert_type(p & jnp.uint32(0xFFFF0000), F32).astype(BF16)
    lo = lax.bitcast_convert_type(p << 16, F32).astype(BF16)
    return hi, lo


def _const_spec(shape):
    nd = len(shape)
    return pl.BlockSpec(shape, lambda *_: (0,) * nd)


def _in_proj_kernel(x_ref, g_ref, w_ref, qg_ref, kg_ref, mqg_ref, rc_ref, ra_ref, rb_ref,
                    q_ref, k_ref, v_ref, u_ref, mq_ref):
    h = _rms(x_ref[...], g_ref[...]).astype(BF16)
    rc, ra, rb = rc_ref[...], ra_ref[...], rb_ref[...]

    def rope(y):
        return y * rc + pltpu.roll(y, LANES - ROT_DIM // 2, 1) * ra + pltpu.roll(y, ROT_DIM // 2, 1) * rb

    col = 0
    zq = _dot(h, w_ref[:, col:col + N_Q_HEADS * HEAD_DIM])
    for j in range(N_Q_HEADS):
        sl = slice(j * HEAD_DIM, (j + 1) * HEAD_DIM)
        q_ref[:, sl] = rope(_rms(zq[:, sl], qg_ref[...])).astype(BF16)
    col += N_Q_HEADS * HEAD_DIM
    tm = x_ref.shape[0]
    zk = _dot(h, w_ref[:, col:col + ATTN_WIDTH])
    for j in range(N_KV_HEADS):
        sl = slice(j * HEAD_DIM, (j + 1) * HEAD_DIM)
        k_ref[pl.ds(j, tm, stride=N_KV_HEADS), :] = rope(_rms(zk[:, sl], kg_ref[...]))
    col += ATTN_WIDTH
    zv = _dot(h, w_ref[:, col:col + ATTN_WIDTH])
    for j in range(N_KV_HEADS):
        v_ref[pl.ds(j, tm, stride=N_KV_HEADS), :] = zv[:, j * HEAD_DIM:(j + 1) * HEAD_DIM]
    col += ATTN_WIDTH
    conv_ch = u_ref.shape[-1]
    za = _dot(h, w_ref[:, col:col + conv_ch])
    zb = _dot(h, w_ref[:, col + conv_ch:col + 2 * conv_ch])
    u_ref[...] = za * jax.nn.sigmoid(zb)
    col += 2 * conv_ch
    zm = _dot(h, w_ref[:, col:col + MEM_WIDTH])
    for j in range(MEM_HEADS):
        sl = slice(j * MEM_HEAD_DIM, (j + 1) * MEM_HEAD_DIM)
        mq_ref[:, sl] = _rms(zm[:, sl], mqg_ref[...]).astype(BF16)


def _in_proj(x, g_mix, w_bf16, q_g, k_g, mq_g, rope_tabs, conv_ch):
    m, d = x.shape
    tm = min(IN_PROJ_TILE, m)
    n_tab_blocks = rope_tabs[0].shape[0] // tm
    row = lambda i: (i, 0)
    tab = lambda i: (i % n_tab_blocks, 0)
    ncols = w_bf16.shape[1]
    out_shape = (
        jax.ShapeDtypeStruct((m, N_Q_HEADS * HEAD_DIM), BF16),
        jax.ShapeDtypeStruct((m * N_KV_HEADS, HEAD_DIM), F32),
        jax.ShapeDtypeStruct((m * N_KV_HEADS, HEAD_DIM), F32),
        jax.ShapeDtypeStruct((m, conv_ch), F32),
        jax.ShapeDtypeStruct((m, MEM_WIDTH), BF16),
    )
    return pl.pallas_call(
        _in_proj_kernel,
        grid=(m // tm,),
        in_specs=[
            pl.BlockSpec((tm, d), row),
            _const_spec((1, d)),
            _const_spec((d, ncols)),
            _const_spec((1, HEAD_DIM)), _const_spec((1, HEAD_DIM)), _const_spec((1, MEM_HEAD_DIM)),
            pl.BlockSpec((tm, LANES), tab), pl.BlockSpec((tm, LANES), tab), pl.BlockSpec((tm, LANES), tab),
        ],
        out_specs=tuple(pl.BlockSpec((tm * s.shape[0] // m, s.shape[1]), row) for s in out_shape),
        out_shape=out_shape,
        compiler_params=_params("parallel"),
        name="in_proj",
    )(x, g_mix, w_bf16, q_g, k_g, mq_g, *rope_tabs)


def _rope_tables(pos):
    half = ROT_DIM // 2
    inv_freq = jnp.power(jnp.float32(ROPE_THETA), -jnp.arange(half, dtype=F32) * (2.0 / ROT_DIM))
    ang = pos.astype(F32)[:, None] * inv_freq[None, :]
    cos, sin = jnp.cos(ang), jnp.sin(ang)
    n = pos.shape[0]
    ones = jnp.ones((n, LANES - ROT_DIM), F32)
    zeros = jnp.zeros((n, LANES - half), F32)
    rc = jnp.concatenate([cos, cos, ones], axis=1)
    ra = jnp.concatenate([-sin, zeros], axis=1)
    rb = jnp.concatenate([jnp.zeros((n, half), F32), sin, jnp.zeros((n, LANES - ROT_DIM), F32)], axis=1)
    return rc, ra, rb


def _to_residue_layout(dst, src, classes_src, ratio):
    len_src = src.shape[0] // classes_src
    len_dst = len_src // ratio
    for c_src in range(classes_src):
        for a in range(ratio):
            c = c_src + classes_src * a
            dst[c * len_dst:(c + 1) * len_dst, :] = src[pl.ds(c_src * len_src + a, len_dst, stride=ratio), :]


def _prompt_attn_kernel(q0_ref, q1_ref, q2_ref, k_ref, v_ref, o_ref, perm_ref, tmp_ref, acc_ref, lse_ref):
    head = pl.program_id(1)
    seq = q0_ref.shape[0]
    q_refs = (q0_ref, q1_ref, q2_ref)
    dils = [d for _, d in DILATED_GROUPS]
    for t, ref in ((1, k_ref), (2, v_ref)):
        perm_ref[0, t] = ref[pl.ds(head, seq, stride=N_KV_HEADS), :]
        for g in range(1, N_GROUPS):
            _to_residue_layout(perm_ref.at[g, t], perm_ref.at[g - 1, t], dils[g - 1], dils[g] // dils[g - 1])
    for g in range(1, N_GROUPS):
        tmp_ref[0] = q_refs[g][...].astype(F32)
        for step in range(1, g + 1):
            dst = perm_ref.at[g, 0] if step == g else tmp_ref.at[step % 2]
            _to_residue_layout(dst, tmp_ref.at[(step - 1) % 2], dils[step - 1], dils[step] // dils[step - 1])
    iq = lax.broadcasted_iota(jnp.int32, (BAND, BAND), 0)
    ik = lax.broadcasted_iota(jnp.int32, (BAND, BAND), 1)
    keep_c = iq >= ik
    scale = HEAD_DIM ** -0.5

    for g, dil in enumerate(dils):
        nb = seq // (dil * BAND)

        def body(jj, carry, g=g, dil=dil, nb=nb):
            blocks = []
            for u in range(ATTN_UNROLL):
                j = jj * ATTN_UNROLL + u
                rows = pl.ds(pl.multiple_of(j * BAND, BAND), BAND)
                prev = pl.ds(pl.multiple_of(jnp.maximum(j - 1, 0) * BAND, BAND), BAND)
                q = q0_ref[rows, :] if g == 0 else perm_ref[g, 0, rows, :].astype(BF16)
                s_c = jnp.where(keep_c, _dot_nt(q, perm_ref[g, 1, rows, :].astype(BF16)) * scale, NEG)
                s_p = None
                if nb > 1:
                    keep_p = jnp.logical_and(ik >= iq, j % nb > 0)
                    s_p = jnp.where(keep_p, _dot_nt(q, perm_ref[g, 1, prev, :].astype(BF16)) * scale, NEG)
                blocks.append((j, rows, prev, s_c, s_p))
            probs = []
            for j, rows, prev, s_c, s_p in blocks:
                m = jnp.max(s_c if nb == 1 else jnp.maximum(s_c, s_p), axis=-1, keepdims=True)
                p_c = jnp.exp(s_c - m)
                p_p = None if nb == 1 else jnp.exp(s_p - m)
                l = jnp.sum(p_c if nb == 1 else p_c + p_p, axis=-1, keepdims=True)
                probs.append((m, l, p_c.astype(BF16), None if p_p is None else p_p.astype(BF16)))
            for (j, rows, prev, _, _), (m, l, p_c, p_p) in zip(blocks, probs):
                acc = _dot(p_c, perm_ref[g, 2, rows, :].astype(BF16))
                if nb > 1:
                    acc = acc + _dot(p_p, perm_ref[g, 2, prev, :].astype(BF16))
                out_rows = rows if dil == 1 else pl.ds((j % nb) * (BAND * dil) + j // nb, BAND, stride=dil)
                acc_ref[g, out_rows, :] = acc * (1.0 / l)
                lse_ref[g, out_rows, :] = jnp.broadcast_to(m + jnp.log(l), (BAND, LANES))
            return carry

        assert (dil * nb) % ATTN_UNROLL == 0
        lax.fori_loop(0, dil * nb // ATTN_UNROLL, body, 0)

    def combine(c, carry):
        rows = pl.ds(pl.multiple_of(c * BAND, BAND), BAND)
        lses = [lse_ref[g, rows, :] for g in range(N_GROUPS)]
        mx = functools.reduce(jnp.maximum, lses)
        ws = [jnp.exp(l - mx) for l in lses]
        out = functools.reduce(jnp.add, [w * acc_ref[g, rows, :] for g, w in enumerate(ws)])
        o_ref[rows, :] = (out * (1.0 / functools.reduce(jnp.add, ws))).astype(BF16)
        return carry

    lax.fori_loop(0, seq // BAND, combine, 0)


def _prompt_attention(q, k_flat, v_flat, batch, seq):
    for window, dil in DILATED_GROUPS:
        assert window // dil == BAND and seq % (dil * BAND) == 0
    qspec = lambda g: pl.BlockSpec((seq, HEAD_DIM), lambda b, h: (b, g * N_KV_HEADS + h))
    kvspec = pl.BlockSpec((seq * N_KV_HEADS, HEAD_DIM), lambda b, h: (b, 0))
    return pl.pallas_call(
        _prompt_attn_kernel,
        grid=(batch, N_KV_HEADS),
        in_specs=[qspec(0), qspec(1), qspec(2), kvspec, kvspec],
        out_specs=pl.BlockSpec((seq, HEAD_DIM), lambda b, h: (b, h)),
        out_shape=jax.ShapeDtypeStruct((batch * seq, ATTN_WIDTH), BF16),
        scratch_shapes=[pltpu.VMEM((N_GROUPS, 3, seq, HEAD_DIM), F32), pltpu.VMEM((2, seq, HEAD_DIM), F32),
                        pltpu.VMEM((N_GROUPS, seq, HEAD_DIM), F32), pltpu.VMEM((N_GROUPS, seq, LANES), F32)],
        compiler_params=_params("parallel", "arbitrary"),
        name="prompt_attn",
    )(q, q, q, k_flat, v_flat)


def _mem_kv_kernel(x_ref, g_ref, w_ref, kg_ref, k_ref, v_ref):
    tm = x_ref.shape[0]
    h = _rms(x_ref[...], g_ref[...]).astype(BF16)
    zk = _dot(h, w_ref[:, :MEM_WIDTH])
    zv = _dot(h, w_ref[:, MEM_WIDTH:])
    for j in range(MEM_HEADS):
        sl = slice(j * MEM_HEAD_DIM, (j + 1) * MEM_HEAD_DIM)
        k_ref[pl.ds(j, tm, stride=MEM_HEADS), :] = _rms(zk[:, sl], kg_ref[...])
        v_ref[pl.ds(j, tm, stride=MEM_HEADS), :] = zv[:, sl]


def _mem_kv(mem, g, w_bf16, k_g):
    m, d = mem.shape
    tm = min(ROW_TILE, m)
    row = lambda i: (i, 0)
    shp = jax.ShapeDtypeStruct((m * MEM_HEADS, MEM_HEAD_DIM), F32)
    ospec = pl.BlockSpec((tm * MEM_HEADS, MEM_HEAD_DIM), row)
    return pl.pallas_call(
        _mem_kv_kernel,
        grid=(m // tm,),
        in_specs=[pl.BlockSpec((tm, d), row), _const_spec((1, d)), _const_spec((d, 2 * MEM_WIDTH)),
                  _const_spec((1, MEM_HEAD_DIM))],
        out_specs=(ospec, ospec),
        out_shape=(shp, shp),
        compiler_params=_params("parallel"),
        name="mem_kv",
    )(mem, g, w_bf16, k_g)


def _mem_attn_kernel(q_ref, k_ref, v_ref, o_ref):
    scale = MEM_HEAD_DIM ** -0.5
    n_mem = k_ref.shape[0] // MEM_HEADS
    for h in range(MEM_HEADS):
        sl = slice(h * MEM_HEAD_DIM, (h + 1) * MEM_HEAD_DIM)
        head_rows = pl.ds(h, n_mem, stride=MEM_HEADS)
        s = _dot_nt(q_ref[:, sl], k_ref[head_rows, :].astype(BF16)) * scale
        p = jnp.exp(s - jnp.max(s, axis=-1, keepdims=True))
        l = jnp.sum(p, axis=-1, keepdims=True)
        o_ref[:, sl] = (_dot(p.astype(BF16), v_ref[head_rows, :].astype(BF16)) * (1.0 / l)).astype(BF16)


def _mem_attention(mq, mem_k_flat, mem_v_flat, batch, seq):
    rows = mem_k_flat.shape[0] // batch
    tq = min(512, seq)
    nq = seq // tq
    kspec = pl.BlockSpec((rows, MEM_HEAD_DIM), lambda b, i: (b, 0))
    return pl.pallas_call(
        _mem_attn_kernel,
        grid=(batch, nq),
        in_specs=[pl.BlockSpec((tq, MEM_WIDTH), lambda b, i: (b * nq + i, 0)), kspec, kspec],
        out_specs=pl.BlockSpec((tq, MEM_WIDTH), lambda b, i: (b * nq + i, 0)),
        out_shape=jax.ShapeDtypeStruct((batch * seq, MEM_WIDTH), BF16),
        compiler_params=_params("parallel", "parallel"),
        name="mem_attn",
    )(mq, mem_k_flat, mem_v_flat)


CONV_HALO = 32
CONV_CHUNK = 32
CONV_ACC_ROWS = 128
CONV_TAP_UNROLL = 8


def _ln_swish(c, g, b):
    mu = jnp.mean(c, axis=-1, keepdims=True)
    xc = c - mu
    y = xc * lax.rsqrt(jnp.mean(xc * xc, axis=-1, keepdims=True) + EPS) * g + b
    return y * jax.nn.sigmoid(y)


def _conv_prompt_kernel(halo_ref, u_ref, w_ref, b_ref, g_ref, beta_ref, o_ref, ext_ref, conv_ref, wb_ref):
    tc, ch = u_ref.shape
    first = pl.program_id(1) == 0
    lane_tiles = [slice(j * LANES, (j + 1) * LANES) for j in range(ch // LANES)]
    for j, sl in enumerate(lane_tiles):
        halo = jnp.where(first, 0.0, halo_ref[:, sl])
        for dup in range(2):
            ext_ref[j, pl.ds(dup, CONV_HALO, stride=2), :] = halo
            ext_ref[j, pl.ds(2 * CONV_HALO + dup, tc, stride=2), :] = u_ref[:, sl]
    for w in range(CONV_WIDTH):
        wb_ref[w] = jnp.broadcast_to(w_ref[w:w + 1, :], (8, ch))
    lead = CONV_HALO - (CONV_WIDTH - 1)
    rows = CONV_ACC_ROWS
    for j, sl in enumerate(lane_tiles):
        for r0 in range(0, tc, rows):
            def tap(w, acc, j=j, sl=sl, r0=r0):
                win = ext_ref[j, pl.ds(2 * (r0 + lead + w), rows, stride=2), :].reshape(rows // 8, 8, LANES)
                return acc + wb_ref[w, :, sl][None] * win

            acc = lax.fori_loop(0, CONV_WIDTH, tap, jnp.zeros((rows // 8, 8, LANES), F32) + b_ref[:, sl],
                                unroll=CONV_TAP_UNROLL)
            conv_ref[r0:r0 + rows, sl] = acc.reshape(rows, LANES)
    for c0 in range(0, tc, CONV_CHUNK):
        o_ref[c0:c0 + CONV_CHUNK, :] = _ln_swish(conv_ref[c0:c0 + CONV_CHUNK, :], g_ref[...], beta_ref[...]).astype(BF16)


def _conv_prompt(u, conv_w, conv_b, ln_g, ln_b, batch, seq):
    ch = u.shape[1]
    tc = min(ROW_TILE, seq)
    nt = seq // tc
    ratio = tc // CONV_HALO
    return pl.pallas_call(
        _conv_prompt_kernel,
        grid=(batch, nt),
        in_specs=[
            pl.BlockSpec((CONV_HALO, ch), lambda b, i: (jnp.maximum((b * nt + i) * ratio - 1, 0), 0)),
            pl.BlockSpec((tc, ch), lambda b, i: (b * nt + i, 0)),
            _const_spec((CONV_WIDTH, ch)), _const_spec((1, ch)), _const_spec((1, ch)), _const_spec((1, ch)),
        ],
        out_specs=pl.BlockSpec((tc, ch), lambda b, i: (b * nt + i, 0)),
        out_shape=jax.ShapeDtypeStruct((batch * seq, ch), BF16),
        scratch_shapes=[pltpu.VMEM((ch // LANES, 2 * (CONV_HALO + tc), LANES), F32), pltpu.VMEM((tc, ch), F32),
                        pltpu.VMEM((CONV_WIDTH, 8, ch), F32)],
        compiler_params=_params("parallel", "parallel"),
        name="conv_prompt",
    )(u, u, conv_w, conv_b, ln_g, ln_b)


def _conv_sample_kernel(state_ref, new_ref, w_ref, b_ref, g_ref, beta_ref, o_ref, state_out_ref, ext_ref):
    nb, ctx, ch = state_ref.shape
    t = new_ref.shape[1]
    ext_ref[:, 0:ctx, :] = state_ref[...]
    ext_ref[:, ctx:ctx + t, :] = new_ref[...]
    acc = jnp.zeros((nb, t, ch), F32) + b_ref[...]
    for w in range(CONV_WIDTH):
        acc = acc + w_ref[w:w + 1, :] * ext_ref[:, w:w + t, :]
    o_ref[...] = _ln_swish(acc, g_ref[...], beta_ref[...])
    state_out_ref[...] = ext_ref[:, t:t + ctx, :]


def _conv_sample(state, u_new, conv_w, conv_b, ln_g, ln_b):
    batch, ctx, ch = state.shape
    t = u_new.shape[1]
    nb = 8
    blk = lambda n: pl.BlockSpec((nb, n, ch), lambda i: (i, 0, 0))
    return pl.pallas_call(
        _conv_sample_kernel,
        grid=(batch // nb,),
        in_specs=[blk(ctx), blk(t), _const_spec((CONV_WIDTH, ch)), _const_spec((1, ch)), _const_spec((1, ch)),
                  _const_spec((1, ch))],
        out_specs=(blk(t), blk(ctx)),
        out_shape=(jax.ShapeDtypeStruct((batch, t, ch), F32), jax.ShapeDtypeStruct((batch, ctx, ch), F32)),
        scratch_shapes=[pltpu.VMEM((nb, ctx + t + 6, ch), F32)],
        compiler_params=_params("parallel"),
        name="conv_sample",
    )(state, u_new, conv_w, conv_b, ln_g, ln_b)


GROUP_LANES = 16


def _spread_groups(vec, combine):
    t = combine(combine(vec, pltpu.roll(vec, LANES - GROUP_LANES, 1)), pltpu.roll(vec, LANES - 2 * GROUP_LANES, 1))
    lane = lax.broadcasted_iota(jnp.int32, vec.shape, 1)
    return jnp.where(lane < GROUP_LANES, t,
                     jnp.where(lane < 2 * GROUP_LANES, pltpu.roll(t, GROUP_LANES, 1), pltpu.roll(t, 2 * GROUP_LANES, 1)))


SLAB_POS = DILATED_GROUPS[-1][1]
SLAB_ROWS = SLAB_POS * N_KV_HEADS
TAIL_POS = max(w for w, d in DILATED_GROUPS if d < SLAB_POS)


def _sample_attn_kernel(kc_ref, vc_ref, kn_ref, vn_ref, wq_ref, bias_d_ref, bias_t_ref, bias_n_ref, mk_ref, mv_ref,
                        wm_ref, bias_m_ref, kw_ref, vw_ref, a_ref, m_ref, kx_ref, vx_ref):
    n_slab = kc_ref.shape[1]
    new_rows = kn_ref.shape[1]
    t = new_rows // N_KV_HEADS
    tail_slabs = TAIL_POS // SLAB_POS
    n_used = N_GROUPS * GROUP_LANES
    lane = lax.broadcasted_iota(jnp.int32, (1, LANES), 1)

    for src, new, dst in ((kc_ref, kn_ref, kw_ref), (vc_ref, vn_ref, vw_ref)):
        dst[0, :, 0:SLAB_ROWS - new_rows, :] = src[0, :, new_rows:SLAB_ROWS, :]
        dst[0, 0:n_slab - 1, SLAB_ROWS - new_rows:SLAB_ROWS, :] = src[0, 1:n_slab, 0:new_rows, :]
        dst[0, n_slab - 1, SLAB_ROWS - new_rows:SLAB_ROWS, :] = new[0]

    kx_ref[...] = jnp.zeros(kx_ref.shape, F32)
    vx_ref[...] = jnp.zeros(vx_ref.shape, F32)
    kx_ref[0:new_rows, :] = kn_ref[0]
    vx_ref[0:new_rows, :] = vn_ref[0]

    def dilated(ref):
        return ref[0, :, 0:new_rows, :].reshape(n_slab * new_rows, HEAD_DIM).astype(BF16)

    def tail(ref):
        return ref[0, n_slab - tail_slabs:n_slab, :, :].reshape(tail_slabs * SLAB_ROWS, HEAD_DIM).astype(BF16)

    scale = HEAD_DIM ** -0.5
    wq = wq_ref[0]
    s_d = _dot(dilated(kc_ref), wq) * scale + bias_d_ref[...]
    s_t = _dot(tail(kc_ref), wq) * scale + bias_t_ref[...]
    s_n = _dot(kx_ref[...].astype(BF16), wq) * scale + bias_n_ref[...]
    col_max = lambda s: jnp.max(s, axis=0, keepdims=True)
    col_sum = lambda p: jnp.sum(p, axis=0, keepdims=True)
    m_col = jnp.maximum(jnp.maximum(col_max(s_d), col_max(s_t)), col_max(s_n))
    m_joint = jnp.where(lane < n_used, _spread_groups(m_col, jnp.maximum), 0.0)
    p_d, p_t, p_n = jnp.exp(s_d - m_joint), jnp.exp(s_t - m_joint), jnp.exp(s_n - m_joint)
    l_col = col_sum(p_d) + col_sum(p_t) + col_sum(p_n)
    inv = 1.0 / jnp.where(lane < n_used, _spread_groups(l_col, jnp.add), 1.0)
    o = (_dot_tn((p_d * inv).astype(BF16), dilated(vc_ref)) + _dot_tn((p_t * inv).astype(BF16), tail(vc_ref))
         + _dot_tn((p_n * inv).astype(BF16), vx_ref[...].astype(BF16)))
    for h in range(N_KV_HEADS):
        r = h * t
        a_ref[0, :, h * HEAD_DIM:(h + 1) * HEAD_DIM] = functools.reduce(
            jnp.add, [o[g * GROUP_LANES + r:g * GROUP_LANES + r + t, :] for g in range(N_GROUPS)])

    sm = _dot(mk_ref[0].astype(BF16), wm_ref[0]) * (MEM_HEAD_DIM ** -0.5) + bias_m_ref[...]
    pm = jnp.exp(sm - jnp.where(lane < MEM_HEADS * t, col_max(sm), 0.0))
    lm = jnp.where(lane < MEM_HEADS * t, col_sum(pm), 1.0)
    om = _dot_tn((pm * (1.0 / lm)).astype(BF16), mv_ref[0].astype(BF16))
    for h in range(MEM_HEADS):
        m_ref[0, :, h * MEM_HEAD_DIM:(h + 1) * MEM_HEAD_DIM] = om[h * t:(h + 1) * t, :]


def _sample_masks(cache_len, t, n_mem):
    col = np.arange(LANES)[None, :]
    g, c_head, c_tok = col // GROUP_LANES, (col % GROUP_LANES) // t, (col % GROUP_LANES) % t
    used = (col < N_GROUPS * GROUP_LANES) & (col % GROUP_LANES < N_KV_HEADS * t)
    pad = [1] * (LANES // GROUP_LANES - N_GROUPS)
    dil = np.array([d for _, d in DILATED_GROUPS] + pad)[g]
    win = np.array([w for w, _ in DILATED_GROUPS] + pad)[g]
    sparse = dil >= SLAB_POS

    def keep(pos, head, group_sel):
        dist = cache_len + c_tok - pos
        return used & group_sel & (head == c_head) & (dist >= 0) & (dist % dil == 0) & (dist <= win)

    n_slab = cache_len // SLAB_POS
    y = np.arange(n_slab * t * N_KV_HEADS)[:, None]
    keep_d = keep((y // (t * N_KV_HEADS)) * SLAB_POS + (y % (t * N_KV_HEADS)) // N_KV_HEADS, y % N_KV_HEADS, sparse)
    x = np.arange(TAIL_POS * N_KV_HEADS)[:, None]
    keep_t = keep(cache_len - TAIL_POS + x // N_KV_HEADS, x % N_KV_HEADS, ~sparse)
    z = np.arange(LANES)[:, None]
    keep_n = keep(cache_len + z // N_KV_HEADS, z % N_KV_HEADS, True) & (z < t * N_KV_HEADS)
    w = np.arange(n_mem * MEM_HEADS)[:, None]
    keep_m = (col < MEM_HEADS * t) & (w % MEM_HEADS == col // t)
    return tuple(jnp.asarray(np.where(k, 0.0, NEG), F32) for k in (keep_d, keep_t, keep_n, keep_m))


def _query_columns(q, batch, t, n_groups):
    qt = q.reshape(batch, t, n_groups, N_KV_HEADS, HEAD_DIM).transpose(0, 4, 2, 3, 1)
    qt = qt.reshape(batch, HEAD_DIM, n_groups, N_KV_HEADS * t)
    qt = jnp.pad(qt, ((0, 0), (0, 0), (0, 0), (0, GROUP_LANES - N_KV_HEADS * t)))
    qt = qt.reshape(batch, HEAD_DIM, n_groups * GROUP_LANES)
    return jnp.pad(qt, ((0, 0), (0, 0), (0, LANES - n_groups * GROUP_LANES)))


def _sample_attention(q, k_new, v_new, mq, cache_k, cache_v, mem_k, mem_v):
    batch, cache_len = cache_k.shape[0], cache_k.shape[1]
    new_rows = k_new.shape[0] // batch
    t = new_rows // N_KV_HEADS
    n_mem = mem_k.shape[1]
    assert N_KV_HEADS * t <= GROUP_LANES and t <= SLAB_POS and new_rows % 8 == 0
    assert cache_len % SLAB_POS == 0 and cache_len >= max(w for w, _ in DILATED_GROUPS)
    assert all(d == SLAB_POS or w <= TAIL_POS for w, d in DILATED_GROUPS)
    n_slab = cache_len // SLAB_POS
    wq = _query_columns(q, batch, t, N_GROUPS)
    wm = _query_columns(mq, batch, t, 1)
    bias_d, bias_t, bias_n, bias_m = _sample_masks(cache_len, t, n_mem)
    per_b = lambda *shape: pl.BlockSpec((1,) + shape, lambda b: (b,) + (0,) * len(shape))
    slabs = lambda c: c.reshape(batch, n_slab, SLAB_ROWS, HEAD_DIM)
    mem_rows = n_mem * MEM_HEADS
    win = jax.ShapeDtypeStruct((batch, n_slab, SLAB_ROWS, HEAD_DIM), F32)
    k_win, v_win, a, m = pl.pallas_call(
        _sample_attn_kernel,
        grid=(batch,),
        in_specs=[
            per_b(n_slab, SLAB_ROWS, HEAD_DIM), per_b(n_slab, SLAB_ROWS, HEAD_DIM),
            per_b(new_rows, HEAD_DIM), per_b(new_rows, HEAD_DIM), per_b(HEAD_DIM, LANES),
            _const_spec(bias_d.shape), _const_spec(bias_t.shape), _const_spec(bias_n.shape),
            per_b(mem_rows, MEM_HEAD_DIM), per_b(mem_rows, MEM_HEAD_DIM), per_b(MEM_HEAD_DIM, LANES),
            _const_spec(bias_m.shape),
        ],
        out_specs=(per_b(n_slab, SLAB_ROWS, HEAD_DIM), per_b(n_slab, SLAB_ROWS, HEAD_DIM), per_b(t, ATTN_WIDTH),
                   per_b(t, MEM_WIDTH)),
        out_shape=(win, win, jax.ShapeDtypeStruct((batch, t, ATTN_WIDTH), F32),
                   jax.ShapeDtypeStruct((batch, t, MEM_WIDTH), F32)),
        scratch_shapes=[pltpu.VMEM((LANES, HEAD_DIM), F32), pltpu.VMEM((LANES, HEAD_DIM), F32)],
        compiler_params=_params("parallel"),
        name="sample_attn",
    )(slabs(cache_k), slabs(cache_v), k_new.reshape(batch, new_rows, HEAD_DIM), v_new.reshape(batch, new_rows, HEAD_DIM),
      wq, bias_d, bias_t, bias_n, mem_k.reshape(batch, mem_rows, MEM_HEAD_DIM),
      mem_v.reshape(batch, mem_rows, MEM_HEAD_DIM), wm, bias_m)
    return k_win.reshape(cache_k.shape), v_win.reshape(cache_v.shape), a, m


ROUTE_GROUP_LANE0 = 0
ROUTE_EXPERT_LANE0 = N_EXPERT_GROUPS


def _route(logits):
    lane = lax.broadcasted_iota(jnp.int32, logits.shape, 1).astype(F32)
    big = float(LANES)

    def masked_softmax(keep):
        z = jnp.where(keep, logits, NEG)
        e = jnp.where(keep, jnp.exp(z - jnp.max(z, axis=-1, keepdims=True)), 0.0)
        return e / jnp.sum(e, axis=-1, keepdims=True)

    def first_argmax(vals, keep):
        top = jnp.max(jnp.where(keep, vals, -1.0), axis=-1, keepdims=True)
        idx = jnp.min(jnp.where(jnp.logical_and(keep, vals == top), lane, big), axis=-1, keepdims=True)
        return top, idx

    is_group = lane < N_EXPERT_GROUPS
    pg = masked_softmax(is_group)
    pg_top, g_idx = first_argmax(pg, is_group)
    lo = ROUTE_EXPERT_LANE0 + g_idx * EXPERTS_PER_GROUP
    in_group = jnp.logical_and(lane >= lo, lane < lo + EXPERTS_PER_GROUP)
    pe = masked_softmax(in_group)
    p1, i1 = first_argmax(pe, in_group)
    p2, i2 = first_argmax(pe, jnp.logical_and(in_group, lane != i1))
    denom = p1 + p2
    w1 = pg_top * p1 / denom
    w2 = pg_top * p2 / denom
    return lane, i1 - ROUTE_EXPERT_LANE0, i2 - ROUTE_EXPERT_LANE0, w1, w2


def _merge_kernel(x_ref, gmix_ref, ao_ref, cc_ref, mo_ref, wgate_ref, wa_ref, wc_ref, wm_ref, wo_ref, gffn_ref, wr_ref,
                  br_ref, cnt_in_ref, x1_ref, h2_ref, route_ref, cnt_out_ref, cnt_ref):
    @pl.when(pl.program_id(0) == 0)
    def _():
        cnt_ref[...] = cnt_in_ref[...]

    d = x_ref.shape[1]
    x = x_ref[...]
    h = _rms(x, gmix_ref[...]).astype(BF16)
    a = _dot(ao_ref[...].astype(BF16), wa_ref[...])
    c = _dot(cc_ref[...].astype(BF16), wc_ref[...])
    m = _dot(mo_ref[...].astype(BF16), wm_ref[...])
    z = jax.nn.sigmoid(_dot(h, wgate_ref[:, 0:d])) * a
    z = z + jax.nn.sigmoid(_dot(h, wgate_ref[:, d:2 * d])) * c
    z = z + jax.nn.sigmoid(_dot(h, wgate_ref[:, 2 * d:3 * d])) * m
    x1 = x + _dot(z.astype(BF16), wo_ref[...])
    x1_ref[...] = x1
    h2 = _rms(x1, gffn_ref[...]).astype(BF16)
    packed = _pack_halves(h2)
    for c in range(packed.shape[1] // LANES):
        h2_ref[pl.ds(c, x_ref.shape[0], stride=packed.shape[1] // LANES), :] = packed[:, c * LANES:(c + 1) * LANES]
    logits = _dot(h2, wr_ref[...]) + br_ref[...]
    lane, e1, e2, w1, w2 = _route(logits)

    tm = x_ref.shape[0]
    hit1 = jnp.where(lane == e1, 1.0, 0.0)
    hit2 = jnp.where(lane == e2, 1.0, 0.0)
    hits = hit1 + hit2
    earlier = (lax.broadcasted_iota(jnp.int32, (tm, tm), 1) < lax.broadcasted_iota(jnp.int32, (tm, tm), 0))
    before = _dot(jnp.where(earlier, 1.0, 0.0).astype(BF16), hits.astype(BF16)) + cnt_ref[...]
    rank1 = jnp.sum(hit1 * before, axis=-1, keepdims=True)
    rank2 = jnp.sum(hit2 * before, axis=-1, keepdims=True)
    cnt_ref[...] = cnt_ref[...] + jnp.sum(hits, axis=0, keepdims=True)
    cnt_out_ref[...] = cnt_ref[...]
    route = jnp.zeros(logits.shape, F32)
    for i, val in enumerate((e1, e2, w1, w2, rank1, rank2)):
        route = jnp.where(lane == i, val, route)
    route_ref[...] = route


ROUTE_E, ROUTE_W, ROUTE_RANK = 0, 2, 4


def _merge(x, g_mix, ao, cc, mo, w_gate, w_a, w_c, w_m, w_o, g_ffn, w_r, b_r, counts):
    m, d = x.shape
    tm = min(MERGE_TILE, m)
    row = lambda i: (i, 0)
    rows = lambda arr: pl.BlockSpec((tm, arr.shape[1]), row)
    ins = [x, g_mix, ao, cc, mo, w_gate, w_a, w_c, w_m, w_o, g_ffn, w_r, b_r, counts]
    specs = ([rows(x), _const_spec(g_mix.shape), rows(ao), rows(cc), rows(mo)]
             + [_const_spec(a.shape) for a in (w_gate, w_a, w_c, w_m, w_o, g_ffn, w_r, b_r, counts)])
    out_shape = (jax.ShapeDtypeStruct((m, d), F32), jax.ShapeDtypeStruct((m * (d // 2 // LANES), LANES), jnp.uint32),
                 jax.ShapeDtypeStruct((m, LANES), F32), jax.ShapeDtypeStruct((1, LANES), F32))
    return pl.pallas_call(
        _merge_kernel,
        grid=(m // tm,),
        in_specs=specs,
        out_specs=(rows(out_shape[0]), pl.BlockSpec((tm * (d // 2 // LANES), LANES), row), rows(out_shape[2]),
                   _const_spec((1, LANES))),
        out_shape=out_shape,
        scratch_shapes=[pltpu.VMEM((1, LANES), F32)],
        compiler_params=_params("arbitrary"),
        name="merge",
    )(*ins)


def _routing_tables(counts, routes):
    cnt = counts[0, :N_EXPERTS].astype(jnp.int32)
    tiles = (cnt + MOE_TILE - 1) // MOE_TILE
    tile_start = jnp.cumsum(tiles) - tiles
    row_start = tile_start * MOE_TILE
    ids = routes[:, ROUTE_E:ROUTE_E + TOP_K].astype(jnp.int32)
    start_of = jnp.sum(jnp.where(ids[:, :, None] == jnp.arange(N_EXPERTS)[None, None, :], row_start[None, None, :], 0),
                       axis=-1)
    pos = start_of + routes[:, ROUTE_RANK:ROUTE_RANK + TOP_K].astype(jnp.int32)
    return pos.reshape(-1), tile_start.astype(jnp.int32), tiles.astype(jnp.int32)


def _dispatch_kernel(pos_ref, h_ref, xs_in, xs_out, sem, *, row0):
    del xs_in
    tm = h_ref.shape[0]
    base = (row0 + pl.program_id(0) * tm) * TOP_K

    def issue(j, carry):
        for k in range(TOP_K):
            slot = pos_ref[base + j * TOP_K + k]
            pltpu.make_async_copy(h_ref.at[j], xs_out.at[slot], sem).start()
        return carry

    lax.fori_loop(0, tm, issue, 0, unroll=8)
    for k in range(TOP_K):
        pltpu.make_async_copy(h_ref, xs_out.at[pl.ds(0, tm)], sem).wait()


def _dispatch(h2, pos, xs, row0):
    m = h2.shape[0]
    tm = min(ROW_TILE, m)
    return pl.pallas_call(
        functools.partial(_dispatch_kernel, row0=row0),
        grid_spec=pltpu.PrefetchScalarGridSpec(
            num_scalar_prefetch=1,
            grid=(m // tm,),
            in_specs=[pl.BlockSpec((tm,) + h2.shape[1:], lambda i, *_: (i, 0, 0)), pl.BlockSpec(memory_space=pl.ANY)],
            out_specs=pl.BlockSpec(memory_space=pl.ANY),
            scratch_shapes=[pltpu.SemaphoreType.DMA(())],
        ),
        out_shape=jax.ShapeDtypeStruct(xs.shape, xs.dtype),
        input_output_aliases={2: 0},
        compiler_params=_params("arbitrary"),
        name="moe_dispatch",
    )(pos, h2, xs)


def _gmm_kernel(start_ref, count_ref, xs_hbm, wg_ref, wu_ref, wd_ref, ys_hbm, xbuf, obuf, wg_s, wu_s, wd_s,
                in_sems, out_sems):
    e = pl.program_id(0)
    nb = GMM_BUFS
    first, n = start_ref[e], count_ref[e]

    x_rows, y_rows = xbuf.shape[1], obuf.shape[1]
    x_tiles, y_tiles = x_rows // MOE_TILE, y_rows // MOE_TILE

    def copies(expert_first):
        def rows(i, n_rows):
            return pl.ds(pl.multiple_of((expert_first + i) * n_rows, n_rows), n_rows)

        def in_copy(i):
            return pltpu.make_async_copy(xs_hbm.at[rows(i, x_rows)], xbuf.at[i % nb], in_sems.at[i % nb])

        def out_copy(i):
            return pltpu.make_async_copy(obuf.at[i % nb], ys_hbm.at[rows(i, y_rows)], out_sems.at[i % nb])

        return in_copy, out_copy

    in_copy, out_copy = copies(first)

    def start_first_reads(expert):
        copy, _ = copies(start_ref[expert])
        lax.fori_loop(0, jnp.minimum(count_ref[expert], nb - 1), lambda i, c: (copy(i).start(), c)[1], 0)

    @pl.when(e == 0)
    def _():
        start_first_reads(0)

    @pl.when(n > 0)
    def _():
        wg_s[...] = wg_ref[0].astype(BF16)
        wu_s[...] = wu_ref[0].astype(BF16)
        wd_s[...] = wd_ref[0].astype(BF16)
        half = x_tiles * LANES

        def tile(i, carry):
            slot = i % nb
            in_copy(i).wait()

            @pl.when(i + nb - 1 < n)
            def _():
                in_copy(i + nb - 1).start()

            @pl.when(i >= nb)
            def _():
                out_copy(i - nb).wait()

            packed = jnp.concatenate([xbuf[slot, pl.ds(c, MOE_TILE, stride=x_tiles), :] for c in range(x_tiles)], axis=1)
            x_hi, x_lo = _unpack_halves(packed)
            gate = _dot(x_hi, wg_s[0:half, :]) + _dot(x_lo, wg_s[half:, :])
            up = _dot(x_hi, wu_s[0:half, :]) + _dot(x_lo, wu_s[half:, :])
            hid = gate * jax.nn.sigmoid(gate) * up
            res = _dot(hid.astype(BF16), wd_s[...])
            for c in range(y_tiles):
                obuf[slot, pl.ds(c, MOE_TILE, stride=y_tiles), :] = res[:, c * LANES:(c + 1) * LANES]
            out_copy(i).start()
            return carry

        lax.fori_loop(0, n, tile, 0)
        lax.fori_loop(jnp.maximum(n - nb, 0), n, lambda i, c: (out_copy(i).wait(), c)[1], 0)

    @pl.when(e + 1 < pl.num_programs(0))
    def _():
        start_first_reads(e + 1)

    @pl.when(e == pl.num_programs(0) - 1)
    def _():
        obuf[0] = jnp.zeros(obuf.shape[1:], F32)

        def fill(t, carry):
            dst = ys_hbm.at[pl.ds(pl.multiple_of(t * y_rows, y_rows), y_rows)]
            copy = pltpu.make_async_copy(obuf.at[0], dst, out_sems.at[0])
            copy.start()
            copy.wait()
            return carry

        lax.fori_loop(first + n, ys_hbm.shape[0] // y_rows, fill, 0)


def _grouped_mlp(xs, tile_start, tile_count, w_gate, w_up, w_down):
    d, ff = w_gate.shape[1], w_gate.shape[2]
    x_tiles, y_tiles = d // 2 // LANES, d // LANES
    n_slots = xs.shape[0] // x_tiles
    wspec = lambda a, b: pl.BlockSpec((1, a, b), lambda e, *_: (e, 0, 0))
    any_spec = pl.BlockSpec(memory_space=pl.ANY)
    return pl.pallas_call(
        _gmm_kernel,
        grid_spec=pltpu.PrefetchScalarGridSpec(
            num_scalar_prefetch=2,
            grid=(N_EXPERTS,),
            in_specs=[any_spec, wspec(d, ff), wspec(d, ff), wspec(ff, d)],
            out_specs=any_spec,
            scratch_shapes=[pltpu.VMEM((GMM_BUFS, MOE_TILE * x_tiles, LANES), xs.dtype),
                            pltpu.VMEM((GMM_BUFS, MOE_TILE * y_tiles, LANES), F32),
                            pltpu.VMEM((d, ff), BF16), pltpu.VMEM((d, ff), BF16), pltpu.VMEM((ff, d), BF16),
                            pltpu.SemaphoreType.DMA((GMM_BUFS,)), pltpu.SemaphoreType.DMA((GMM_BUFS,))],
        ),
        out_shape=jax.ShapeDtypeStruct((n_slots * y_tiles, LANES), F32),
        compiler_params=_params("arbitrary"),
        name="moe_gmm",
    )(tile_start, tile_count, xs, w_gate, w_up, w_down)


def _combine_kernel(pos_ref, x1_ref, route_ref, ys_hbm, ys_flat, y_ref, buf_ref, sems, *, row0):
    tm = x1_ref.shape[0]
    y_tiles = ys_hbm.shape[1]
    i = pl.program_id(0)

    def gather(tile, half):
        base = (row0 + tile * tm) * TOP_K

        def issue(j, carry):
            for k in range(TOP_K):
                slot = pos_ref[base + j * TOP_K + k]
                dst = buf_ref.at[half, k, pl.ds(pl.multiple_of(j * y_tiles, y_tiles), y_tiles)]
                pltpu.make_async_copy(ys_hbm.at[slot], dst, sems.at[half]).start()
            return carry

        lax.fori_loop(0, tm, issue, 0, unroll=4)

    @pl.when(i == 0)
    def _():
        gather(0, 0)

    @pl.when(i + 1 < pl.num_programs(0))
    def _():
        gather(i + 1, (i + 1) % 2)

    half = i % 2
    for k in range(TOP_K):
        pltpu.make_async_copy(ys_flat.at[pl.ds(0, tm * y_tiles)], buf_ref.at[half, k], sems.at[half]).wait()
    route = route_ref[...]
    for c in range(y_tiles):
        sl = slice(c * LANES, (c + 1) * LANES)
        rows = pl.ds(c, tm, stride=y_tiles)
        y_ref[:, sl] = (x1_ref[:, sl] + route[:, ROUTE_W:ROUTE_W + 1] * buf_ref[half, 0, rows, :]
                        + route[:, ROUTE_W + 1:ROUTE_W + 2] * buf_ref[half, 1, rows, :])


def _combine(x1, route, pos, ys, row0):
    m, d = x1.shape
    tm = min(ROW_TILE, m)
    row = lambda i, *_: (i, 0)
    any_spec = pl.BlockSpec(memory_space=pl.ANY)
    return pl.pallas_call(
        functools.partial(_combine_kernel, row0=row0),
        grid_spec=pltpu.PrefetchScalarGridSpec(
            num_scalar_prefetch=1,
            grid=(m // tm,),
            in_specs=[pl.BlockSpec((tm, d), row), pl.BlockSpec((tm, LANES), row), any_spec, any_spec],
            out_specs=pl.BlockSpec((tm, d), row),
            scratch_shapes=[pltpu.VMEM((2, TOP_K, tm * (d // LANES), LANES), F32), pltpu.SemaphoreType.DMA((2,))],
        ),
        out_shape=jax.ShapeDtypeStruct((m, d), F32),
        compiler_params=_params("arbitrary"),
        name="moe_combine",
    )(pos, x1, route, ys.reshape(-1, d // LANES, LANES), ys)


def _layer(layer, x_prompt, x_sample, mem_prompt, cache_k, cache_v, state_conv, cache_mem_k, cache_mem_v, p):
    batch, seq, d = x_prompt.shape
    dec_batch, dec_seq, _ = x_sample.shape
    conv_ch = p["conv_w"].shape[-1]
    n_in = N_Q_HEADS * HEAD_DIM + 2 * ATTN_WIDTH + 2 * conv_ch + MEM_WIDTH
    past_len = cache_k.shape[2]

    row2 = lambda name: p[name][layer][None, :]
    w_in = p["w_in"][layer]
    w_main = w_in[:, :n_in].astype(BF16)
    w_gate = w_in[:, n_in:].astype(BF16)
    w_a, w_c, w_m, w_o = (p[n][layer].astype(BF16) for n in ("w_attn_proj", "w_conv_proj", "w_mem_proj", "w_out"))
    w_router = jnp.concatenate(
        [p["w_router_group"][layer], p["w_router_expert"][layer].transpose(1, 0, 2).reshape(d, N_EXPERTS)], axis=1)
    w_router = jnp.pad(w_router, ((0, 0), (0, LANES - w_router.shape[1]))).astype(BF16)
    b_router = jnp.concatenate([p["b_router_group"][layer], p["b_router_expert"][layer].reshape(-1)])
    b_router = jnp.pad(b_router, (0, LANES - b_router.shape[0]))[None, :]
    conv_args = (p["conv_w"][layer], row2("conv_b"), row2("conv_ln_g"), row2("conv_ln_b"))
    merge_w = (w_gate, w_a, w_c, w_m, w_o, row2("norm_ffn_g"), w_router, b_router)

    xp = x_prompt.reshape(batch * seq, d)
    tabs_p = _rope_tables(jnp.arange(seq, dtype=jnp.int32))
    q_p, k_p, v_p, u_p, mq_p = _in_proj(xp, row2("norm_mix_g"), w_main, row2("q_norm_g"), row2("k_norm_g"),
                                        row2("mq_norm_g"), tabs_p, conv_ch)
    ao_p = _prompt_attention(q_p, k_p, v_p, batch, seq)
    mem_k_p, mem_v_p = _mem_kv(mem_prompt.reshape(-1, d), row2("mem_norm_g"), p["w_mem_kv"][layer].astype(BF16),
                               row2("mk_norm_g"))
    mo_p = _mem_attention(mq_p, mem_k_p, mem_v_p, batch, seq)
    cc_p = _conv_prompt(u_p, *conv_args, batch, seq)
    x1_p, h2_p, route_p, counts = _merge(xp, row2("norm_mix_g"), ao_p, cc_p, mo_p, *merge_w,
                                         jnp.zeros((1, LANES), F32))

    xs = x_sample.reshape(dec_batch * dec_seq, d)
    tabs_s = _rope_tables(jnp.tile(past_len + jnp.arange(dec_seq, dtype=jnp.int32), dec_batch))
    q_s, k_s, v_s, u_s, mq_s = _in_proj(xs, row2("norm_mix_g"), w_main, row2("q_norm_g"), row2("k_norm_g"),
                                        row2("mq_norm_g"), tabs_s, conv_ch)
    k_win_s, v_win_s, a_s, mo_s = _sample_attention(q_s, k_s, v_s, mq_s, cache_k[layer], cache_v[layer],
                                                    cache_mem_k[layer], cache_mem_v[layer])
    cc_s, conv_state_s = _conv_sample(state_conv[layer], u_s.reshape(dec_batch, dec_seq, conv_ch), *conv_args)
    x1_s, h2_s, route_s, counts = _merge(xs, row2("norm_mix_g"), a_s.reshape(-1, ATTN_WIDTH),
                                         cc_s.reshape(-1, conv_ch), mo_s.reshape(-1, MEM_WIDTH), *merge_w, counts)

    n_p, n_s = xp.shape[0], xs.shape[0]
    n_tok = n_p + n_s
    n_tiles = (TOP_K * n_tok + N_EXPERTS * (MOE_TILE - 1)) // MOE_TILE + 1
    pos, tile_start, tile_count = _routing_tables(counts, jnp.concatenate([route_p, route_s], axis=0))
    per_token = lambda h: h.reshape(-1, d // 2 // LANES, LANES)
    slots = _dispatch(per_token(h2_p), pos, jnp.zeros((n_tiles * MOE_TILE, d // 2 // LANES, LANES), h2_p.dtype), 0)
    slots = _dispatch(per_token(h2_s), pos, slots, n_p)
    ys = _grouped_mlp(slots.reshape(-1, LANES), tile_start, tile_count, p["w_expert_gate"][layer],
                      p["w_expert_up"][layer], p["w_expert_down"][layer])
    y_p = _combine(x1_p, route_p, pos, ys, 0)
    y_s = _combine(x1_s, route_s, pos, ys, n_p)

    state_p = (k_p.reshape(batch, seq, N_KV_HEADS, HEAD_DIM), v_p.reshape(batch, seq, N_KV_HEADS, HEAD_DIM),
               u_p.reshape(batch, seq, conv_ch)[:, seq - (CONV_WIDTH - 1):],
               mem_k_p.reshape(batch, -1, MEM_HEADS, MEM_HEAD_DIM), mem_v_p.reshape(batch, -1, MEM_HEADS, MEM_HEAD_DIM))
    state_s = (k_win_s, v_win_s, conv_state_s)
    return y_p.reshape(batch, seq, d), y_s.reshape(dec_batch, dec_seq, d), state_p, state_s


def kernel(x_prompt, x_sample, mem_prompt, cache_k, cache_v, state_conv, cache_mem_k, cache_mem_v, norm_mix_g, w_in, q_norm_g, k_norm_g, conv_w, conv_b, conv_ln_g, conv_ln_b, mem_norm_g, w_mem_kv, mq_norm_g, mk_norm_g, w_attn_proj, w_conv_proj, w_mem_proj, w_out, norm_ffn_g, w_router_group, b_router_group, w_router_expert, b_router_expert, w_expert_gate, w_expert_up, w_expert_down):
    p = dict(norm_mix_g=norm_mix_g, w_in=w_in, q_norm_g=q_norm_g, k_norm_g=k_norm_g, conv_w=conv_w, conv_b=conv_b,
             conv_ln_g=conv_ln_g, conv_ln_b=conv_ln_b, mem_norm_g=mem_norm_g, w_mem_kv=w_mem_kv, mq_norm_g=mq_norm_g,
             mk_norm_g=mk_norm_g, w_attn_proj=w_attn_proj, w_conv_proj=w_conv_proj, w_mem_proj=w_mem_proj,
             w_out=w_out, norm_ffn_g=norm_ffn_g, w_router_group=w_router_group, b_router_group=b_router_group,
             w_router_expert=w_router_expert, b_router_expert=b_router_expert, w_expert_gate=w_expert_gate,
             w_expert_up=w_expert_up, w_expert_down=w_expert_down)
    depth = w_in.shape[0]
    seq = x_prompt.shape[1]
    assert seq <= max(w for w, _ in DILATED_GROUPS)
    y_p, y_s = x_prompt, x_sample
    states_p, states_s = [], []
    for layer in range(depth):
        y_p, y_s, st_p, st_s = _layer(layer, y_p, y_s, mem_prompt, cache_k, cache_v, state_conv, cache_mem_k,
                                      cache_mem_v, p)
        states_p.append(st_p)
        states_s.append(st_s)
    stack = lambda states, i: jnp.stack([s[i] for s in states], axis=0)
    return (y_p, y_s, stack(states_p, 0), stack(states_p, 1), stack(states_p, 2), stack(states_p, 3),
            stack(states_p, 4), stack(states_s, 0), stack(states_s, 1), stack(states_s, 2))
```

```python
import functools

import jax
import jax.numpy as jnp
import numpy as np
from jax import lax
from jax.experimental import pallas as pl
from jax.experimental.pallas import tpu as pltpu

HEAD_DIM = 128
N_KV_HEADS = 4
DILATED_GROUPS = ((128, 1), (512, 4), (2048, 16))
N_GROUPS = len(DILATED_GROUPS)
N_Q_HEADS = N_GROUPS * N_KV_HEADS
ATTN_WIDTH = N_KV_HEADS * HEAD_DIM
BAND = 128
ATTN_UNROLL = 8
ROPE_THETA = 500000.0
ROT_DIM = HEAD_DIM // 4
CONV_WIDTH = 31
MEM_HEADS = 4
MEM_HEAD_DIM = 128
MEM_WIDTH = MEM_HEADS * MEM_HEAD_DIM
N_EXPERT_GROUPS = 4
EXPERTS_PER_GROUP = 8
N_EXPERTS = N_EXPERT_GROUPS * EXPERTS_PER_GROUP
TOP_K = 2
EPS = 1e-6
NEG = -1e30

LANES = 128
ROW_TILE = 256
WIDE_ROW_TILE = 512
MERGE_TILE = 512
IN_PROJ_TILE = 256
MOE_TILE = 256
GMM_BUFS = 4
VMEM_LIMIT = 56 * 1024 * 1024

BF16 = jnp.bfloat16
F32 = jnp.float32


def _params(*sem):
    return pltpu.CompilerParams(dimension_semantics=sem, vmem_limit_bytes=VMEM_LIMIT)


def _dot(a, b):
    return jnp.dot(a, b, preferred_element_type=F32)


def _dot_nt(a, b):
    return lax.dot_general(a, b, (((1,), (1,)), ((), ())), preferred_element_type=F32)


def _dot_tn(a, b):
    return lax.dot_general(a, b, (((0,), (0,)), ((), ())), preferred_element_type=F32)


def _rms(x, g):
    return x * lax.rsqrt(jnp.mean(x * x, axis=-1, keepdims=True) + EPS) * g


def _pack_halves(x):
    c = x.shape[1] // 2
    hi = lax.bitcast_convert_type(x[:, :c].astype(F32), jnp.uint32)
    lo = lax.bitcast_convert_type(x[:, c:].astype(F32), jnp.uint32)
    return hi | (lo >> 16)


def _unpack_halves(p):
    hi = lax.bitcast_convert_type(p & jnp.uint32(0xFFFF0000), F32).astype(BF16)
    lo = lax.bitcast_convert_type(p << 16, F32).astype(BF16)
    return hi, lo


def _const_spec(shape):
    nd = len(shape)
    return pl.BlockSpec(shape, lambda *_: (0,) * nd)


def _in_proj_kernel(x_ref, g_ref, w_ref, qg_ref, kg_ref, mqg_ref, rc_ref, ra_ref, rb_ref,
                    q_ref, k_ref, v_ref, u_ref, mq_ref):
    h = _rms(x_ref[...], g_ref[...]).astype(BF16)
    rc, ra, rb = rc_ref[...], ra_ref[...], rb_ref[...]

    def rope(y):
        return y * rc + pltpu.roll(y, LANES - ROT_DIM // 2, 1) * ra + pltpu.roll(y, ROT_DIM // 2, 1) * rb

    col = 0
    zq = _dot(h, w_ref[:, col:col + N_Q_HEADS * HEAD_DIM])
    for j in range(N_Q_HEADS):
        sl = slice(j * HEAD_DIM, (j + 1) * HEAD_DIM)
        q_ref[:, sl] = rope(_rms(zq[:, sl], qg_ref[...])).astype(BF16)
    col += N_Q_HEADS * HEAD_DIM
    tm = x_ref.shape[0]
    zk = _dot(h, w_ref[:, col:col + ATTN_WIDTH])
    for j in range(N_KV_HEADS):
        sl = slice(j * HEAD_DIM, (j + 1) * HEAD_DIM)
        k_ref[pl.ds(j, tm, stride=N_KV_HEADS), :] = rope(_rms(zk[:, sl], kg_ref[...]))
    col += ATTN_WIDTH
    zv = _dot(h, w_ref[:, col:col + ATTN_WIDTH])
    for j in range(N_KV_HEADS):
        v_ref[pl.ds(j, tm, stride=N_KV_HEADS), :] = zv[:, j * HEAD_DIM:(j + 1) * HEAD_DIM]
    col += ATTN_WIDTH
    conv_ch = u_ref.shape[-1]
    za = _dot(h, w_ref[:, col:col + conv_ch])
    zb = _dot(h, w_ref[:, col + conv_ch:col + 2 * conv_ch])
    u_ref[...] = za * jax.nn.sigmoid(zb)
    col += 2 * conv_ch
    zm = _dot(h, w_ref[:, col:col + MEM_WIDTH])
    for j in range(MEM_HEADS):
        sl = slice(j * MEM_HEAD_DIM, (j + 1) * MEM_HEAD_DIM)
        mq_ref[:, sl] = _rms(zm[:, sl], mqg_ref[...]).astype(BF16)


def _in_proj(x, g_mix, w_bf16, q_g, k_g, mq_g, rope_tabs, conv_ch):
    m, d = x.shape
    tm = min(IN_PROJ_TILE, m)
    n_tab_blocks = rope_tabs[0].shape[0] // tm
    row = lambda i: (i, 0)
    tab = lambda i: (i % n_tab_blocks, 0)
    ncols = w_bf16.shape[1]
    out_shape = (
        jax.ShapeDtypeStruct((m, N_Q_HEADS * HEAD_DIM), BF16),
        jax.ShapeDtypeStruct((m * N_KV_HEADS, HEAD_DIM), F32),
        jax.ShapeDtypeStruct((m * N_KV_HEADS, HEAD_DIM), F32),
        jax.ShapeDtypeStruct((m, conv_ch), F32),
        jax.ShapeDtypeStruct((m, MEM_WIDTH), BF16),
    )
    return pl.pallas_call(
        _in_proj_kernel,
        grid=(m // tm,),
        in_specs=[
            pl.BlockSpec((tm, d), row),
            _const_spec((1, d)),
            _const_spec((d, ncols)),
            _const_spec((1, HEAD_DIM)), _const_spec((1, HEAD_DIM)), _const_spec((1, MEM_HEAD_DIM)),
            pl.BlockSpec((tm, LANES), tab), pl.BlockSpec((tm, LANES), tab), pl.BlockSpec((tm, LANES), tab),
        ],
        out_specs=tuple(pl.BlockSpec((tm * s.shape[0] // m, s.shape[1]), row) for s in out_shape),
        out_shape=out_shape,
        compiler_params=_params("parallel"),
        name="in_proj",
    )(x, g_mix, w_bf16, q_g, k_g, mq_g, *rope_tabs)


def _rope_tables(pos):
    half = ROT_DIM // 2
    inv_freq = jnp.power(jnp.float32(ROPE_THETA), -jnp.arange(half, dtype=F32) * (2.0 / ROT_DIM))
    ang = pos.astype(F32)[:, None] * inv_freq[None, :]
    cos, sin = jnp.cos(ang), jnp.sin(ang)
    n = pos.shape[0]
    ones = jnp.ones((n, LANES - ROT_DIM), F32)
    zeros = jnp.zeros((n, LANES - half), F32)
    rc = jnp.concatenate([cos, cos, ones], axis=1)
    ra = jnp.concatenate([-sin, zeros], axis=1)
    rb = jnp.concatenate([jnp.zeros((n, half), F32), sin, jnp.zeros((n, LANES - ROT_DIM), F32)], axis=1)
    return rc, ra, rb


def _to_residue_layout(dst, src, classes_src, ratio):
    len_src = src.shape[0] // classes_src
    len_dst = len_src // ratio
    for c_src in range(classes_src):
        for a in range(ratio):
            c = c_src + classes_src * a
            dst[c * len_dst:(c + 1) * len_dst, :] = src[pl.ds(c_src * len_src + a, len_dst, stride=ratio), :]


def _prompt_attn_kernel(q0_ref, q1_ref, q2_ref, k_ref, v_ref, o_ref, perm_ref, tmp_ref, acc_ref, lse_ref):
    head = pl.program_id(1)
    seq = q0_ref.shape[0]
    q_refs = (q0_ref, q1_ref, q2_ref)
    dils = [d for _, d in DILATED_GROUPS]
    for t, ref in ((1, k_ref), (2, v_ref)):
        perm_ref[0, t] = ref[pl.ds(head, seq, stride=N_KV_HEADS), :]
        for g in range(1, N_GROUPS):
            _to_residue_layout(perm_ref.at[g, t], perm_ref.at[g - 1, t], dils[g - 1], dils[g] // dils[g - 1])
    for g in range(1, N_GROUPS):
        tmp_ref[0] = q_refs[g][...].astype(F32)
        for step in range(1, g + 1):
            dst = perm_ref.at[g, 0] if step == g else tmp_ref.at[step % 2]
            _to_residue_layout(dst, tmp_ref.at[(step - 1) % 2], dils[step - 1], dils[step] // dils[step - 1])
    iq = lax.broadcasted_iota(jnp.int32, (BAND, BAND), 0)
    ik = lax.broadcasted_iota(jnp.int32, (BAND, BAND), 1)
    keep_c = iq >= ik
    scale = HEAD_DIM ** -0.5

    for g, dil in enumerate(dils):
        nb = seq // (dil * BAND)

        def body(jj, carry, g=g, dil=dil, nb=nb):
            blocks = []
            for u in range(ATTN_UNROLL):
                j = jj * ATTN_UNROLL + u
                rows = pl.ds(pl.multiple_of(j * BAND, BAND), BAND)
                prev = pl.ds(pl.multiple_of(jnp.maximum(j - 1, 0) * BAND, BAND), BAND)
                q = q0_ref[rows, :] if g == 0 else perm_ref[g, 0, rows, :].astype(BF16)
                s_c = jnp.where(keep_c, _dot_nt(q, perm_ref[g, 1, rows, :].astype(BF16)) * scale, NEG)
                s_p = None
                if nb > 1:
                    keep_p = jnp.logical_and(ik >= iq, j % nb > 0)
                    s_p = jnp.where(keep_p, _dot_nt(q, perm_ref[g, 1, prev, :].astype(BF16)) * scale, NEG)
                blocks.append((j, rows, prev, s_c, s_p))
            probs = []
            for j, rows, prev, s_c, s_p in blocks:
                m = jnp.max(s_c if nb == 1 else jnp.maximum(s_c, s_p), axis=-1, keepdims=True)
                p_c = jnp.exp(s_c - m)
                p_p = None if nb == 1 else jnp.exp(s_p - m)
                l = jnp.sum(p_c if nb == 1 else p_c + p_p, axis=-1, keepdims=True)
                probs.append((m, l, p_c.astype(BF16), None if p_p is None else p_p.astype(BF16)))
            for (j, rows, prev, _, _), (m, l, p_c, p_p) in zip(blocks, probs):
                acc = _dot(p_c, perm_ref[g, 2, rows, :].astype(BF16))
                if nb > 1:
                    acc = acc + _dot(p_p, perm_ref[g, 2, prev, :].astype(BF16))
                out_rows = rows if dil == 1 else pl.ds((j % nb) * (BAND * dil) + j // nb, BAND, stride=dil)
                acc_ref[g, out_rows, :] = acc * (1.0 / l)
                lse_ref[g, out_rows, :] = jnp.broadcast_to(m + jnp.log(l), (BAND, LANES))
            return carry

        assert (dil * nb) % ATTN_UNROLL == 0
        lax.fori_loop(0, dil * nb // ATTN_UNROLL, body, 0)

    def combine(c, carry):
        rows = pl.ds(pl.multiple_of(c * BAND, BAND), BAND)
        lses = [lse_ref[g, rows, :] for g in range(N_GROUPS)]
        mx = functools.reduce(jnp.maximum, lses)
        ws = [jnp.exp(l - mx) for l in lses]
        out = functools.reduce(jnp.add, [w * acc_ref[g, rows, :] for g, w in enumerate(ws)])
        o_ref[rows, :] = (out * (1.0 / functools.reduce(jnp.add, ws))).astype(BF16)
        return carry

    lax.fori_loop(0, seq // BAND, combine, 0)


def _prompt_attention(q, k_flat, v_flat, batch, seq):
    for window, dil in DILATED_GROUPS:
        assert window // dil == BAND and seq % (dil * BAND) == 0
    qspec = lambda g: pl.BlockSpec((seq, HEAD_DIM), lambda b, h: (b, g * N_KV_HEADS + h))
    kvspec = pl.BlockSpec((seq * N_KV_HEADS, HEAD_DIM), lambda b, h: (b, 0))
    return pl.pallas_call(
        _prompt_attn_kernel,
        grid=(batch, N_KV_HEADS),
        in_specs=[qspec(0), qspec(1), qspec(2), kvspec, kvspec],
        out_specs=pl.BlockSpec((seq, HEAD_DIM), lambda b, h: (b, h)),
        out_shape=jax.ShapeDtypeStruct((batch * seq, ATTN_WIDTH), BF16),
        scratch_shapes=[pltpu.VMEM((N_GROUPS, 3, seq, HEAD_DIM), F32), pltpu.VMEM((2, seq, HEAD_DIM), F32),
                        pltpu.VMEM((N_GROUPS, seq, HEAD_DIM), F32), pltpu.VMEM((N_GROUPS, seq, LANES), F32)],
        compiler_params=_params("parallel", "arbitrary"),
        name="prompt_attn",
    )(q, q, q, k_flat, v_flat)


def _mem_kv_kernel(x_ref, g_ref, w_ref, kg_ref, k_ref, v_ref):
    tm = x_ref.shape[0]
    h = _rms(x_ref[...], g_ref[...]).astype(BF16)
    zk = _dot(h, w_ref[:, :MEM_WIDTH])
    zv = _dot(h, w_ref[:, MEM_WIDTH:])
    for j in range(MEM_HEADS):
        sl = slice(j * MEM_HEAD_DIM, (j + 1) * MEM_HEAD_DIM)
        k_ref[pl.ds(j, tm, stride=MEM_HEADS), :] = _rms(zk[:, sl], kg_ref[...])
        v_ref[pl.ds(j, tm, stride=MEM_HEADS), :] = zv[:, sl]


def _mem_kv(mem, g, w_bf16, k_g):
    m, d = mem.shape
    tm = min(ROW_TILE, m)
    row = lambda i: (i, 0)
    shp = jax.ShapeDtypeStruct((m * MEM_HEADS, MEM_HEAD_DIM), F32)
    ospec = pl.BlockSpec((tm * MEM_HEADS, MEM_HEAD_DIM), row)
    return pl.pallas_call(
        _mem_kv_kernel,
        grid=(m // tm,),
        in_specs=[pl.BlockSpec((tm, d), row), _const_spec((1, d)), _const_spec((d, 2 * MEM_WIDTH)),
                  _const_spec((1, MEM_HEAD_DIM))],
        out_specs=(ospec, ospec),
        out_shape=(shp, shp),
        compiler_params=_params("parallel"),
        name="mem_kv",
    )(mem, g, w_bf16, k_g)


def _mem_attn_kernel(q_ref, k_ref, v_ref, o_ref):
    scale = MEM_HEAD_DIM ** -0.5
    n_mem = k_ref.shape[0] // MEM_HEADS
    for h in range(MEM_HEADS):
        sl = slice(h * MEM_HEAD_DIM, (h + 1) * MEM_HEAD_DIM)
        head_rows = pl.ds(h, n_mem, stride=MEM_HEADS)
        s = _dot_nt(q_ref[:, sl], k_ref[head_rows, :].astype(BF16)) * scale
        p = jnp.exp(s - jnp.max(s, axis=-1, keepdims=True))
        l = jnp.sum(p, axis=-1, keepdims=True)
        o_ref[:, sl] = (_dot(p.astype(BF16), v_ref[head_rows, :].astype(BF16)) * (1.0 / l)).astype(BF16)


def _mem_attention(mq, mem_k_flat, mem_v_flat, batch, seq):
    rows = mem_k_flat.shape[0] // batch
    tq = min(512, seq)
    nq = seq // tq
    kspec = pl.BlockSpec((rows, MEM_HEAD_DIM), lambda b, i: (b, 0))
    return pl.pallas_call(
        _mem_attn_kernel,
        grid=(batch, nq),
        in_specs=[pl.BlockSpec((tq, MEM_WIDTH), lambda b, i: (b * nq + i, 0)), kspec, kspec],
        out_specs=pl.BlockSpec((tq, MEM_WIDTH), lambda b, i: (b * nq + i, 0)),
        out_shape=jax.ShapeDtypeStruct((batch * seq, MEM_WIDTH), BF16),
        compiler_params=_params("parallel", "parallel"),
        name="mem_attn",
    )(mq, mem_k_flat, mem_v_flat)


CONV_HALO = 32
CONV_CHUNK = 32
CONV_ACC_ROWS = 128
CONV_TAP_UNROLL = 8


def _ln_swish(c, g, b):
    mu = jnp.mean(c, axis=-1, keepdims=True)
    xc = c - mu
    y = xc * lax.rsqrt(jnp.mean(xc * xc, axis=-1, keepdims=True) + EPS) * g + b
    return y * jax.nn.sigmoid(y)


def _conv_prompt_kernel(halo_ref, u_ref, w_ref, b_ref, g_ref, beta_ref, o_ref, ext_ref, conv_ref, wb_ref):
    tc, ch = u_ref.shape
    first = pl.program_id(1) == 0
    lane_tiles = [slice(j * LANES, (j + 1) * LANES) for j in range(ch // LANES)]
    for j, sl in enumerate(lane_tiles):
        halo = jnp.where(first, 0.0, halo_ref[:, sl])
        for dup in range(2):
            ext_ref[j, pl.ds(dup, CONV_HALO, stride=2), :] = halo
            ext_ref[j, pl.ds(2 * CONV_HALO + dup, tc, stride=2), :] = u_ref[:, sl]
    for w in range(CONV_WIDTH):
        wb_ref[w] = jnp.broadcast_to(w_ref[w:w + 1, :], (8, ch))
    lead = CONV_HALO - (CONV_WIDTH - 1)
    rows = CONV_ACC_ROWS
    for j, sl in enumerate(lane_tiles):
        for r0 in range(0, tc, rows):
            def tap(w, acc, j=j, sl=sl, r0=r0):
                win = ext_ref[j, pl.ds(2 * (r0 + lead + w), rows, stride=2), :].reshape(rows // 8, 8, LANES)
                return acc + wb_ref[w, :, sl][None] * win

            acc = lax.fori_loop(0, CONV_WIDTH, tap, jnp.zeros((rows // 8, 8, LANES), F32) + b_ref[:, sl],
                                unroll=CONV_TAP_UNROLL)
            conv_ref[r0:r0 + rows, sl] = acc.reshape(rows, LANES)
    for c0 in range(0, tc, CONV_CHUNK):
        o_ref[c0:c0 + CONV_CHUNK, :] = _ln_swish(conv_ref[c0:c0 + CONV_CHUNK, :], g_ref[...], beta_ref[...]).astype(BF16)


def _conv_prompt(u, conv_w, conv_b, ln_g, ln_b, batch, seq):
    ch = u.shape[1]
    tc = min(WIDE_ROW_TILE, seq)
    nt = seq // tc
    ratio = tc // CONV_HALO
    return pl.pallas_call(
        _conv_prompt_kernel,
        grid=(batch, nt),
        in_specs=[
            pl.BlockSpec((CONV_HALO, ch), lambda b, i: (jnp.maximum((b * nt + i) * ratio - 1, 0), 0)),
            pl.BlockSpec((tc, ch), lambda b, i: (b * nt + i, 0)),
            _const_spec((CONV_WIDTH, ch)), _const_spec((1, ch)), _const_spec((1, ch)), _const_spec((1, ch)),
        ],
        out_specs=pl.BlockSpec((tc, ch), lambda b, i: (b * nt + i, 0)),
        out_shape=jax.ShapeDtypeStruct((batch * seq, ch), BF16),
        scratch_shapes=[pltpu.VMEM((ch // LANES, 2 * (CONV_HALO + tc), LANES), F32), pltpu.VMEM((tc, ch), F32),
                        pltpu.VMEM((CONV_WIDTH, 8, ch), F32)],
        compiler_params=_params("parallel", "parallel"),
        name="conv_prompt",
    )(u, u, conv_w, conv_b, ln_g, ln_b)


def _conv_sample_kernel(state_ref, new_ref, w_ref, b_ref, g_ref, beta_ref, o_ref, state_out_ref, ext_ref):
    nb, ctx, ch = state_ref.shape
    t = new_ref.shape[1]
    ext_ref[:, 0:ctx, :] = state_ref[...]
    ext_ref[:, ctx:ctx + t, :] = new_ref[...]
    acc = jnp.zeros((nb, t, ch), F32) + b_ref[...]
    for w in range(CONV_WIDTH):
        acc = acc + w_ref[w:w + 1, :] * ext_ref[:, w:w + t, :]
    o_ref[...] = _ln_swish(acc, g_ref[...], beta_ref[...])
    state_out_ref[...] = ext_ref[:, t:t + ctx, :]


def _conv_sample(state, u_new, conv_w, conv_b, ln_g, ln_b):
    batch, ctx, ch = state.shape
    t = u_new.shape[1]
    nb = 8
    blk = lambda n: pl.BlockSpec((nb, n, ch), lambda i: (i, 0, 0))
    return pl.pallas_call(
        _conv_sample_kernel,
        grid=(batch // nb,),
        in_specs=[blk(ctx), blk(t), _const_spec((CONV_WIDTH, ch)), _const_spec((1, ch)), _const_spec((1, ch)),
                  _const_spec((1, ch))],
        out_specs=(blk(t), blk(ctx)),
        out_shape=(jax.ShapeDtypeStruct((batch, t, ch), F32), jax.ShapeDtypeStruct((batch, ctx, ch), F32)),
        scratch_shapes=[pltpu.VMEM((nb, ctx + t + 6, ch), F32)],
        compiler_params=_params("parallel"),
        name="conv_sample",
    )(state, u_new, conv_w, conv_b, ln_g, ln_b)


GROUP_LANES = 16


def _spread_groups(vec, combine):
    t = combine(combine(vec, pltpu.roll(vec, LANES - GROUP_LANES, 1)), pltpu.roll(vec, LANES - 2 * GROUP_LANES, 1))
    lane = lax.broadcasted_iota(jnp.int32, vec.shape, 1)
    return jnp.where(lane < GROUP_LANES, t,
                     jnp.where(lane < 2 * GROUP_LANES, pltpu.roll(t, GROUP_LANES, 1), pltpu.roll(t, 2 * GROUP_LANES, 1)))


SLAB_POS = DILATED_GROUPS[-1][1]
SLAB_ROWS = SLAB_POS * N_KV_HEADS
TAIL_POS = max(w for w, d in DILATED_GROUPS if d < SLAB_POS)


def _sample_attn_kernel(kc_ref, vc_ref, kn_ref, vn_ref, wq_ref, bias_d_ref, bias_t_ref, bias_n_ref, mk_ref, mv_ref,
                        wm_ref, bias_m_ref, kw_ref, vw_ref, a_ref, m_ref, kx_ref, vx_ref):
    n_slab = kc_ref.shape[1]
    new_rows = kn_ref.shape[1]
    t = new_rows // N_KV_HEADS
    tail_slabs = TAIL_POS // SLAB_POS
    n_used = N_GROUPS * GROUP_LANES
    lane = lax.broadcasted_iota(jnp.int32, (1, LANES), 1)

    for src, new, dst in ((kc_ref, kn_ref, kw_ref), (vc_ref, vn_ref, vw_ref)):
        dst[0, :, 0:SLAB_ROWS - new_rows, :] = src[0, :, new_rows:SLAB_ROWS, :]
        dst[0, 0:n_slab - 1, SLAB_ROWS - new_rows:SLAB_ROWS, :] = src[0, 1:n_slab, 0:new_rows, :]
        dst[0, n_slab - 1, SLAB_ROWS - new_rows:SLAB_ROWS, :] = new[0]

    kx_ref[...] = jnp.zeros(kx_ref.shape, F32)
    vx_ref[...] = jnp.zeros(vx_ref.shape, F32)
    kx_ref[0:new_rows, :] = kn_ref[0]
    vx_ref[0:new_rows, :] = vn_ref[0]

    def dilated(ref):
        return ref[0, :, 0:new_rows, :].reshape(n_slab * new_rows, HEAD_DIM).astype(BF16)

    def tail(ref):
        return ref[0, n_slab - tail_slabs:n_slab, :, :].reshape(tail_slabs * SLAB_ROWS, HEAD_DIM).astype(BF16)

    scale = HEAD_DIM ** -0.5
    wq = wq_ref[0]
    s_d = _dot(dilated(kc_ref), wq) * scale + bias_d_ref[...]
    s_t = _dot(tail(kc_ref), wq) * scale + bias_t_ref[...]
    s_n = _dot(kx_ref[...].astype(BF16), wq) * scale + bias_n_ref[...]
    col_max = lambda s: jnp.max(s, axis=0, keepdims=True)
    col_sum = lambda p: jnp.sum(p, axis=0, keepdims=True)
    m_col = jnp.maximum(jnp.maximum(col_max(s_d), col_max(s_t)), col_max(s_n))
    m_joint = jnp.where(lane < n_used, _spread_groups(m_col, jnp.maximum), 0.0)
    p_d, p_t, p_n = jnp.exp(s_d - m_joint), jnp.exp(s_t - m_joint), jnp.exp(s_n - m_joint)
    l_col = col_sum(p_d) + col_sum(p_t) + col_sum(p_n)
    inv = 1.0 / jnp.where(lane < n_used, _spread_groups(l_col, jnp.add), 1.0)
    o = (_dot_tn((p_d * inv).astype(BF16), dilated(vc_ref)) + _dot_tn((p_t * inv).astype(BF16), tail(vc_ref))
         + _dot_tn((p_n * inv).astype(BF16), vx_ref[...].astype(BF16)))
    for h in range(N_KV_HEADS):
        r = h * t
        a_ref[0, :, h * HEAD_DIM:(h + 1) * HEAD_DIM] = functools.reduce(
            jnp.add, [o[g * GROUP_LANES + r:g * GROUP_LANES + r + t, :] for g in range(N_GROUPS)])

    sm = _dot(mk_ref[0].astype(BF16), wm_ref[0]) * (MEM_HEAD_DIM ** -0.5) + bias_m_ref[...]
    pm = jnp.exp(sm - jnp.where(lane < MEM_HEADS * t, col_max(sm), 0.0))
    lm = jnp.where(lane < MEM_HEADS * t, col_sum(pm), 1.0)
    om = _dot_tn((pm * (1.0 / lm)).astype(BF16), mv_ref[0].astype(BF16))
    for h in range(MEM_HEADS):
        m_ref[0, :, h * MEM_HEAD_DIM:(h + 1) * MEM_HEAD_DIM] = om[h * t:(h + 1) * t, :]


def _sample_masks(cache_len, t, n_mem):
    col = np.arange(LANES)[None, :]
    g, c_head, c_tok = col // GROUP_LANES, (col % GROUP_LANES) // t, (col % GROUP_LANES) % t
    used = (col < N_GROUPS * GROUP_LANES) & (col % GROUP_LANES < N_KV_HEADS * t)
    pad = [1] * (LANES // GROUP_LANES - N_GROUPS)
    dil = np.array([d for _, d in DILATED_GROUPS] + pad)[g]
    win = np.array([w for w, _ in DILATED_GROUPS] + pad)[g]
    sparse = dil >= SLAB_POS

    def keep(pos, head, group_sel):
        dist = cache_len + c_tok - pos
        return used & group_sel & (head == c_head) & (dist >= 0) & (dist % dil == 0) & (dist <= win)

    n_slab = cache_len // SLAB_POS
    y = np.arange(n_slab * t * N_KV_HEADS)[:, None]
    keep_d = keep((y // (t * N_KV_HEADS)) * SLAB_POS + (y % (t * N_KV_HEADS)) // N_KV_HEADS, y % N_KV_HEADS, sparse)
    x = np.arange(TAIL_POS * N_KV_HEADS)[:, None]
    keep_t = keep(cache_len - TAIL_POS + x // N_KV_HEADS, x % N_KV_HEADS, ~sparse)
    z = np.arange(LANES)[:, None]
    keep_n = keep(cache_len + z // N_KV_HEADS, z % N_KV_HEADS, True) & (z < t * N_KV_HEADS)
    w = np.arange(n_mem * MEM_HEADS)[:, None]
    keep_m = (col < MEM_HEADS * t) & (w % MEM_HEADS == col // t)
    return tuple(jnp.asarray(np.where(k, 0.0, NEG), F32) for k in (keep_d, keep_t, keep_n, keep_m))


def _query_columns(q, batch, t, n_groups):
    qt = q.reshape(batch, t, n_groups, N_KV_HEADS, HEAD_DIM).transpose(0, 4, 2, 3, 1)
    qt = qt.reshape(batch, HEAD_DIM, n_groups, N_KV_HEADS * t)
    qt = jnp.pad(qt, ((0, 0), (0, 0), (0, 0), (0, GROUP_LANES - N_KV_HEADS * t)))
    qt = qt.reshape(batch, HEAD_DIM, n_groups * GROUP_LANES)
    return jnp.pad(qt, ((0, 0), (0, 0), (0, LANES - n_groups * GROUP_LANES)))


def _sample_attention(q, k_new, v_new, mq, cache_k, cache_v, mem_k, mem_v):
    batch, cache_len = cache_k.shape[0], cache_k.shape[1]
    new_rows = k_new.shape[0] // batch
    t = new_rows // N_KV_HEADS
    n_mem = mem_k.shape[1]
    assert N_KV_HEADS * t <= GROUP_LANES and t <= SLAB_POS and new_rows % 8 == 0
    assert cache_len % SLAB_POS == 0 and cache_len >= max(w for w, _ in DILATED_GROUPS)
    assert all(d == SLAB_POS or w <= TAIL_POS for w, d in DILATED_GROUPS)
    n_slab = cache_len // SLAB_POS
    wq = _query_columns(q, batch, t, N_GROUPS)
    wm = _query_columns(mq, batch, t, 1)
    bias_d, bias_t, bias_n, bias_m = _sample_masks(cache_len, t, n_mem)
    per_b = lambda *shape: pl.BlockSpec((1,) + shape, lambda b: (b,) + (0,) * len(shape))
    slabs = lambda c: c.reshape(batch, n_slab, SLAB_ROWS, HEAD_DIM)
    mem_rows = n_mem * MEM_HEADS
    win = jax.ShapeDtypeStruct((batch, n_slab, SLAB_ROWS, HEAD_DIM), F32)
    k_win, v_win, a, m = pl.pallas_call(
        _sample_attn_kernel,
        grid=(batch,),
        in_specs=[
            per_b(n_slab, SLAB_ROWS, HEAD_DIM), per_b(n_slab, SLAB_ROWS, HEAD_DIM),
            per_b(new_rows, HEAD_DIM), per_b(new_rows, HEAD_DIM), per_b(HEAD_DIM, LANES),
            _const_spec(bias_d.shape), _const_spec(bias_t.shape), _const_spec(bias_n.shape),
            per_b(mem_rows, MEM_HEAD_DIM), per_b(mem_rows, MEM_HEAD_DIM), per_b(MEM_HEAD_DIM, LANES),
            _const_spec(bias_m.shape),
        ],
        out_specs=(per_b(n_slab, SLAB_ROWS, HEAD_DIM), per_b(n_slab, SLAB_ROWS, HEAD_DIM), per_b(t, ATTN_WIDTH),
                   per_b(t, MEM_WIDTH)),
        out_shape=(win, win, jax.ShapeDtypeStruct((batch, t, ATTN_WIDTH), F32),
                   jax.ShapeDtypeStruct((batch, t, MEM_WIDTH), F32)),
        scratch_shapes=[pltpu.VMEM((LANES, HEAD_DIM), F32), pltpu.VMEM((LANES, HEAD_DIM), F32)],
        compiler_params=_params("parallel"),
        name="sample_attn",
    )(slabs(cache_k), slabs(cache_v), k_new.reshape(batch, new_rows, HEAD_DIM), v_new.reshape(batch, new_rows, HEAD_DIM),
      wq, bias_d, bias_t, bias_n, mem_k.reshape(batch, mem_rows, MEM_HEAD_DIM),
      mem_v.reshape(batch, mem_rows, MEM_HEAD_DIM), wm, bias_m)
    return k_win.reshape(cache_k.shape), v_win.reshape(cache_v.shape), a, m


ROUTE_GROUP_LANE0 = 0
ROUTE_EXPERT_LANE0 = N_EXPERT_GROUPS


def _route(logits):
    lane = lax.broadcasted_iota(jnp.int32, logits.shape, 1).astype(F32)
    big = float(LANES)

    def masked_softmax(keep):
        z = jnp.where(keep, logits, NEG)
        e = jnp.where(keep, jnp.exp(z - jnp.max(z, axis=-1, keepdims=True)), 0.0)
        return e / jnp.sum(e, axis=-1, keepdims=True)

    def first_argmax(vals, keep):
        top = jnp.max(jnp.where(keep, vals, -1.0), axis=-1, keepdims=True)
        idx = jnp.min(jnp.where(jnp.logical_and(keep, vals == top), lane, big), axis=-1, keepdims=True)
        return top, idx

    is_group = lane < N_EXPERT_GROUPS
    pg = masked_softmax(is_group)
    pg_top, g_idx = first_argmax(pg, is_group)
    lo = ROUTE_EXPERT_LANE0 + g_idx * EXPERTS_PER_GROUP
    in_group = jnp.logical_and(lane >= lo, lane < lo + EXPERTS_PER_GROUP)
    pe = masked_softmax(in_group)
    p1, i1 = first_argmax(pe, in_group)
    p2, i2 = first_argmax(pe, jnp.logical_and(in_group, lane != i1))
    denom = p1 + p2
    w1 = pg_top * p1 / denom
    w2 = pg_top * p2 / denom
    return lane, i1 - ROUTE_EXPERT_LANE0, i2 - ROUTE_EXPERT_LANE0, w1, w2


def _merge_kernel(x_ref, gmix_ref, ao_ref, cc_ref, mo_ref, wgate_ref, wa_ref, wc_ref, wm_ref, wo_ref, gffn_ref, wr_ref,
                  br_ref, cnt_in_ref, x1_ref, h2_ref, route_ref, cnt_out_ref, cnt_ref):
    @pl.when(pl.program_id(0) == 0)
    def _():
        cnt_ref[...] = cnt_in_ref[...]

    d = x_ref.shape[1]
    x = x_ref[...]
    h = _rms(x, gmix_ref[...]).astype(BF16)
    a = _dot(ao_ref[...].astype(BF16), wa_ref[...])
    c = _dot(cc_ref[...].astype(BF16), wc_ref[...])
    m = _dot(mo_ref[...].astype(BF16), wm_ref[...])
    z = jax.nn.sigmoid(_dot(h, wgate_ref[:, 0:d])) * a
    z = z + jax.nn.sigmoid(_dot(h, wgate_ref[:, d:2 * d])) * c
    z = z + jax.nn.sigmoid(_dot(h, wgate_ref[:, 2 * d:3 * d])) * m
    x1 = x + _dot(z.astype(BF16), wo_ref[...])
    x1_ref[...] = x1
    h2 = _rms(x1, gffn_ref[...]).astype(BF16)
    packed = _pack_halves(h2)
    for c in range(packed.shape[1] // LANES):
        h2_ref[pl.ds(c, x_ref.shape[0], stride=packed.shape[1] // LANES), :] = packed[:, c * LANES:(c + 1) * LANES]
    logits = _dot(h2, wr_ref[...]) + br_ref[...]
    lane, e1, e2, w1, w2 = _route(logits)

    tm = x_ref.shape[0]
    hit1 = jnp.where(lane == e1, 1.0, 0.0)
    hit2 = jnp.where(lane == e2, 1.0, 0.0)
    hits = hit1 + hit2
    earlier = (lax.broadcasted_iota(jnp.int32, (tm, tm), 1) < lax.broadcasted_iota(jnp.int32, (tm, tm), 0))
    before = _dot(jnp.where(earlier, 1.0, 0.0).astype(BF16), hits.astype(BF16)) + cnt_ref[...]
    rank1 = jnp.sum(hit1 * before, axis=-1, keepdims=True)
    rank2 = jnp.sum(hit2 * before, axis=-1, keepdims=True)
    cnt_ref[...] = cnt_ref[...] + jnp.sum(hits, axis=0, keepdims=True)
    cnt_out_ref[...] = cnt_ref[...]
    route = jnp.zeros(logits.shape, F32)
    for i, val in enumerate((e1, e2, w1, w2, rank1, rank2)):
        route = jnp.where(lane == i, val, route)
    route_ref[...] = route


ROUTE_E, ROUTE_W, ROUTE_RANK = 0, 2, 4


def _merge(x, g_mix, ao, cc, mo, w_gate, w_a, w_c, w_m, w_o, g_ffn, w_r, b_r, counts):
    m, d = x.shape
    tm = min(MERGE_TILE, m)
    row = lambda i: (i, 0)
    rows = lambda arr: pl.BlockSpec((tm, arr.shape[1]), row)
    ins = [x, g_mix, ao, cc, mo, w_gate, w_a, w_c, w_m, w_o, g_ffn, w_r, b_r, counts]
    specs = ([rows(x), _const_spec(g_mix.shape), rows(ao), rows(cc), rows(mo)]
             + [_const_spec(a.shape) for a in (w_gate, w_a, w_c, w_m, w_o, g_ffn, w_r, b_r, counts)])
    out_shape = (jax.ShapeDtypeStruct((m, d), F32), jax.ShapeDtypeStruct((m * (d // 2 // LANES), LANES), jnp.uint32),
                 jax.ShapeDtypeStruct((m, LANES), F32), jax.ShapeDtypeStruct((1, LANES), F32))
    return pl.pallas_call(
        _merge_kernel,
        grid=(m // tm,),
        in_specs=specs,
        out_specs=(rows(out_shape[0]), pl.BlockSpec((tm * (d // 2 // LANES), LANES), row), rows(out_shape[2]),
                   _const_spec((1, LANES))),
        out_shape=out_shape,
        scratch_shapes=[pltpu.VMEM((1, LANES), F32)],
        compiler_params=_params("arbitrary"),
        name="merge",
    )(*ins)


def _routing_tables(counts, routes):
    cnt = counts[0, :N_EXPERTS].astype(jnp.int32)
    tiles = (cnt + MOE_TILE - 1) // MOE_TILE
    tile_start = jnp.cumsum(tiles) - tiles
    row_start = tile_start * MOE_TILE
    ids = routes[:, ROUTE_E:ROUTE_E + TOP_K].astype(jnp.int32)
    start_of = jnp.sum(jnp.where(ids[:, :, None] == jnp.arange(N_EXPERTS)[None, None, :], row_start[None, None, :], 0),
                       axis=-1)
    pos = start_of + routes[:, ROUTE_RANK:ROUTE_RANK + TOP_K].astype(jnp.int32)
    return pos.reshape(-1), tile_start.astype(jnp.int32), tiles.astype(jnp.int32)


def _dispatch_kernel(pos_ref, h_ref, xs_in, xs_out, sem, *, row0):
    del xs_in
    tm = h_ref.shape[0]
    base = (row0 + pl.program_id(0) * tm) * TOP_K

    def issue(j, carry):
        for k in range(TOP_K):
            slot = pos_ref[base + j * TOP_K + k]
            pltpu.make_async_copy(h_ref.at[j], xs_out.at[slot], sem).start()
        return carry

    lax.fori_loop(0, tm, issue, 0, unroll=8)
    for k in range(TOP_K):
        pltpu.make_async_copy(h_ref, xs_out.at[pl.ds(0, tm)], sem).wait()


def _dispatch(h2, pos, xs, row0):
    m = h2.shape[0]
    tm = min(WIDE_ROW_TILE, m)
    return pl.pallas_call(
        functools.partial(_dispatch_kernel, row0=row0),
        grid_spec=pltpu.PrefetchScalarGridSpec(
            num_scalar_prefetch=1,
            grid=(m // tm,),
            in_specs=[pl.BlockSpec((tm,) + h2.shape[1:], lambda i, *_: (i, 0, 0)), pl.BlockSpec(memory_space=pl.ANY)],
            out_specs=pl.BlockSpec(memory_space=pl.ANY),
            scratch_shapes=[pltpu.SemaphoreType.DMA(())],
        ),
        out_shape=jax.ShapeDtypeStruct(xs.shape, xs.dtype),
        input_output_aliases={2: 0},
        compiler_params=_params("arbitrary"),
        name="moe_dispatch",
    )(pos, h2, xs)


def _gmm_kernel(start_ref, count_ref, xs_hbm, wg_ref, wu_ref, wd_ref, ys_hbm, xbuf, obuf, wg_s, wu_s, wd_s,
                in_sems, out_sems):
    e = pl.program_id(0)
    nb = GMM_BUFS
    first, n = start_ref[e], count_ref[e]

    x_rows, y_rows = xbuf.shape[1], obuf.shape[1]
    x_tiles, y_tiles = x_rows // MOE_TILE, y_rows // MOE_TILE

    def copies(expert_first):
        def rows(i, n_rows):
            return pl.ds(pl.multiple_of((expert_first + i) * n_rows, n_rows), n_rows)

        def in_copy(i):
            return pltpu.make_async_copy(xs_hbm.at[rows(i, x_rows)], xbuf.at[i % nb], in_sems.at[i % nb])

        def out_copy(i):
            return pltpu.make_async_copy(obuf.at[i % nb], ys_hbm.at[rows(i, y_rows)], out_sems.at[i % nb])

        return in_copy, out_copy

    in_copy, out_copy = copies(first)

    def start_first_reads(expert):
        copy, _ = copies(start_ref[expert])
        lax.fori_loop(0, jnp.minimum(count_ref[expert], nb - 1), lambda i, c: (copy(i).start(), c)[1], 0)

    @pl.when(e == 0)
    def _():
        start_first_reads(0)

    @pl.when(n > 0)
    def _():
        wg_s[...] = wg_ref[0].astype(BF16)
        wu_s[...] = wu_ref[0].astype(BF16)
        wd_s[...] = wd_ref[0].astype(BF16)
        half = x_tiles * LANES

        def tile(i, carry):
            slot = i % nb
            in_copy(i).wait()

            @pl.when(i + nb - 1 < n)
            def _():
                in_copy(i + nb - 1).start()

            @pl.when(i >= nb)
            def _():
                out_copy(i - nb).wait()

            packed = jnp.concatenate([xbuf[slot, pl.ds(c, MOE_TILE, stride=x_tiles), :] for c in range(x_tiles)], axis=1)
            x_hi, x_lo = _unpack_halves(packed)
            gate = _dot(x_hi, wg_s[0:half, :]) + _dot(x_lo, wg_s[half:, :])
            up = _dot(x_hi, wu_s[0:half, :]) + _dot(x_lo, wu_s[half:, :])
            hid = gate * jax.nn.sigmoid(gate) * up
            res = _dot(hid.astype(BF16), wd_s[...])
            for c in range(y_tiles):
                obuf[slot, pl.ds(c, MOE_TILE, stride=y_tiles), :] = res[:, c * LANES:(c + 1) * LANES]
            out_copy(i).start()
            return carry

        lax.fori_loop(0, n, tile, 0)
        lax.fori_loop(jnp.maximum(n - nb, 0), n, lambda i, c: (out_copy(i).wait(), c)[1], 0)

    @pl.when(e + 1 < pl.num_programs(0))
    def _():
        start_first_reads(e + 1)

    @pl.when(e == pl.num_programs(0) - 1)
    def _():
        obuf[0] = jnp.zeros(obuf.shape[1:], F32)

        def fill(t, carry):
            dst = ys_hbm.at[pl.ds(pl.multiple_of(t * y_rows, y_rows), y_rows)]
            copy = pltpu.make_async_copy(obuf.at[0], dst, out_sems.at[0])
            copy.start()
            copy.wait()
            return carry

        lax.fori_loop(first + n, ys_hbm.shape[0] // y_rows, fill, 0)


def _grouped_mlp(xs, tile_start, tile_count, w_gate, w_up, w_down):
    d, ff = w_gate.shape[1], w_gate.shape[2]
    x_tiles, y_tiles = d // 2 // LANES, d // LANES
    n_slots = xs.shape[0] // x_tiles
    wspec = lambda a, b: pl.BlockSpec((1, a, b), lambda e, *_: (e, 0, 0))
    any_spec = pl.BlockSpec(memory_space=pl.ANY)
    return pl.pallas_call(
        _gmm_kernel,
        grid_spec=pltpu.PrefetchScalarGridSpec(
            num_scalar_prefetch=2,
            grid=(N_EXPERTS,),
            in_specs=[any_spec, wspec(d, ff), wspec(d, ff), wspec(ff, d)],
            out_specs=any_spec,
            scratch_shapes=[pltpu.VMEM((GMM_BUFS, MOE_TILE * x_tiles, LANES), xs.dtype),
                            pltpu.VMEM((GMM_BUFS, MOE_TILE * y_tiles, LANES), F32),
                            pltpu.VMEM((d, ff), BF16), pltpu.VMEM((d, ff), BF16), pltpu.VMEM((ff, d), BF16),
                            pltpu.SemaphoreType.DMA((GMM_BUFS,)), pltpu.SemaphoreType.DMA((GMM_BUFS,))],
        ),
        out_shape=jax.ShapeDtypeStruct((n_slots * y_tiles, LANES), F32),
        compiler_params=_params("arbitrary"),
        name="moe_gmm",
    )(tile_start, tile_count, xs, w_gate, w_up, w_down)


def _combine_kernel(pos_ref, x1_ref, route_ref, ys_hbm, ys_flat, y_ref, buf_ref, sems, *, row0):
    tm = x1_ref.shape[0]
    y_tiles = ys_hbm.shape[1]
    i = pl.program_id(0)

    def gather(tile, half):
        base = (row0 + tile * tm) * TOP_K

        def issue(j, carry):
            for k in range(TOP_K):
                slot = pos_ref[base + j * TOP_K + k]
                dst = buf_ref.at[half, k, pl.ds(pl.multiple_of(j * y_tiles, y_tiles), y_tiles)]
                pltpu.make_async_copy(ys_hbm.at[slot], dst, sems.at[half]).start()
            return carry

        lax.fori_loop(0, tm, issue, 0, unroll=4)

    @pl.when(i == 0)
    def _():
        gather(0, 0)

    @pl.when(i + 1 < pl.num_programs(0))
    def _():
        gather(i + 1, (i + 1) % 2)

    half = i % 2
    for k in range(TOP_K):
        pltpu.make_async_copy(ys_flat.at[pl.ds(0, tm * y_tiles)], buf_ref.at[half, k], sems.at[half]).wait()
    route = route_ref[...]
    for c in range(y_tiles):
        sl = slice(c * LANES, (c + 1) * LANES)
        rows = pl.ds(c, tm, stride=y_tiles)
        y_ref[:, sl] = (x1_ref[:, sl] + route[:, ROUTE_W:ROUTE_W + 1] * buf_ref[half, 0, rows, :]
                        + route[:, ROUTE_W + 1:ROUTE_W + 2] * buf_ref[half, 1, rows, :])


def _combine(x1, route, pos, ys, row0):
    m, d = x1.shape
    tm = min(WIDE_ROW_TILE, m)
    row = lambda i, *_: (i, 0)
    any_spec = pl.BlockSpec(memory_space=pl.ANY)
    return pl.pallas_call(
        functools.partial(_combine_kernel, row0=row0),
        grid_spec=pltpu.PrefetchScalarGridSpec(
            num_scalar_prefetch=1,
            grid=(m // tm,),
            in_specs=[pl.BlockSpec((tm, d), row), pl.BlockSpec((tm, LANES), row), any_spec, any_spec],
            out_specs=pl.BlockSpec((tm, d), row),
            scratch_shapes=[pltpu.VMEM((2, TOP_K, tm * (d // LANES), LANES), F32), pltpu.SemaphoreType.DMA((2,))],
        ),
        out_shape=jax.ShapeDtypeStruct((m, d), F32),
        compiler_params=_params("arbitrary"),
        name="moe_combine",
    )(pos, x1, route, ys.reshape(-1, d // LANES, LANES), ys)


def _layer(layer, x_prompt, x_sample, mem_prompt, cache_k, cache_v, state_conv, cache_mem_k, cache_mem_v, p):
    batch, seq, d = x_prompt.shape
    dec_batch, dec_seq, _ = x_sample.shape
    conv_ch = p["conv_w"].shape[-1]
    n_in = N_Q_HEADS * HEAD_DIM + 2 * ATTN_WIDTH + 2 * conv_ch + MEM_WIDTH
    past_len = cache_k.shape[2]

    row2 = lambda name: p[name][layer][None, :]
    w_in = p["w_in"][layer]
    w_main = w_in[:, :n_in].astype(BF16)
    w_gate = w_in[:, n_in:].astype(BF16)
    w_a, w_c, w_m, w_o = (p[n][layer].astype(BF16) for n in ("w_attn_proj", "w_conv_proj", "w_mem_proj", "w_out"))
    w_router = jnp.concatenate(
        [p["w_router_group"][layer], p["w_router_expert"][layer].transpose(1, 0, 2).reshape(d, N_EXPERTS)], axis=1)
    w_router = jnp.pad(w_router, ((0, 0), (0, LANES - w_router.shape[1]))).astype(BF16)
    b_router = jnp.concatenate([p["b_router_group"][layer], p["b_router_expert"][layer].reshape(-1)])
    b_router = jnp.pad(b_router, (0, LANES - b_router.shape[0]))[None, :]
    conv_args = (p["conv_w"][layer], row2("conv_b"), row2("conv_ln_g"), row2("conv_ln_b"))
    merge_w = (w_gate, w_a, w_c, w_m, w_o, row2("norm_ffn_g"), w_router, b_router)

    xp = x_prompt.reshape(batch * seq, d)
    tabs_p = _rope_tables(jnp.arange(seq, dtype=jnp.int32))
    q_p, k_p, v_p, u_p, mq_p = _in_proj(xp, row2("norm_mix_g"), w_main, row2("q_norm_g"), row2("k_norm_g"),
                                        row2("mq_norm_g"), tabs_p, conv_ch)
    ao_p = _prompt_attention(q_p, k_p, v_p, batch, seq)
    mem_k_p, mem_v_p = _mem_kv(mem_prompt.reshape(-1, d), row2("mem_norm_g"), p["w_mem_kv"][layer].astype(BF16),
                               row2("mk_norm_g"))
    mo_p = _mem_attention(mq_p, mem_k_p, mem_v_p, batch, seq)
    cc_p = _conv_prompt(u_p, *conv_args, batch, seq)
    x1_p, h2_p, route_p, counts = _merge(xp, row2("norm_mix_g"), ao_p, cc_p, mo_p, *merge_w,
                                         jnp.zeros((1, LANES), F32))

    xs = x_sample.reshape(dec_batch * dec_seq, d)
    tabs_s = _rope_tables(jnp.tile(past_len + jnp.arange(dec_seq, dtype=jnp.int32), dec_batch))
    q_s, k_s, v_s, u_s, mq_s = _in_proj(xs, row2("norm_mix_g"), w_main, row2("q_norm_g"), row2("k_norm_g"),
                                        row2("mq_norm_g"), tabs_s, conv_ch)
    k_win_s, v_win_s, a_s, mo_s = _sample_attention(q_s, k_s, v_s, mq_s, cache_k[layer], cache_v[layer],
                                                    cache_mem_k[layer], cache_mem_v[layer])
    cc_s, conv_state_s = _conv_sample(state_conv[layer], u_s.reshape(dec_batch, dec_seq, conv_ch), *conv_args)
    x1_s, h2_s, route_s, counts = _merge(xs, row2("norm_mix_g"), a_s.reshape(-1, ATTN_WIDTH),
                                         cc_s.reshape(-1, conv_ch), mo_s.reshape(-1, MEM_WIDTH), *merge_w, counts)

    n_p, n_s = xp.shape[0], xs.shape[0]
    n_tok = n_p + n_s
    n_tiles = (TOP_K * n_tok + N_EXPERTS * (MOE_TILE - 1)) // MOE_TILE + 1
    pos, tile_start, tile_count = _routing_tables(counts, jnp.concatenate([route_p, route_s], axis=0))
    per_token = lambda h: h.reshape(-1, d // 2 // LANES, LANES)
    slots = _dispatch(per_token(h2_p), pos, jnp.zeros((n_tiles * MOE_TILE, d // 2 // LANES, LANES), h2_p.dtype), 0)
    slots = _dispatch(per_token(h2_s), pos, slots, n_p)
    ys = _grouped_mlp(slots.reshape(-1, LANES), tile_start, tile_count, p["w_expert_gate"][layer],
                      p["w_expert_up"][layer], p["w_expert_down"][layer])
    y_p = _combine(x1_p, route_p, pos, ys, 0)
    y_s = _combine(x1_s, route_s, pos, ys, n_p)

    state_p = (k_p.reshape(batch, seq, N_KV_HEADS, HEAD_DIM), v_p.reshape(batch, seq, N_KV_HEADS, HEAD_DIM),
               u_p.reshape(batch, seq, conv_ch)[:, seq - (CONV_WIDTH - 1):],
               mem_k_p.reshape(batch, -1, MEM_HEADS, MEM_HEAD_DIM), mem_v_p.reshape(batch, -1, MEM_HEADS, MEM_HEAD_DIM))
    state_s = (k_win_s, v_win_s, conv_state_s)
    return y_p.reshape(batch, seq, d), y_s.reshape(dec_batch, dec_seq, d), state_p, state_s


def kernel(x_prompt, x_sample, mem_prompt, cache_k, cache_v, state_conv, cache_mem_k, cache_mem_v, norm_mix_g, w_in, q_norm_g, k_norm_g, conv_w, conv_b, conv_ln_g, conv_ln_b, mem_norm_g, w_mem_kv, mq_norm_g, mk_norm_g, w_attn_proj, w_conv_proj, w_mem_proj, w_out, norm_ffn_g, w_router_group, b_router_group, w_router_expert, b_router_expert, w_expert_gate, w_expert_up, w_expert_down):
    p = dict(norm_mix_g=norm_mix_g, w_in=w_in, q_norm_g=q_norm_g, k_norm_g=k_norm_g, conv_w=conv_w, conv_b=conv_b,
             conv_ln_g=conv_ln_g, conv_ln_b=conv_ln_b, mem_norm_g=mem_norm_g, w_mem_kv=w_mem_kv, mq_norm_g=mq_norm_g,
             mk_norm_g=mk_norm_g, w_attn_proj=w_attn_proj, w_conv_proj=w_conv_proj, w_mem_proj=w_mem_proj,
             w_out=w_out, norm_ffn_g=norm_ffn_g, w_router_group=w_router_group, b_router_group=b_router_group,
             w_router_expert=w_router_expert, b_router_expert=b_router_expert, w_expert_gate=w_expert_gate,
             w_expert_up=w_expert_up, w_expert_down=w_expert_down)
    depth = w_in.shape[0]
    seq = x_prompt.shape[1]
    assert seq <= max(w for w, _ in DILATED_GROUPS)
    y_p, y_s = x_prompt, x_sample
    states_p, states_s = [], []
    for layer in range(depth):
        y_p, y_s, st_p, st_s = _layer(layer, y_p, y_s, mem_prompt, cache_k, cache_v, state_conv, cache_mem_k,
                                      cache_mem_v, p)
        states_p.append(st_p)
        states_s.append(st_s)
    stack = lambda states, i: jnp.stack([s[i] for s in states], axis=0)
    return (y_p, y_s, stack(states_p, 0), stack(states_p, 1), stack(states_p, 2), stack(states_p, 3),
            stack(states_p, 4), stack(states_s, 0), stack(states_s, 1), stack(states_s, 2))
```

```python
import functools

import jax
import jax.numpy as jnp
import numpy as np
from jax import lax
from jax.experimental import pallas as pl
from jax.experimental.pallas import tpu as pltpu

HEAD_DIM = 128
N_KV_HEADS = 4
DILATED_GROUPS = ((128, 1), (512, 4), (2048, 16))
N_GROUPS = len(DILATED_GROUPS)
N_Q_HEADS = N_GROUPS * N_KV_HEADS
ATTN_WIDTH = N_KV_HEADS * HEAD_DIM
BAND = 128
ATTN_UNROLL = 8
ROPE_THETA = 500000.0
ROT_DIM = HEAD_DIM // 4
CONV_WIDTH = 31
MEM_HEADS = 4
MEM_HEAD_DIM = 128
MEM_WIDTH = MEM_HEADS * MEM_HEAD_DIM
N_EXPERT_GROUPS = 4
EXPERTS_PER_GROUP = 8
N_EXPERTS = N_EXPERT_GROUPS * EXPERTS_PER_GROUP
TOP_K = 2
EPS = 1e-6
NEG = -1e30

LANES = 128
ROW_TILE = 256
WIDE_ROW_TILE = 512
MERGE_TILE = 512
IN_PROJ_TILE = 256
MOE_TILE = 512
GMM_BUFS = 4
VMEM_LIMIT = 56 * 1024 * 1024

BF16 = jnp.bfloat16
F32 = jnp.float32


def _params(*sem):
    return pltpu.CompilerParams(dimension_semantics=sem, vmem_limit_bytes=VMEM_LIMIT)


def _dot(a, b):
    return jnp.dot(a, b, preferred_element_type=F32)


def _dot_nt(a, b):
    return lax.dot_general(a, b, (((1,), (1,)), ((), ())), preferred_element_type=F32)


def _dot_tn(a, b):
    return lax.dot_general(a, b, (((0,), (0,)), ((), ())), preferred_element_type=F32)


def _rms(x, g):
    return x * lax.rsqrt(jnp.mean(x * x, axis=-1, keepdims=True) + EPS) * g


def _pack_halves(x):
    c = x.shape[1] // 2
    hi = lax.bitcast_convert_type(x[:, :c].astype(F32), jnp.uint32)
    lo = lax.bitcast_convert_type(x[:, c:].astype(F32), jnp.uint32)
    return hi | (lo >> 16)


def _unpack_halves(p):
    hi = lax.bitcast_convert_type(p & jnp.uint32(0xFFFF0000), F32).astype(BF16)
    lo = lax.bitcast_convert_type(p << 16, F32).astype(BF16)
    return hi, lo


def _const_spec(shape):
    nd = len(shape)
    return pl.BlockSpec(shape, lambda *_: (0,) * nd)


def _in_proj_kernel(x_ref, g_ref, w_ref, qg_ref, kg_ref, mqg_ref, rc_ref, ra_ref, rb_ref,
                    q_ref, k_ref, v_ref, u_ref, mq_ref):
    h = _rms(x_ref[...], g_ref[...]).astype(BF16)
    rc, ra, rb = rc_ref[...], ra_ref[...], rb_ref[...]

    def rope(y):
        return y * rc + pltpu.roll(y, LANES - ROT_DIM // 2, 1) * ra + pltpu.roll(y, ROT_DIM // 2, 1) * rb

    col = 0
    zq = _dot(h, w_ref[:, col:col + N_Q_HEADS * HEAD_DIM])
    for j in range(N_Q_HEADS):
        sl = slice(j * HEAD_DIM, (j + 1) * HEAD_DIM)
        q_ref[:, sl] = rope(_rms(zq[:, sl], qg_ref[...])).astype(BF16)
    col += N_Q_HEADS * HEAD_DIM
    tm = x_ref.shape[0]
    zk = _dot(h, w_ref[:, col:col + ATTN_WIDTH])
    for j in range(N_KV_HEADS):
        sl = slice(j * HEAD_DIM, (j + 1) * HEAD_DIM)
        k_ref[pl.ds(j, tm, stride=N_KV_HEADS), :] = rope(_rms(zk[:, sl], kg_ref[...]))
    col += ATTN_WIDTH
    zv = _dot(h, w_ref[:, col:col + ATTN_WIDTH])
    for j in range(N_KV_HEADS):
        v_ref[pl.ds(j, tm, stride=N_KV_HEADS), :] = zv[:, j * HEAD_DIM:(j + 1) * HEAD_DIM]
    col += ATTN_WIDTH
    conv_ch = u_ref.shape[-1]
    za = _dot(h, w_ref[:, col:col + conv_ch])
    zb = _dot(h, w_ref[:, col + conv_ch:col + 2 * conv_ch])
    u_ref[...] = za * jax.nn.sigmoid(zb)
    col += 2 * conv_ch
    zm = _dot(h, w_ref[:, col:col + MEM_WIDTH])
    for j in range(MEM_HEADS):
        sl = slice(j * MEM_HEAD_DIM, (j + 1) * MEM_HEAD_DIM)
        mq_ref[:, sl] = _rms(zm[:, sl], mqg_ref[...]).astype(BF16)


def _in_proj(x, g_mix, w_bf16, q_g, k_g, mq_g, rope_tabs, conv_ch):
    m, d = x.shape
    tm = min(IN_PROJ_TILE, m)
    n_tab_blocks = rope_tabs[0].shape[0] // tm
    row = lambda i: (i, 0)
    tab = lambda i: (i % n_tab_blocks, 0)
    ncols = w_bf16.shape[1]
    out_shape = (
        jax.ShapeDtypeStruct((m, N_Q_HEADS * HEAD_DIM), BF16),
        jax.ShapeDtypeStruct((m * N_KV_HEADS, HEAD_DIM), F32),
        jax.ShapeDtypeStruct((m * N_KV_HEADS, HEAD_DIM), F32),
        jax.ShapeDtypeStruct((m, conv_ch), F32),
        jax.ShapeDtypeStruct((m, MEM_WIDTH), BF16),
    )
    return pl.pallas_call(
        _in_proj_kernel,
        grid=(m // tm,),
        in_specs=[
            pl.BlockSpec((tm, d), row),
            _const_spec((1, d)),
            _const_spec((d, ncols)),
            _const_spec((1, HEAD_DIM)), _const_spec((1, HEAD_DIM)), _const_spec((1, MEM_HEAD_DIM)),
            pl.BlockSpec((tm, LANES), tab), pl.BlockSpec((tm, LANES), tab), pl.BlockSpec((tm, LANES), tab),
        ],
        out_specs=tuple(pl.BlockSpec((tm * s.shape[0] // m, s.shape[1]), row) for s in out_shape),
        out_shape=out_shape,
        compiler_params=_params("parallel"),
        name="in_proj",
    )(x, g_mix, w_bf16, q_g, k_g, mq_g, *rope_tabs)


def _rope_tables(pos):
    half = ROT_DIM // 2
    inv_freq = jnp.power(jnp.float32(ROPE_THETA), -jnp.arange(half, dtype=F32) * (2.0 / ROT_DIM))
    ang = pos.astype(F32)[:, None] * inv_freq[None, :]
    cos, sin = jnp.cos(ang), jnp.sin(ang)
    n = pos.shape[0]
    ones = jnp.ones((n, LANES - ROT_DIM), F32)
    zeros = jnp.zeros((n, LANES - half), F32)
    rc = jnp.concatenate([cos, cos, ones], axis=1)
    ra = jnp.concatenate([-sin, zeros], axis=1)
    rb = jnp.concatenate([jnp.zeros((n, half), F32), sin, jnp.zeros((n, LANES - ROT_DIM), F32)], axis=1)
    return rc, ra, rb


def _to_residue_layout(dst, src, classes_src, ratio):
    len_src = src.shape[0] // classes_src
    len_dst = len_src // ratio
    for c_src in range(classes_src):
        for a in range(ratio):
            c = c_src + classes_src * a
            dst[c * len_dst:(c + 1) * len_dst, :] = src[pl.ds(c_src * len_src + a, len_dst, stride=ratio), :]


def _prompt_attn_kernel(q0_ref, q1_ref, q2_ref, k_ref, v_ref, o_ref, perm_ref, tmp_ref, acc_ref, lse_ref):
    head = pl.program_id(1)
    seq = q0_ref.shape[0]
    q_refs = (q0_ref, q1_ref, q2_ref)
    dils = [d for _, d in DILATED_GROUPS]
    for t, ref in ((1, k_ref), (2, v_ref)):
        perm_ref[0, t] = ref[pl.ds(head, seq, stride=N_KV_HEADS), :]
        for g in range(1, N_GROUPS):
            _to_residue_layout(perm_ref.at[g, t], perm_ref.at[g - 1, t], dils[g - 1], dils[g] // dils[g - 1])
    for g in range(1, N_GROUPS):
        tmp_ref[0] = q_refs[g][...].astype(F32)
        for step in range(1, g + 1):
            dst = perm_ref.at[g, 0] if step == g else tmp_ref.at[step % 2]
            _to_residue_layout(dst, tmp_ref.at[(step - 1) % 2], dils[step - 1], dils[step] // dils[step - 1])
    iq = lax.broadcasted_iota(jnp.int32, (BAND, BAND), 0)
    ik = lax.broadcasted_iota(jnp.int32, (BAND, BAND), 1)
    keep_c = iq >= ik
    scale = HEAD_DIM ** -0.5

    for g, dil in enumerate(dils):
        nb = seq // (dil * BAND)

        def body(jj, carry, g=g, dil=dil, nb=nb):
            blocks = []
            for u in range(ATTN_UNROLL):
                j = jj * ATTN_UNROLL + u
                rows = pl.ds(pl.multiple_of(j * BAND, BAND), BAND)
                prev = pl.ds(pl.multiple_of(jnp.maximum(j - 1, 0) * BAND, BAND), BAND)
                q = q0_ref[rows, :] if g == 0 else perm_ref[g, 0, rows, :].astype(BF16)
                s_c = jnp.where(keep_c, _dot_nt(q, perm_ref[g, 1, rows, :].astype(BF16)) * scale, NEG)
                s_p = None
                if nb > 1:
                    keep_p = jnp.logical_and(ik >= iq, j % nb > 0)
                    s_p = jnp.where(keep_p, _dot_nt(q, perm_ref[g, 1, prev, :].astype(BF16)) * scale, NEG)
                blocks.append((j, rows, prev, s_c, s_p))
            probs = []
            for j, rows, prev, s_c, s_p in blocks:
                m = jnp.max(s_c if nb == 1 else jnp.maximum(s_c, s_p), axis=-1, keepdims=True)
                p_c = jnp.exp(s_c - m)
                p_p = None if nb == 1 else jnp.exp(s_p - m)
                l = jnp.sum(p_c if nb == 1 else p_c + p_p, axis=-1, keepdims=True)
                probs.append((m, l, p_c.astype(BF16), None if p_p is None else p_p.astype(BF16)))
            for (j, rows, prev, _, _), (m, l, p_c, p_p) in zip(blocks, probs):
                acc = _dot(p_c, perm_ref[g, 2, rows, :].astype(BF16))
                if nb > 1:
                    acc = acc + _dot(p_p, perm_ref[g, 2, prev, :].astype(BF16))
                out_rows = rows if dil == 1 else pl.ds((j % nb) * (BAND * dil) + j // nb, BAND, stride=dil)
                acc_ref[g, out_rows, :] = acc * (1.0 / l)
                lse_ref[g, out_rows, :] = jnp.broadcast_to(m + jnp.log(l), (BAND, LANES))
            return carry

        assert (dil * nb) % ATTN_UNROLL == 0
        lax.fori_loop(0, dil * nb // ATTN_UNROLL, body, 0)

    def combine(c, carry):
        rows = pl.ds(pl.multiple_of(c * BAND, BAND), BAND)
        lses = [lse_ref[g, rows, :] for g in range(N_GROUPS)]
        mx = functools.reduce(jnp.maximum, lses)
        ws = [jnp.exp(l - mx) for l in lses]
        out = functools.reduce(jnp.add, [w * acc_ref[g, rows, :] for g, w in enumerate(ws)])
        o_ref[rows, :] = (out * (1.0 / functools.reduce(jnp.add, ws))).astype(BF16)
        return carry

    lax.fori_loop(0, seq // BAND, combine, 0)


def _prompt_attention(q, k_flat, v_flat, batch, seq):
    for window, dil in DILATED_GROUPS:
        assert window // dil == BAND and seq % (dil * BAND) == 0
    qspec = lambda g: pl.BlockSpec((seq, HEAD_DIM), lambda b, h: (b, g * N_KV_HEADS + h))
    kvspec = pl.BlockSpec((seq * N_KV_HEADS, HEAD_DIM), lambda b, h: (b, 0))
    return pl.pallas_call(
        _prompt_attn_kernel,
        grid=(batch, N_KV_HEADS),
        in_specs=[qspec(0), qspec(1), qspec(2), kvspec, kvspec],
        out_specs=pl.BlockSpec((seq, HEAD_DIM), lambda b, h: (b, h)),
        out_shape=jax.ShapeDtypeStruct((batch * seq, ATTN_WIDTH), BF16),
        scratch_shapes=[pltpu.VMEM((N_GROUPS, 3, seq, HEAD_DIM), F32), pltpu.VMEM((2, seq, HEAD_DIM), F32),
                        pltpu.VMEM((N_GROUPS, seq, HEAD_DIM), F32), pltpu.VMEM((N_GROUPS, seq, LANES), F32)],
        compiler_params=_params("parallel", "arbitrary"),
        name="prompt_attn",
    )(q, q, q, k_flat, v_flat)


def _mem_kv_kernel(x_ref, g_ref, w_ref, kg_ref, k_ref, v_ref):
    tm = x_ref.shape[0]
    h = _rms(x_ref[...], g_ref[...]).astype(BF16)
    zk = _dot(h, w_ref[:, :MEM_WIDTH])
    zv = _dot(h, w_ref[:, MEM_WIDTH:])
    for j in range(MEM_HEADS):
        sl = slice(j * MEM_HEAD_DIM, (j + 1) * MEM_HEAD_DIM)
        k_ref[pl.ds(j, tm, stride=MEM_HEADS), :] = _rms(zk[:, sl], kg_ref[...])
        v_ref[pl.ds(j, tm, stride=MEM_HEADS), :] = zv[:, sl]


def _mem_kv(mem, g, w_bf16, k_g):
    m, d = mem.shape
    tm = min(ROW_TILE, m)
    row = lambda i: (i, 0)
    shp = jax.ShapeDtypeStruct((m * MEM_HEADS, MEM_HEAD_DIM), F32)
    ospec = pl.BlockSpec((tm * MEM_HEADS, MEM_HEAD_DIM), row)
    return pl.pallas_call(
        _mem_kv_kernel,
        grid=(m // tm,),
        in_specs=[pl.BlockSpec((tm, d), row), _const_spec((1, d)), _const_spec((d, 2 * MEM_WIDTH)),
                  _const_spec((1, MEM_HEAD_DIM))],
        out_specs=(ospec, ospec),
        out_shape=(shp, shp),
        compiler_params=_params("parallel"),
        name="mem_kv",
    )(mem, g, w_bf16, k_g)


def _mem_attn_kernel(q_ref, k_ref, v_ref, o_ref):
    scale = MEM_HEAD_DIM ** -0.5
    n_mem = k_ref.shape[0] // MEM_HEADS
    for h in range(MEM_HEADS):
        sl = slice(h * MEM_HEAD_DIM, (h + 1) * MEM_HEAD_DIM)
        head_rows = pl.ds(h, n_mem, stride=MEM_HEADS)
        s = _dot_nt(q_ref[:, sl], k_ref[head_rows, :].astype(BF16)) * scale
        p = jnp.exp(s - jnp.max(s, axis=-1, keepdims=True))
        l = jnp.sum(p, axis=-1, keepdims=True)
        o_ref[:, sl] = (_dot(p.astype(BF16), v_ref[head_rows, :].astype(BF16)) * (1.0 / l)).astype(BF16)


def _mem_attention(mq, mem_k_flat, mem_v_flat, batch, seq):
    rows = mem_k_flat.shape[0] // batch
    tq = min(512, seq)
    nq = seq // tq
    kspec = pl.BlockSpec((rows, MEM_HEAD_DIM), lambda b, i: (b, 0))
    return pl.pallas_call(
        _mem_attn_kernel,
        grid=(batch, nq),
        in_specs=[pl.BlockSpec((tq, MEM_WIDTH), lambda b, i: (b * nq + i, 0)), kspec, kspec],
        out_specs=pl.BlockSpec((tq, MEM_WIDTH), lambda b, i: (b * nq + i, 0)),
        out_shape=jax.ShapeDtypeStruct((batch * seq, MEM_WIDTH), BF16),
        compiler_params=_params("parallel", "parallel"),
        name="mem_attn",
    )(mq, mem_k_flat, mem_v_flat)


CONV_HALO = 32
CONV_CHUNK = 32
CONV_ACC_ROWS = 128
CONV_TAP_UNROLL = 8


def _ln_swish(c, g, b):
    mu = jnp.mean(c, axis=-1, keepdims=True)
    xc = c - mu
    y = xc * lax.rsqrt(jnp.mean(xc * xc, axis=-1, keepdims=True) + EPS) * g + b
    return y * jax.nn.sigmoid(y)


def _conv_prompt_kernel(halo_ref, u_ref, w_ref, b_ref, g_ref, beta_ref, o_ref, ext_ref, conv_ref, wb_ref):
    tc, ch = u_ref.shape
    first = pl.program_id(1) == 0
    lane_tiles = [slice(j * LANES, (j + 1) * LANES) for j in range(ch // LANES)]
    for j, sl in enumerate(lane_tiles):
        halo = jnp.where(first, 0.0, halo_ref[:, sl])
        for dup in range(2):
            ext_ref[j, pl.ds(dup, CONV_HALO, stride=2), :] = halo
            ext_ref[j, pl.ds(2 * CONV_HALO + dup, tc, stride=2), :] = u_ref[:, sl]
    for w in range(CONV_WIDTH):
        wb_ref[w] = jnp.broadcast_to(w_ref[w:w + 1, :], (8, ch))
    lead = CONV_HALO - (CONV_WIDTH - 1)
    rows = CONV_ACC_ROWS
    for j, sl in enumerate(lane_tiles):
        for r0 in range(0, tc, rows):
            def tap(w, acc, j=j, sl=sl, r0=r0):
                win = ext_ref[j, pl.ds(2 * (r0 + lead + w), rows, stride=2), :].reshape(rows // 8, 8, LANES)
                return acc + wb_ref[w, :, sl][None] * win

            acc = lax.fori_loop(0, CONV_WIDTH, tap, jnp.zeros((rows // 8, 8, LANES), F32) + b_ref[:, sl],
                                unroll=CONV_TAP_UNROLL)
            conv_ref[r0:r0 + rows, sl] = acc.reshape(rows, LANES)
    for c0 in range(0, tc, CONV_CHUNK):
        o_ref[c0:c0 + CONV_CHUNK, :] = _ln_swish(conv_ref[c0:c0 + CONV_CHUNK, :], g_ref[...], beta_ref[...]).astype(BF16)


def _conv_prompt(u, conv_w, conv_b, ln_g, ln_b, batch, seq):
    ch = u.shape[1]
    tc = min(ROW_TILE, seq)
    nt = seq // tc
    ratio = tc // CONV_HALO
    return pl.pallas_call(
        _conv_prompt_kernel,
        grid=(batch, nt),
        in_specs=[
            pl.BlockSpec((CONV_HALO, ch), lambda b, i: (jnp.maximum((b * nt + i) * ratio - 1, 0), 0)),
            pl.BlockSpec((tc, ch), lambda b, i: (b * nt + i, 0)),
            _const_spec((CONV_WIDTH, ch)), _const_spec((1, ch)), _const_spec((1, ch)), _const_spec((1, ch)),
        ],
        out_specs=pl.BlockSpec((tc, ch), lambda b, i: (b * nt + i, 0)),
        out_shape=jax.ShapeDtypeStruct((batch * seq, ch), BF16),
        scratch_shapes=[pltpu.VMEM((ch // LANES, 2 * (CONV_HALO + tc), LANES), F32), pltpu.VMEM((tc, ch), F32),
                        pltpu.VMEM((CONV_WIDTH, 8, ch), F32)],
        compiler_params=_params("parallel", "parallel"),
        name="conv_prompt",
    )(u, u, conv_w, conv_b, ln_g, ln_b)


def _conv_sample_kernel(state_ref, new_ref, w_ref, b_ref, g_ref, beta_ref, o_ref, state_out_ref, ext_ref):
    nb, ctx, ch = state_ref.shape
    t = new_ref.shape[1]
    ext_ref[:, 0:ctx, :] = state_ref[...]
    ext_ref[:, ctx:ctx + t, :] = new_ref[...]
    acc = jnp.zeros((nb, t, ch), F32) + b_ref[...]
    for w in range(CONV_WIDTH):
        acc = acc + w_ref[w:w + 1, :] * ext_ref[:, w:w + t, :]
    o_ref[...] = _ln_swish(acc, g_ref[...], beta_ref[...])
    state_out_ref[...] = ext_ref[:, t:t + ctx, :]


def _conv_sample(state, u_new, conv_w, conv_b, ln_g, ln_b):
    batch, ctx, ch = state.shape
    t = u_new.shape[1]
    nb = 8
    blk = lambda n: pl.BlockSpec((nb, n, ch), lambda i: (i, 0, 0))
    return pl.pallas_call(
        _conv_sample_kernel,
        grid=(batch // nb,),
        in_specs=[blk(ctx), blk(t), _const_spec((CONV_WIDTH, ch)), _const_spec((1, ch)), _const_spec((1, ch)),
                  _const_spec((1, ch))],
        out_specs=(blk(t), blk(ctx)),
        out_shape=(jax.ShapeDtypeStruct((batch, t, ch), F32), jax.ShapeDtypeStruct((batch, ctx, ch), F32)),
        scratch_shapes=[pltpu.VMEM((nb, ctx + t + 6, ch), F32)],
        compiler_params=_params("parallel"),
        name="conv_sample",
    )(state, u_new, conv_w, conv_b, ln_g, ln_b)


GROUP_LANES = 16


def _spread_groups(vec, combine):
    t = combine(combine(vec, pltpu.roll(vec, LANES - GROUP_LANES, 1)), pltpu.roll(vec, LANES - 2 * GROUP_LANES, 1))
    lane = lax.broadcasted_iota(jnp.int32, vec.shape, 1)
    return jnp.where(lane < GROUP_LANES, t,
                     jnp.where(lane < 2 * GROUP_LANES, pltpu.roll(t, GROUP_LANES, 1), pltpu.roll(t, 2 * GROUP_LANES, 1)))


SLAB_POS = DILATED_GROUPS[-1][1]
SLAB_ROWS = SLAB_POS * N_KV_HEADS
TAIL_POS = max(w for w, d in DILATED_GROUPS if d < SLAB_POS)


def _sample_attn_kernel(kc_ref, vc_ref, kn_ref, vn_ref, wq_ref, bias_d_ref, bias_t_ref, bias_n_ref, mk_ref, mv_ref,
                        wm_ref, bias_m_ref, kw_ref, vw_ref, a_ref, m_ref, kx_ref, vx_ref):
    n_slab = kc_ref.shape[1]
    new_rows = kn_ref.shape[1]
    t = new_rows // N_KV_HEADS
    tail_slabs = TAIL_POS // SLAB_POS
    n_used = N_GROUPS * GROUP_LANES
    lane = lax.broadcasted_iota(jnp.int32, (1, LANES), 1)

    for src, new, dst in ((kc_ref, kn_ref, kw_ref), (vc_ref, vn_ref, vw_ref)):
        dst[0, :, 0:SLAB_ROWS - new_rows, :] = src[0, :, new_rows:SLAB_ROWS, :]
        dst[0, 0:n_slab - 1, SLAB_ROWS - new_rows:SLAB_ROWS, :] = src[0, 1:n_slab, 0:new_rows, :]
        dst[0, n_slab - 1, SLAB_ROWS - new_rows:SLAB_ROWS, :] = new[0]

    kx_ref[...] = jnp.zeros(kx_ref.shape, F32)
    vx_ref[...] = jnp.zeros(vx_ref.shape, F32)
    kx_ref[0:new_rows, :] = kn_ref[0]
    vx_ref[0:new_rows, :] = vn_ref[0]

    def dilated(ref):
        return ref[0, :, 0:new_rows, :].reshape(n_slab * new_rows, HEAD_DIM).astype(BF16)

    def tail(ref):
        return ref[0, n_slab - tail_slabs:n_slab, :, :].reshape(tail_slabs * SLAB_ROWS, HEAD_DIM).astype(BF16)

    scale = HEAD_DIM ** -0.5
    wq = wq_ref[0]
    s_d = _dot(dilated(kc_ref), wq) * scale + bias_d_ref[...]
    s_t = _dot(tail(kc_ref), wq) * scale + bias_t_ref[...]
    s_n = _dot(kx_ref[...].astype(BF16), wq) * scale + bias_n_ref[...]
    col_max = lambda s: jnp.max(s, axis=0, keepdims=True)
    col_sum = lambda p: jnp.sum(p, axis=0, keepdims=True)
    m_col = jnp.maximum(jnp.maximum(col_max(s_d), col_max(s_t)), col_max(s_n))
    m_joint = jnp.where(lane < n_used, _spread_groups(m_col, jnp.maximum), 0.0)
    p_d, p_t, p_n = jnp.exp(s_d - m_joint), jnp.exp(s_t - m_joint), jnp.exp(s_n - m_joint)
    l_col = col_sum(p_d) + col_sum(p_t) + col_sum(p_n)
    inv = 1.0 / jnp.where(lane < n_used, _spread_groups(l_col, jnp.add), 1.0)
    o = (_dot_tn((p_d * inv).astype(BF16), dilated(vc_ref)) + _dot_tn((p_t * inv).astype(BF16), tail(vc_ref))
         + _dot_tn((p_n * inv).astype(BF16), vx_ref[...].astype(BF16)))
    for h in range(N_KV_HEADS):
        r = h * t
        a_ref[0, :, h * HEAD_DIM:(h + 1) * HEAD_DIM] = functools.reduce(
            jnp.add, [o[g * GROUP_LANES + r:g * GROUP_LANES + r + t, :] for g in range(N_GROUPS)])

    sm = _dot(mk_ref[0].astype(BF16), wm_ref[0]) * (MEM_HEAD_DIM ** -0.5) + bias_m_ref[...]
    pm = jnp.exp(sm - jnp.where(lane < MEM_HEADS * t, col_max(sm), 0.0))
    lm = jnp.where(lane < MEM_HEADS * t, col_sum(pm), 1.0)
    om = _dot_tn((pm * (1.0 / lm)).astype(BF16), mv_ref[0].astype(BF16))
    for h in range(MEM_HEADS):
        m_ref[0, :, h * MEM_HEAD_DIM:(h + 1) * MEM_HEAD_DIM] = om[h * t:(h + 1) * t, :]


def _sample_masks(cache_len, t, n_mem):
    col = np.arange(LANES)[None, :]
    g, c_head, c_tok = col // GROUP_LANES, (col % GROUP_LANES) // t, (col % GROUP_LANES) % t
    used = (col < N_GROUPS * GROUP_LANES) & (col % GROUP_LANES < N_KV_HEADS * t)
    pad = [1] * (LANES // GROUP_LANES - N_GROUPS)
    dil = np.array([d for _, d in DILATED_GROUPS] + pad)[g]
    win = np.array([w for w, _ in DILATED_GROUPS] + pad)[g]
    sparse = dil >= SLAB_POS

    def keep(pos, head, group_sel):
        dist = cache_len + c_tok - pos
        return used & group_sel & (head == c_head) & (dist >= 0) & (dist % dil == 0) & (dist <= win)

    n_slab = cache_len // SLAB_POS
    y = np.arange(n_slab * t * N_KV_HEADS)[:, None]
    keep_d = keep((y // (t * N_KV_HEADS)) * SLAB_POS + (y % (t * N_KV_HEADS)) // N_KV_HEADS, y % N_KV_HEADS, sparse)
    x = np.arange(TAIL_POS * N_KV_HEADS)[:, None]
    keep_t = keep(cache_len - TAIL_POS + x // N_KV_HEADS, x % N_KV_HEADS, ~sparse)
    z = np.arange(LANES)[:, None]
    keep_n = keep(cache_len + z // N_KV_HEADS, z % N_KV_HEADS, True) & (z < t * N_KV_HEADS)
    w = np.arange(n_mem * MEM_HEADS)[:, None]
    keep_m = (col < MEM_HEADS * t) & (w % MEM_HEADS == col // t)
    return tuple(jnp.asarray(np.where(k, 0.0, NEG), F32) for k in (keep_d, keep_t, keep_n, keep_m))


def _query_columns(q, batch, t, n_groups):
    qt = q.reshape(batch, t, n_groups, N_KV_HEADS, HEAD_DIM).transpose(0, 4, 2, 3, 1)
    qt = qt.reshape(batch, HEAD_DIM, n_groups, N_KV_HEADS * t)
    qt = jnp.pad(qt, ((0, 0), (0, 0), (0, 0), (0, GROUP_LANES - N_KV_HEADS * t)))
    qt = qt.reshape(batch, HEAD_DIM, n_groups * GROUP_LANES)
    return jnp.pad(qt, ((0, 0), (0, 0), (0, LANES - n_groups * GROUP_LANES)))


def _sample_attention(q, k_new, v_new, mq, cache_k, cache_v, mem_k, mem_v):
    batch, cache_len = cache_k.shape[0], cache_k.shape[1]
    new_rows = k_new.shape[0] // batch
    t = new_rows // N_KV_HEADS
    n_mem = mem_k.shape[1]
    assert N_KV_HEADS * t <= GROUP_LANES and t <= SLAB_POS and new_rows % 8 == 0
    assert cache_len % SLAB_POS == 0 and cache_len >= max(w for w, _ in DILATED_GROUPS)
    assert all(d == SLAB_POS or w <= TAIL_POS for w, d in DILATED_GROUPS)
    n_slab = cache_len // SLAB_POS
    wq = _query_columns(q, batch, t, N_GROUPS)
    wm = _query_columns(mq, batch, t, 1)
    bias_d, bias_t, bias_n, bias_m = _sample_masks(cache_len, t, n_mem)
    per_b = lambda *shape: pl.BlockSpec((1,) + shape, lambda b: (b,) + (0,) * len(shape))
    slabs = lambda c: c.reshape(batch, n_slab, SLAB_ROWS, HEAD_DIM)
    mem_rows = n_mem * MEM_HEADS
    win = jax.ShapeDtypeStruct((batch, n_slab, SLAB_ROWS, HEAD_DIM), F32)
    k_win, v_win, a, m = pl.pallas_call(
        _sample_attn_kernel,
        grid=(batch,),
        in_specs=[
            per_b(n_slab, SLAB_ROWS, HEAD_DIM), per_b(n_slab, SLAB_ROWS, HEAD_DIM),
            per_b(new_rows, HEAD_DIM), per_b(new_rows, HEAD_DIM), per_b(HEAD_DIM, LANES),
            _const_spec(bias_d.shape), _const_spec(bias_t.shape), _const_spec(bias_n.shape),
            per_b(mem_rows, MEM_HEAD_DIM), per_b(mem_rows, MEM_HEAD_DIM), per_b(MEM_HEAD_DIM, LANES),
            _const_spec(bias_m.shape),
        ],
        out_specs=(per_b(n_slab, SLAB_ROWS, HEAD_DIM), per_b(n_slab, SLAB_ROWS, HEAD_DIM), per_b(t, ATTN_WIDTH),
                   per_b(t, MEM_WIDTH)),
        out_shape=(win, win, jax.ShapeDtypeStruct((batch, t, ATTN_WIDTH), F32),
                   jax.ShapeDtypeStruct((batch, t, MEM_WIDTH), F32)),
        scratch_shapes=[pltpu.VMEM((LANES, HEAD_DIM), F32), pltpu.VMEM((LANES, HEAD_DIM), F32)],
        compiler_params=_params("parallel"),
        name="sample_attn",
    )(slabs(cache_k), slabs(cache_v), k_new.reshape(batch, new_rows, HEAD_DIM), v_new.reshape(batch, new_rows, HEAD_DIM),
      wq, bias_d, bias_t, bias_n, mem_k.reshape(batch, mem_rows, MEM_HEAD_DIM),
      mem_v.reshape(batch, mem_rows, MEM_HEAD_DIM), wm, bias_m)
    return k_win.reshape(cache_k.shape), v_win.reshape(cache_v.shape), a, m


ROUTE_GROUP_LANE0 = 0
ROUTE_EXPERT_LANE0 = N_EXPERT_GROUPS


def _route(logits):
    lane = lax.broadcasted_iota(jnp.int32, logits.shape, 1).astype(F32)
    big = float(LANES)

    def masked_softmax(keep):
        z = jnp.where(keep, logits, NEG)
        e = jnp.where(keep, jnp.exp(z - jnp.max(z, axis=-1, keepdims=True)), 0.0)
        return e / jnp.sum(e, axis=-1, keepdims=True)

    def first_argmax(vals, keep):
        top = jnp.max(jnp.where(keep, vals, -1.0), axis=-1, keepdims=True)
        idx = jnp.min(jnp.where(jnp.logical_and(keep, vals == top), lane, big), axis=-1, keepdims=True)
        return top, idx

    is_group = lane < N_EXPERT_GROUPS
    pg = masked_softmax(is_group)
    pg_top, g_idx = first_argmax(pg, is_group)
    lo = ROUTE_EXPERT_LANE0 + g_idx * EXPERTS_PER_GROUP
    in_group = jnp.logical_and(lane >= lo, lane < lo + EXPERTS_PER_GROUP)
    pe = masked_softmax(in_group)
    p1, i1 = first_argmax(pe, in_group)
    p2, i2 = first_argmax(pe, jnp.logical_and(in_group, lane != i1))
    denom = p1 + p2
    w1 = pg_top * p1 / denom
    w2 = pg_top * p2 / denom
    return lane, i1 - ROUTE_EXPERT_LANE0, i2 - ROUTE_EXPERT_LANE0, w1, w2


def _merge_kernel(x_ref, gmix_ref, ao_ref, cc_ref, mo_ref, wgate_ref, wa_ref, wc_ref, wm_ref, wo_ref, gffn_ref, wr_ref,
                  br_ref, cnt_in_ref, x1_ref, h2_ref, route_ref, cnt_out_ref, cnt_ref):
    @pl.when(pl.program_id(0) == 0)
    def _():
        cnt_ref[...] = cnt_in_ref[...]

    d = x_ref.shape[1]
    x = x_ref[...]
    h = _rms(x, gmix_ref[...]).astype(BF16)
    a = _dot(ao_ref[...].astype(BF16), wa_ref[...])
    c = _dot(cc_ref[...].astype(BF16), wc_ref[...])
    m = _dot(mo_ref[...].astype(BF16), wm_ref[...])
    z = jax.nn.sigmoid(_dot(h, wgate_ref[:, 0:d])) * a
    z = z + jax.nn.sigmoid(_dot(h, wgate_ref[:, d:2 * d])) * c
    z = z + jax.nn.sigmoid(_dot(h, wgate_ref[:, 2 * d:3 * d])) * m
    x1 = x + _dot(z.astype(BF16), wo_ref[...])
    x1_ref[...] = x1
    h2 = _rms(x1, gffn_ref[...]).astype(BF16)
    packed = _pack_halves(h2)
    for c in range(packed.shape[1] // LANES):
        h2_ref[pl.ds(c, x_ref.shape[0], stride=packed.shape[1] // LANES), :] = packed[:, c * LANES:(c + 1) * LANES]
    logits = _dot(h2, wr_ref[...]) + br_ref[...]
    lane, e1, e2, w1, w2 = _route(logits)

    tm = x_ref.shape[0]
    hit1 = jnp.where(lane == e1, 1.0, 0.0)
    hit2 = jnp.where(lane == e2, 1.0, 0.0)
    hits = hit1 + hit2
    earlier = (lax.broadcasted_iota(jnp.int32, (tm, tm), 1) < lax.broadcasted_iota(jnp.int32, (tm, tm), 0))
    before = _dot(jnp.where(earlier, 1.0, 0.0).astype(BF16), hits.astype(BF16)) + cnt_ref[...]
    rank1 = jnp.sum(hit1 * before, axis=-1, keepdims=True)
    rank2 = jnp.sum(hit2 * before, axis=-1, keepdims=True)
    cnt_ref[...] = cnt_ref[...] + jnp.sum(hits, axis=0, keepdims=True)
    cnt_out_ref[...] = cnt_ref[...]
    route = jnp.zeros(logits.shape, F32)
    for i, val in enumerate((e1, e2, w1, w2, rank1, rank2)):
        route = jnp.where(lane == i, val, route)
    route_ref[...] = route


ROUTE_E, ROUTE_W, ROUTE_RANK = 0, 2, 4


def _merge(x, g_mix, ao, cc, mo, w_gate, w_a, w_c, w_m, w_o, g_ffn, w_r, b_r, counts):
    m, d = x.shape
    tm = min(MERGE_TILE, m)
    row = lambda i: (i, 0)
    rows = lambda arr: pl.BlockSpec((tm, arr.shape[1]), row)
    ins = [x, g_mix, ao, cc, mo, w_gate, w_a, w_c, w_m, w_o, g_ffn, w_r, b_r, counts]
    specs = ([rows(x), _const_spec(g_mix.shape), rows(ao), rows(cc), rows(mo)]
             + [_const_spec(a.shape) for a in (w_gate, w_a, w_c, w_m, w_o, g_ffn, w_r, b_r, counts)])
    out_shape = (jax.ShapeDtypeStruct((m, d), F32), jax.ShapeDtypeStruct((m * (d // 2 // LANES), LANES), jnp.uint32),
                 jax.ShapeDtypeStruct((m, LANES), F32), jax.ShapeDtypeStruct((1, LANES), F32))
    return pl.pallas_call(
        _merge_kernel,
        grid=(m // tm,),
        in_specs=specs,
        out_specs=(rows(out_shape[0]), pl.BlockSpec((tm * (d // 2 // LANES), LANES), row), rows(out_shape[2]),
                   _const_spec((1, LANES))),
        out_shape=out_shape,
        scratch_shapes=[pltpu.VMEM((1, LANES), F32)],
        compiler_params=_params("arbitrary"),
        name="merge",
    )(*ins)


def _routing_tables(counts, routes):
    cnt = counts[0, :N_EXPERTS].astype(jnp.int32)
    tiles = (cnt + MOE_TILE - 1) // MOE_TILE
    tile_start = jnp.cumsum(tiles) - tiles
    row_start = tile_start * MOE_TILE
    ids = routes[:, ROUTE_E:ROUTE_E + TOP_K].astype(jnp.int32)
    start_of = jnp.sum(jnp.where(ids[:, :, None] == jnp.arange(N_EXPERTS)[None, None, :], row_start[None, None, :], 0),
                       axis=-1)
    pos = start_of + routes[:, ROUTE_RANK:ROUTE_RANK + TOP_K].astype(jnp.int32)
    return pos.reshape(-1), tile_start.astype(jnp.int32), tiles.astype(jnp.int32)


def _dispatch_kernel(pos_ref, h_ref, xs_in, xs_out, sem, *, row0):
    del xs_in
    tm = h_ref.shape[0]
    base = (row0 + pl.program_id(0) * tm) * TOP_K

    def issue(j, carry):
        for k in range(TOP_K):
            slot = pos_ref[base + j * TOP_K + k]
            pltpu.make_async_copy(h_ref.at[j], xs_out.at[slot], sem).start()
        return carry

    lax.fori_loop(0, tm, issue, 0, unroll=8)
    for k in range(TOP_K):
        pltpu.make_async_copy(h_ref, xs_out.at[pl.ds(0, tm)], sem).wait()


def _dispatch(h2, pos, xs, row0):
    m = h2.shape[0]
    tm = min(WIDE_ROW_TILE, m)
    return pl.pallas_call(
        functools.partial(_dispatch_kernel, row0=row0),
        grid_spec=pltpu.PrefetchScalarGridSpec(
            num_scalar_prefetch=1,
            grid=(m // tm,),
            in_specs=[pl.BlockSpec((tm,) + h2.shape[1:], lambda i, *_: (i, 0, 0)), pl.BlockSpec(memory_space=pl.ANY)],
            out_specs=pl.BlockSpec(memory_space=pl.ANY),
            scratch_shapes=[pltpu.SemaphoreType.DMA(())],
        ),
        out_shape=jax.ShapeDtypeStruct(xs.shape, xs.dtype),
        input_output_aliases={2: 0},
        compiler_params=_params("arbitrary"),
        name="moe_dispatch",
    )(pos, h2, xs)


def _gmm_kernel(start_ref, count_ref, xs_hbm, wg_ref, wu_ref, wd_ref, ys_hbm, xbuf, obuf, wg_s, wu_s, wd_s,
                in_sems, out_sems):
    e = pl.program_id(0)
    nb = GMM_BUFS
    first, n = start_ref[e], count_ref[e]

    x_rows, y_rows = xbuf.shape[1], obuf.shape[1]
    x_tiles, y_tiles = x_rows // MOE_TILE, y_rows // MOE_TILE

    def copies(expert_first):
        def rows(i, n_rows):
            return pl.ds(pl.multiple_of((expert_first + i) * n_rows, n_rows), n_rows)

        def in_copy(i):
            return pltpu.make_async_copy(xs_hbm.at[rows(i, x_rows)], xbuf.at[i % nb], in_sems.at[i % nb])

        def out_copy(i):
            return pltpu.make_async_copy(obuf.at[i % nb], ys_hbm.at[rows(i, y_rows)], out_sems.at[i % nb])

        return in_copy, out_copy

    in_copy, out_copy = copies(first)

    def start_first_reads(expert):
        copy, _ = copies(start_ref[expert])
        lax.fori_loop(0, jnp.minimum(count_ref[expert], nb - 1), lambda i, c: (copy(i).start(), c)[1], 0)

    @pl.when(e == 0)
    def _():
        start_first_reads(0)

    @pl.when(n > 0)
    def _():
        wg_s[...] = wg_ref[0].astype(BF16)
        wu_s[...] = wu_ref[0].astype(BF16)
        wd_s[...] = wd_ref[0].astype(BF16)
        half = x_tiles * LANES

        def tile(i, carry):
            slot = i % nb
            in_copy(i).wait()

            @pl.when(i + nb - 1 < n)
            def _():
                in_copy(i + nb - 1).start()

            @pl.when(i >= nb)
            def _():
                out_copy(i - nb).wait()

            packed = jnp.concatenate([xbuf[slot, pl.ds(c, MOE_TILE, stride=x_tiles), :] for c in range(x_tiles)], axis=1)
            x_hi, x_lo = _unpack_halves(packed)
            gate = _dot(x_hi, wg_s[0:half, :]) + _dot(x_lo, wg_s[half:, :])
            up = _dot(x_hi, wu_s[0:half, :]) + _dot(x_lo, wu_s[half:, :])
            hid = gate * jax.nn.sigmoid(gate) * up
            res = _dot(hid.astype(BF16), wd_s[...])
            for c in range(y_tiles):
                obuf[slot, pl.ds(c, MOE_TILE, stride=y_tiles), :] = res[:, c * LANES:(c + 1) * LANES]
            out_copy(i).start()
            return carry

        lax.fori_loop(0, n, tile, 0)
        lax.fori_loop(jnp.maximum(n - nb, 0), n, lambda i, c: (out_copy(i).wait(), c)[1], 0)

    @pl.when(e + 1 < pl.num_programs(0))
    def _():
        start_first_reads(e + 1)

    @pl.when(e == pl.num_programs(0) - 1)
    def _():
        obuf[0] = jnp.zeros(obuf.shape[1:], F32)

        def fill(t, carry):
            dst = ys_hbm.at[pl.ds(pl.multiple_of(t * y_rows, y_rows), y_rows)]
            copy = pltpu.make_async_copy(obuf.at[0], dst, out_sems.at[0])
            copy.start()
            copy.wait()
            return carry

        lax.fori_loop(first + n, ys_hbm.shape[0] // y_rows, fill, 0)


def _grouped_mlp(xs, tile_start, tile_count, w_gate, w_up, w_down):
    d, ff = w_gate.shape[1], w_gate.shape[2]
    x_tiles, y_tiles = d // 2 // LANES, d // LANES
    n_slots = xs.shape[0] // x_tiles
    wspec = lambda a, b: pl.BlockSpec((1, a, b), lambda e, *_: (e, 0, 0))
    any_spec = pl.BlockSpec(memory_space=pl.ANY)
    return pl.pallas_call(
        _gmm_kernel,
        grid_spec=pltpu.PrefetchScalarGridSpec(
            num_scalar_prefetch=2,
            grid=(N_EXPERTS,),
            in_specs=[any_spec, wspec(d, ff), wspec(d, ff), wspec(ff, d)],
            out_specs=any_spec,
            scratch_shapes=[pltpu.VMEM((GMM_BUFS, MOE_TILE * x_tiles, LANES), xs.dtype),
                            pltpu.VMEM((GMM_BUFS, MOE_TILE * y_tiles, LANES), F32),
                            pltpu.VMEM((d, ff), BF16), pltpu.VMEM((d, ff), BF16), pltpu.VMEM((ff, d), BF16),
                            pltpu.SemaphoreType.DMA((GMM_BUFS,)), pltpu.SemaphoreType.DMA((GMM_BUFS,))],
        ),
        out_shape=jax.ShapeDtypeStruct((n_slots * y_tiles, LANES), F32),
        compiler_params=_params("arbitrary"),
        name="moe_gmm",
    )(tile_start, tile_count, xs, w_gate, w_up, w_down)


def _combine_kernel(pos_ref, x1_ref, route_ref, ys_hbm, ys_flat, y_ref, buf_ref, sems, *, row0):
    tm = x1_ref.shape[0]
    y_tiles = ys_hbm.shape[1]
    i = pl.program_id(0)

    def gather(tile, half):
        base = (row0 + tile * tm) * TOP_K

        def issue(j, carry):
            for k in range(TOP_K):
                slot = pos_ref[base + j * TOP_K + k]
                dst = buf_ref.at[half, k, pl.ds(pl.multiple_of(j * y_tiles, y_tiles), y_tiles)]
                pltpu.make_async_copy(ys_hbm.at[slot], dst, sems.at[half]).start()
            return carry

        lax.fori_loop(0, tm, issue, 0, unroll=4)

    @pl.when(i == 0)
    def _():
        gather(0, 0)

    @pl.when(i + 1 < pl.num_programs(0))
    def _():
        gather(i + 1, (i + 1) % 2)

    half = i % 2
    for k in range(TOP_K):
        pltpu.make_async_copy(ys_flat.at[pl.ds(0, tm * y_tiles)], buf_ref.at[half, k], sems.at[half]).wait()
    route = route_ref[...]
    for c in range(y_tiles):
        sl = slice(c * LANES, (c + 1) * LANES)
        rows = pl.ds(c, tm, stride=y_tiles)
        y_ref[:, sl] = (x1_ref[:, sl] + route[:, ROUTE_W:ROUTE_W + 1] * buf_ref[half, 0, rows, :]
                        + route[:, ROUTE_W + 1:ROUTE_W + 2] * buf_ref[half, 1, rows, :])


def _combine(x1, route, pos, ys, row0):
    m, d = x1.shape
    tm = min(WIDE_ROW_TILE, m)
    row = lambda i, *_: (i, 0)
    any_spec = pl.BlockSpec(memory_space=pl.ANY)
    return pl.pallas_call(
        functools.partial(_combine_kernel, row0=row0),
        grid_spec=pltpu.PrefetchScalarGridSpec(
            num_scalar_prefetch=1,
            grid=(m // tm,),
            in_specs=[pl.BlockSpec((tm, d), row), pl.BlockSpec((tm, LANES), row), any_spec, any_spec],
            out_specs=pl.BlockSpec((tm, d), row),
            scratch_shapes=[pltpu.VMEM((2, TOP_K, tm * (d // LANES), LANES), F32), pltpu.SemaphoreType.DMA((2,))],
        ),
        out_shape=jax.ShapeDtypeStruct((m, d), F32),
        compiler_params=_params("arbitrary"),
        name="moe_combine",
    )(pos, x1, route, ys.reshape(-1, d // LANES, LANES), ys)


def _layer(layer, x_prompt, x_sample, mem_prompt, cache_k, cache_v, state_conv, cache_mem_k, cache_mem_v, p):
    batch, seq, d = x_prompt.shape
    dec_batch, dec_seq, _ = x_sample.shape
    conv_ch = p["conv_w"].shape[-1]
    n_in = N_Q_HEADS * HEAD_DIM + 2 * ATTN_WIDTH + 2 * conv_ch + MEM_WIDTH
    past_len = cache_k.shape[2]

    row2 = lambda name: p[name][layer][None, :]
    w_in = p["w_in"][layer]
    w_main = w_in[:, :n_in].astype(BF16)
    w_gate = w_in[:, n_in:].astype(BF16)
    w_a, w_c, w_m, w_o = (p[n][layer].astype(BF16) for n in ("w_attn_proj", "w_conv_proj", "w_mem_proj", "w_out"))
    w_router = jnp.concatenate(
        [p["w_router_group"][layer], p["w_router_expert"][layer].transpose(1, 0, 2).reshape(d, N_EXPERTS)], axis=1)
    w_router = jnp.pad(w_router, ((0, 0), (0, LANES - w_router.shape[1]))).astype(BF16)
    b_router = jnp.concatenate([p["b_router_group"][layer], p["b_router_expert"][layer].reshape(-1)])
    b_router = jnp.pad(b_router, (0, LANES - b_router.shape[0]))[None, :]
    conv_args = (p["conv_w"][layer], row2("conv_b"), row2("conv_ln_g"), row2("conv_ln_b"))
    merge_w = (w_gate, w_a, w_c, w_m, w_o, row2("norm_ffn_g"), w_router, b_router)

    xp = x_prompt.reshape(batch * seq, d)
    tabs_p = _rope_tables(jnp.arange(seq, dtype=jnp.int32))
    q_p, k_p, v_p, u_p, mq_p = _in_proj(xp, row2("norm_mix_g"), w_main, row2("q_norm_g"), row2("k_norm_g"),
                                        row2("mq_norm_g"), tabs_p, conv_ch)
    ao_p = _prompt_attention(q_p, k_p, v_p, batch, seq)
    mem_k_p, mem_v_p = _mem_kv(mem_prompt.reshape(-1, d), row2("mem_norm_g"), p["w_mem_kv"][layer].astype(BF16),
                               row2("mk_norm_g"))
    mo_p = _mem_attention(mq_p, mem_k_p, mem_v_p, batch, seq)
    cc_p = _conv_prompt(u_p, *conv_args, batch, seq)
    x1_p, h2_p, route_p, counts = _merge(xp, row2("norm_mix_g"), ao_p, cc_p, mo_p, *merge_w,
                                         jnp.zeros((1, LANES), F32))

    xs = x_sample.reshape(dec_batch * dec_seq, d)
    tabs_s = _rope_tables(jnp.tile(past_len + jnp.arange(dec_seq, dtype=jnp.int32), dec_batch))
    q_s, k_s, v_s, u_s, mq_s = _in_proj(xs, row2("norm_mix_g"), w_main, row2("q_norm_g"), row2("k_norm_g"),
                                        row2("mq_norm_g"), tabs_s, conv_ch)
    k_win_s, v_win_s, a_s, mo_s = _sample_attention(q_s, k_s, v_s, mq_s, cache_k[layer], cache_v[layer],
                                                    cache_mem_k[layer], cache_mem_v[layer])
    cc_s, conv_state_s = _conv_sample(state_conv[layer], u_s.reshape(dec_batch, dec_seq, conv_ch), *conv_args)
    x1_s, h2_s, route_s, counts = _merge(xs, row2("norm_mix_g"), a_s.reshape(-1, ATTN_WIDTH),
                                         cc_s.reshape(-1, conv_ch), mo_s.reshape(-1, MEM_WIDTH), *merge_w, counts)

    n_p, n_s = xp.shape[0], xs.shape[0]
    n_tok = n_p + n_s
    n_tiles = (TOP_K * n_tok + N_EXPERTS * (MOE_TILE - 1)) // MOE_TILE + 1
    pos, tile_start, tile_count = _routing_tables(counts, jnp.concatenate([route_p, route_s], axis=0))
    per_token = lambda h: h.reshape(-1, d // 2 // LANES, LANES)
    slots = _dispatch(per_token(h2_p), pos, jnp.zeros((n_tiles * MOE_TILE, d // 2 // LANES, LANES), h2_p.dtype), 0)
    slots = _dispatch(per_token(h2_s), pos, slots, n_p)
    ys = _grouped_mlp(slots.reshape(-1, LANES), tile_start, tile_count, p["w_expert_gate"][layer],
                      p["w_expert_up"][layer], p["w_expert_down"][layer])
    y_p = _combine(x1_p, route_p, pos, ys, 0)
    y_s = _combine(x1_s, route_s, pos, ys, n_p)

    state_p = (k_p.reshape(batch, seq, N_KV_HEADS, HEAD_DIM), v_p.reshape(batch, seq, N_KV_HEADS, HEAD_DIM),
               u_p.reshape(batch, seq, conv_ch)[:, seq - (CONV_WIDTH - 1):],
               mem_k_p.reshape(batch, -1, MEM_HEADS, MEM_HEAD_DIM), mem_v_p.reshape(batch, -1, MEM_HEADS, MEM_HEAD_DIM))
    state_s = (k_win_s, v_win_s, conv_state_s)
    return y_p.reshape(batch, seq, d), y_s.reshape(dec_batch, dec_seq, d), state_p, state_s


def kernel(x_prompt, x_sample, mem_prompt, cache_k, cache_v, state_conv, cache_mem_k, cache_mem_v, norm_mix_g, w_in, q_norm_g, k_norm_g, conv_w, conv_b, conv_ln_g, conv_ln_b, mem_norm_g, w_mem_kv, mq_norm_g, mk_norm_g, w_attn_proj, w_conv_proj, w_mem_proj, w_out, norm_ffn_g, w_router_group, b_router_group, w_router_expert, b_router_expert, w_expert_gate, w_expert_up, w_expert_down):
    p = dict(norm_mix_g=norm_mix_g, w_in=w_in, q_norm_g=q_norm_g, k_norm_g=k_norm_g, conv_w=conv_w, conv_b=conv_b,
             conv_ln_g=conv_ln_g, conv_ln_b=conv_ln_b, mem_norm_g=mem_norm_g, w_mem_kv=w_mem_kv, mq_norm_g=mq_norm_g,
             mk_norm_g=mk_norm_g, w_attn_proj=w_attn_proj, w_conv_proj=w_conv_proj, w_mem_proj=w_mem_proj,
             w_out=w_out, norm_ffn_g=norm_ffn_g, w_router_group=w_router_group, b_router_group=b_router_group,
             w_router_expert=w_router_expert, b_router_expert=b_router_expert, w_expert_gate=w_expert_gate,
             w_expert_up=w_expert_up, w_expert_down=w_expert_down)
    depth = w_in.shape[0]
    seq = x_prompt.shape[1]
    assert seq <= max(w for w, _ in DILATED_GROUPS)
    y_p, y_s = x_prompt, x_sample
    states_p, states_s = [], []
    for layer in range(depth):
        y_p, y_s, st_p, st_s = _layer(layer, y_p, y_s, mem_prompt, cache_k, cache_v, state_conv, cache_mem_k,
                                      cache_mem_v, p)
        states_p.append(st_p)
        states_s.append(st_s)
    stack = lambda states, i: jnp.stack([s[i] for s in states], axis=0)
    return (y_p, y_s, stack(states_p, 0), stack(states_p, 1), stack(states_p, 2), stack(states_p, 3),
            stack(states_p, 4), stack(states_s, 0), stack(states_s, 1), stack(states_s, 2))
```

```python
import functools

import jax
import jax.numpy as jnp
import numpy as np
from jax import lax
from jax.experimental import pallas as pl
from jax.experimental.pallas import tpu as pltpu

HEAD_DIM = 128
N_KV_HEADS = 4
DILATED_GROUPS = ((128, 1), (512, 4), (2048, 16))
N_GROUPS = len(DILATED_GROUPS)
N_Q_HEADS = N_GROUPS * N_KV_HEADS
ATTN_WIDTH = N_KV_HEADS * HEAD_DIM
BAND = 128
ATTN_UNROLL = 8
ROPE_THETA = 500000.0
ROT_DIM = HEAD_DIM // 4
CONV_WIDTH = 31
MEM_HEADS = 4
MEM_HEAD_DIM = 128
MEM_WIDTH = MEM_HEADS * MEM_HEAD_DIM
N_EXPERT_GROUPS = 4
EXPERTS_PER_GROUP = 8
N_EXPERTS = N_EXPERT_GROUPS * EXPERTS_PER_GROUP
TOP_K = 2
EPS = 1e-6
NEG = -1e30

LANES = 128
ROW_TILE = 256
WIDE_ROW_TILE = 512
MERGE_TILE = 512
IN_PROJ_TILE = 256
MOE_TILE = 256
GMM_BUFS = 4
VMEM_LIMIT = 56 * 1024 * 1024

BF16 = jnp.bfloat16
F32 = jnp.float32


def _params(*sem):
    return pltpu.CompilerParams(dimension_semantics=sem, vmem_limit_bytes=VMEM_LIMIT)


def _dot(a, b):
    return jnp.dot(a, b, preferred_element_type=F32)


def _dot_nt(a, b):
    return lax.dot_general(a, b, (((1,), (1,)), ((), ())), preferred_element_type=F32)


def _dot_tn(a, b):
    return lax.dot_general(a, b, (((0,), (0,)), ((), ())), preferred_element_type=F32)


def _rms(x, g):
    return x * lax.rsqrt(jnp.mean(x * x, axis=-1, keepdims=True) + EPS) * g


def _pack_halves(x):
    c = x.shape[1] // 2
    hi = lax.bitcast_convert_type(x[:, :c].astype(F32), jnp.uint32)
    lo = lax.bitcast_convert_type(x[:, c:].astype(F32), jnp.uint32)
    return hi | (lo >> 16)


def _unpack_halves(p):
    hi = lax.bitcast_convert_type(p & jnp.uint32(0xFFFF0000), F32).astype(BF16)
    lo = lax.bitcast_convert_type(p << 16, F32).astype(BF16)
    return hi, lo


def _const_spec(shape):
    nd = len(shape)
    return pl.BlockSpec(shape, lambda *_: (0,) * nd)


def _in_proj_kernel(x_ref, g_ref, w_ref, qg_ref, kg_ref, mqg_ref, rc_ref, ra_ref, rb_ref,
                    q_ref, k_ref, v_ref, u_ref, mq_ref):
    h = _rms(x_ref[...], g_ref[...]).astype(BF16)
    rc, ra, rb = rc_ref[...], ra_ref[...], rb_ref[...]

    def rope(y):
        return y * rc + pltpu.roll(y, LANES - ROT_DIM // 2, 1) * ra + pltpu.roll(y, ROT_DIM // 2, 1) * rb

    col = 0
    zq = _dot(h, w_ref[:, col:col + N_Q_HEADS * HEAD_DIM])
    for j in range(N_Q_HEADS):
        sl = slice(j * HEAD_DIM, (j + 1) * HEAD_DIM)
        q_ref[:, sl] = rope(_rms(zq[:, sl], qg_ref[...])).astype(BF16)
    col += N_Q_HEADS * HEAD_DIM
    tm = x_ref.shape[0]
    zk = _dot(h, w_ref[:, col:col + ATTN_WIDTH])
    for j in range(N_KV_HEADS):
        sl = slice(j * HEAD_DIM, (j + 1) * HEAD_DIM)
        k_ref[pl.ds(j, tm, stride=N_KV_HEADS), :] = rope(_rms(zk[:, sl], kg_ref[...]))
    col += ATTN_WIDTH
    zv = _dot(h, w_ref[:, col:col + ATTN_WIDTH])
    for j in range(N_KV_HEADS):
        v_ref[pl.ds(j, tm, stride=N_KV_HEADS), :] = zv[:, j * HEAD_DIM:(j + 1) * HEAD_DIM]
    col += ATTN_WIDTH
    conv_ch = u_ref.shape[-1]
    za = _dot(h, w_ref[:, col:col + conv_ch])
    zb = _dot(h, w_ref[:, col + conv_ch:col + 2 * conv_ch])
    u_ref[...] = za * jax.nn.sigmoid(zb)
    col += 2 * conv_ch
    zm = _dot(h, w_ref[:, col:col + MEM_WIDTH])
    for j in range(MEM_HEADS):
        sl = slice(j * MEM_HEAD_DIM, (j + 1) * MEM_HEAD_DIM)
        mq_ref[:, sl] = _rms(zm[:, sl], mqg_ref[...]).astype(BF16)


def _in_proj(x, g_mix, w_bf16, q_g, k_g, mq_g, rope_tabs, conv_ch):
    m, d = x.shape
    tm = min(IN_PROJ_TILE, m)
    n_tab_blocks = rope_tabs[0].shape[0] // tm
    row = lambda i: (i, 0)
    tab = lambda i: (i % n_tab_blocks, 0)
    ncols = w_bf16.shape[1]
    out_shape = (
        jax.ShapeDtypeStruct((m, N_Q_HEADS * HEAD_DIM), BF16),
        jax.ShapeDtypeStruct((m * N_KV_HEADS, HEAD_DIM), F32),
        jax.ShapeDtypeStruct((m * N_KV_HEADS, HEAD_DIM), F32),
        jax.ShapeDtypeStruct((m, conv_ch), F32),
        jax.ShapeDtypeStruct((m, MEM_WIDTH), BF16),
    )
    return pl.pallas_call(
        _in_proj_kernel,
        grid=(m // tm,),
        in_specs=[
            pl.BlockSpec((tm, d), row),
            _const_spec((1, d)),
            _const_spec((d, ncols)),
            _const_spec((1, HEAD_DIM)), _const_spec((1, HEAD_DIM)), _const_spec((1, MEM_HEAD_DIM)),
            pl.BlockSpec((tm, LANES), tab), pl.BlockSpec((tm, LANES), tab), pl.BlockSpec((tm, LANES), tab),
        ],
        out_specs=tuple(pl.BlockSpec((tm * s.shape[0] // m, s.shape[1]), row) for s in out_shape),
        out_shape=out_shape,
        compiler_params=_params("parallel"),
        name="in_proj",
    )(x, g_mix, w_bf16, q_g, k_g, mq_g, *rope_tabs)


def _rope_tables(pos):
    half = ROT_DIM // 2
    inv_freq = jnp.power(jnp.float32(ROPE_THETA), -jnp.arange(half, dtype=F32) * (2.0 / ROT_DIM))
    ang = pos.astype(F32)[:, None] * inv_freq[None, :]
    cos, sin = jnp.cos(ang), jnp.sin(ang)
    n = pos.shape[0]
    ones = jnp.ones((n, LANES - ROT_DIM), F32)
    zeros = jnp.zeros((n, LANES - half), F32)
    rc = jnp.concatenate([cos, cos, ones], axis=1)
    ra = jnp.concatenate([-sin, zeros], axis=1)
    rb = jnp.concatenate([jnp.zeros((n, half), F32), sin, jnp.zeros((n, LANES - ROT_DIM), F32)], axis=1)
    return rc, ra, rb


def _to_residue_layout(dst, src, classes_src, ratio):
    len_src = src.shape[0] // classes_src
    len_dst = len_src // ratio
    for c_src in range(classes_src):
        for a in range(ratio):
            c = c_src + classes_src * a
            dst[c * len_dst:(c + 1) * len_dst, :] = src[pl.ds(c_src * len_src + a, len_dst, stride=ratio), :]


def _prompt_attn_kernel(q0_ref, q1_ref, q2_ref, k_ref, v_ref, o_ref, perm_ref, tmp_ref, acc_ref, lse_ref):
    head = pl.program_id(1)
    seq = q0_ref.shape[0]
    q_refs = (q0_ref, q1_ref, q2_ref)
    dils = [d for _, d in DILATED_GROUPS]
    for t, ref in ((1, k_ref), (2, v_ref)):
        perm_ref[0, t] = ref[pl.ds(head, seq, stride=N_KV_HEADS), :]
        for g in range(1, N_GROUPS):
            _to_residue_layout(perm_ref.at[g, t], perm_ref.at[g - 1, t], dils[g - 1], dils[g] // dils[g - 1])
    for g in range(1, N_GROUPS):
        tmp_ref[0] = q_refs[g][...].astype(F32)
        for step in range(1, g + 1):
            dst = perm_ref.at[g, 0] if step == g else tmp_ref.at[step % 2]
            _to_residue_layout(dst, tmp_ref.at[(step - 1) % 2], dils[step - 1], dils[step] // dils[step - 1])
    iq = lax.broadcasted_iota(jnp.int32, (BAND, BAND), 0)
    ik = lax.broadcasted_iota(jnp.int32, (BAND, BAND), 1)
    keep_c = iq >= ik
    scale = HEAD_DIM ** -0.5

    for g, dil in enumerate(dils):
        nb = seq // (dil * BAND)

        def body(jj, carry, g=g, dil=dil, nb=nb):
            blocks = []
            for u in range(ATTN_UNROLL):
                j = jj * ATTN_UNROLL + u
                rows = pl.ds(pl.multiple_of(j * BAND, BAND), BAND)
                prev = pl.ds(pl.multiple_of(jnp.maximum(j - 1, 0) * BAND, BAND), BAND)
                q = q0_ref[rows, :] if g == 0 else perm_ref[g, 0, rows, :].astype(BF16)
                s_c = jnp.where(keep_c, _dot_nt(q, perm_ref[g, 1, rows, :].astype(BF16)) * scale, NEG)
                s_p = None
                if nb > 1:
                    keep_p = jnp.logical_and(ik >= iq, j % nb > 0)
                    s_p = jnp.where(keep_p, _dot_nt(q, perm_ref[g, 1, prev, :].astype(BF16)) * scale, NEG)
                blocks.append((j, rows, prev, s_c, s_p))
            probs = []
            for j, rows, prev, s_c, s_p in blocks:
                m = jnp.max(s_c if nb == 1 else jnp.maximum(s_c, s_p), axis=-1, keepdims=True)
                p_c = jnp.exp(s_c - m)
                p_p = None if nb == 1 else jnp.exp(s_p - m)
                l = jnp.sum(p_c if nb == 1 else p_c + p_p, axis=-1, keepdims=True)
                probs.append((m, l, p_c.astype(BF16), None if p_p is None else p_p.astype(BF16)))
            for (j, rows, prev, _, _), (m, l, p_c, p_p) in zip(blocks, probs):
                acc = _dot(p_c, perm_ref[g, 2, rows, :].astype(BF16))
                if nb > 1:
                    acc = acc + _dot(p_p, perm_ref[g, 2, prev, :].astype(BF16))
                out_rows = rows if dil == 1 else pl.ds((j % nb) * (BAND * dil) + j // nb, BAND, stride=dil)
                acc_ref[g, out_rows, :] = acc * (1.0 / l)
                lse_ref[g, out_rows, :] = jnp.broadcast_to(m + jnp.log(l), (BAND, LANES))
            return carry

        assert (dil * nb) % ATTN_UNROLL == 0
        lax.fori_loop(0, dil * nb // ATTN_UNROLL, body, 0)

    def combine(c, carry):
        rows = pl.ds(pl.multiple_of(c * BAND, BAND), BAND)
        lses = [lse_ref[g, rows, :] for g in range(N_GROUPS)]
        mx = functools.reduce(jnp.maximum, lses)
        ws = [jnp.exp(l - mx) for l in lses]
        out = functools.reduce(jnp.add, [w * acc_ref[g, rows, :] for g, w in enumerate(ws)])
        o_ref[rows, :] = (out * (1.0 / functools.reduce(jnp.add, ws))).astype(BF16)
        return carry

    lax.fori_loop(0, seq // BAND, combine, 0)


def _prompt_attention(q, k_flat, v_flat, batch, seq):
    for window, dil in DILATED_GROUPS:
        assert window // dil == BAND and seq % (dil * BAND) == 0
    qspec = lambda g: pl.BlockSpec((seq, HEAD_DIM), lambda b, h: (b, g * N_KV_HEADS + h))
    kvspec = pl.BlockSpec((seq * N_KV_HEADS, HEAD_DIM), lambda b, h: (b, 0))
    return pl.pallas_call(
        _prompt_attn_kernel,
        grid=(batch, N_KV_HEADS),
        in_specs=[qspec(0), qspec(1), qspec(2), kvspec, kvspec],
        out_specs=pl.BlockSpec((seq, HEAD_DIM), lambda b, h: (b, h)),
        out_shape=jax.ShapeDtypeStruct((batch * seq, ATTN_WIDTH), BF16),
        scratch_shapes=[pltpu.VMEM((N_GROUPS, 3, seq, HEAD_DIM), F32), pltpu.VMEM((2, seq, HEAD_DIM), F32),
                        pltpu.VMEM((N_GROUPS, seq, HEAD_DIM), F32), pltpu.VMEM((N_GROUPS, seq, LANES), F32)],
        compiler_params=_params("parallel", "arbitrary"),
        name="prompt_attn",
    )(q, q, q, k_flat, v_flat)


def _mem_kv_kernel(x_ref, g_ref, w_ref, kg_ref, k_ref, v_ref):
    tm = x_ref.shape[0]
    h = _rms(x_ref[...], g_ref[...]).astype(BF16)
    zk = _dot(h, w_ref[:, :MEM_WIDTH])
    zv = _dot(h, w_ref[:, MEM_WIDTH:])
    for j in range(MEM_HEADS):
        sl = slice(j * MEM_HEAD_DIM, (j + 1) * MEM_HEAD_DIM)
        k_ref[pl.ds(j, tm, stride=MEM_HEADS), :] = _rms(zk[:, sl], kg_ref[...])
        v_ref[pl.ds(j, tm, stride=MEM_HEADS), :] = zv[:, sl]


def _mem_kv(mem, g, w_bf16, k_g):
    m, d = mem.shape
    tm = min(ROW_TILE, m)
    row = lambda i: (i, 0)
    shp = jax.ShapeDtypeStruct((m * MEM_HEADS, MEM_HEAD_DIM), F32)
    ospec = pl.BlockSpec((tm * MEM_HEADS, MEM_HEAD_DIM), row)
    return pl.pallas_call(
        _mem_kv_kernel,
        grid=(m // tm,),
        in_specs=[pl.BlockSpec((tm, d), row), _const_spec((1, d)), _const_spec((d, 2 * MEM_WIDTH)),
                  _const_spec((1, MEM_HEAD_DIM))],
        out_specs=(ospec, ospec),
        out_shape=(shp, shp),
        compiler_params=_params("parallel"),
        name="mem_kv",
    )(mem, g, w_bf16, k_g)


def _mem_attn_kernel(q_ref, k_ref, v_ref, o_ref):
    scale = MEM_HEAD_DIM ** -0.5
    n_mem = k_ref.shape[0] // MEM_HEADS
    for h in range(MEM_HEADS):
        sl = slice(h * MEM_HEAD_DIM, (h + 1) * MEM_HEAD_DIM)
        head_rows = pl.ds(h, n_mem, stride=MEM_HEADS)
        s = _dot_nt(q_ref[:, sl], k_ref[head_rows, :].astype(BF16)) * scale
        p = jnp.exp(s - jnp.max(s, axis=-1, keepdims=True))
        l = jnp.sum(p, axis=-1, keepdims=True)
        o_ref[:, sl] = (_dot(p.astype(BF16), v_ref[head_rows, :].astype(BF16)) * (1.0 / l)).astype(BF16)


def _mem_attention(mq, mem_k_flat, mem_v_flat, batch, seq):
    rows = mem_k_flat.shape[0] // batch
    tq = min(1024, seq)
    nq = seq // tq
    kspec = pl.BlockSpec((rows, MEM_HEAD_DIM), lambda b, i: (b, 0))
    return pl.pallas_call(
        _mem_attn_kernel,
        grid=(batch, nq),
        in_specs=[pl.BlockSpec((tq, MEM_WIDTH), lambda b, i: (b * nq + i, 0)), kspec, kspec],
        out_specs=pl.BlockSpec((tq, MEM_WIDTH), lambda b, i: (b * nq + i, 0)),
        out_shape=jax.ShapeDtypeStruct((batch * seq, MEM_WIDTH), BF16),
        compiler_params=_params("parallel", "parallel"),
        name="mem_attn",
    )(mq, mem_k_flat, mem_v_flat)


CONV_HALO = 32
CONV_CHUNK = 32
CONV_ACC_ROWS = 128
CONV_TAP_UNROLL = 8


def _ln_swish(c, g, b):
    mu = jnp.mean(c, axis=-1, keepdims=True)
    xc = c - mu
    y = xc * lax.rsqrt(jnp.mean(xc * xc, axis=-1, keepdims=True) + EPS) * g + b
    return y * jax.nn.sigmoid(y)


def _conv_prompt_kernel(halo_ref, u_ref, w_ref, b_ref, g_ref, beta_ref, o_ref, ext_ref, conv_ref, wb_ref):
    tc, ch = u_ref.shape
    first = pl.program_id(1) == 0
    lane_tiles = [slice(j * LANES, (j + 1) * LANES) for j in range(ch // LANES)]
    for j, sl in enumerate(lane_tiles):
        halo = jnp.where(first, 0.0, halo_ref[:, sl])
        for dup in range(2):
            ext_ref[j, pl.ds(dup, CONV_HALO, stride=2), :] = halo
            ext_ref[j, pl.ds(2 * CONV_HALO + dup, tc, stride=2), :] = u_ref[:, sl]
    for w in range(CONV_WIDTH):
        wb_ref[w] = jnp.broadcast_to(w_ref[w:w + 1, :], (8, ch))
    lead = CONV_HALO - (CONV_WIDTH - 1)
    rows = CONV_ACC_ROWS
    for j, sl in enumerate(lane_tiles):
        for r0 in range(0, tc, rows):
            def tap(w, acc, j=j, sl=sl, r0=r0):
                win = ext_ref[j, pl.ds(2 * (r0 + lead + w), rows, stride=2), :].reshape(rows // 8, 8, LANES)
                return acc + wb_ref[w, :, sl][None] * win

            acc = lax.fori_loop(0, CONV_WIDTH, tap, jnp.zeros((rows // 8, 8, LANES), F32) + b_ref[:, sl],
                                unroll=CONV_TAP_UNROLL)
            conv_ref[r0:r0 + rows, sl] = acc.reshape(rows, LANES)
    for c0 in range(0, tc, CONV_CHUNK):
        o_ref[c0:c0 + CONV_CHUNK, :] = _ln_swish(conv_ref[c0:c0 + CONV_CHUNK, :], g_ref[...], beta_ref[...]).astype(BF16)


def _conv_prompt(u, conv_w, conv_b, ln_g, ln_b, batch, seq):
    ch = u.shape[1]
    tc = min(ROW_TILE, seq)
    nt = seq // tc
    ratio = tc // CONV_HALO
    return pl.pallas_call(
        _conv_prompt_kernel,
        grid=(batch, nt),
        in_specs=[
            pl.BlockSpec((CONV_HALO, ch), lambda b, i: (jnp.maximum((b * nt + i) * ratio - 1, 0), 0)),
            pl.BlockSpec((tc, ch), lambda b, i: (b * nt + i, 0)),
            _const_spec((CONV_WIDTH, ch)), _const_spec((1, ch)), _const_spec((1, ch)), _const_spec((1, ch)),
        ],
        out_specs=pl.BlockSpec((tc, ch), lambda b, i: (b * nt + i, 0)),
        out_shape=jax.ShapeDtypeStruct((batch * seq, ch), BF16),
        scratch_shapes=[pltpu.VMEM((ch // LANES, 2 * (CONV_HALO + tc), LANES), F32), pltpu.VMEM((tc, ch), F32),
                        pltpu.VMEM((CONV_WIDTH, 8, ch), F32)],
        compiler_params=_params("parallel", "parallel"),
        name="conv_prompt",
    )(u, u, conv_w, conv_b, ln_g, ln_b)


def _conv_sample_kernel(state_ref, new_ref, w_ref, b_ref, g_ref, beta_ref, o_ref, state_out_ref, ext_ref):
    nb, ctx, ch = state_ref.shape
    t = new_ref.shape[1]
    ext_ref[:, 0:ctx, :] = state_ref[...]
    ext_ref[:, ctx:ctx + t, :] = new_ref[...]
    acc = jnp.zeros((nb, t, ch), F32) + b_ref[...]
    for w in range(CONV_WIDTH):
        acc = acc + w_ref[w:w + 1, :] * ext_ref[:, w:w + t, :]
    o_ref[...] = _ln_swish(acc, g_ref[...], beta_ref[...])
    state_out_ref[...] = ext_ref[:, t:t + ctx, :]


def _conv_sample(state, u_new, conv_w, conv_b, ln_g, ln_b):
    batch, ctx, ch = state.shape
    t = u_new.shape[1]
    nb = 8
    blk = lambda n: pl.BlockSpec((nb, n, ch), lambda i: (i, 0, 0))
    return pl.pallas_call(
        _conv_sample_kernel,
        grid=(batch // nb,),
        in_specs=[blk(ctx), blk(t), _const_spec((CONV_WIDTH, ch)), _const_spec((1, ch)), _const_spec((1, ch)),
                  _const_spec((1, ch))],
        out_specs=(blk(t), blk(ctx)),
        out_shape=(jax.ShapeDtypeStruct((batch, t, ch), F32), jax.ShapeDtypeStruct((batch, ctx, ch), F32)),
        scratch_shapes=[pltpu.VMEM((nb, ctx + t + 6, ch), F32)],
        compiler_params=_params("parallel"),
        name="conv_sample",
    )(state, u_new, conv_w, conv_b, ln_g, ln_b)


GROUP_LANES = 16


def _spread_groups(vec, combine):
    t = combine(combine(vec, pltpu.roll(vec, LANES - GROUP_LANES, 1)), pltpu.roll(vec, LANES - 2 * GROUP_LANES, 1))
    lane = lax.broadcasted_iota(jnp.int32, vec.shape, 1)
    return jnp.where(lane < GROUP_LANES, t,
                     jnp.where(lane < 2 * GROUP_LANES, pltpu.roll(t, GROUP_LANES, 1), pltpu.roll(t, 2 * GROUP_LANES, 1)))


SLAB_POS = DILATED_GROUPS[-1][1]
SLAB_ROWS = SLAB_POS * N_KV_HEADS
TAIL_POS = max(w for w, d in DILATED_GROUPS if d < SLAB_POS)


def _sample_attn_kernel(kc_ref, vc_ref, kn_ref, vn_ref, wq_ref, bias_d_ref, bias_t_ref, bias_n_ref, mk_ref, mv_ref,
                        wm_ref, bias_m_ref, kw_ref, vw_ref, a_ref, m_ref, kx_ref, vx_ref):
    n_slab = kc_ref.shape[1]
    new_rows = kn_ref.shape[1]
    t = new_rows // N_KV_HEADS
    tail_slabs = TAIL_POS // SLAB_POS
    n_used = N_GROUPS * GROUP_LANES
    lane = lax.broadcasted_iota(jnp.int32, (1, LANES), 1)

    for src, new, dst in ((kc_ref, kn_ref, kw_ref), (vc_ref, vn_ref, vw_ref)):
        dst[0, :, 0:SLAB_ROWS - new_rows, :] = src[0, :, new_rows:SLAB_ROWS, :]
        dst[0, 0:n_slab - 1, SLAB_ROWS - new_rows:SLAB_ROWS, :] = src[0, 1:n_slab, 0:new_rows, :]
        dst[0, n_slab - 1, SLAB_ROWS - new_rows:SLAB_ROWS, :] = new[0]

    kx_ref[...] = jnp.zeros(kx_ref.shape, F32)
    vx_ref[...] = jnp.zeros(vx_ref.shape, F32)
    kx_ref[0:new_rows, :] = kn_ref[0]
    vx_ref[0:new_rows, :] = vn_ref[0]

    def dilated(ref):
        return ref[0, :, 0:new_rows, :].reshape(n_slab * new_rows, HEAD_DIM).astype(BF16)

    def tail(ref):
        return ref[0, n_slab - tail_slabs:n_slab, :, :].reshape(tail_slabs * SLAB_ROWS, HEAD_DIM).astype(BF16)

    scale = HEAD_DIM ** -0.5
    wq = wq_ref[0]
    s_d = _dot(dilated(kc_ref), wq) * scale + bias_d_ref[...]
    s_t = _dot(tail(kc_ref), wq) * scale + bias_t_ref[...]
    s_n = _dot(kx_ref[...].astype(BF16), wq) * scale + bias_n_ref[...]
    col_max = lambda s: jnp.max(s, axis=0, keepdims=True)
    col_sum = lambda p: jnp.sum(p, axis=0, keepdims=True)
    m_col = jnp.maximum(jnp.maximum(col_max(s_d), col_max(s_t)), col_max(s_n))
    m_joint = jnp.where(lane < n_used, _spread_groups(m_col, jnp.maximum), 0.0)
    p_d, p_t, p_n = jnp.exp(s_d - m_joint), jnp.exp(s_t - m_joint), jnp.exp(s_n - m_joint)
    l_col = col_sum(p_d) + col_sum(p_t) + col_sum(p_n)
    inv = 1.0 / jnp.where(lane < n_used, _spread_groups(l_col, jnp.add), 1.0)
    o = (_dot_tn((p_d * inv).astype(BF16), dilated(vc_ref)) + _dot_tn((p_t * inv).astype(BF16), tail(vc_ref))
         + _dot_tn((p_n * inv).astype(BF16), vx_ref[...].astype(BF16)))
    for h in range(N_KV_HEADS):
        r = h * t
        a_ref[0, :, h * HEAD_DIM:(h + 1) * HEAD_DIM] = functools.reduce(
            jnp.add, [o[g * GROUP_LANES + r:g * GROUP_LANES + r + t, :] for g in range(N_GROUPS)])

    sm = _dot(mk_ref[0].astype(BF16), wm_ref[0]) * (MEM_HEAD_DIM ** -0.5) + bias_m_ref[...]
    pm = jnp.exp(sm - jnp.where(lane < MEM_HEADS * t, col_max(sm), 0.0))
    lm = jnp.where(lane < MEM_HEADS * t, col_sum(pm), 1.0)
    om = _dot_tn((pm * (1.0 / lm)).astype(BF16), mv_ref[0].astype(BF16))
    for h in range(MEM_HEADS):
        m_ref[0, :, h * MEM_HEAD_DIM:(h + 1) * MEM_HEAD_DIM] = om[h * t:(h + 1) * t, :]


def _sample_masks(cache_len, t, n_mem):
    col = np.arange(LANES)[None, :]
    g, c_head, c_tok = col // GROUP_LANES, (col % GROUP_LANES) // t, (col % GROUP_LANES) % t
    used = (col < N_GROUPS * GROUP_LANES) & (col % GROUP_LANES < N_KV_HEADS * t)
    pad = [1] * (LANES // GROUP_LANES - N_GROUPS)
    dil = np.array([d for _, d in DILATED_GROUPS] + pad)[g]
    win = np.array([w for w, _ in DILATED_GROUPS] + pad)[g]
    sparse = dil >= SLAB_POS

    def keep(pos, head, group_sel):
        dist = cache_len + c_tok - pos
        return used & group_sel & (head == c_head) & (dist >= 0) & (dist % dil == 0) & (dist <= win)

    n_slab = cache_len // SLAB_POS
    y = np.arange(n_slab * t * N_KV_HEADS)[:, None]
    keep_d = keep((y // (t * N_KV_HEADS)) * SLAB_POS + (y % (t * N_KV_HEADS)) // N_KV_HEADS, y % N_KV_HEADS, sparse)
    x = np.arange(TAIL_POS * N_KV_HEADS)[:, None]
    keep_t = keep(cache_len - TAIL_POS + x // N_KV_HEADS, x % N_KV_HEADS, ~sparse)
    z = np.arange(LANES)[:, None]
    keep_n = keep(cache_len + z // N_KV_HEADS, z % N_KV_HEADS, True) & (z < t * N_KV_HEADS)
    w = np.arange(n_mem * MEM_HEADS)[:, None]
    keep_m = (col < MEM_HEADS * t) & (w % MEM_HEADS == col // t)
    return tuple(jnp.asarray(np.where(k, 0.0, NEG), F32) for k in (keep_d, keep_t, keep_n, keep_m))


def _query_columns(q, batch, t, n_groups):
    qt = q.reshape(batch, t, n_groups, N_KV_HEADS, HEAD_DIM).transpose(0, 4, 2, 3, 1)
    qt = qt.reshape(batch, HEAD_DIM, n_groups, N_KV_HEADS * t)
    qt = jnp.pad(qt, ((0, 0), (0, 0), (0, 0), (0, GROUP_LANES - N_KV_HEADS * t)))
    qt = qt.reshape(batch, HEAD_DIM, n_groups * GROUP_LANES)
    return jnp.pad(qt, ((0, 0), (0, 0), (0, LANES - n_groups * GROUP_LANES)))


def _sample_attention(q, k_new, v_new, mq, cache_k, cache_v, mem_k, mem_v):
    batch, cache_len = cache_k.shape[0], cache_k.shape[1]
    new_rows = k_new.shape[0] // batch
    t = new_rows // N_KV_HEADS
    n_mem = mem_k.shape[1]
    assert N_KV_HEADS * t <= GROUP_LANES and t <= SLAB_POS and new_rows % 8 == 0
    assert cache_len % SLAB_POS == 0 and cache_len >= max(w for w, _ in DILATED_GROUPS)
    assert all(d == SLAB_POS or w <= TAIL_POS for w, d in DILATED_GROUPS)
    n_slab = cache_len // SLAB_POS
    wq = _query_columns(q, batch, t, N_GROUPS)
    wm = _query_columns(mq, batch, t, 1)
    bias_d, bias_t, bias_n, bias_m = _sample_masks(cache_len, t, n_mem)
    per_b = lambda *shape: pl.BlockSpec((1,) + shape, lambda b: (b,) + (0,) * len(shape))
    slabs = lambda c: c.reshape(batch, n_slab, SLAB_ROWS, HEAD_DIM)
    mem_rows = n_mem * MEM_HEADS
    win = jax.ShapeDtypeStruct((batch, n_slab, SLAB_ROWS, HEAD_DIM), F32)
    k_win, v_win, a, m = pl.pallas_call(
        _sample_attn_kernel,
        grid=(batch,),
        in_specs=[
            per_b(n_slab, SLAB_ROWS, HEAD_DIM), per_b(n_slab, SLAB_ROWS, HEAD_DIM),
            per_b(new_rows, HEAD_DIM), per_b(new_rows, HEAD_DIM), per_b(HEAD_DIM, LANES),
            _const_spec(bias_d.shape), _const_spec(bias_t.shape), _const_spec(bias_n.shape),
            per_b(mem_rows, MEM_HEAD_DIM), per_b(mem_rows, MEM_HEAD_DIM), per_b(MEM_HEAD_DIM, LANES),
            _const_spec(bias_m.shape),
        ],
        out_specs=(per_b(n_slab, SLAB_ROWS, HEAD_DIM), per_b(n_slab, SLAB_ROWS, HEAD_DIM), per_b(t, ATTN_WIDTH),
                   per_b(t, MEM_WIDTH)),
        out_shape=(win, win, jax.ShapeDtypeStruct((batch, t, ATTN_WIDTH), F32),
                   jax.ShapeDtypeStruct((batch, t, MEM_WIDTH), F32)),
        scratch_shapes=[pltpu.VMEM((LANES, HEAD_DIM), F32), pltpu.VMEM((LANES, HEAD_DIM), F32)],
        compiler_params=_params("parallel"),
        name="sample_attn",
    )(slabs(cache_k), slabs(cache_v), k_new.reshape(batch, new_rows, HEAD_DIM), v_new.reshape(batch, new_rows, HEAD_DIM),
      wq, bias_d, bias_t, bias_n, mem_k.reshape(batch, mem_rows, MEM_HEAD_DIM),
      mem_v.reshape(batch, mem_rows, MEM_HEAD_DIM), wm, bias_m)
    return k_win.reshape(cache_k.shape), v_win.reshape(cache_v.shape), a, m


ROUTE_GROUP_LANE0 = 0
ROUTE_EXPERT_LANE0 = N_EXPERT_GROUPS


def _route(logits):
    lane = lax.broadcasted_iota(jnp.int32, logits.shape, 1).astype(F32)
    big = float(LANES)

    def masked_softmax(keep):
        z = jnp.where(keep, logits, NEG)
        e = jnp.where(keep, jnp.exp(z - jnp.max(z, axis=-1, keepdims=True)), 0.0)
        return e / jnp.sum(e, axis=-1, keepdims=True)

    def first_argmax(vals, keep):
        top = jnp.max(jnp.where(keep, vals, -1.0), axis=-1, keepdims=True)
        idx = jnp.min(jnp.where(jnp.logical_and(keep, vals == top), lane, big), axis=-1, keepdims=True)
        return top, idx

    is_group = lane < N_EXPERT_GROUPS
    pg = masked_softmax(is_group)
    pg_top, g_idx = first_argmax(pg, is_group)
    lo = ROUTE_EXPERT_LANE0 + g_idx * EXPERTS_PER_GROUP
    in_group = jnp.logical_and(lane >= lo, lane < lo + EXPERTS_PER_GROUP)
    pe = masked_softmax(in_group)
    p1, i1 = first_argmax(pe, in_group)
    p2, i2 = first_argmax(pe, jnp.logical_and(in_group, lane != i1))
    denom = p1 + p2
    w1 = pg_top * p1 / denom
    w2 = pg_top * p2 / denom
    return lane, i1 - ROUTE_EXPERT_LANE0, i2 - ROUTE_EXPERT_LANE0, w1, w2


def _merge_kernel(x_ref, gmix_ref, ao_ref, cc_ref, mo_ref, wgate_ref, wa_ref, wc_ref, wm_ref, wo_ref, gffn_ref, wr_ref,
                  br_ref, cnt_in_ref, x1_ref, h2_ref, route_ref, cnt_out_ref, cnt_ref):
    @pl.when(pl.program_id(0) == 0)
    def _():
        cnt_ref[...] = cnt_in_ref[...]

    d = x_ref.shape[1]
    x = x_ref[...]
    h = _rms(x, gmix_ref[...]).astype(BF16)
    a = _dot(ao_ref[...].astype(BF16), wa_ref[...])
    c = _dot(cc_ref[...].astype(BF16), wc_ref[...])
    m = _dot(mo_ref[...].astype(BF16), wm_ref[...])
    z = jax.nn.sigmoid(_dot(h, wgate_ref[:, 0:d])) * a
    z = z + jax.nn.sigmoid(_dot(h, wgate_ref[:, d:2 * d])) * c
    z = z + jax.nn.sigmoid(_dot(h, wgate_ref[:, 2 * d:3 * d])) * m
    x1 = x + _dot(z.astype(BF16), wo_ref[...])
    x1_ref[...] = x1
    h2 = _rms(x1, gffn_ref[...]).astype(BF16)
    packed = _pack_halves(h2)
    for c in range(packed.shape[1] // LANES):
        h2_ref[pl.ds(c, x_ref.shape[0], stride=packed.shape[1] // LANES), :] = packed[:, c * LANES:(c + 1) * LANES]
    logits = _dot(h2, wr_ref[...]) + br_ref[...]
    lane, e1, e2, w1, w2 = _route(logits)

    tm = x_ref.shape[0]
    hit1 = jnp.where(lane == e1, 1.0, 0.0)
    hit2 = jnp.where(lane == e2, 1.0, 0.0)
    hits = hit1 + hit2
    earlier = (lax.broadcasted_iota(jnp.int32, (tm, tm), 1) < lax.broadcasted_iota(jnp.int32, (tm, tm), 0))
    before = _dot(jnp.where(earlier, 1.0, 0.0).astype(BF16), hits.astype(BF16)) + cnt_ref[...]
    rank1 = jnp.sum(hit1 * before, axis=-1, keepdims=True)
    rank2 = jnp.sum(hit2 * before, axis=-1, keepdims=True)
    cnt_ref[...] = cnt_ref[...] + jnp.sum(hits, axis=0, keepdims=True)
    cnt_out_ref[...] = cnt_ref[...]
    route = jnp.zeros(logits.shape, F32)
    for i, val in enumerate((e1, e2, w1, w2, rank1, rank2)):
        route = jnp.where(lane == i, val, route)
    route_ref[...] = route


ROUTE_E, ROUTE_W, ROUTE_RANK = 0, 2, 4


def _merge(x, g_mix, ao, cc, mo, w_gate, w_a, w_c, w_m, w_o, g_ffn, w_r, b_r, counts):
    m, d = x.shape
    tm = min(MERGE_TILE, m)
    row = lambda i: (i, 0)
    rows = lambda arr: pl.BlockSpec((tm, arr.shape[1]), row)
    ins = [x, g_mix, ao, cc, mo, w_gate, w_a, w_c, w_m, w_o, g_ffn, w_r, b_r, counts]
    specs = ([rows(x), _const_spec(g_mix.shape), rows(ao), rows(cc), rows(mo)]
             + [_const_spec(a.shape) for a in (w_gate, w_a, w_c, w_m, w_o, g_ffn, w_r, b_r, counts)])
    out_shape = (jax.ShapeDtypeStruct((m, d), F32), jax.ShapeDtypeStruct((m * (d // 2 // LANES), LANES), jnp.uint32),
                 jax.ShapeDtypeStruct((m, LANES), F32), jax.ShapeDtypeStruct((1, LANES), F32))
    return pl.pallas_call(
        _merge_kernel,
        grid=(m // tm,),
        in_specs=specs,
        out_specs=(rows(out_shape[0]), pl.BlockSpec((tm * (d // 2 // LANES), LANES), row), rows(out_shape[2]),
                   _const_spec((1, LANES))),
        out_shape=out_shape,
        scratch_shapes=[pltpu.VMEM((1, LANES), F32)],
        compiler_params=_params("arbitrary"),
        name="merge",
    )(*ins)


def _routing_tables(counts, routes):
    cnt = counts[0, :N_EXPERTS].astype(jnp.int32)
    tiles = (cnt + MOE_TILE - 1) // MOE_TILE
    tile_start = jnp.cumsum(tiles) - tiles
    row_start = tile_start * MOE_TILE
    ids = routes[:, ROUTE_E:ROUTE_E + TOP_K].astype(jnp.int32)
    start_of = jnp.sum(jnp.where(ids[:, :, None] == jnp.arange(N_EXPERTS)[None, None, :], row_start[None, None, :], 0),
                       axis=-1)
    pos = start_of + routes[:, ROUTE_RANK:ROUTE_RANK + TOP_K].astype(jnp.int32)
    return pos.reshape(-1), tile_start.astype(jnp.int32), tiles.astype(jnp.int32)


def _dispatch_kernel(pos_ref, h_ref, xs_in, xs_out, sem, *, row0):
    del xs_in
    tm = h_ref.shape[0]
    base = (row0 + pl.program_id(0) * tm) * TOP_K

    def issue(j, carry):
        for k in range(TOP_K):
            slot = pos_ref[base + j * TOP_K + k]
            pltpu.make_async_copy(h_ref.at[j], xs_out.at[slot], sem).start()
        return carry

    lax.fori_loop(0, tm, issue, 0, unroll=8)
    for k in range(TOP_K):
        pltpu.make_async_copy(h_ref, xs_out.at[pl.ds(0, tm)], sem).wait()


def _dispatch(h2, pos, xs, row0):
    m = h2.shape[0]
    tm = min(WIDE_ROW_TILE, m)
    return pl.pallas_call(
        functools.partial(_dispatch_kernel, row0=row0),
        grid_spec=pltpu.PrefetchScalarGridSpec(
            num_scalar_prefetch=1,
            grid=(m // tm,),
            in_specs=[pl.BlockSpec((tm,) + h2.shape[1:], lambda i, *_: (i, 0, 0)), pl.BlockSpec(memory_space=pl.ANY)],
            out_specs=pl.BlockSpec(memory_space=pl.ANY),
            scratch_shapes=[pltpu.SemaphoreType.DMA(())],
        ),
        out_shape=jax.ShapeDtypeStruct(xs.shape, xs.dtype),
        input_output_aliases={2: 0},
        compiler_params=_params("arbitrary"),
        name="moe_dispatch",
    )(pos, h2, xs)


def _gmm_kernel(start_ref, count_ref, xs_hbm, wg_ref, wu_ref, wd_ref, ys_hbm, xbuf, obuf, wg_s, wu_s, wd_s,
                in_sems, out_sems):
    e = pl.program_id(0)
    nb = GMM_BUFS
    first, n = start_ref[e], count_ref[e]

    x_rows, y_rows = xbuf.shape[1], obuf.shape[1]
    x_tiles, y_tiles = x_rows // MOE_TILE, y_rows // MOE_TILE

    def copies(expert_first):
        def rows(i, n_rows):
            return pl.ds(pl.multiple_of((expert_first + i) * n_rows, n_rows), n_rows)

        def in_copy(i):
            return pltpu.make_async_copy(xs_hbm.at[rows(i, x_rows)], xbuf.at[i % nb], in_sems.at[i % nb])

        def out_copy(i):
            return pltpu.make_async_copy(obuf.at[i % nb], ys_hbm.at[rows(i, y_rows)], out_sems.at[i % nb])

        return in_copy, out_copy

    in_copy, out_copy = copies(first)

    def start_first_reads(expert):
        copy, _ = copies(start_ref[expert])
        lax.fori_loop(0, jnp.minimum(count_ref[expert], nb - 1), lambda i, c: (copy(i).start(), c)[1], 0)

    @pl.when(e == 0)
    def _():
        start_first_reads(0)

    @pl.when(n > 0)
    def _():
        wg_s[...] = wg_ref[0].astype(BF16)
        wu_s[...] = wu_ref[0].astype(BF16)
        wd_s[...] = wd_ref[0].astype(BF16)
        half = x_tiles * LANES

        def tile(i, carry):
            slot = i % nb
            in_copy(i).wait()

            @pl.when(i + nb - 1 < n)
            def _():
                in_copy(i + nb - 1).start()

            @pl.when(i >= nb)
            def _():
                out_copy(i - nb).wait()

            packed = jnp.concatenate([xbuf[slot, pl.ds(c, MOE_TILE, stride=x_tiles), :] for c in range(x_tiles)], axis=1)
            x_hi, x_lo = _unpack_halves(packed)
            gate = _dot(x_hi, wg_s[0:half, :]) + _dot(x_lo, wg_s[half:, :])
            up = _dot(x_hi, wu_s[0:half, :]) + _dot(x_lo, wu_s[half:, :])
            hid = gate * jax.nn.sigmoid(gate) * up
            res = _dot(hid.astype(BF16), wd_s[...])
            for c in range(y_tiles):
                obuf[slot, pl.ds(c, MOE_TILE, stride=y_tiles), :] = res[:, c * LANES:(c + 1) * LANES]
            out_copy(i).start()
            return carry

        lax.fori_loop(0, n, tile, 0)
        lax.fori_loop(jnp.maximum(n - nb, 0), n, lambda i, c: (out_copy(i).wait(), c)[1], 0)

    @pl.when(e + 1 < pl.num_programs(0))
    def _():
        start_first_reads(e + 1)

    @pl.when(e == pl.num_programs(0) - 1)
    def _():
        obuf[0] = jnp.zeros(obuf.shape[1:], F32)

        def fill(t, carry):
            dst = ys_hbm.at[pl.ds(pl.multiple_of(t * y_rows, y_rows), y_rows)]
            copy = pltpu.make_async_copy(obuf.at[0], dst, out_sems.at[0])
            copy.start()
            copy.wait()
            return carry

        lax.fori_loop(first + n, ys_hbm.shape[0] // y_rows, fill, 0)


def _grouped_mlp(xs, tile_start, tile_count, w_gate, w_up, w_down):
    d, ff = w_gate.shape[1], w_gate.shape[2]
    x_tiles, y_tiles = d // 2 // LANES, d // LANES
    n_slots = xs.shape[0] // x_tiles
    wspec = lambda a, b: pl.BlockSpec((1, a, b), lambda e, *_: (e, 0, 0))
    any_spec = pl.BlockSpec(memory_space=pl.ANY)
    return pl.pallas_call(
        _gmm_kernel,
        grid_spec=pltpu.PrefetchScalarGridSpec(
            num_scalar_prefetch=2,
            grid=(N_EXPERTS,),
            in_specs=[any_spec, wspec(d, ff), wspec(d, ff), wspec(ff, d)],
            out_specs=any_spec,
            scratch_shapes=[pltpu.VMEM((GMM_BUFS, MOE_TILE * x_tiles, LANES), xs.dtype),
                            pltpu.VMEM((GMM_BUFS, MOE_TILE * y_tiles, LANES), F32),
                            pltpu.VMEM((d, ff), BF16), pltpu.VMEM((d, ff), BF16), pltpu.VMEM((ff, d), BF16),
                            pltpu.SemaphoreType.DMA((GMM_BUFS,)), pltpu.SemaphoreType.DMA((GMM_BUFS,))],
        ),
        out_shape=jax.ShapeDtypeStruct((n_slots * y_tiles, LANES), F32),
        compiler_params=_params("arbitrary"),
        name="moe_gmm",
    )(tile_start, tile_count, xs, w_gate, w_up, w_down)


def _combine_kernel(pos_ref, x1_ref, route_ref, ys_hbm, ys_flat, y_ref, buf_ref, sems, *, row0):
    tm = x1_ref.shape[0]
    y_tiles = ys_hbm.shape[1]
    i = pl.program_id(0)

    def gather(tile, half):
        base = (row0 + tile * tm) * TOP_K

        def issue(j, carry):
            for k in range(TOP_K):
                slot = pos_ref[base + j * TOP_K + k]
                dst = buf_ref.at[half, k, pl.ds(pl.multiple_of(j * y_tiles, y_tiles), y_tiles)]
                pltpu.make_async_copy(ys_hbm.at[slot], dst, sems.at[half]).start()
            return carry

        lax.fori_loop(0, tm, issue, 0, unroll=4)

    @pl.when(i == 0)
    def _():
        gather(0, 0)

    @pl.when(i + 1 < pl.num_programs(0))
    def _():
        gather(i + 1, (i + 1) % 2)

    half = i % 2
    for k in range(TOP_K):
        pltpu.make_async_copy(ys_flat.at[pl.ds(0, tm * y_tiles)], buf_ref.at[half, k], sems.at[half]).wait()
    route = route_ref[...]
    for c in range(y_tiles):
        sl = slice(c * LANES, (c + 1) * LANES)
        rows = pl.ds(c, tm, stride=y_tiles)
        y_ref[:, sl] = (x1_ref[:, sl] + route[:, ROUTE_W:ROUTE_W + 1] * buf_ref[half, 0, rows, :]
                        + route[:, ROUTE_W + 1:ROUTE_W + 2] * buf_ref[half, 1, rows, :])


def _combine(x1, route, pos, ys, row0):
    m, d = x1.shape
    tm = min(WIDE_ROW_TILE, m)
    row = lambda i, *_: (i, 0)
    any_spec = pl.BlockSpec(memory_space=pl.ANY)
    return pl.pallas_call(
        functools.partial(_combine_kernel, row0=row0),
        grid_spec=pltpu.PrefetchScalarGridSpec(
            num_scalar_prefetch=1,
            grid=(m // tm,),
            in_specs=[pl.BlockSpec((tm, d), row), pl.BlockSpec((tm, LANES), row), any_spec, any_spec],
            out_specs=pl.BlockSpec((tm, d), row),
            scratch_shapes=[pltpu.VMEM((2, TOP_K, tm * (d // LANES), LANES), F32), pltpu.SemaphoreType.DMA((2,))],
        ),
        out_shape=jax.ShapeDtypeStruct((m, d), F32),
        compiler_params=_params("arbitrary"),
        name="moe_combine",
    )(pos, x1, route, ys.reshape(-1, d // LANES, LANES), ys)


def _layer(layer, x_prompt, x_sample, mem_prompt, cache_k, cache_v, state_conv, cache_mem_k, cache_mem_v, p):
    batch, seq, d = x_prompt.shape
    dec_batch, dec_seq, _ = x_sample.shape
    conv_ch = p["conv_w"].shape[-1]
    n_in = N_Q_HEADS * HEAD_DIM + 2 * ATTN_WIDTH + 2 * conv_ch + MEM_WIDTH
    past_len = cache_k.shape[2]

    row2 = lambda name: p[name][layer][None, :]
    w_in = p["w_in"][layer]
    w_main = w_in[:, :n_in].astype(BF16)
    w_gate = w_in[:, n_in:].astype(BF16)
    w_a, w_c, w_m, w_o = (p[n][layer].astype(BF16) for n in ("w_attn_proj", "w_conv_proj", "w_mem_proj", "w_out"))
    w_router = jnp.concatenate(
        [p["w_router_group"][layer], p["w_router_expert"][layer].transpose(1, 0, 2).reshape(d, N_EXPERTS)], axis=1)
    w_router = jnp.pad(w_router, ((0, 0), (0, LANES - w_router.shape[1]))).astype(BF16)
    b_router = jnp.concatenate([p["b_router_group"][layer], p["b_router_expert"][layer].reshape(-1)])
    b_router = jnp.pad(b_router, (0, LANES - b_router.shape[0]))[None, :]
    conv_args = (p["conv_w"][layer], row2("conv_b"), row2("conv_ln_g"), row2("conv_ln_b"))
    merge_w = (w_gate, w_a, w_c, w_m, w_o, row2("norm_ffn_g"), w_router, b_router)

    xp = x_prompt.reshape(batch * seq, d)
    tabs_p = _rope_tables(jnp.arange(seq, dtype=jnp.int32))
    q_p, k_p, v_p, u_p, mq_p = _in_proj(xp, row2("norm_mix_g"), w_main, row2("q_norm_g"), row2("k_norm_g"),
                                        row2("mq_norm_g"), tabs_p, conv_ch)
    ao_p = _prompt_attention(q_p, k_p, v_p, batch, seq)
    mem_k_p, mem_v_p = _mem_kv(mem_prompt.reshape(-1, d), row2("mem_norm_g"), p["w_mem_kv"][layer].astype(BF16),
                               row2("mk_norm_g"))
    mo_p = _mem_attention(mq_p, mem_k_p, mem_v_p, batch, seq)
    cc_p = _conv_prompt(u_p, *conv_args, batch, seq)
    x1_p, h2_p, route_p, counts = _merge(xp, row2("norm_mix_g"), ao_p, cc_p, mo_p, *merge_w,
                                         jnp.zeros((1, LANES), F32))

    xs = x_sample.reshape(dec_batch * dec_seq, d)
    tabs_s = _rope_tables(jnp.tile(past_len + jnp.arange(dec_seq, dtype=jnp.int32), dec_batch))
    q_s, k_s, v_s, u_s, mq_s = _in_proj(xs, row2("norm_mix_g"), w_main, row2("q_norm_g"), row2("k_norm_g"),
                                        row2("mq_norm_g"), tabs_s, conv_ch)
    k_win_s, v_win_s, a_s, mo_s = _sample_attention(q_s, k_s, v_s, mq_s, cache_k[layer], cache_v[layer],
                                                    cache_mem_k[layer], cache_mem_v[layer])
    cc_s, conv_state_s = _conv_sample(state_conv[layer], u_s.reshape(dec_batch, dec_seq, conv_ch), *conv_args)
    x1_s, h2_s, route_s, counts = _merge(xs, row2("norm_mix_g"), a_s.reshape(-1, ATTN_WIDTH),
                                         cc_s.reshape(-1, conv_ch), mo_s.reshape(-1, MEM_WIDTH), *merge_w, counts)

    n_p, n_s = xp.shape[0], xs.shape[0]
    n_tok = n_p + n_s
    n_tiles = (TOP_K * n_tok + N_EXPERTS * (MOE_TILE - 1)) // MOE_TILE + 1
    pos, tile_start, tile_count = _routing_tables(counts, jnp.concatenate([route_p, route_s], axis=0))
    per_token = lambda h: h.reshape(-1, d // 2 // LANES, LANES)
    slots = _dispatch(per_token(h2_p), pos, jnp.zeros((n_tiles * MOE_TILE, d // 2 // LANES, LANES), h2_p.dtype), 0)
    slots = _dispatch(per_token(h2_s), pos, slots, n_p)
    ys = _grouped_mlp(slots.reshape(-1, LANES), tile_start, tile_count, p["w_expert_gate"][layer],
                      p["w_expert_up"][layer], p["w_expert_down"][layer])
    y_p = _combine(x1_p, route_p, pos, ys, 0)
    y_s = _combine(x1_s, route_s, pos, ys, n_p)

    state_p = (k_p.reshape(batch, seq, N_KV_HEADS, HEAD_DIM), v_p.reshape(batch, seq, N_KV_HEADS, HEAD_DIM),
               u_p.reshape(batch, seq, conv_ch)[:, seq - (CONV_WIDTH - 1):],
               mem_k_p.reshape(batch, -1, MEM_HEADS, MEM_HEAD_DIM), mem_v_p.reshape(batch, -1, MEM_HEADS, MEM_HEAD_DIM))
    state_s = (k_win_s, v_win_s, conv_state_s)
    return y_p.reshape(batch, seq, d), y_s.reshape(dec_batch, dec_seq, d), state_p, state_s


def kernel(x_prompt, x_sample, mem_prompt, cache_k, cache_v, state_conv, cache_mem_k, cache_mem_v, norm_mix_g, w_in, q_norm_g, k_norm_g, conv_w, conv_b, conv_ln_g, conv_ln_b, mem_norm_g, w_mem_kv, mq_norm_g, mk_norm_g, w_attn_proj, w_conv_proj, w_mem_proj, w_out, norm_ffn_g, w_router_group, b_router_group, w_router_expert, b_router_expert, w_expert_gate, w_expert_up, w_expert_down):
    p = dict(norm_mix_g=norm_mix_g, w_in=w_in, q_norm_g=q_norm_g, k_norm_g=k_norm_g, conv_w=conv_w, conv_b=conv_b,
             conv_ln_g=conv_ln_g, conv_ln_b=conv_ln_b, mem_norm_g=mem_norm_g, w_mem_kv=w_mem_kv, mq_norm_g=mq_norm_g,
             mk_norm_g=mk_norm_g, w_attn_proj=w_attn_proj, w_conv_proj=w_conv_proj, w_mem_proj=w_mem_proj,
             w_out=w_out, norm_ffn_g=norm_ffn_g, w_router_group=w_router_group, b_router_group=b_router_group,
             w_router_expert=w_router_expert, b_router_expert=b_router_expert, w_expert_gate=w_expert_gate,
             w_expert_up=w_expert_up, w_expert_down=w_expert_down)
    depth = w_in.shape[0]
    seq = x_prompt.shape[1]
    assert seq <= max(w for w, _ in DILATED_GROUPS)
    y_p, y_s = x_prompt, x_sample
    states_p, states_s = [], []
    for layer in range(depth):
        y_p, y_s, st_p, st_s = _layer(layer, y_p, y_s, mem_prompt, cache_k, cache_v, state_conv, cache_mem_k,
                                      cache_mem_v, p)
        states_p.append(st_p)
        states_s.append(st_s)
    stack = lambda states, i: jnp.stack([s[i] for s in states], axis=0)
    return (y_p, y_s, stack(states_p, 0), stack(states_p, 1), stack(states_p, 2), stack(states_p, 3),
            stack(states_p, 4), stack(states_s, 0), stack(states_s, 1), stack(states_s, 2))
```

```python
import functools

import jax
import jax.numpy as jnp
import numpy as np
from jax import lax
from jax.experimental import pallas as pl
from jax.experimental.pallas import tpu as pltpu

HEAD_DIM = 128
N_KV_HEADS = 4
DILATED_GROUPS = ((128, 1), (512, 4), (2048, 16))
N_GROUPS = len(DILATED_GROUPS)
N_Q_HEADS = N_GROUPS * N_KV_HEADS
ATTN_WIDTH = N_KV_HEADS * HEAD_DIM
BAND = 128
ATTN_UNROLL = 8
ROPE_THETA = 500000.0
ROT_DIM = HEAD_DIM // 4
CONV_WIDTH = 31
MEM_HEADS = 4
MEM_HEAD_DIM = 128
MEM_WIDTH = MEM_HEADS * MEM_HEAD_DIM
N_EXPERT_GROUPS = 4
EXPERTS_PER_GROUP = 8
N_EXPERTS = N_EXPERT_GROUPS * EXPERTS_PER_GROUP
TOP_K = 2
EPS = 1e-6
NEG = -1e30

LANES = 128
ROW_TILE = 256
WIDE_ROW_TILE = 512
MERGE_TILE = 512
IN_PROJ_TILE = 256
MOE_TILE = 256
GMM_BUFS = 4
VMEM_LIMIT = 56 * 1024 * 1024

BF16 = jnp.bfloat16
F32 = jnp.float32


def _params(*sem):
    return pltpu.CompilerParams(dimension_semantics=sem, vmem_limit_bytes=VMEM_LIMIT)


def _dot(a, b):
    return jnp.dot(a, b, preferred_element_type=F32)


def _dot_nt(a, b):
    return lax.dot_general(a, b, (((1,), (1,)), ((), ())), preferred_element_type=F32)


def _dot_tn(a, b):
    return lax.dot_general(a, b, (((0,), (0,)), ((), ())), preferred_element_type=F32)


def _rms(x, g):
    return x * lax.rsqrt(jnp.mean(x * x, axis=-1, keepdims=True) + EPS) * g


def _pack_halves(x):
    c = x.shape[1] // 2
    hi = lax.bitcast_convert_type(x[:, :c].astype(F32), jnp.uint32)
    lo = lax.bitcast_convert_type(x[:, c:].astype(F32), jnp.uint32)
    return hi | (lo >> 16)


def _unpack_halves(p):
    hi = lax.bitcast_convert_type(p & jnp.uint32(0xFFFF0000), F32).astype(BF16)
    lo = lax.bitcast_convert_type(p << 16, F32).astype(BF16)
    return hi, lo


def _const_spec(shape):
    nd = len(shape)
    return pl.BlockSpec(shape, lambda *_: (0,) * nd)


def _in_proj_kernel(x_ref, g_ref, w_ref, qg_ref, kg_ref, mqg_ref, rc_ref, ra_ref, rb_ref,
                    q_ref, k_ref, v_ref, u_ref, mq_ref):
    h = _rms(x_ref[...], g_ref[...]).astype(BF16)
    rc, ra, rb = rc_ref[...], ra_ref[...], rb_ref[...]

    def rope(y):
        return y * rc + pltpu.roll(y, LANES - ROT_DIM // 2, 1) * ra + pltpu.roll(y, ROT_DIM // 2, 1) * rb

    col = 0
    zq = _dot(h, w_ref[:, col:col + N_Q_HEADS * HEAD_DIM])
    for j in range(N_Q_HEADS):
        sl = slice(j * HEAD_DIM, (j + 1) * HEAD_DIM)
        q_ref[:, sl] = rope(_rms(zq[:, sl], qg_ref[...])).astype(BF16)
    col += N_Q_HEADS * HEAD_DIM
    tm = x_ref.shape[0]
    zk = _dot(h, w_ref[:, col:col + ATTN_WIDTH])
    for j in range(N_KV_HEADS):
        sl = slice(j * HEAD_DIM, (j + 1) * HEAD_DIM)
        k_ref[pl.ds(j, tm, stride=N_KV_HEADS), :] = rope(_rms(zk[:, sl], kg_ref[...]))
    col += ATTN_WIDTH
    zv = _dot(h, w_ref[:, col:col + ATTN_WIDTH])
    for j in range(N_KV_HEADS):
        v_ref[pl.ds(j, tm, stride=N_KV_HEADS), :] = zv[:, j * HEAD_DIM:(j + 1) * HEAD_DIM]
    col += ATTN_WIDTH
    conv_ch = u_ref.shape[-1]
    za = _dot(h, w_ref[:, col:col + conv_ch])
    zb = _dot(h, w_ref[:, col + conv_ch:col + 2 * conv_ch])
    u_ref[...] = za * jax.nn.sigmoid(zb)
    col += 2 * conv_ch
    zm = _dot(h, w_ref[:, col:col + MEM_WIDTH])
    for j in range(MEM_HEADS):
        sl = slice(j * MEM_HEAD_DIM, (j + 1) * MEM_HEAD_DIM)
        mq_ref[:, sl] = _rms(zm[:, sl], mqg_ref[...]).astype(BF16)


def _in_proj(x, g_mix, w_bf16, q_g, k_g, mq_g, rope_tabs, conv_ch):
    m, d = x.shape
    tm = min(IN_PROJ_TILE, m)
    n_tab_blocks = rope_tabs[0].shape[0] // tm
    row = lambda i: (i, 0)
    tab = lambda i: (i % n_tab_blocks, 0)
    ncols = w_bf16.shape[1]
    out_shape = (
        jax.ShapeDtypeStruct((m, N_Q_HEADS * HEAD_DIM), BF16),
        jax.ShapeDtypeStruct((m * N_KV_HEADS, HEAD_DIM), F32),
        jax.ShapeDtypeStruct((m * N_KV_HEADS, HEAD_DIM), F32),
        jax.ShapeDtypeStruct((m, conv_ch), F32),
        jax.ShapeDtypeStruct((m, MEM_WIDTH), BF16),
    )
    return pl.pallas_call(
        _in_proj_kernel,
        grid=(m // tm,),
        in_specs=[
            pl.BlockSpec((tm, d), row),
            _const_spec((1, d)),
            _const_spec((d, ncols)),
            _const_spec((1, HEAD_DIM)), _const_spec((1, HEAD_DIM)), _const_spec((1, MEM_HEAD_DIM)),
            pl.BlockSpec((tm, LANES), tab), pl.BlockSpec((tm, LANES), tab), pl.BlockSpec((tm, LANES), tab),
        ],
        out_specs=tuple(pl.BlockSpec((tm * s.shape[0] // m, s.shape[1]), row) for s in out_shape),
        out_shape=out_shape,
        compiler_params=_params("parallel"),
        name="in_proj",
    )(x, g_mix, w_bf16, q_g, k_g, mq_g, *rope_tabs)


def _rope_tables(pos):
    half = ROT_DIM // 2
    inv_freq = jnp.power(jnp.float32(ROPE_THETA), -jnp.arange(half, dtype=F32) * (2.0 / ROT_DIM))
    ang = pos.astype(F32)[:, None] * inv_freq[None, :]
    cos, sin = jnp.cos(ang), jnp.sin(ang)
    n = pos.shape[0]
    ones = jnp.ones((n, LANES - ROT_DIM), F32)
    zeros = jnp.zeros((n, LANES - half), F32)
    rc = jnp.concatenate([cos, cos, ones], axis=1)
    ra = jnp.concatenate([-sin, zeros], axis=1)
    rb = jnp.concatenate([jnp.zeros((n, half), F32), sin, jnp.zeros((n, LANES - ROT_DIM), F32)], axis=1)
    return rc, ra, rb


def _to_residue_layout(dst, src, classes_src, ratio):
    len_src = src.shape[0] // classes_src
    len_dst = len_src // ratio
    for c_src in range(classes_src):
        for a in range(ratio):
            c = c_src + classes_src * a
            dst[c * len_dst:(c + 1) * len_dst, :] = src[pl.ds(c_src * len_src + a, len_dst, stride=ratio), :]


def _prompt_attn_kernel(q0_ref, q1_ref, q2_ref, k_ref, v_ref, o_ref, perm_ref, tmp_ref, acc_ref, lse_ref):
    head = pl.program_id(1)
    seq = q0_ref.shape[0]
    q_refs = (q0_ref, q1_ref, q2_ref)
    dils = [d for _, d in DILATED_GROUPS]
    for t, ref in ((1, k_ref), (2, v_ref)):
        perm_ref[0, t] = ref[pl.ds(head, seq, stride=N_KV_HEADS), :]
        for g in range(1, N_GROUPS):
            _to_residue_layout(perm_ref.at[g, t], perm_ref.at[g - 1, t], dils[g - 1], dils[g] // dils[g - 1])
    for g in range(1, N_GROUPS):
        tmp_ref[0] = q_refs[g][...].astype(F32)
        for step in range(1, g + 1):
            dst = perm_ref.at[g, 0] if step == g else tmp_ref.at[step % 2]
            _to_residue_layout(dst, tmp_ref.at[(step - 1) % 2], dils[step - 1], dils[step] // dils[step - 1])
    iq = lax.broadcasted_iota(jnp.int32, (BAND, BAND), 0)
    ik = lax.broadcasted_iota(jnp.int32, (BAND, BAND), 1)
    keep_c = iq >= ik
    scale = HEAD_DIM ** -0.5

    for g, dil in enumerate(dils):
        nb = seq // (dil * BAND)
        unroll = ATTN_UNROLL * (2 if nb == 1 else 1)

        def body(jj, carry, g=g, dil=dil, nb=nb, unroll=unroll):
            blocks = []
            for u in range(unroll):
                j = jj * unroll + u
                rows = pl.ds(pl.multiple_of(j * BAND, BAND), BAND)
                prev = pl.ds(pl.multiple_of(jnp.maximum(j - 1, 0) * BAND, BAND), BAND)
                q = q0_ref[rows, :] if g == 0 else perm_ref[g, 0, rows, :].astype(BF16)
                s_c = jnp.where(keep_c, _dot_nt(q, perm_ref[g, 1, rows, :].astype(BF16)) * scale, NEG)
                s_p = None
                if nb > 1:
                    keep_p = jnp.logical_and(ik >= iq, j % nb > 0)
                    s_p = jnp.where(keep_p, _dot_nt(q, perm_ref[g, 1, prev, :].astype(BF16)) * scale, NEG)
                blocks.append((j, rows, prev, s_c, s_p))
            probs = []
            for j, rows, prev, s_c, s_p in blocks:
                m = jnp.max(s_c if nb == 1 else jnp.maximum(s_c, s_p), axis=-1, keepdims=True)
                p_c = jnp.exp(s_c - m)
                p_p = None if nb == 1 else jnp.exp(s_p - m)
                l = jnp.sum(p_c if nb == 1 else p_c + p_p, axis=-1, keepdims=True)
                probs.append((m, l, p_c.astype(BF16), None if p_p is None else p_p.astype(BF16)))
            for (j, rows, prev, _, _), (m, l, p_c, p_p) in zip(blocks, probs):
                acc = _dot(p_c, perm_ref[g, 2, rows, :].astype(BF16))
                if nb > 1:
                    acc = acc + _dot(p_p, perm_ref[g, 2, prev, :].astype(BF16))
                out_rows = rows if dil == 1 else pl.ds((j % nb) * (BAND * dil) + j // nb, BAND, stride=dil)
                acc_ref[g, out_rows, :] = acc * (1.0 / l)
                lse_ref[g, out_rows, :] = jnp.broadcast_to(m + jnp.log(l), (BAND, LANES))
            return carry

        assert (dil * nb) % unroll == 0
        lax.fori_loop(0, dil * nb // unroll, body, 0)

    def combine(c, carry):
        rows = pl.ds(pl.multiple_of(c * BAND, BAND), BAND)
        lses = [lse_ref[g, rows, :] for g in range(N_GROUPS)]
        mx = functools.reduce(jnp.maximum, lses)
        ws = [jnp.exp(l - mx) for l in lses]
        out = functools.reduce(jnp.add, [w * acc_ref[g, rows, :] for g, w in enumerate(ws)])
        o_ref[rows, :] = (out * (1.0 / functools.reduce(jnp.add, ws))).astype(BF16)
        return carry

    lax.fori_loop(0, seq // BAND, combine, 0)


def _prompt_attention(q, k_flat, v_flat, batch, seq):
    for window, dil in DILATED_GROUPS:
        assert window // dil == BAND and seq % (dil * BAND) == 0
    qspec = lambda g: pl.BlockSpec((seq, HEAD_DIM), lambda b, h: (b, g * N_KV_HEADS + h))
    kvspec = pl.BlockSpec((seq * N_KV_HEADS, HEAD_DIM), lambda b, h: (b, 0))
    return pl.pallas_call(
        _prompt_attn_kernel,
        grid=(batch, N_KV_HEADS),
        in_specs=[qspec(0), qspec(1), qspec(2), kvspec, kvspec],
        out_specs=pl.BlockSpec((seq, HEAD_DIM), lambda b, h: (b, h)),
        out_shape=jax.ShapeDtypeStruct((batch * seq, ATTN_WIDTH), BF16),
        scratch_shapes=[pltpu.VMEM((N_GROUPS, 3, seq, HEAD_DIM), F32), pltpu.VMEM((2, seq, HEAD_DIM), F32),
                        pltpu.VMEM((N_GROUPS, seq, HEAD_DIM), F32), pltpu.VMEM((N_GROUPS, seq, LANES), F32)],
        compiler_params=_params("parallel", "arbitrary"),
        name="prompt_attn",
    )(q, q, q, k_flat, v_flat)


def _mem_kv_kernel(x_ref, g_ref, w_ref, kg_ref, k_ref, v_ref):
    tm = x_ref.shape[0]
    h = _rms(x_ref[...], g_ref[...]).astype(BF16)
    zk = _dot(h, w_ref[:, :MEM_WIDTH])
    zv = _dot(h, w_ref[:, MEM_WIDTH:])
    for j in range(MEM_HEADS):
        sl = slice(j * MEM_HEAD_DIM, (j + 1) * MEM_HEAD_DIM)
        k_ref[pl.ds(j, tm, stride=MEM_HEADS), :] = _rms(zk[:, sl], kg_ref[...])
        v_ref[pl.ds(j, tm, stride=MEM_HEADS), :] = zv[:, sl]


def _mem_kv(mem, g, w_bf16, k_g):
    m, d = mem.shape
    tm = min(ROW_TILE, m)
    row = lambda i: (i, 0)
    shp = jax.ShapeDtypeStruct((m * MEM_HEADS, MEM_HEAD_DIM), F32)
    ospec = pl.BlockSpec((tm * MEM_HEADS, MEM_HEAD_DIM), row)
    return pl.pallas_call(
        _mem_kv_kernel,
        grid=(m // tm,),
        in_specs=[pl.BlockSpec((tm, d), row), _const_spec((1, d)), _const_spec((d, 2 * MEM_WIDTH)),
                  _const_spec((1, MEM_HEAD_DIM))],
        out_specs=(ospec, ospec),
        out_shape=(shp, shp),
        compiler_params=_params("parallel"),
        name="mem_kv",
    )(mem, g, w_bf16, k_g)


def _mem_attn_kernel(q_ref, k_ref, v_ref, o_ref):
    scale = MEM_HEAD_DIM ** -0.5
    n_mem = k_ref.shape[0] // MEM_HEADS
    for h in range(MEM_HEADS):
        sl = slice(h * MEM_HEAD_DIM, (h + 1) * MEM_HEAD_DIM)
        head_rows = pl.ds(h, n_mem, stride=MEM_HEADS)
        s = _dot_nt(q_ref[:, sl], k_ref[head_rows, :].astype(BF16)) * scale
        p = jnp.exp(s - jnp.max(s, axis=-1, keepdims=True))
        l = jnp.sum(p, axis=-1, keepdims=True)
        o_ref[:, sl] = (_dot(p.astype(BF16), v_ref[head_rows, :].astype(BF16)) * (1.0 / l)).astype(BF16)


def _mem_attention(mq, mem_k_flat, mem_v_flat, batch, seq):
    rows = mem_k_flat.shape[0] // batch
    tq = min(1024, seq)
    nq = seq // tq
    kspec = pl.BlockSpec((rows, MEM_HEAD_DIM), lambda b, i: (b, 0))
    return pl.pallas_call(
        _mem_attn_kernel,
        grid=(batch, nq),
        in_specs=[pl.BlockSpec((tq, MEM_WIDTH), lambda b, i: (b * nq + i, 0)), kspec, kspec],
        out_specs=pl.BlockSpec((tq, MEM_WIDTH), lambda b, i: (b * nq + i, 0)),
        out_shape=jax.ShapeDtypeStruct((batch * seq, MEM_WIDTH), BF16),
        compiler_params=_params("parallel", "parallel"),
        name="mem_attn",
    )(mq, mem_k_flat, mem_v_flat)


CONV_HALO = 32
CONV_CHUNK = 32
CONV_ACC_ROWS = 128
CONV_TAP_UNROLL = 8


def _ln_swish(c, g, b):
    mu = jnp.mean(c, axis=-1, keepdims=True)
    xc = c - mu
    y = xc * lax.rsqrt(jnp.mean(xc * xc, axis=-1, keepdims=True) + EPS) * g + b
    return y * jax.nn.sigmoid(y)


def _conv_prompt_kernel(halo_ref, u_ref, w_ref, b_ref, g_ref, beta_ref, o_ref, ext_ref, conv_ref, wb_ref):
    tc, ch = u_ref.shape
    first = pl.program_id(1) == 0
    lane_tiles = [slice(j * LANES, (j + 1) * LANES) for j in range(ch // LANES)]
    for j, sl in enumerate(lane_tiles):
        halo = jnp.where(first, 0.0, halo_ref[:, sl])
        for dup in range(2):
            ext_ref[j, pl.ds(dup, CONV_HALO, stride=2), :] = halo
            ext_ref[j, pl.ds(2 * CONV_HALO + dup, tc, stride=2), :] = u_ref[:, sl]
    for w in range(CONV_WIDTH):
        wb_ref[w] = jnp.broadcast_to(w_ref[w:w + 1, :], (8, ch))
    lead = CONV_HALO - (CONV_WIDTH - 1)
    rows = CONV_ACC_ROWS
    for j, sl in enumerate(lane_tiles):
        for r0 in range(0, tc, rows):
            def tap(w, acc, j=j, sl=sl, r0=r0):
                win = ext_ref[j, pl.ds(2 * (r0 + lead + w), rows, stride=2), :].reshape(rows // 8, 8, LANES)
                return acc + wb_ref[w, :, sl][None] * win

            acc = lax.fori_loop(0, CONV_WIDTH, tap, jnp.zeros((rows // 8, 8, LANES), F32) + b_ref[:, sl],
                                unroll=CONV_TAP_UNROLL)
            conv_ref[r0:r0 + rows, sl] = acc.reshape(rows, LANES)
    for c0 in range(0, tc, CONV_CHUNK):
        o_ref[c0:c0 + CONV_CHUNK, :] = _ln_swish(conv_ref[c0:c0 + CONV_CHUNK, :], g_ref[...], beta_ref[...]).astype(BF16)


def _conv_prompt(u, conv_w, conv_b, ln_g, ln_b, batch, seq):
    ch = u.shape[1]
    tc = min(ROW_TILE, seq)
    nt = seq // tc
    ratio = tc // CONV_HALO
    return pl.pallas_call(
        _conv_prompt_kernel,
        grid=(batch, nt),
        in_specs=[
            pl.BlockSpec((CONV_HALO, ch), lambda b, i: (jnp.maximum((b * nt + i) * ratio - 1, 0), 0)),
            pl.BlockSpec((tc, ch), lambda b, i: (b * nt + i, 0)),
            _const_spec((CONV_WIDTH, ch)), _const_spec((1, ch)), _const_spec((1, ch)), _const_spec((1, ch)),
        ],
        out_specs=pl.BlockSpec((tc, ch), lambda b, i: (b * nt + i, 0)),
        out_shape=jax.ShapeDtypeStruct((batch * seq, ch), BF16),
        scratch_shapes=[pltpu.VMEM((ch // LANES, 2 * (CONV_HALO + tc), LANES), F32), pltpu.VMEM((tc, ch), F32),
                        pltpu.VMEM((CONV_WIDTH, 8, ch), F32)],
        compiler_params=_params("parallel", "parallel"),
        name="conv_prompt",
    )(u, u, conv_w, conv_b, ln_g, ln_b)


def _conv_sample_kernel(state_ref, new_ref, w_ref, b_ref, g_ref, beta_ref, o_ref, state_out_ref, ext_ref):
    nb, ctx, ch = state_ref.shape
    t = new_ref.shape[1]
    ext_ref[:, 0:ctx, :] = state_ref[...]
    ext_ref[:, ctx:ctx + t, :] = new_ref[...]
    acc = jnp.zeros((nb, t, ch), F32) + b_ref[...]
    for w in range(CONV_WIDTH):
        acc = acc + w_ref[w:w + 1, :] * ext_ref[:, w:w + t, :]
    o_ref[...] = _ln_swish(acc, g_ref[...], beta_ref[...])
    state_out_ref[...] = ext_ref[:, t:t + ctx, :]


def _conv_sample(state, u_new, conv_w, conv_b, ln_g, ln_b):
    batch, ctx, ch = state.shape
    t = u_new.shape[1]
    nb = 8
    blk = lambda n: pl.BlockSpec((nb, n, ch), lambda i: (i, 0, 0))
    return pl.pallas_call(
        _conv_sample_kernel,
        grid=(batch // nb,),
        in_specs=[blk(ctx), blk(t), _const_spec((CONV_WIDTH, ch)), _const_spec((1, ch)), _const_spec((1, ch)),
                  _const_spec((1, ch))],
        out_specs=(blk(t), blk(ctx)),
        out_shape=(jax.ShapeDtypeStruct((batch, t, ch), F32), jax.ShapeDtypeStruct((batch, ctx, ch), F32)),
        scratch_shapes=[pltpu.VMEM((nb, ctx + t + 6, ch), F32)],
        compiler_params=_params("parallel"),
        name="conv_sample",
    )(state, u_new, conv_w, conv_b, ln_g, ln_b)


GROUP_LANES = 16


def _spread_groups(vec, combine):
    t = combine(combine(vec, pltpu.roll(vec, LANES - GROUP_LANES, 1)), pltpu.roll(vec, LANES - 2 * GROUP_LANES, 1))
    lane = lax.broadcasted_iota(jnp.int32, vec.shape, 1)
    return jnp.where(lane < GROUP_LANES, t,
                     jnp.where(lane < 2 * GROUP_LANES, pltpu.roll(t, GROUP_LANES, 1), pltpu.roll(t, 2 * GROUP_LANES, 1)))


SLAB_POS = DILATED_GROUPS[-1][1]
SLAB_ROWS = SLAB_POS * N_KV_HEADS
TAIL_POS = max(w for w, d in DILATED_GROUPS if d < SLAB_POS)


def _sample_attn_kernel(kc_ref, vc_ref, kn_ref, vn_ref, wq_ref, bias_d_ref, bias_t_ref, bias_n_ref, mk_ref, mv_ref,
                        wm_ref, bias_m_ref, kw_ref, vw_ref, a_ref, m_ref, kx_ref, vx_ref):
    n_slab = kc_ref.shape[1]
    new_rows = kn_ref.shape[1]
    t = new_rows // N_KV_HEADS
    tail_slabs = TAIL_POS // SLAB_POS
    n_used = N_GROUPS * GROUP_LANES
    lane = lax.broadcasted_iota(jnp.int32, (1, LANES), 1)

    for src, new, dst in ((kc_ref, kn_ref, kw_ref), (vc_ref, vn_ref, vw_ref)):
        dst[0, :, 0:SLAB_ROWS - new_rows, :] = src[0, :, new_rows:SLAB_ROWS, :]
        dst[0, 0:n_slab - 1, SLAB_ROWS - new_rows:SLAB_ROWS, :] = src[0, 1:n_slab, 0:new_rows, :]
        dst[0, n_slab - 1, SLAB_ROWS - new_rows:SLAB_ROWS, :] = new[0]

    kx_ref[...] = jnp.zeros(kx_ref.shape, F32)
    vx_ref[...] = jnp.zeros(vx_ref.shape, F32)
    kx_ref[0:new_rows, :] = kn_ref[0]
    vx_ref[0:new_rows, :] = vn_ref[0]

    def dilated(ref):
        return ref[0, :, 0:new_rows, :].reshape(n_slab * new_rows, HEAD_DIM).astype(BF16)

    def tail(ref):
        return ref[0, n_slab - tail_slabs:n_slab, :, :].reshape(tail_slabs * SLAB_ROWS, HEAD_DIM).astype(BF16)

    scale = HEAD_DIM ** -0.5
    wq = wq_ref[0]
    s_d = _dot(dilated(kc_ref), wq) * scale + bias_d_ref[...]
    s_t = _dot(tail(kc_ref), wq) * scale + bias_t_ref[...]
    s_n = _dot(kx_ref[...].astype(BF16), wq) * scale + bias_n_ref[...]
    col_max = lambda s: jnp.max(s, axis=0, keepdims=True)
    col_sum = lambda p: jnp.sum(p, axis=0, keepdims=True)
    m_col = jnp.maximum(jnp.maximum(col_max(s_d), col_max(s_t)), col_max(s_n))
    m_joint = jnp.where(lane < n_used, _spread_groups(m_col, jnp.maximum), 0.0)
    p_d, p_t, p_n = jnp.exp(s_d - m_joint), jnp.exp(s_t - m_joint), jnp.exp(s_n - m_joint)
    l_col = col_sum(p_d) + col_sum(p_t) + col_sum(p_n)
    inv = 1.0 / jnp.where(lane < n_used, _spread_groups(l_col, jnp.add), 1.0)
    o = (_dot_tn((p_d * inv).astype(BF16), dilated(vc_ref)) + _dot_tn((p_t * inv).astype(BF16), tail(vc_ref))
         + _dot_tn((p_n * inv).astype(BF16), vx_ref[...].astype(BF16)))
    for h in range(N_KV_HEADS):
        r = h * t
        a_ref[0, :, h * HEAD_DIM:(h + 1) * HEAD_DIM] = functools.reduce(
            jnp.add, [o[g * GROUP_LANES + r:g * GROUP_LANES + r + t, :] for g in range(N_GROUPS)])

    sm = _dot(mk_ref[0].astype(BF16), wm_ref[0]) * (MEM_HEAD_DIM ** -0.5) + bias_m_ref[...]
    pm = jnp.exp(sm - jnp.where(lane < MEM_HEADS * t, col_max(sm), 0.0))
    lm = jnp.where(lane < MEM_HEADS * t, col_sum(pm), 1.0)
    om = _dot_tn((pm * (1.0 / lm)).astype(BF16), mv_ref[0].astype(BF16))
    for h in range(MEM_HEADS):
        m_ref[0, :, h * MEM_HEAD_DIM:(h + 1) * MEM_HEAD_DIM] = om[h * t:(h + 1) * t, :]


def _sample_masks(cache_len, t, n_mem):
    col = np.arange(LANES)[None, :]
    g, c_head, c_tok = col // GROUP_LANES, (col % GROUP_LANES) // t, (col % GROUP_LANES) % t
    used = (col < N_GROUPS * GROUP_LANES) & (col % GROUP_LANES < N_KV_HEADS * t)
    pad = [1] * (LANES // GROUP_LANES - N_GROUPS)
    dil = np.array([d for _, d in DILATED_GROUPS] + pad)[g]
    win = np.array([w for w, _ in DILATED_GROUPS] + pad)[g]
    sparse = dil >= SLAB_POS

    def keep(pos, head, group_sel):
        dist = cache_len + c_tok - pos
        return used & group_sel & (head == c_head) & (dist >= 0) & (dist % dil == 0) & (dist <= win)

    n_slab = cache_len // SLAB_POS
    y = np.arange(n_slab * t * N_KV_HEADS)[:, None]
    keep_d = keep((y // (t * N_KV_HEADS)) * SLAB_POS + (y % (t * N_KV_HEADS)) // N_KV_HEADS, y % N_KV_HEADS, sparse)
    x = np.arange(TAIL_POS * N_KV_HEADS)[:, None]
    keep_t = keep(cache_len - TAIL_POS + x // N_KV_HEADS, x % N_KV_HEADS, ~sparse)
    z = np.arange(LANES)[:, None]
    keep_n = keep(cache_len + z // N_KV_HEADS, z % N_KV_HEADS, True) & (z < t * N_KV_HEADS)
    w = np.arange(n_mem * MEM_HEADS)[:, None]
    keep_m = (col < MEM_HEADS * t) & (w % MEM_HEADS == col // t)
    return tuple(jnp.asarray(np.where(k, 0.0, NEG), F32) for k in (keep_d, keep_t, keep_n, keep_m))


def _query_columns(q, batch, t, n_groups):
    qt = q.reshape(batch, t, n_groups, N_KV_HEADS, HEAD_DIM).transpose(0, 4, 2, 3, 1)
    qt = qt.reshape(batch, HEAD_DIM, n_groups, N_KV_HEADS * t)
    qt = jnp.pad(qt, ((0, 0), (0, 0), (0, 0), (0, GROUP_LANES - N_KV_HEADS * t)))
    qt = qt.reshape(batch, HEAD_DIM, n_groups * GROUP_LANES)
    return jnp.pad(qt, ((0, 0), (0, 0), (0, LANES - n_groups * GROUP_LANES)))


def _sample_attention(q, k_new, v_new, mq, cache_k, cache_v, mem_k, mem_v):
    batch, cache_len = cache_k.shape[0], cache_k.shape[1]
    new_rows = k_new.shape[0] // batch
    t = new_rows // N_KV_HEADS
    n_mem = mem_k.shape[1]
    assert N_KV_HEADS * t <= GROUP_LANES and t <= SLAB_POS and new_rows % 8 == 0
    assert cache_len % SLAB_POS == 0 and cache_len >= max(w for w, _ in DILATED_GROUPS)
    assert all(d == SLAB_POS or w <= TAIL_POS for w, d in DILATED_GROUPS)
    n_slab = cache_len // SLAB_POS
    wq = _query_columns(q, batch, t, N_GROUPS)
    wm = _query_columns(mq, batch, t, 1)
    bias_d, bias_t, bias_n, bias_m = _sample_masks(cache_len, t, n_mem)
    per_b = lambda *shape: pl.BlockSpec((1,) + shape, lambda b: (b,) + (0,) * len(shape))
    slabs = lambda c: c.reshape(batch, n_slab, SLAB_ROWS, HEAD_DIM)
    mem_rows = n_mem * MEM_HEADS
    win = jax.ShapeDtypeStruct((batch, n_slab, SLAB_ROWS, HEAD_DIM), F32)
    k_win, v_win, a, m = pl.pallas_call(
        _sample_attn_kernel,
        grid=(batch,),
        in_specs=[
            per_b(n_slab, SLAB_ROWS, HEAD_DIM), per_b(n_slab, SLAB_ROWS, HEAD_DIM),
            per_b(new_rows, HEAD_DIM), per_b(new_rows, HEAD_DIM), per_b(HEAD_DIM, LANES),
            _const_spec(bias_d.shape), _const_spec(bias_t.shape), _const_spec(bias_n.shape),
            per_b(mem_rows, MEM_HEAD_DIM), per_b(mem_rows, MEM_HEAD_DIM), per_b(MEM_HEAD_DIM, LANES),
            _const_spec(bias_m.shape),
        ],
        out_specs=(per_b(n_slab, SLAB_ROWS, HEAD_DIM), per_b(n_slab, SLAB_ROWS, HEAD_DIM), per_b(t, ATTN_WIDTH),
                   per_b(t, MEM_WIDTH)),
        out_shape=(win, win, jax.ShapeDtypeStruct((batch, t, ATTN_WIDTH), F32),
                   jax.ShapeDtypeStruct((batch, t, MEM_WIDTH), F32)),
        scratch_shapes=[pltpu.VMEM((LANES, HEAD_DIM), F32), pltpu.VMEM((LANES, HEAD_DIM), F32)],
        compiler_params=_params("parallel"),
        name="sample_attn",
    )(slabs(cache_k), slabs(cache_v), k_new.reshape(batch, new_rows, HEAD_DIM), v_new.reshape(batch, new_rows, HEAD_DIM),
      wq, bias_d, bias_t, bias_n, mem_k.reshape(batch, mem_rows, MEM_HEAD_DIM),
      mem_v.reshape(batch, mem_rows, MEM_HEAD_DIM), wm, bias_m)
    return k_win.reshape(cache_k.shape), v_win.reshape(cache_v.shape), a, m


ROUTE_GROUP_LANE0 = 0
ROUTE_EXPERT_LANE0 = N_EXPERT_GROUPS


def _route(logits):
    lane = lax.broadcasted_iota(jnp.int32, logits.shape, 1).astype(F32)
    big = float(LANES)

    def masked_softmax(keep):
        z = jnp.where(keep, logits, NEG)
        e = jnp.where(keep, jnp.exp(z - jnp.max(z, axis=-1, keepdims=True)), 0.0)
        return e / jnp.sum(e, axis=-1, keepdims=True)

    def first_argmax(vals, keep):
        top = jnp.max(jnp.where(keep, vals, -1.0), axis=-1, keepdims=True)
        idx = jnp.min(jnp.where(jnp.logical_and(keep, vals == top), lane, big), axis=-1, keepdims=True)
        return top, idx

    is_group = lane < N_EXPERT_GROUPS
    pg = masked_softmax(is_group)
    pg_top, g_idx = first_argmax(pg, is_group)
    lo = ROUTE_EXPERT_LANE0 + g_idx * EXPERTS_PER_GROUP
    in_group = jnp.logical_and(lane >= lo, lane < lo + EXPERTS_PER_GROUP)
    pe = masked_softmax(in_group)
    p1, i1 = first_argmax(pe, in_group)
    p2, i2 = first_argmax(pe, jnp.logical_and(in_group, lane != i1))
    denom = p1 + p2
    w1 = pg_top * p1 / denom
    w2 = pg_top * p2 / denom
    return lane, i1 - ROUTE_EXPERT_LANE0, i2 - ROUTE_EXPERT_LANE0, w1, w2


def _merge_kernel(x_ref, gmix_ref, ao_ref, cc_ref, mo_ref, wgate_ref, wa_ref, wc_ref, wm_ref, wo_ref, gffn_ref, wr_ref,
                  br_ref, cnt_in_ref, x1_ref, h2_ref, route_ref, cnt_out_ref, cnt_ref):
    @pl.when(pl.program_id(0) == 0)
    def _():
        cnt_ref[...] = cnt_in_ref[...]

    d = x_ref.shape[1]
    x = x_ref[...]
    h = _rms(x, gmix_ref[...]).astype(BF16)
    a = _dot(ao_ref[...].astype(BF16), wa_ref[...])
    c = _dot(cc_ref[...].astype(BF16), wc_ref[...])
    m = _dot(mo_ref[...].astype(BF16), wm_ref[...])
    z = jax.nn.sigmoid(_dot(h, wgate_ref[:, 0:d])) * a
    z = z + jax.nn.sigmoid(_dot(h, wgate_ref[:, d:2 * d])) * c
    z = z + jax.nn.sigmoid(_dot(h, wgate_ref[:, 2 * d:3 * d])) * m
    x1 = x + _dot(z.astype(BF16), wo_ref[...])
    x1_ref[...] = x1
    h2 = _rms(x1, gffn_ref[...]).astype(BF16)
    packed = _pack_halves(h2)
    for c in range(packed.shape[1] // LANES):
        h2_ref[pl.ds(c, x_ref.shape[0], stride=packed.shape[1] // LANES), :] = packed[:, c * LANES:(c + 1) * LANES]
    logits = _dot(h2, wr_ref[...]) + br_ref[...]
    lane, e1, e2, w1, w2 = _route(logits)

    tm = x_ref.shape[0]
    hit1 = jnp.where(lane == e1, 1.0, 0.0)
    hit2 = jnp.where(lane == e2, 1.0, 0.0)
    hits = hit1 + hit2
    earlier = (lax.broadcasted_iota(jnp.int32, (tm, tm), 1) < lax.broadcasted_iota(jnp.int32, (tm, tm), 0))
    before = _dot(jnp.where(earlier, 1.0, 0.0).astype(BF16), hits.astype(BF16)) + cnt_ref[...]
    rank1 = jnp.sum(hit1 * before, axis=-1, keepdims=True)
    rank2 = jnp.sum(hit2 * before, axis=-1, keepdims=True)
    cnt_ref[...] = cnt_ref[...] + jnp.sum(hits, axis=0, keepdims=True)
    cnt_out_ref[...] = cnt_ref[...]
    route = jnp.zeros(logits.shape, F32)
    for i, val in enumerate((e1, e2, w1, w2, rank1, rank2)):
        route = jnp.where(lane == i, val, route)
    route_ref[...] = route


ROUTE_E, ROUTE_W, ROUTE_RANK = 0, 2, 4


def _merge(x, g_mix, ao, cc, mo, w_gate, w_a, w_c, w_m, w_o, g_ffn, w_r, b_r, counts):
    m, d = x.shape
    tm = min(MERGE_TILE, m)
    row = lambda i: (i, 0)
    rows = lambda arr: pl.BlockSpec((tm, arr.shape[1]), row)
    ins = [x, g_mix, ao, cc, mo, w_gate, w_a, w_c, w_m, w_o, g_ffn, w_r, b_r, counts]
    specs = ([rows(x), _const_spec(g_mix.shape), rows(ao), rows(cc), rows(mo)]
             + [_const_spec(a.shape) for a in (w_gate, w_a, w_c, w_m, w_o, g_ffn, w_r, b_r, counts)])
    out_shape = (jax.ShapeDtypeStruct((m, d), F32), jax.ShapeDtypeStruct((m * (d // 2 // LANES), LANES), jnp.uint32),
                 jax.ShapeDtypeStruct((m, LANES), F32), jax.ShapeDtypeStruct((1, LANES), F32))
    return pl.pallas_call(
        _merge_kernel,
        grid=(m // tm,),
        in_specs=specs,
        out_specs=(rows(out_shape[0]), pl.BlockSpec((tm * (d // 2 // LANES), LANES), row), rows(out_shape[2]),
                   _const_spec((1, LANES))),
        out_shape=out_shape,
        scratch_shapes=[pltpu.VMEM((1, LANES), F32)],
        compiler_params=_params("arbitrary"),
        name="merge",
    )(*ins)


def _routing_tables(counts, routes):
    cnt = counts[0, :N_EXPERTS].astype(jnp.int32)
    tiles = (cnt + MOE_TILE - 1) // MOE_TILE
    tile_start = jnp.cumsum(tiles) - tiles
    row_start = tile_start * MOE_TILE
    ids = routes[:, ROUTE_E:ROUTE_E + TOP_K].astype(jnp.int32)
    start_of = jnp.sum(jnp.where(ids[:, :, None] == jnp.arange(N_EXPERTS)[None, None, :], row_start[None, None, :], 0),
                       axis=-1)
    pos = start_of + routes[:, ROUTE_RANK:ROUTE_RANK + TOP_K].astype(jnp.int32)
    return pos.reshape(-1), tile_start.astype(jnp.int32), tiles.astype(jnp.int32)


def _dispatch_kernel(pos_ref, h_ref, xs_in, xs_out, sem, *, row0):
    del xs_in
    tm = h_ref.shape[0]
    base = (row0 + pl.program_id(0) * tm) * TOP_K

    def issue(j, carry):
        for k in range(TOP_K):
            slot = pos_ref[base + j * TOP_K + k]
            pltpu.make_async_copy(h_ref.at[j], xs_out.at[slot], sem).start()
        return carry

    lax.fori_loop(0, tm, issue, 0, unroll=8)
    for k in range(TOP_K):
        pltpu.make_async_copy(h_ref, xs_out.at[pl.ds(0, tm)], sem).wait()


def _dispatch(h2, pos, xs, row0):
    m = h2.shape[0]
    tm = min(WIDE_ROW_TILE, m)
    return pl.pallas_call(
        functools.partial(_dispatch_kernel, row0=row0),
        grid_spec=pltpu.PrefetchScalarGridSpec(
            num_scalar_prefetch=1,
            grid=(m // tm,),
            in_specs=[pl.BlockSpec((tm,) + h2.shape[1:], lambda i, *_: (i, 0, 0)), pl.BlockSpec(memory_space=pl.ANY)],
            out_specs=pl.BlockSpec(memory_space=pl.ANY),
            scratch_shapes=[pltpu.SemaphoreType.DMA(())],
        ),
        out_shape=jax.ShapeDtypeStruct(xs.shape, xs.dtype),
        input_output_aliases={2: 0},
        compiler_params=_params("arbitrary"),
        name="moe_dispatch",
    )(pos, h2, xs)


def _gmm_kernel(start_ref, count_ref, xs_hbm, wg_ref, wu_ref, wd_ref, ys_hbm, xbuf, obuf, wg_s, wu_s, wd_s,
                in_sems, out_sems):
    e = pl.program_id(0)
    nb = GMM_BUFS
    first, n = start_ref[e], count_ref[e]

    x_rows, y_rows = xbuf.shape[1], obuf.shape[1]
    x_tiles, y_tiles = x_rows // MOE_TILE, y_rows // MOE_TILE

    def copies(expert_first):
        def rows(i, n_rows):
            return pl.ds(pl.multiple_of((expert_first + i) * n_rows, n_rows), n_rows)

        def in_copy(i):
            return pltpu.make_async_copy(xs_hbm.at[rows(i, x_rows)], xbuf.at[i % nb], in_sems.at[i % nb])

        def out_copy(i):
            return pltpu.make_async_copy(obuf.at[i % nb], ys_hbm.at[rows(i, y_rows)], out_sems.at[i % nb])

        return in_copy, out_copy

    in_copy, out_copy = copies(first)

    def start_first_reads(expert):
        copy, _ = copies(start_ref[expert])
        lax.fori_loop(0, jnp.minimum(count_ref[expert], nb - 1), lambda i, c: (copy(i).start(), c)[1], 0)

    @pl.when(e == 0)
    def _():
        start_first_reads(0)

    @pl.when(n > 0)
    def _():
        wg_s[...] = wg_ref[0].astype(BF16)
        wu_s[...] = wu_ref[0].astype(BF16)
        wd_s[...] = wd_ref[0].astype(BF16)
        half = x_tiles * LANES

        def tile(i, carry):
            slot = i % nb
            in_copy(i).wait()

            @pl.when(i + nb - 1 < n)
            def _():
                in_copy(i + nb - 1).start()

            @pl.when(i >= nb)
            def _():
                out_copy(i - nb).wait()

            packed = jnp.concatenate([xbuf[slot, pl.ds(c, MOE_TILE, stride=x_tiles), :] for c in range(x_tiles)], axis=1)
            x_hi, x_lo = _unpack_halves(packed)
            gate = _dot(x_hi, wg_s[0:half, :]) + _dot(x_lo, wg_s[half:, :])
            up = _dot(x_hi, wu_s[0:half, :]) + _dot(x_lo, wu_s[half:, :])
            hid = gate * jax.nn.sigmoid(gate) * up
            res = _dot(hid.astype(BF16), wd_s[...])
            for c in range(y_tiles):
                obuf[slot, pl.ds(c, MOE_TILE, stride=y_tiles), :] = res[:, c * LANES:(c + 1) * LANES]
            out_copy(i).start()
            return carry

        lax.fori_loop(0, n, tile, 0)
        lax.fori_loop(jnp.maximum(n - nb, 0), n, lambda i, c: (out_copy(i).wait(), c)[1], 0)

    @pl.when(e + 1 < pl.num_programs(0))
    def _():
        start_first_reads(e + 1)

    @pl.when(e == pl.num_programs(0) - 1)
    def _():
        obuf[0] = jnp.zeros(obuf.shape[1:], F32)

        def fill(t, carry):
            dst = ys_hbm.at[pl.ds(pl.multiple_of(t * y_rows, y_rows), y_rows)]
            copy = pltpu.make_async_copy(obuf.at[0], dst, out_sems.at[0])
            copy.start()
            copy.wait()
            return carry

        lax.fori_loop(first + n, ys_hbm.shape[0] // y_rows, fill, 0)


def _grouped_mlp(xs, tile_start, tile_count, w_gate, w_up, w_down):
    d, ff = w_gate.shape[1], w_gate.shape[2]
    x_tiles, y_tiles = d // 2 // LANES, d // LANES
    n_slots = xs.shape[0] // x_tiles
    wspec = lambda a, b: pl.BlockSpec((1, a, b), lambda e, *_: (e, 0, 0))
    any_spec = pl.BlockSpec(memory_space=pl.ANY)
    return pl.pallas_call(
        _gmm_kernel,
        grid_spec=pltpu.PrefetchScalarGridSpec(
            num_scalar_prefetch=2,
            grid=(N_EXPERTS,),
            in_specs=[any_spec, wspec(d, ff), wspec(d, ff), wspec(ff, d)],
            out_specs=any_spec,
            scratch_shapes=[pltpu.VMEM((GMM_BUFS, MOE_TILE * x_tiles, LANES), xs.dtype),
                            pltpu.VMEM((GMM_BUFS, MOE_TILE * y_tiles, LANES), F32),
                            pltpu.VMEM((d, ff), BF16), pltpu.VMEM((d, ff), BF16), pltpu.VMEM((ff, d), BF16),
                            pltpu.SemaphoreType.DMA((GMM_BUFS,)), pltpu.SemaphoreType.DMA((GMM_BUFS,))],
        ),
        out_shape=jax.ShapeDtypeStruct((n_slots * y_tiles, LANES), F32),
        compiler_params=_params("arbitrary"),
        name="moe_gmm",
    )(tile_start, tile_count, xs, w_gate, w_up, w_down)


def _combine_kernel(pos_ref, x1_ref, route_ref, ys_hbm, ys_flat, y_ref, buf_ref, sems, *, row0):
    tm = x1_ref.shape[0]
    y_tiles = ys_hbm.shape[1]
    i = pl.program_id(0)

    def gather(tile, half):
        base = (row0 + tile * tm) * TOP_K

        def issue(j, carry):
            for k in range(TOP_K):
                slot = pos_ref[base + j * TOP_K + k]
                dst = buf_ref.at[half, k, pl.ds(pl.multiple_of(j * y_tiles, y_tiles), y_tiles)]
                pltpu.make_async_copy(ys_hbm.at[slot], dst, sems.at[half]).start()
            return carry

        lax.fori_loop(0, tm, issue, 0, unroll=4)

    @pl.when(i == 0)
    def _():
        gather(0, 0)

    @pl.when(i + 1 < pl.num_programs(0))
    def _():
        gather(i + 1, (i + 1) % 2)

    half = i % 2
    for k in range(TOP_K):
        pltpu.make_async_copy(ys_flat.at[pl.ds(0, tm * y_tiles)], buf_ref.at[half, k], sems.at[half]).wait()
    route = route_ref[...]
    for c in range(y_tiles):
        sl = slice(c * LANES, (c + 1) * LANES)
        rows = pl.ds(c, tm, stride=y_tiles)
        y_ref[:, sl] = (x1_ref[:, sl] + route[:, ROUTE_W:ROUTE_W + 1] * buf_ref[half, 0, rows, :]
                        + route[:, ROUTE_W + 1:ROUTE_W + 2] * buf_ref[half, 1, rows, :])


def _combine(x1, route, pos, ys, row0):
    m, d = x1.shape
    tm = min(WIDE_ROW_TILE, m)
    row = lambda i, *_: (i, 0)
    any_spec = pl.BlockSpec(memory_space=pl.ANY)
    return pl.pallas_call(
        functools.partial(_combine_kernel, row0=row0),
        grid_spec=pltpu.PrefetchScalarGridSpec(
            num_scalar_prefetch=1,
            grid=(m // tm,),
            in_specs=[pl.BlockSpec((tm, d), row), pl.BlockSpec((tm, LANES), row), any_spec, any_spec],
            out_specs=pl.BlockSpec((tm, d), row),
            scratch_shapes=[pltpu.VMEM((2, TOP_K, tm * (d // LANES), LANES), F32), pltpu.SemaphoreType.DMA((2,))],
        ),
        out_shape=jax.ShapeDtypeStruct((m, d), F32),
        compiler_params=_params("arbitrary"),
        name="moe_combine",
    )(pos, x1, route, ys.reshape(-1, d // LANES, LANES), ys)


def _layer(layer, x_prompt, x_sample, mem_prompt, cache_k, cache_v, state_conv, cache_mem_k, cache_mem_v, p):
    batch, seq, d = x_prompt.shape
    dec_batch, dec_seq, _ = x_sample.shape
    conv_ch = p["conv_w"].shape[-1]
    n_in = N_Q_HEADS * HEAD_DIM + 2 * ATTN_WIDTH + 2 * conv_ch + MEM_WIDTH
    past_len = cache_k.shape[2]

    row2 = lambda name: p[name][layer][None, :]
    w_in = p["w_in"][layer]
    w_main = w_in[:, :n_in].astype(BF16)
    w_gate = w_in[:, n_in:].astype(BF16)
    w_a, w_c, w_m, w_o = (p[n][layer].astype(BF16) for n in ("w_attn_proj", "w_conv_proj", "w_mem_proj", "w_out"))
    w_router = jnp.concatenate(
        [p["w_router_group"][layer], p["w_router_expert"][layer].transpose(1, 0, 2).reshape(d, N_EXPERTS)], axis=1)
    w_router = jnp.pad(w_router, ((0, 0), (0, LANES - w_router.shape[1]))).astype(BF16)
    b_router = jnp.concatenate([p["b_router_group"][layer], p["b_router_expert"][layer].reshape(-1)])
    b_router = jnp.pad(b_router, (0, LANES - b_router.shape[0]))[None, :]
    conv_args = (p["conv_w"][layer], row2("conv_b"), row2("conv_ln_g"), row2("conv_ln_b"))
    merge_w = (w_gate, w_a, w_c, w_m, w_o, row2("norm_ffn_g"), w_router, b_router)

    xp = x_prompt.reshape(batch * seq, d)
    tabs_p = _rope_tables(jnp.arange(seq, dtype=jnp.int32))
    q_p, k_p, v_p, u_p, mq_p = _in_proj(xp, row2("norm_mix_g"), w_main, row2("q_norm_g"), row2("k_norm_g"),
                                        row2("mq_norm_g"), tabs_p, conv_ch)
    ao_p = _prompt_attention(q_p, k_p, v_p, batch, seq)
    mem_k_p, mem_v_p = _mem_kv(mem_prompt.reshape(-1, d), row2("mem_norm_g"), p["w_mem_kv"][layer].astype(BF16),
                               row2("mk_norm_g"))
    mo_p = _mem_attention(mq_p, mem_k_p, mem_v_p, batch, seq)
    cc_p = _conv_prompt(u_p, *conv_args, batch, seq)
    x1_p, h2_p, route_p, counts = _merge(xp, row2("norm_mix_g"), ao_p, cc_p, mo_p, *merge_w,
                                         jnp.zeros((1, LANES), F32))

    xs = x_sample.reshape(dec_batch * dec_seq, d)
    tabs_s = _rope_tables(jnp.tile(past_len + jnp.arange(dec_seq, dtype=jnp.int32), dec_batch))
    q_s, k_s, v_s, u_s, mq_s = _in_proj(xs, row2("norm_mix_g"), w_main, row2("q_norm_g"), row2("k_norm_g"),
                                        row2("mq_norm_g"), tabs_s, conv_ch)
    k_win_s, v_win_s, a_s, mo_s = _sample_attention(q_s, k_s, v_s, mq_s, cache_k[layer], cache_v[layer],
                                                    cache_mem_k[layer], cache_mem_v[layer])
    cc_s, conv_state_s = _conv_sample(state_conv[layer], u_s.reshape(dec_batch, dec_seq, conv_ch), *conv_args)
    x1_s, h2_s, route_s, counts = _merge(xs, row2("norm_mix_g"), a_s.reshape(-1, ATTN_WIDTH),
                                         cc_s.reshape(-1, conv_ch), mo_s.reshape(-1, MEM_WIDTH), *merge_w, counts)

    n_p, n_s = xp.shape[0], xs.shape[0]
    n_tok = n_p + n_s
    n_tiles = (TOP_K * n_tok + N_EXPERTS * (MOE_TILE - 1)) // MOE_TILE + 1
    pos, tile_start, tile_count = _routing_tables(counts, jnp.concatenate([route_p, route_s], axis=0))
    per_token = lambda h: h.reshape(-1, d // 2 // LANES, LANES)
    slots = _dispatch(per_token(h2_p), pos, jnp.zeros((n_tiles * MOE_TILE, d // 2 // LANES, LANES), h2_p.dtype), 0)
    slots = _dispatch(per_token(h2_s), pos, slots, n_p)
    ys = _grouped_mlp(slots.reshape(-1, LANES), tile_start, tile_count, p["w_expert_gate"][layer],
                      p["w_expert_up"][layer], p["w_expert_down"][layer])
    y_p = _combine(x1_p, route_p, pos, ys, 0)
    y_s = _combine(x1_s, route_s, pos, ys, n_p)

    state_p = (k_p.reshape(batch, seq, N_KV_HEADS, HEAD_DIM), v_p.reshape(batch, seq, N_KV_HEADS, HEAD_DIM),
               u_p.reshape(batch, seq, conv_ch)[:, seq - (CONV_WIDTH - 1):],
               mem_k_p.reshape(batch, -1, MEM_HEADS, MEM_HEAD_DIM), mem_v_p.reshape(batch, -1, MEM_HEADS, MEM_HEAD_DIM))
    state_s = (k_win_s, v_win_s, conv_state_s)
    return y_p.reshape(batch, seq, d), y_s.reshape(dec_batch, dec_seq, d), state_p, state_s


def kernel(x_prompt, x_sample, mem_prompt, cache_k, cache_v, state_conv, cache_mem_k, cache_mem_v, norm_mix_g, w_in, q_norm_g, k_norm_g, conv_w, conv_b, conv_ln_g, conv_ln_b, mem_norm_g, w_mem_kv, mq_norm_g, mk_norm_g, w_attn_proj, w_conv_proj, w_mem_proj, w_out, norm_ffn_g, w_router_group, b_router_group, w_router_expert, b_router_expert, w_expert_gate, w_expert_up, w_expert_down):
    p = dict(norm_mix_g=norm_mix_g, w_in=w_in, q_norm_g=q_norm_g, k_norm_g=k_norm_g, conv_w=conv_w, conv_b=conv_b,
             conv_ln_g=conv_ln_g, conv_ln_b=conv_ln_b, mem_norm_g=mem_norm_g, w_mem_kv=w_mem_kv, mq_norm_g=mq_norm_g,
             mk_norm_g=mk_norm_g, w_attn_proj=w_attn_proj, w_conv_proj=w_conv_proj, w_mem_proj=w_mem_proj,
             w_out=w_out, norm_ffn_g=norm_ffn_g, w_router_group=w_router_group, b_router_group=b_router_group,
             w_router_expert=w_router_expert, b_router_expert=b_router_expert, w_expert_gate=w_expert_gate,
             w_expert_up=w_expert_up, w_expert_down=w_expert_down)
    depth = w_in.shape[0]
    seq = x_prompt.shape[1]
    assert seq <= max(w for w, _ in DILATED_GROUPS)
    y_p, y_s = x_prompt, x_sample
    states_p, states_s = [], []
    for layer in range(depth):
        y_p, y_s, st_p, st_s = _layer(layer, y_p, y_s, mem_prompt, cache_k, cache_v, state_conv, cache_mem_k,
                                      cache_mem_v, p)
        states_p.append(st_p)
        states_s.append(st_s)
    stack = lambda states, i: jnp.stack([s[i] for s in states], axis=0)
    return (y_p, y_s, stack(states_p, 0), stack(states_p, 1), stack(states_p, 2), stack(states_p, 3),
            stack(states_p, 4), stack(states_s, 0), stack(states_s, 1), stack(states_s, 2))
```

```python
import functools

import jax
import jax.numpy as jnp
import numpy as np
from jax import lax
from jax.experimental import pallas as pl
from jax.experimental.pallas import tpu as pltpu

HEAD_DIM = 128
N_KV_HEADS = 4
DILATED_GROUPS = ((128, 1), (512, 4), (2048, 16))
N_GROUPS = len(DILATED_GROUPS)
N_Q_HEADS = N_GROUPS * N_KV_HEADS
ATTN_WIDTH = N_KV_HEADS * HEAD_DIM
BAND = 128
ATTN_UNROLL = 8
ROPE_THETA = 500000.0
ROT_DIM = HEAD_DIM // 4
CONV_WIDTH = 31
MEM_HEADS = 4
MEM_HEAD_DIM = 128
MEM_WIDTH = MEM_HEADS * MEM_HEAD_DIM
N_EXPERT_GROUPS = 4
EXPERTS_PER_GROUP = 8
N_EXPERTS = N_EXPERT_GROUPS * EXPERTS_PER_GROUP
TOP_K = 2
EPS = 1e-6
NEG = -1e30

LANES = 128
ROW_TILE = 256
WIDE_ROW_TILE = 512
MERGE_TILE = 512
IN_PROJ_TILE = 256
MOE_TILE = 256
GMM_BUFS = 4
VMEM_LIMIT = 56 * 1024 * 1024

BF16 = jnp.bfloat16
F32 = jnp.float32


def _params(*sem):
    return pltpu.CompilerParams(dimension_semantics=sem, vmem_limit_bytes=VMEM_LIMIT)


def _dot(a, b):
    return jnp.dot(a, b, preferred_element_type=F32)


def _dot_nt(a, b):
    return lax.dot_general(a, b, (((1,), (1,)), ((), ())), preferred_element_type=F32)


def _dot_tn(a, b):
    return lax.dot_general(a, b, (((0,), (0,)), ((), ())), preferred_element_type=F32)


def _rms(x, g):
    return x * lax.rsqrt(jnp.mean(x * x, axis=-1, keepdims=True) + EPS) * g


def _pack_halves(x):
    c = x.shape[1] // 2
    hi = lax.bitcast_convert_type(x[:, :c].astype(F32), jnp.uint32)
    lo = lax.bitcast_convert_type(x[:, c:].astype(F32), jnp.uint32)
    return hi | (lo >> 16)


def _unpack_halves(p):
    hi = lax.bitcast_convert_type(p & jnp.uint32(0xFFFF0000), F32).astype(BF16)
    lo = lax.bitcast_convert_type(p << 16, F32).astype(BF16)
    return hi, lo


def _const_spec(shape):
    nd = len(shape)
    return pl.BlockSpec(shape, lambda *_: (0,) * nd)


def _in_proj_kernel(x_ref, g_ref, w_ref, qg_ref, kg_ref, mqg_ref, rc_ref, ra_ref, rb_ref,
                    q_ref, k_ref, v_ref, u_ref, mq_ref):
    h = _rms(x_ref[...], g_ref[...]).astype(BF16)
    rc, ra, rb = rc_ref[...], ra_ref[...], rb_ref[...]

    def rope(y):
        return y * rc + pltpu.roll(y, LANES - ROT_DIM // 2, 1) * ra + pltpu.roll(y, ROT_DIM // 2, 1) * rb

    col = 0
    zq = _dot(h, w_ref[:, col:col + N_Q_HEADS * HEAD_DIM])
    for j in range(N_Q_HEADS):
        sl = slice(j * HEAD_DIM, (j + 1) * HEAD_DIM)
        q_ref[:, sl] = rope(_rms(zq[:, sl], qg_ref[...])).astype(BF16)
    col += N_Q_HEADS * HEAD_DIM
    tm = x_ref.shape[0]
    zk = _dot(h, w_ref[:, col:col + ATTN_WIDTH])
    for j in range(N_KV_HEADS):
        sl = slice(j * HEAD_DIM, (j + 1) * HEAD_DIM)
        k_ref[pl.ds(j, tm, stride=N_KV_HEADS), :] = rope(_rms(zk[:, sl], kg_ref[...]))
    col += ATTN_WIDTH
    zv = _dot(h, w_ref[:, col:col + ATTN_WIDTH])
    for j in range(N_KV_HEADS):
        v_ref[pl.ds(j, tm, stride=N_KV_HEADS), :] = zv[:, j * HEAD_DIM:(j + 1) * HEAD_DIM]
    col += ATTN_WIDTH
    conv_ch = u_ref.shape[-1]
    za = _dot(h, w_ref[:, col:col + conv_ch])
    zb = _dot(h, w_ref[:, col + conv_ch:col + 2 * conv_ch])
    u_ref[...] = za * jax.nn.sigmoid(zb)
    col += 2 * conv_ch
    zm = _dot(h, w_ref[:, col:col + MEM_WIDTH])
    for j in range(MEM_HEADS):
        sl = slice(j * MEM_HEAD_DIM, (j + 1) * MEM_HEAD_DIM)
        mq_ref[:, sl] = _rms(zm[:, sl], mqg_ref[...]).astype(BF16)


def _in_proj(x, g_mix, w_bf16, q_g, k_g, mq_g, rope_tabs, conv_ch):
    m, d = x.shape
    tm = min(IN_PROJ_TILE, m)
    n_tab_blocks = rope_tabs[0].shape[0] // tm
    row = lambda i: (i, 0)
    tab = lambda i: (i % n_tab_blocks, 0)
    ncols = w_bf16.shape[1]
    out_shape = (
        jax.ShapeDtypeStruct((m, N_Q_HEADS * HEAD_DIM), BF16),
        jax.ShapeDtypeStruct((m * N_KV_HEADS, HEAD_DIM), F32),
        jax.ShapeDtypeStruct((m * N_KV_HEADS, HEAD_DIM), F32),
        jax.ShapeDtypeStruct((m, conv_ch), F32),
        jax.ShapeDtypeStruct((m, MEM_WIDTH), BF16),
    )
    return pl.pallas_call(
        _in_proj_kernel,
        grid=(m // tm,),
        in_specs=[
            pl.BlockSpec((tm, d), row),
            _const_spec((1, d)),
            _const_spec((d, ncols)),
            _const_spec((1, HEAD_DIM)), _const_spec((1, HEAD_DIM)), _const_spec((1, MEM_HEAD_DIM)),
            pl.BlockSpec((tm, LANES), tab), pl.BlockSpec((tm, LANES), tab), pl.BlockSpec((tm, LANES), tab),
        ],
        out_specs=tuple(pl.BlockSpec((tm * s.shape[0] // m, s.shape[1]), row) for s in out_shape),
        out_shape=out_shape,
        compiler_params=_params("parallel"),
        name="in_proj",
    )(x, g_mix, w_bf16, q_g, k_g, mq_g, *rope_tabs)


def _rope_tables(pos):
    half = ROT_DIM // 2
    inv_freq = jnp.power(jnp.float32(ROPE_THETA), -jnp.arange(half, dtype=F32) * (2.0 / ROT_DIM))
    ang = pos.astype(F32)[:, None] * inv_freq[None, :]
    cos, sin = jnp.cos(ang), jnp.sin(ang)
    n = pos.shape[0]
    ones = jnp.ones((n, LANES - ROT_DIM), F32)
    zeros = jnp.zeros((n, LANES - half), F32)
    rc = jnp.concatenate([cos, cos, ones], axis=1)
    ra = jnp.concatenate([-sin, zeros], axis=1)
    rb = jnp.concatenate([jnp.zeros((n, half), F32), sin, jnp.zeros((n, LANES - ROT_DIM), F32)], axis=1)
    return rc, ra, rb


def _to_residue_layout(dst, src, classes_src, ratio):
    len_src = src.shape[0] // classes_src
    len_dst = len_src // ratio
    for c_src in range(classes_src):
        for a in range(ratio):
            c = c_src + classes_src * a
            dst[c * len_dst:(c + 1) * len_dst, :] = src[pl.ds(c_src * len_src + a, len_dst, stride=ratio), :]


def _prompt_attn_kernel(q0_ref, q1_ref, q2_ref, k_ref, v_ref, o_ref, perm_ref, tmp_ref, acc_ref, lse_ref):
    head = pl.program_id(1)
    seq = q0_ref.shape[0]
    q_refs = (q0_ref, q1_ref, q2_ref)
    dils = [d for _, d in DILATED_GROUPS]
    for t, ref in ((1, k_ref), (2, v_ref)):
        perm_ref[0, t] = ref[pl.ds(head, seq, stride=N_KV_HEADS), :]
        for g in range(1, N_GROUPS):
            _to_residue_layout(perm_ref.at[g, t], perm_ref.at[g - 1, t], dils[g - 1], dils[g] // dils[g - 1])
    for g in range(1, N_GROUPS):
        tmp_ref[0] = q_refs[g][...].astype(F32)
        for step in range(1, g + 1):
            dst = perm_ref.at[g, 0] if step == g else tmp_ref.at[step % 2]
            _to_residue_layout(dst, tmp_ref.at[(step - 1) % 2], dils[step - 1], dils[step] // dils[step - 1])
    iq = lax.broadcasted_iota(jnp.int32, (BAND, BAND), 0)
    ik = lax.broadcasted_iota(jnp.int32, (BAND, BAND), 1)
    keep_c = iq >= ik
    scale = HEAD_DIM ** -0.5

    for g, dil in enumerate(dils):
        nb = seq // (dil * BAND)

        def body(jj, carry, g=g, dil=dil, nb=nb):
            blocks = []
            for u in range(ATTN_UNROLL):
                j = jj * ATTN_UNROLL + u
                rows = pl.ds(pl.multiple_of(j * BAND, BAND), BAND)
                prev = pl.ds(pl.multiple_of(jnp.maximum(j - 1, 0) * BAND, BAND), BAND)
                q = q0_ref[rows, :] if g == 0 else perm_ref[g, 0, rows, :].astype(BF16)
                s_c = jnp.where(keep_c, _dot_nt(q, perm_ref[g, 1, rows, :].astype(BF16)) * scale, NEG)
                s_p = None
                if nb > 1:
                    keep_p = jnp.logical_and(ik >= iq, j % nb > 0)
                    s_p = jnp.where(keep_p, _dot_nt(q, perm_ref[g, 1, prev, :].astype(BF16)) * scale, NEG)
                blocks.append((j, rows, prev, s_c, s_p))
            probs = []
            for j, rows, prev, s_c, s_p in blocks:
                m = jnp.max(s_c if nb == 1 else jnp.maximum(s_c, s_p), axis=-1, keepdims=True)
                p_c = jnp.exp(s_c - m)
                p_p = None if nb == 1 else jnp.exp(s_p - m)
                l = jnp.sum(p_c if nb == 1 else p_c + p_p, axis=-1, keepdims=True)
                probs.append((m, l, p_c.astype(BF16), None if p_p is None else p_p.astype(BF16)))
            for (j, rows, prev, _, _), (m, l, p_c, p_p) in zip(blocks, probs):
                acc = _dot(p_c, perm_ref[g, 2, rows, :].astype(BF16))
                if nb > 1:
                    acc = acc + _dot(p_p, perm_ref[g, 2, prev, :].astype(BF16))
                out_rows = rows if dil == 1 else pl.ds((j % nb) * (BAND * dil) + j // nb, BAND, stride=dil)
                acc_ref[g, out_rows, :] = acc * (1.0 / l)
                lse_ref[g, out_rows, :] = jnp.broadcast_to(m + jnp.log(l), (BAND, LANES))
            return carry

        assert (dil * nb) % ATTN_UNROLL == 0
        lax.fori_loop(0, dil * nb // ATTN_UNROLL, body, 0)

    def combine(c, carry):
        rows = pl.ds(pl.multiple_of(c * BAND, BAND), BAND)
        lses = [lse_ref[g, rows, :] for g in range(N_GROUPS)]
        mx = functools.reduce(jnp.maximum, lses)
        ws = [jnp.exp(l - mx) for l in lses]
        out = functools.reduce(jnp.add, [w * acc_ref[g, rows, :] for g, w in enumerate(ws)])
        o_ref[rows, :] = (out * (1.0 / functools.reduce(jnp.add, ws))).astype(BF16)
        return carry

    lax.fori_loop(0, seq // BAND, combine, 0)


def _prompt_attention(q, k_flat, v_flat, batch, seq):
    for window, dil in DILATED_GROUPS:
        assert window // dil == BAND and seq % (dil * BAND) == 0
    qspec = lambda g: pl.BlockSpec((seq, HEAD_DIM), lambda b, h: (b, g * N_KV_HEADS + h))
    kvspec = pl.BlockSpec((seq * N_KV_HEADS, HEAD_DIM), lambda b, h: (b, 0))
    return pl.pallas_call(
        _prompt_attn_kernel,
        grid=(batch, N_KV_HEADS),
        in_specs=[qspec(0), qspec(1), qspec(2), kvspec, kvspec],
        out_specs=pl.BlockSpec((seq, HEAD_DIM), lambda b, h: (b, h)),
        out_shape=jax.ShapeDtypeStruct((batch * seq, ATTN_WIDTH), BF16),
        scratch_shapes=[pltpu.VMEM((N_GROUPS, 3, seq, HEAD_DIM), F32), pltpu.VMEM((2, seq, HEAD_DIM), F32),
                        pltpu.VMEM((N_GROUPS, seq, HEAD_DIM), F32), pltpu.VMEM((N_GROUPS, seq, LANES), F32)],
        compiler_params=_params("parallel", "arbitrary"),
        name="prompt_attn",
    )(q, q, q, k_flat, v_flat)


def _mem_kv_kernel(x_ref, g_ref, w_ref, kg_ref, k_ref, v_ref):
    tm = x_ref.shape[0]
    h = _rms(x_ref[...], g_ref[...]).astype(BF16)
    zk = _dot(h, w_ref[:, :MEM_WIDTH])
    zv = _dot(h, w_ref[:, MEM_WIDTH:])
    for j in range(MEM_HEADS):
        sl = slice(j * MEM_HEAD_DIM, (j + 1) * MEM_HEAD_DIM)
        k_ref[pl.ds(j, tm, stride=MEM_HEADS), :] = _rms(zk[:, sl], kg_ref[...])
        v_ref[pl.ds(j, tm, stride=MEM_HEADS), :] = zv[:, sl]


def _mem_kv(mem, g, w_bf16, k_g):
    m, d = mem.shape
    tm = min(ROW_TILE, m)
    row = lambda i: (i, 0)
    shp = jax.ShapeDtypeStruct((m * MEM_HEADS, MEM_HEAD_DIM), F32)
    ospec = pl.BlockSpec((tm * MEM_HEADS, MEM_HEAD_DIM), row)
    return pl.pallas_call(
        _mem_kv_kernel,
        grid=(m // tm,),
        in_specs=[pl.BlockSpec((tm, d), row), _const_spec((1, d)), _const_spec((d, 2 * MEM_WIDTH)),
                  _const_spec((1, MEM_HEAD_DIM))],
        out_specs=(ospec, ospec),
        out_shape=(shp, shp),
        compiler_params=_params("parallel"),
        name="mem_kv",
    )(mem, g, w_bf16, k_g)


def _mem_attn_kernel(q_ref, k_ref, v_ref, o_ref):
    scale = MEM_HEAD_DIM ** -0.5
    n_mem = k_ref.shape[0] // MEM_HEADS
    for h in range(MEM_HEADS):
        sl = slice(h * MEM_HEAD_DIM, (h + 1) * MEM_HEAD_DIM)
        head_rows = pl.ds(h, n_mem, stride=MEM_HEADS)
        s = _dot_nt(q_ref[:, sl], k_ref[head_rows, :].astype(BF16)) * scale
        p = jnp.exp(s - jnp.max(s, axis=-1, keepdims=True))
        l = jnp.sum(p, axis=-1, keepdims=True)
        o_ref[:, sl] = (_dot(p.astype(BF16), v_ref[head_rows, :].astype(BF16)) * (1.0 / l)).astype(BF16)


def _mem_attention(mq, mem_k_flat, mem_v_flat, batch, seq):
    rows = mem_k_flat.shape[0] // batch
    tq = min(1024, seq)
    nq = seq // tq
    kspec = pl.BlockSpec((rows, MEM_HEAD_DIM), lambda b, i: (b, 0))
    return pl.pallas_call(
        _mem_attn_kernel,
        grid=(batch, nq),
        in_specs=[pl.BlockSpec((tq, MEM_WIDTH), lambda b, i: (b * nq + i, 0)), kspec, kspec],
        out_specs=pl.BlockSpec((tq, MEM_WIDTH), lambda b, i: (b * nq + i, 0)),
        out_shape=jax.ShapeDtypeStruct((batch * seq, MEM_WIDTH), BF16),
        compiler_params=_params("parallel", "parallel"),
        name="mem_attn",
    )(mq, mem_k_flat, mem_v_flat)


CONV_HALO = 32
CONV_CHUNK = 32
CONV_ACC_ROWS = 128
CONV_TAP_UNROLL = 8


def _ln_swish(c, g, b):
    mu = jnp.mean(c, axis=-1, keepdims=True)
    xc = c - mu
    y = xc * lax.rsqrt(jnp.mean(xc * xc, axis=-1, keepdims=True) + EPS) * g + b
    return y * jax.nn.sigmoid(y)


def _conv_prompt_kernel(halo_ref, u_ref, w_ref, b_ref, g_ref, beta_ref, o_ref, ext_ref, conv_ref, wb_ref):
    tc, ch = u_ref.shape
    first = pl.program_id(1) == 0
    lane_tiles = [slice(j * LANES, (j + 1) * LANES) for j in range(ch // LANES)]
    for j, sl in enumerate(lane_tiles):
        halo = jnp.where(first, 0.0, halo_ref[:, sl])
        for dup in range(2):
            ext_ref[j, pl.ds(dup, CONV_HALO, stride=2), :] = halo
            ext_ref[j, pl.ds(2 * CONV_HALO + dup, tc, stride=2), :] = u_ref[:, sl]
    for w in range(CONV_WIDTH):
        wb_ref[w] = jnp.broadcast_to(w_ref[w:w + 1, :], (8, ch))
    lead = CONV_HALO - (CONV_WIDTH - 1)
    rows = CONV_ACC_ROWS
    for j, sl in enumerate(lane_tiles):
        for r0 in range(0, tc, rows):
            def tap(w, acc, j=j, sl=sl, r0=r0):
                win = ext_ref[j, pl.ds(2 * (r0 + lead + w), rows, stride=2), :].reshape(rows // 8, 8, LANES)
                return acc + wb_ref[w, :, sl][None] * win

            acc = lax.fori_loop(0, CONV_WIDTH, tap, jnp.zeros((rows // 8, 8, LANES), F32) + b_ref[:, sl],
                                unroll=CONV_TAP_UNROLL)
            conv_ref[r0:r0 + rows, sl] = acc.reshape(rows, LANES)
    for c0 in range(0, tc, CONV_CHUNK):
        o_ref[c0:c0 + CONV_CHUNK, :] = _ln_swish(conv_ref[c0:c0 + CONV_CHUNK, :], g_ref[...], beta_ref[...]).astype(BF16)


def _conv_prompt(u, conv_w, conv_b, ln_g, ln_b, batch, seq):
    ch = u.shape[1]
    tc = min(ROW_TILE, seq)
    nt = seq // tc
    ratio = tc // CONV_HALO
    return pl.pallas_call(
        _conv_prompt_kernel,
        grid=(batch, nt),
        in_specs=[
            pl.BlockSpec((CONV_HALO, ch), lambda b, i: (jnp.maximum((b * nt + i) * ratio - 1, 0), 0)),
            pl.BlockSpec((tc, ch), lambda b, i: (b * nt + i, 0)),
            _const_spec((CONV_WIDTH, ch)), _const_spec((1, ch)), _const_spec((1, ch)), _const_spec((1, ch)),
        ],
        out_specs=pl.BlockSpec((tc, ch), lambda b, i: (b * nt + i, 0)),
        out_shape=jax.ShapeDtypeStruct((batch * seq, ch), BF16),
        scratch_shapes=[pltpu.VMEM((ch // LANES, 2 * (CONV_HALO + tc), LANES), F32), pltpu.VMEM((tc, ch), F32),
                        pltpu.VMEM((CONV_WIDTH, 8, ch), F32)],
        compiler_params=_params("parallel", "parallel"),
        name="conv_prompt",
    )(u, u, conv_w, conv_b, ln_g, ln_b)


def _conv_sample_kernel(state_ref, new_ref, w_ref, b_ref, g_ref, beta_ref, o_ref, state_out_ref, ext_ref):
    nb, ctx, ch = state_ref.shape
    t = new_ref.shape[1]
    ext_ref[:, 0:ctx, :] = state_ref[...]
    ext_ref[:, ctx:ctx + t, :] = new_ref[...]
    acc = jnp.zeros((nb, t, ch), F32) + b_ref[...]
    for w in range(CONV_WIDTH):
        acc = acc + w_ref[w:w + 1, :] * ext_ref[:, w:w + t, :]
    o_ref[...] = _ln_swish(acc, g_ref[...], beta_ref[...])
    state_out_ref[...] = ext_ref[:, t:t + ctx, :]


def _conv_sample(state, u_new, conv_w, conv_b, ln_g, ln_b):
    batch, ctx, ch = state.shape
    t = u_new.shape[1]
    nb = 8
    blk = lambda n: pl.BlockSpec((nb, n, ch), lambda i: (i, 0, 0))
    return pl.pallas_call(
        _conv_sample_kernel,
        grid=(batch // nb,),
        in_specs=[blk(ctx), blk(t), _const_spec((CONV_WIDTH, ch)), _const_spec((1, ch)), _const_spec((1, ch)),
                  _const_spec((1, ch))],
        out_specs=(blk(t), blk(ctx)),
        out_shape=(jax.ShapeDtypeStruct((batch, t, ch), F32), jax.ShapeDtypeStruct((batch, ctx, ch), F32)),
        scratch_shapes=[pltpu.VMEM((nb, ctx + t + 6, ch), F32)],
        compiler_params=_params("parallel"),
        name="conv_sample",
    )(state, u_new, conv_w, conv_b, ln_g, ln_b)


GROUP_LANES = 16


def _spread_groups(vec, combine):
    t = combine(combine(vec, pltpu.roll(vec, LANES - GROUP_LANES, 1)), pltpu.roll(vec, LANES - 2 * GROUP_LANES, 1))
    lane = lax.broadcasted_iota(jnp.int32, vec.shape, 1)
    return jnp.where(lane < GROUP_LANES, t,
                     jnp.where(lane < 2 * GROUP_LANES, pltpu.roll(t, GROUP_LANES, 1), pltpu.roll(t, 2 * GROUP_LANES, 1)))


SLAB_POS = DILATED_GROUPS[-1][1]
SLAB_ROWS = SLAB_POS * N_KV_HEADS
TAIL_POS = max(w for w, d in DILATED_GROUPS if d < SLAB_POS)


def _sample_attn_kernel(kc_ref, vc_ref, kn_ref, vn_ref, wq_ref, bias_d_ref, bias_t_ref, bias_n_ref, mk_ref, mv_ref,
                        wm_ref, bias_m_ref, kw_ref, vw_ref, a_ref, m_ref, kx_ref, vx_ref):
    n_slab = kc_ref.shape[1]
    new_rows = kn_ref.shape[1]
    t = new_rows // N_KV_HEADS
    tail_slabs = TAIL_POS // SLAB_POS
    n_used = N_GROUPS * GROUP_LANES
    lane = lax.broadcasted_iota(jnp.int32, (1, LANES), 1)

    for src, new, dst in ((kc_ref, kn_ref, kw_ref), (vc_ref, vn_ref, vw_ref)):
        dst[0, :, 0:SLAB_ROWS - new_rows, :] = src[0, :, new_rows:SLAB_ROWS, :]
        dst[0, 0:n_slab - 1, SLAB_ROWS - new_rows:SLAB_ROWS, :] = src[0, 1:n_slab, 0:new_rows, :]
        dst[0, n_slab - 1, SLAB_ROWS - new_rows:SLAB_ROWS, :] = new[0]

    kx_ref[...] = jnp.zeros(kx_ref.shape, F32)
    vx_ref[...] = jnp.zeros(vx_ref.shape, F32)
    kx_ref[0:new_rows, :] = kn_ref[0]
    vx_ref[0:new_rows, :] = vn_ref[0]

    def dilated(ref):
        return ref[0, :, 0:new_rows, :].reshape(n_slab * new_rows, HEAD_DIM).astype(BF16)

    def tail(ref):
        return ref[0, n_slab - tail_slabs:n_slab, :, :].reshape(tail_slabs * SLAB_ROWS, HEAD_DIM).astype(BF16)

    scale = HEAD_DIM ** -0.5
    wq = wq_ref[0]
    s_d = _dot(dilated(kc_ref), wq) * scale + bias_d_ref[...]
    s_t = _dot(tail(kc_ref), wq) * scale + bias_t_ref[...]
    s_n = _dot(kx_ref[...].astype(BF16), wq) * scale + bias_n_ref[...]
    col_max = lambda s: jnp.max(s, axis=0, keepdims=True)
    col_sum = lambda p: jnp.sum(p, axis=0, keepdims=True)
    m_col = jnp.maximum(jnp.maximum(col_max(s_d), col_max(s_t)), col_max(s_n))
    m_joint = jnp.where(lane < n_used, _spread_groups(m_col, jnp.maximum), 0.0)
    p_d, p_t, p_n = jnp.exp(s_d - m_joint), jnp.exp(s_t - m_joint), jnp.exp(s_n - m_joint)
    l_col = col_sum(p_d) + col_sum(p_t) + col_sum(p_n)
    inv = 1.0 / jnp.where(lane < n_used, _spread_groups(l_col, jnp.add), 1.0)
    o = (_dot_tn((p_d * inv).astype(BF16), dilated(vc_ref)) + _dot_tn((p_t * inv).astype(BF16), tail(vc_ref))
         + _dot_tn((p_n * inv).astype(BF16), vx_ref[...].astype(BF16)))
    for h in range(N_KV_HEADS):
        r = h * t
        a_ref[0, :, h * HEAD_DIM:(h + 1) * HEAD_DIM] = functools.reduce(
            jnp.add, [o[g * GROUP_LANES + r:g * GROUP_LANES + r + t, :] for g in range(N_GROUPS)])

    sm = _dot(mk_ref[0].astype(BF16), wm_ref[0]) * (MEM_HEAD_DIM ** -0.5) + bias_m_ref[...]
    pm = jnp.exp(sm - jnp.where(lane < MEM_HEADS * t, col_max(sm), 0.0))
    lm = jnp.where(lane < MEM_HEADS * t, col_sum(pm), 1.0)
    om = _dot_tn((pm * (1.0 / lm)).astype(BF16), mv_ref[0].astype(BF16))
    for h in range(MEM_HEADS):
        m_ref[0, :, h * MEM_HEAD_DIM:(h + 1) * MEM_HEAD_DIM] = om[h * t:(h + 1) * t, :]


def _sample_masks(cache_len, t, n_mem):
    col = np.arange(LANES)[None, :]
    g, c_head, c_tok = col // GROUP_LANES, (col % GROUP_LANES) // t, (col % GROUP_LANES) % t
    used = (col < N_GROUPS * GROUP_LANES) & (col % GROUP_LANES < N_KV_HEADS * t)
    pad = [1] * (LANES // GROUP_LANES - N_GROUPS)
    dil = np.array([d for _, d in DILATED_GROUPS] + pad)[g]
    win = np.array([w for w, _ in DILATED_GROUPS] + pad)[g]
    sparse = dil >= SLAB_POS

    def keep(pos, head, group_sel):
        dist = cache_len + c_tok - pos
        return used & group_sel & (head == c_head) & (dist >= 0) & (dist % dil == 0) & (dist <= win)

    n_slab = cache_len // SLAB_POS
    y = np.arange(n_slab * t * N_KV_HEADS)[:, None]
    keep_d = keep((y // (t * N_KV_HEADS)) * SLAB_POS + (y % (t * N_KV_HEADS)) // N_KV_HEADS, y % N_KV_HEADS, sparse)
    x = np.arange(TAIL_POS * N_KV_HEADS)[:, None]
    keep_t = keep(cache_len - TAIL_POS + x // N_KV_HEADS, x % N_KV_HEADS, ~sparse)
    z = np.arange(LANES)[:, None]
    keep_n = keep(cache_len + z // N_KV_HEADS, z % N_KV_HEADS, True) & (z < t * N_KV_HEADS)
    w = np.arange(n_mem * MEM_HEADS)[:, None]
    keep_m = (col < MEM_HEADS * t) & (w % MEM_HEADS == col // t)
    return tuple(jnp.asarray(np.where(k, 0.0, NEG), F32) for k in (keep_d, keep_t, keep_n, keep_m))


def _query_columns(q, batch, t, n_groups):
    qt = q.reshape(batch, t, n_groups, N_KV_HEADS, HEAD_DIM).transpose(0, 4, 2, 3, 1)
    qt = qt.reshape(batch, HEAD_DIM, n_groups, N_KV_HEADS * t)
    qt = jnp.pad(qt, ((0, 0), (0, 0), (0, 0), (0, GROUP_LANES - N_KV_HEADS * t)))
    qt = qt.reshape(batch, HEAD_DIM, n_groups * GROUP_LANES)
    return jnp.pad(qt, ((0, 0), (0, 0), (0, LANES - n_groups * GROUP_LANES)))


def _sample_attention(q, k_new, v_new, mq, cache_k, cache_v, mem_k, mem_v):
    batch, cache_len = cache_k.shape[0], cache_k.shape[1]
    new_rows = k_new.shape[0] // batch
    t = new_rows // N_KV_HEADS
    n_mem = mem_k.shape[1]
    assert N_KV_HEADS * t <= GROUP_LANES and t <= SLAB_POS and new_rows % 8 == 0
    assert cache_len % SLAB_POS == 0 and cache_len >= max(w for w, _ in DILATED_GROUPS)
    assert all(d == SLAB_POS or w <= TAIL_POS for w, d in DILATED_GROUPS)
    n_slab = cache_len // SLAB_POS
    wq = _query_columns(q, batch, t, N_GROUPS)
    wm = _query_columns(mq, batch, t, 1)
    bias_d, bias_t, bias_n, bias_m = _sample_masks(cache_len, t, n_mem)
    per_b = lambda *shape: pl.BlockSpec((1,) + shape, lambda b: (b,) + (0,) * len(shape))
    slabs = lambda c: c.reshape(batch, n_slab, SLAB_ROWS, HEAD_DIM)
    mem_rows = n_mem * MEM_HEADS
    win = jax.ShapeDtypeStruct((batch, n_slab, SLAB_ROWS, HEAD_DIM), F32)
    k_win, v_win, a, m = pl.pallas_call(
        _sample_attn_kernel,
        grid=(batch,),
        in_specs=[
            per_b(n_slab, SLAB_ROWS, HEAD_DIM), per_b(n_slab, SLAB_ROWS, HEAD_DIM),
            per_b(new_rows, HEAD_DIM), per_b(new_rows, HEAD_DIM), per_b(HEAD_DIM, LANES),
            _const_spec(bias_d.shape), _const_spec(bias_t.shape), _const_spec(bias_n.shape),
            per_b(mem_rows, MEM_HEAD_DIM), per_b(mem_rows, MEM_HEAD_DIM), per_b(MEM_HEAD_DIM, LANES),
            _const_spec(bias_m.shape),
        ],
        out_specs=(per_b(n_slab, SLAB_ROWS, HEAD_DIM), per_b(n_slab, SLAB_ROWS, HEAD_DIM), per_b(t, ATTN_WIDTH),
                   per_b(t, MEM_WIDTH)),
        out_shape=(win, win, jax.ShapeDtypeStruct((batch, t, ATTN_WIDTH), F32),
                   jax.ShapeDtypeStruct((batch, t, MEM_WIDTH), F32)),
        scratch_shapes=[pltpu.VMEM((LANES, HEAD_DIM), F32), pltpu.VMEM((LANES, HEAD_DIM), F32)],
        compiler_params=_params("parallel"),
        name="sample_attn",
    )(slabs(cache_k), slabs(cache_v), k_new.reshape(batch, new_rows, HEAD_DIM), v_new.reshape(batch, new_rows, HEAD_DIM),
      wq, bias_d, bias_t, bias_n, mem_k.reshape(batch, mem_rows, MEM_HEAD_DIM),
      mem_v.reshape(batch, mem_rows, MEM_HEAD_DIM), wm, bias_m)
    return k_win.reshape(cache_k.shape), v_win.reshape(cache_v.shape), a, m


ROUTE_GROUP_LANE0 = 0
ROUTE_EXPERT_LANE0 = N_EXPERT_GROUPS


def _route(logits):
    lane = lax.broadcasted_iota(jnp.int32, logits.shape, 1).astype(F32)
    big = float(LANES)

    def masked_softmax(keep):
        z = jnp.where(keep, logits, NEG)
        e = jnp.where(keep, jnp.exp(z - jnp.max(z, axis=-1, keepdims=True)), 0.0)
        return e / jnp.sum(e, axis=-1, keepdims=True)

    def first_argmax(vals, keep):
        top = jnp.max(jnp.where(keep, vals, -1.0), axis=-1, keepdims=True)
        idx = jnp.min(jnp.where(jnp.logical_and(keep, vals == top), lane, big), axis=-1, keepdims=True)
        return top, idx

    is_group = lane < N_EXPERT_GROUPS
    pg = masked_softmax(is_group)
    pg_top, g_idx = first_argmax(pg, is_group)
    lo = ROUTE_EXPERT_LANE0 + g_idx * EXPERTS_PER_GROUP
    in_group = jnp.logical_and(lane >= lo, lane < lo + EXPERTS_PER_GROUP)
    pe = masked_softmax(in_group)
    p1, i1 = first_argmax(pe, in_group)
    p2, i2 = first_argmax(pe, jnp.logical_and(in_group, lane != i1))
    denom = p1 + p2
    w1 = pg_top * p1 / denom
    w2 = pg_top * p2 / denom
    return lane, i1 - ROUTE_EXPERT_LANE0, i2 - ROUTE_EXPERT_LANE0, w1, w2


def _merge_kernel(x_ref, gmix_ref, ao_ref, cc_ref, mo_ref, wgate_ref, wa_ref, wc_ref, wm_ref, wo_ref, gffn_ref, wr_ref,
                  br_ref, cnt_in_ref, x1_ref, h2_ref, route_ref, cnt_out_ref, cnt_ref):
    @pl.when(pl.program_id(0) == 0)
    def _():
        cnt_ref[...] = cnt_in_ref[...]

    d = x_ref.shape[1]
    x = x_ref[...]
    h = _rms(x, gmix_ref[...]).astype(BF16)
    a = _dot(ao_ref[...].astype(BF16), wa_ref[...])
    c = _dot(cc_ref[...].astype(BF16), wc_ref[...])
    m = _dot(mo_ref[...].astype(BF16), wm_ref[...])
    z = jax.nn.sigmoid(_dot(h, wgate_ref[:, 0:d])) * a
    z = z + jax.nn.sigmoid(_dot(h, wgate_ref[:, d:2 * d])) * c
    z = z + jax.nn.sigmoid(_dot(h, wgate_ref[:, 2 * d:3 * d])) * m
    x1 = x + _dot(z.astype(BF16), wo_ref[...])
    x1_ref[...] = x1
    h2 = _rms(x1, gffn_ref[...]).astype(BF16)
    packed = _pack_halves(h2)
    for c in range(packed.shape[1] // LANES):
        h2_ref[pl.ds(c, x_ref.shape[0], stride=packed.shape[1] // LANES), :] = packed[:, c * LANES:(c + 1) * LANES]
    logits = _dot(h2, wr_ref[...]) + br_ref[...]
    lane, e1, e2, w1, w2 = _route(logits)

    tm = x_ref.shape[0]
    hit1 = jnp.where(lane == e1, 1.0, 0.0)
    hit2 = jnp.where(lane == e2, 1.0, 0.0)
    hits = hit1 + hit2
    earlier = (lax.broadcasted_iota(jnp.int32, (tm, tm), 1) < lax.broadcasted_iota(jnp.int32, (tm, tm), 0))
    before = _dot(jnp.where(earlier, 1.0, 0.0).astype(BF16), hits.astype(BF16)) + cnt_ref[...]
    rank1 = jnp.sum(hit1 * before, axis=-1, keepdims=True)
    rank2 = jnp.sum(hit2 * before, axis=-1, keepdims=True)
    cnt_ref[...] = cnt_ref[...] + jnp.sum(hits, axis=0, keepdims=True)
    cnt_out_ref[...] = cnt_ref[...]
    route = jnp.zeros(logits.shape, F32)
    for i, val in enumerate((e1, e2, w1, w2, rank1, rank2)):
        route = jnp.where(lane == i, val, route)
    route_ref[...] = route


ROUTE_E, ROUTE_W, ROUTE_RANK = 0, 2, 4


def _merge(x, g_mix, ao, cc, mo, w_gate, w_a, w_c, w_m, w_o, g_ffn, w_r, b_r, counts):
    m, d = x.shape
    tm = min(MERGE_TILE, m)
    row = lambda i: (i, 0)
    rows = lambda arr: pl.BlockSpec((tm, arr.shape[1]), row)
    ins = [x, g_mix, ao, cc, mo, w_gate, w_a, w_c, w_m, w_o, g_ffn, w_r, b_r, counts]
    specs = ([rows(x), _const_spec(g_mix.shape), rows(ao), rows(cc), rows(mo)]
             + [_const_spec(a.shape) for a in (w_gate, w_a, w_c, w_m, w_o, g_ffn, w_r, b_r, counts)])
    out_shape = (jax.ShapeDtypeStruct((m, d), F32), jax.ShapeDtypeStruct((m * (d // 2 // LANES), LANES), jnp.uint32),
                 jax.ShapeDtypeStruct((m, LANES), F32), jax.ShapeDtypeStruct((1, LANES), F32))
    return pl.pallas_call(
        _merge_kernel,
        grid=(m // tm,),
        in_specs=specs,
        out_specs=(rows(out_shape[0]), pl.BlockSpec((tm * (d // 2 // LANES), LANES), row), rows(out_shape[2]),
                   _const_spec((1, LANES))),
        out_shape=out_shape,
        scratch_shapes=[pltpu.VMEM((1, LANES), F32)],
        compiler_params=_params("arbitrary"),
        name="merge",
    )(*ins)


def _routing_tables(counts, routes):
    cnt = counts[0, :N_EXPERTS].astype(jnp.int32)
    tiles = (cnt + MOE_TILE - 1) // MOE_TILE
    tile_start = jnp.cumsum(tiles) - tiles
    row_start = tile_start * MOE_TILE
    ids = routes[:, ROUTE_E:ROUTE_E + TOP_K].astype(jnp.int32)
    start_of = jnp.sum(jnp.where(ids[:, :, None] == jnp.arange(N_EXPERTS)[None, None, :], row_start[None, None, :], 0),
                       axis=-1)
    pos = start_of + routes[:, ROUTE_RANK:ROUTE_RANK + TOP_K].astype(jnp.int32)
    return pos.reshape(-1), tile_start.astype(jnp.int32), tiles.astype(jnp.int32)


def _dispatch_kernel(pos_ref, h_ref, xs_in, xs_out, sem, *, row0):
    del xs_in
    tm = h_ref.shape[0]
    base = (row0 + pl.program_id(0) * tm) * TOP_K

    def issue(j, carry):
        for k in range(TOP_K):
            slot = pos_ref[base + j * TOP_K + k]
            pltpu.make_async_copy(h_ref.at[j], xs_out.at[slot], sem).start(priority=k % 2)
        return carry

    lax.fori_loop(0, tm, issue, 0, unroll=8)
    for k in range(TOP_K):
        pltpu.make_async_copy(h_ref, xs_out.at[pl.ds(0, tm)], sem).wait()


def _dispatch(h2, pos, xs, row0):
    m = h2.shape[0]
    tm = min(WIDE_ROW_TILE, m)
    return pl.pallas_call(
        functools.partial(_dispatch_kernel, row0=row0),
        grid_spec=pltpu.PrefetchScalarGridSpec(
            num_scalar_prefetch=1,
            grid=(m // tm,),
            in_specs=[pl.BlockSpec((tm,) + h2.shape[1:], lambda i, *_: (i, 0, 0)), pl.BlockSpec(memory_space=pl.ANY)],
            out_specs=pl.BlockSpec(memory_space=pl.ANY),
            scratch_shapes=[pltpu.SemaphoreType.DMA(())],
        ),
        out_shape=jax.ShapeDtypeStruct(xs.shape, xs.dtype),
        input_output_aliases={2: 0},
        compiler_params=_params("arbitrary"),
        name="moe_dispatch",
    )(pos, h2, xs)


def _gmm_kernel(start_ref, count_ref, xs_hbm, wg_ref, wu_ref, wd_ref, ys_hbm, xbuf, obuf, wg_s, wu_s, wd_s,
                in_sems, out_sems):
    e = pl.program_id(0)
    nb = GMM_BUFS
    first, n = start_ref[e], count_ref[e]

    x_rows, y_rows = xbuf.shape[1], obuf.shape[1]
    x_tiles, y_tiles = x_rows // MOE_TILE, y_rows // MOE_TILE

    def copies(expert_first):
        def rows(i, n_rows):
            return pl.ds(pl.multiple_of((expert_first + i) * n_rows, n_rows), n_rows)

        def in_copy(i):
            return pltpu.make_async_copy(xs_hbm.at[rows(i, x_rows)], xbuf.at[i % nb], in_sems.at[i % nb])

        def out_copy(i):
            return pltpu.make_async_copy(obuf.at[i % nb], ys_hbm.at[rows(i, y_rows)], out_sems.at[i % nb])

        return in_copy, out_copy

    in_copy, out_copy = copies(first)

    def start_first_reads(expert):
        copy, _ = copies(start_ref[expert])
        lax.fori_loop(0, jnp.minimum(count_ref[expert], nb - 1), lambda i, c: (copy(i).start(), c)[1], 0)

    @pl.when(e == 0)
    def _():
        start_first_reads(0)

    @pl.when(n > 0)
    def _():
        wg_s[...] = wg_ref[0].astype(BF16)
        wu_s[...] = wu_ref[0].astype(BF16)
        wd_s[...] = wd_ref[0].astype(BF16)
        half = x_tiles * LANES

        def tile(i, carry):
            slot = i % nb
            in_copy(i).wait()

            @pl.when(i + nb - 1 < n)
            def _():
                in_copy(i + nb - 1).start()

            @pl.when(i >= nb)
            def _():
                out_copy(i - nb).wait()

            packed = jnp.concatenate([xbuf[slot, pl.ds(c, MOE_TILE, stride=x_tiles), :] for c in range(x_tiles)], axis=1)
            x_hi, x_lo = _unpack_halves(packed)
            gate = _dot(x_hi, wg_s[0:half, :]) + _dot(x_lo, wg_s[half:, :])
            up = _dot(x_hi, wu_s[0:half, :]) + _dot(x_lo, wu_s[half:, :])
            hid = gate * jax.nn.sigmoid(gate) * up
            res = _dot(hid.astype(BF16), wd_s[...])
            for c in range(y_tiles):
                obuf[slot, pl.ds(c, MOE_TILE, stride=y_tiles), :] = res[:, c * LANES:(c + 1) * LANES]
            out_copy(i).start()
            return carry

        lax.fori_loop(0, n, tile, 0)
        lax.fori_loop(jnp.maximum(n - nb, 0), n, lambda i, c: (out_copy(i).wait(), c)[1], 0)

    @pl.when(e + 1 < pl.num_programs(0))
    def _():
        start_first_reads(e + 1)

    @pl.when(e == pl.num_programs(0) - 1)
    def _():
        obuf[0] = jnp.zeros(obuf.shape[1:], F32)

        def fill(t, carry):
            dst = ys_hbm.at[pl.ds(pl.multiple_of(t * y_rows, y_rows), y_rows)]
            copy = pltpu.make_async_copy(obuf.at[0], dst, out_sems.at[0])
            copy.start()
            copy.wait()
            return carry

        lax.fori_loop(first + n, ys_hbm.shape[0] // y_rows, fill, 0)


def _grouped_mlp(xs, tile_start, tile_count, w_gate, w_up, w_down):
    d, ff = w_gate.shape[1], w_gate.shape[2]
    x_tiles, y_tiles = d // 2 // LANES, d // LANES
    n_slots = xs.shape[0] // x_tiles
    wspec = lambda a, b: pl.BlockSpec((1, a, b), lambda e, *_: (e, 0, 0))
    any_spec = pl.BlockSpec(memory_space=pl.ANY)
    return pl.pallas_call(
        _gmm_kernel,
        grid_spec=pltpu.PrefetchScalarGridSpec(
            num_scalar_prefetch=2,
            grid=(N_EXPERTS,),
            in_specs=[any_spec, wspec(d, ff), wspec(d, ff), wspec(ff, d)],
            out_specs=any_spec,
            scratch_shapes=[pltpu.VMEM((GMM_BUFS, MOE_TILE * x_tiles, LANES), xs.dtype),
                            pltpu.VMEM((GMM_BUFS, MOE_TILE * y_tiles, LANES), F32),
                            pltpu.VMEM((d, ff), BF16), pltpu.VMEM((d, ff), BF16), pltpu.VMEM((ff, d), BF16),
                            pltpu.SemaphoreType.DMA((GMM_BUFS,)), pltpu.SemaphoreType.DMA((GMM_BUFS,))],
        ),
        out_shape=jax.ShapeDtypeStruct((n_slots * y_tiles, LANES), F32),
        compiler_params=_params("arbitrary"),
        name="moe_gmm",
    )(tile_start, tile_count, xs, w_gate, w_up, w_down)


def _combine_kernel(pos_ref, x1_ref, route_ref, ys_hbm, ys_flat, y_ref, buf_ref, sems, *, row0):
    tm = x1_ref.shape[0]
    y_tiles = ys_hbm.shape[1]
    i = pl.program_id(0)

    def gather(tile, half):
        base = (row0 + tile * tm) * TOP_K

        def issue(j, carry):
            for k in range(TOP_K):
                slot = pos_ref[base + j * TOP_K + k]
                dst = buf_ref.at[half, k, pl.ds(pl.multiple_of(j * y_tiles, y_tiles), y_tiles)]
                pltpu.make_async_copy(ys_hbm.at[slot], dst, sems.at[half]).start(priority=k % 2)
            return carry

        lax.fori_loop(0, tm, issue, 0, unroll=4)

    @pl.when(i == 0)
    def _():
        gather(0, 0)

    @pl.when(i + 1 < pl.num_programs(0))
    def _():
        gather(i + 1, (i + 1) % 2)

    half = i % 2
    for k in range(TOP_K):
        pltpu.make_async_copy(ys_flat.at[pl.ds(0, tm * y_tiles)], buf_ref.at[half, k], sems.at[half]).wait()
    route = route_ref[...]
    for c in range(y_tiles):
        sl = slice(c * LANES, (c + 1) * LANES)
        rows = pl.ds(c, tm, stride=y_tiles)
        y_ref[:, sl] = (x1_ref[:, sl] + route[:, ROUTE_W:ROUTE_W + 1] * buf_ref[half, 0, rows, :]
                        + route[:, ROUTE_W + 1:ROUTE_W + 2] * buf_ref[half, 1, rows, :])


def _combine(x1, route, pos, ys, row0):
    m, d = x1.shape
    tm = min(WIDE_ROW_TILE, m)
    row = lambda i, *_: (i, 0)
    any_spec = pl.BlockSpec(memory_space=pl.ANY)
    return pl.pallas_call(
        functools.partial(_combine_kernel, row0=row0),
        grid_spec=pltpu.PrefetchScalarGridSpec(
            num_scalar_prefetch=1,
            grid=(m // tm,),
            in_specs=[pl.BlockSpec((tm, d), row), pl.BlockSpec((tm, LANES), row), any_spec, any_spec],
            out_specs=pl.BlockSpec((tm, d), row),
            scratch_shapes=[pltpu.VMEM((2, TOP_K, tm * (d // LANES), LANES), F32), pltpu.SemaphoreType.DMA((2,))],
        ),
        out_shape=jax.ShapeDtypeStruct((m, d), F32),
        compiler_params=_params("arbitrary"),
        name="moe_combine",
    )(pos, x1, route, ys.reshape(-1, d // LANES, LANES), ys)


def _layer(layer, x_prompt, x_sample, mem_prompt, cache_k, cache_v, state_conv, cache_mem_k, cache_mem_v, p):
    batch, seq, d = x_prompt.shape
    dec_batch, dec_seq, _ = x_sample.shape
    conv_ch = p["conv_w"].shape[-1]
    n_in = N_Q_HEADS * HEAD_DIM + 2 * ATTN_WIDTH + 2 * conv_ch + MEM_WIDTH
    past_len = cache_k.shape[2]

    row2 = lambda name: p[name][layer][None, :]
    w_in = p["w_in"][layer]
    w_main = w_in[:, :n_in].astype(BF16)
    w_gate = w_in[:, n_in:].astype(BF16)
    w_a, w_c, w_m, w_o = (p[n][layer].astype(BF16) for n in ("w_attn_proj", "w_conv_proj", "w_mem_proj", "w_out"))
    w_router = jnp.concatenate(
        [p["w_router_group"][layer], p["w_router_expert"][layer].transpose(1, 0, 2).reshape(d, N_EXPERTS)], axis=1)
    w_router = jnp.pad(w_router, ((0, 0), (0, LANES - w_router.shape[1]))).astype(BF16)
    b_router = jnp.concatenate([p["b_router_group"][layer], p["b_router_expert"][layer].reshape(-1)])
    b_router = jnp.pad(b_router, (0, LANES - b_router.shape[0]))[None, :]
    conv_args = (p["conv_w"][layer], row2("conv_b"), row2("conv_ln_g"), row2("conv_ln_b"))
    merge_w = (w_gate, w_a, w_c, w_m, w_o, row2("norm_ffn_g"), w_router, b_router)

    xp = x_prompt.reshape(batch * seq, d)
    tabs_p = _rope_tables(jnp.arange(seq, dtype=jnp.int32))
    q_p, k_p, v_p, u_p, mq_p = _in_proj(xp, row2("norm_mix_g"), w_main, row2("q_norm_g"), row2("k_norm_g"),
                                        row2("mq_norm_g"), tabs_p, conv_ch)
    ao_p = _prompt_attention(q_p, k_p, v_p, batch, seq)
    mem_k_p, mem_v_p = _mem_kv(mem_prompt.reshape(-1, d), row2("mem_norm_g"), p["w_mem_kv"][layer].astype(BF16),
                               row2("mk_norm_g"))
    mo_p = _mem_attention(mq_p, mem_k_p, mem_v_p, batch, seq)
    cc_p = _conv_prompt(u_p, *conv_args, batch, seq)
    x1_p, h2_p, route_p, counts = _merge(xp, row2("norm_mix_g"), ao_p, cc_p, mo_p, *merge_w,
                                         jnp.zeros((1, LANES), F32))

    xs = x_sample.reshape(dec_batch * dec_seq, d)
    tabs_s = _rope_tables(jnp.tile(past_len + jnp.arange(dec_seq, dtype=jnp.int32), dec_batch))
    q_s, k_s, v_s, u_s, mq_s = _in_proj(xs, row2("norm_mix_g"), w_main, row2("q_norm_g"), row2("k_norm_g"),
                                        row2("mq_norm_g"), tabs_s, conv_ch)
    k_win_s, v_win_s, a_s, mo_s = _sample_attention(q_s, k_s, v_s, mq_s, cache_k[layer], cache_v[layer],
                                                    cache_mem_k[layer], cache_mem_v[layer])
    cc_s, conv_state_s = _conv_sample(state_conv[layer], u_s.reshape(dec_batch, dec_seq, conv_ch), *conv_args)
    x1_s, h2_s, route_s, counts = _merge(xs, row2("norm_mix_g"), a_s.reshape(-1, ATTN_WIDTH),
                                         cc_s.reshape(-1, conv_ch), mo_s.reshape(-1, MEM_WIDTH), *merge_w, counts)

    n_p, n_s = xp.shape[0], xs.shape[0]
    n_tok = n_p + n_s
    n_tiles = (TOP_K * n_tok + N_EXPERTS * (MOE_TILE - 1)) // MOE_TILE + 1
    pos, tile_start, tile_count = _routing_tables(counts, jnp.concatenate([route_p, route_s], axis=0))
    per_token = lambda h: h.reshape(-1, d // 2 // LANES, LANES)
    slots = _dispatch(per_token(h2_p), pos, jnp.zeros((n_tiles * MOE_TILE, d // 2 // LANES, LANES), h2_p.dtype), 0)
    slots = _dispatch(per_token(h2_s), pos, slots, n_p)
    ys = _grouped_mlp(slots.reshape(-1, LANES), tile_start, tile_count, p["w_expert_gate"][layer],
                      p["w_expert_up"][layer], p["w_expert_down"][layer])
    y_p = _combine(x1_p, route_p, pos, ys, 0)
    y_s = _combine(x1_s, route_s, pos, ys, n_p)

    state_p = (k_p.reshape(batch, seq, N_KV_HEADS, HEAD_DIM), v_p.reshape(batch, seq, N_KV_HEADS, HEAD_DIM),
               u_p.reshape(batch, seq, conv_ch)[:, seq - (CONV_WIDTH - 1):],
               mem_k_p.reshape(batch, -1, MEM_HEADS, MEM_HEAD_DIM), mem_v_p.reshape(batch, -1, MEM_HEADS, MEM_HEAD_DIM))
    state_s = (k_win_s, v_win_s, conv_state_s)
    return y_p.reshape(batch, seq, d), y_s.reshape(dec_batch, dec_seq, d), state_p, state_s


def kernel(x_prompt, x_sample, mem_prompt, cache_k, cache_v, state_conv, cache_mem_k, cache_mem_v, norm_mix_g, w_in, q_norm_g, k_norm_g, conv_w, conv_b, conv_ln_g, conv_ln_b, mem_norm_g, w_mem_kv, mq_norm_g, mk_norm_g, w_attn_proj, w_conv_proj, w_mem_proj, w_out, norm_ffn_g, w_router_group, b_router_group, w_router_expert, b_router_expert, w_expert_gate, w_expert_up, w_expert_down):
    p = dict(norm_mix_g=norm_mix_g, w_in=w_in, q_norm_g=q_norm_g, k_norm_g=k_norm_g, conv_w=conv_w, conv_b=conv_b,
             conv_ln_g=conv_ln_g, conv_ln_b=conv_ln_b, mem_norm_g=mem_norm_g, w_mem_kv=w_mem_kv, mq_norm_g=mq_norm_g,
             mk_norm_g=mk_norm_g, w_attn_proj=w_attn_proj, w_conv_proj=w_conv_proj, w_mem_proj=w_mem_proj,
             w_out=w_out, norm_ffn_g=norm_ffn_g, w_router_group=w_router_group, b_router_group=b_router_group,
             w_router_expert=w_router_expert, b_router_expert=b_router_expert, w_expert_gate=w_expert_gate,
             w_expert_up=w_expert_up, w_expert_down=w_expert_down)
    depth = w_in.shape[0]
    seq = x_prompt.shape[1]
    assert seq <= max(w for w, _ in DILATED_GROUPS)
    y_p, y_s = x_prompt, x_sample
    states_p, states_s = [], []
    for layer in range(depth):
        y_p, y_s, st_p, st_s = _layer(layer, y_p, y_s, mem_prompt, cache_k, cache_v, state_conv, cache_mem_k,
                                      cache_mem_v, p)
        states_p.append(st_p)
        states_s.append(st_s)
    stack = lambda states, i: jnp.stack([s[i] for s in states], axis=0)
    return (y_p, y_s, stack(states_p, 0), stack(states_p, 1), stack(states_p, 2), stack(states_p, 3),
            stack(states_p, 4), stack(states_s, 0), stack(states_s, 1), stack(states_s, 2))
```
